```python
import math
import jax, jax.numpy as jnp
from jax import lax
import numpy as np

D_MODEL = 2048
BATCH = 8
SEQ = 2048
DEPTH = 1

NORM_EPS = 1e-6
HG_HEADS = 16
HG_HEAD_DIM = D_MODEL // HG_HEADS
HG_WIDTH = HG_HEADS * HG_HEAD_DIM
HG_CHUNK = 32
ATT_GROUPS = ((128, 1), (512, 4), (2048, 16))
ATT_HEADS_PER_GROUP = 4
ATT_HEAD_DIM = 128
ATT_HEADS = len(ATT_GROUPS) * ATT_HEADS_PER_GROUP
ATT_WIDTH = ATT_HEADS * ATT_HEAD_DIM
ATT_OUT_WIDTH = ATT_HEADS_PER_GROUP * ATT_HEAD_DIM
ATT_BLOCK = 128
REL_BUCKETS = 32
REL_MAX_DIST = 2048
IN_SPLITS = (HG_WIDTH, HG_WIDTH, HG_WIDTH, HG_WIDTH, ATT_WIDTH, ATT_WIDTH, ATT_WIDTH, D_MODEL, D_MODEL)
IN_WIDTH = sum(IN_SPLITS)
MOE_GROUPS = 8
MOE_EXPERTS_PER_GROUP = 8
MOE_EXPERTS = MOE_GROUPS * MOE_EXPERTS_PER_GROUP
MOE_TOPK = 2
MOE_FF = 512
MOE_BLOCK = 128

kernel_name = "hybrid_hgrn2_dilated_attn_hiermoe"


def rms_norm(x, gain):
    xf = x.astype(jnp.float32)
    y = xf * lax.rsqrt(jnp.mean(xf * xf, axis=-1, keepdims=True) + NORM_EPS)
    return (y * gain.astype(jnp.float32)).astype(x.dtype)


def hgrn2(q, f_pre, i_in, g, lower_bound, norm_gain):
    B, S, _ = q.shape
    H, E, C = HG_HEADS, HG_HEAD_DIM, HG_CHUNK
    nc = S // C
    f = lower_bound + (1.0 - lower_bound) * jax.nn.sigmoid(f_pre.astype(jnp.float32))
    log_f = jnp.log(f)
    k = 1.0 - f

    def chunks(t):
        return t.astype(jnp.float32).reshape(B, nc, C, H, E).transpose(1, 0, 3, 2, 4)

    causal = jnp.tril(jnp.ones((C, C), dtype=bool))

    def step(state, inp):
        qc, kc, vc, lc = inp
        b = jnp.cumsum(lc, axis=2)
        o_inter = jnp.einsum("bhtk,bhkv->bhtv", qc * jnp.exp(b), state)
        diff = b[:, :, :, None, :] - b[:, :, None, :, :]
        decay = jnp.exp(jnp.where(causal[:, :, None], diff, -jnp.inf))
        scores = jnp.einsum("bhtk,bhsk,bhtsk->bhts", qc, kc, decay)
        o_intra = jnp.einsum("bhts,bhsv->bhtv", scores, vc)
        b_last = b[:, :, -1]
        state = jnp.exp(b_last)[..., None] * state + jnp.einsum(
            "bhsk,bhsv->bhkv", kc * jnp.exp(b_last[:, :, None] - b), vc)
        return state, o_inter + o_intra

    s0 = jnp.zeros((B, H, E, E), jnp.float32)
    _, o = lax.scan(step, s0, (chunks(q), chunks(k), chunks(i_in), chunks(log_f)))
    o = o.transpose(1, 0, 3, 2, 4).reshape(B, S, H, E)
    o = o * lax.rsqrt(jnp.mean(o * o, axis=-1, keepdims=True) + NORM_EPS)
    o = o.reshape(B, S, H * E) * norm_gain.astype(jnp.float32) * jax.nn.sigmoid(g.astype(jnp.float32))
    return o.astype(q.dtype)


def t5_bucket(dist):
    exact = REL_BUCKETS // 2
    d_f = jnp.maximum(dist, 1).astype(jnp.float32)
    log_b = exact + (jnp.log(d_f / exact) / math.log(REL_MAX_DIST / exact)
                     * (REL_BUCKETS - exact)).astype(jnp.int32)
    return jnp.where(dist < exact, dist, jnp.minimum(log_b, REL_BUCKETS - 1))


def dilated_group_attention(q, k, v, rel_bias, window, dilation):
    B, S, H, E = q.shape
    n_back = window // dilation
    L = S // dilation
    blk = min(ATT_BLOCK, L)
    nb = -(-L // blk)
    Lp = nb * blk

    def to_blocks(t):
        t = t.astype(jnp.float32).reshape(B, L, dilation, H, E).transpose(0, 2, 3, 1, 4)
        t = jnp.pad(t, ((0, 0), (0, 0), (0, 0), (0, Lp - L), (0, 0)))
        return t.reshape(B, dilation, H, nb, blk, E)

    def with_prev(t):
        prev = jnp.pad(t[:, :, :, :-1], ((0, 0), (0, 0), (0, 0), (1, 0), (0, 0), (0, 0)))
        return jnp.concatenate([prev, t], axis=4)

    qb = to_blocks(q)
    kc = with_prev(to_blocks(k))
    vc = with_prev(to_blocks(v))
    i = jnp.arange(blk)[:, None]
    c = jnp.arange(2 * blk)[None, :]
    delta = i + blk - c
    n = jnp.arange(nb)[:, None, None]
    valid = (delta >= 0) & (delta <= n_back) & (n * blk + c - blk >= 0)
    bias = rel_bias[t5_bucket(jnp.maximum(delta, 0) * dilation)]
    bias = jnp.transpose(bias, (2, 0, 1)).astype(jnp.float32)[:, None]
    logits = jnp.einsum("brhnqe,brhnke->brhnqk", qb, kc) * (E ** -0.5) + bias
    logits = jnp.where(valid, logits, -1e30)
    m = jnp.max(logits, axis=-1, keepdims=True)
    p = jnp.exp(logits - m)
    den = jnp.sum(p, axis=-1)
    o = jnp.einsum("brhnqk,brhnke->brhnqe", p, vc) / den[..., None]
    lse = m[..., 0] + jnp.log(den)
    o = o.reshape(B, dilation, H, Lp, E)[:, :, :, :L].transpose(0, 3, 1, 2, 4).reshape(B, S, H, E)
    lse = lse.reshape(B, dilation, H, Lp)[..., :L].transpose(0, 3, 1, 2).reshape(B, S, H)
    return o, lse


def dilated_attention(q, k, v, rel_bias):
    B, S = q.shape[:2]
    outs, lses = [], []
    for gi, (window, dilation) in enumerate(ATT_GROUPS):
        hs = slice(gi * ATT_HEADS_PER_GROUP, (gi + 1) * ATT_HEADS_PER_GROUP)
        o, lse = dilated_group_attention(q[:, :, hs], k[:, :, hs], v[:, :, hs],
                                         rel_bias[:, hs], window, dilation)
        outs.append(o)
        lses.append(lse)
    alpha = jax.nn.softmax(jnp.stack(lses, axis=0), axis=0)
    o = jnp.einsum("gbsh,gbshe->bshe", alpha, jnp.stack(outs, axis=0))
    return o.reshape(B, S, ATT_OUT_WIDTH)


def hybrid_mixer(h, w_in, lower_bound, hg_norm_gain, rel_bias, w_branch_a, w_branch_b, w_out):
    B, S, _ = h.shape
    proj = h @ w_in
    split_points = np.cumsum(IN_SPLITS)[:-1].tolist()
    q_a, f_a, i_a, g_a, q_b, k_b, v_b, gate_a, gate_b = jnp.split(proj, split_points, axis=-1)
    y_a = hgrn2(q_a, f_a, i_a, g_a, lower_bound, hg_norm_gain)
    heads = lambda t: t.reshape(B, S, ATT_HEADS, ATT_HEAD_DIM)
    y_b = dilated_attention(heads(q_b), heads(k_b), heads(v_b), rel_bias).astype(h.dtype)
    merged = jax.nn.sigmoid(gate_a) * (y_a @ w_branch_a) + jax.nn.sigmoid(gate_b) * (y_b @ w_branch_b)
    return merged @ w_out


def hierarchical_moe(h, w_rg, b_rg, w_re, b_re, w_gate, w_up, w_down):
    B, S, D = h.shape
    T = B * S
    hf = h.reshape(T, D)
    h32 = hf.astype(jnp.float32)
    lg_group = h32 @ w_rg.astype(jnp.float32) + b_rg.astype(jnp.float32)
    p_group = jax.nn.softmax(lg_group, axis=-1)
    g_sel = jnp.argmax(lg_group, axis=-1).astype(jnp.int32)
    pg = jnp.take_along_axis(p_group, g_sel[:, None], axis=-1)
    lg_exp = (h32 @ w_re.astype(jnp.float32) + b_re.astype(jnp.float32)).reshape(
        T, MOE_GROUPS, MOE_EXPERTS_PER_GROUP)
    lg_in = jnp.take_along_axis(lg_exp, g_sel[:, None, None], axis=1)[:, 0]
    top_v, top_i = lax.top_k(lg_in, MOE_TOPK)
    gate = pg * jax.nn.softmax(top_v, axis=-1)
    expert = g_sel[:, None] * MOE_EXPERTS_PER_GROUP + top_i

    A = T * MOE_TOPK
    eid = expert.reshape(A)
    tok = jnp.repeat(jnp.arange(T, dtype=jnp.int32), MOE_TOPK)
    wts = gate.reshape(A)
    order = jnp.argsort(eid)
    e_s, tok_s, w_s = eid[order], tok[order], wts[order]
    counts = jnp.bincount(eid, length=MOE_EXPERTS)
    offs = jnp.cumsum(counts) - counts
    pcounts = (counts + MOE_BLOCK - 1) // MOE_BLOCK * MOE_BLOCK
    pends = jnp.cumsum(pcounts)
    poffs = pends - pcounts
    dest = poffs[e_s] + (jnp.arange(A) - offs[e_s])
    n_blocks = -(-A // MOE_BLOCK) + MOE_EXPERTS
    P = n_blocks * MOE_BLOCK
    buf_tok = jnp.full((P,), T, jnp.int32).at[dest].set(tok_s)
    buf_w = jnp.zeros((P,), jnp.float32).at[dest].set(w_s)
    block_expert = jnp.minimum(jnp.searchsorted(pends, jnp.arange(n_blocks) * MOE_BLOCK, side="right"),
                               MOE_EXPERTS - 1).astype(jnp.int32)
    h_pad = jnp.concatenate([hf, jnp.zeros((1, D), hf.dtype)], axis=0)

    def run_block(args):
        e, tk = args
        xb = h_pad[tk]
        u = jax.nn.silu(xb @ w_gate[e]) * (xb @ w_up[e])
        return u @ w_down[e]

    y_blocks = lax.map(run_block, (block_expert, buf_tok.reshape(n_blocks, MOE_BLOCK)))
    y_rows = y_blocks.reshape(P, D) * buf_w[:, None].astype(y_blocks.dtype)
    y = jnp.zeros((T + 1, D), y_blocks.dtype).at[buf_tok].add(y_rows)
    return y[:T].reshape(B, S, D)


def setup_inputs(seed: int = 0) -> dict:
    key = jax.random.key(seed)
    ks = jax.random.split(key, 18)
    f32 = jnp.float32
    D, F, G, E = D_MODEL, MOE_FF, MOE_GROUPS, MOE_EXPERTS

    def nrm(k, shape, scale):
        return jax.random.normal(k, shape, f32) * scale

    return {
        "x": nrm(ks[0], (BATCH, SEQ, D), 1.0),
        "norm1_gain": 1.0 + nrm(ks[1], (DEPTH, D), 0.02),
        "w_in": nrm(ks[2], (DEPTH, D, IN_WIDTH), D ** -0.5),
        "hg_lb_logits": nrm(ks[3], (DEPTH + 1, HG_WIDTH), 0.5),
        "hg_norm_gain": 1.0 + nrm(ks[4], (DEPTH, HG_WIDTH), 0.02),
        "rel_bias": nrm(ks[5], (REL_BUCKETS, ATT_HEADS), 0.5),
        "w_branch_a": nrm(ks[6], (DEPTH, HG_WIDTH, D), HG_WIDTH ** -0.5),
        "w_branch_b": nrm(ks[7], (DEPTH, ATT_OUT_WIDTH, D), ATT_OUT_WIDTH ** -0.5),
        "w_out": nrm(ks[8], (DEPTH, D, D), D ** -0.5),
        "norm2_gain": 1.0 + nrm(ks[9], (DEPTH, D), 0.02),
        "w_router_group": nrm(ks[10], (DEPTH, D, G), D ** -0.5),
        "b_router_group": nrm(ks[11], (DEPTH, G), 0.01),
        "w_router_expert": nrm(ks[12], (DEPTH, D, E), D ** -0.5),
        "b_router_expert": nrm(ks[13], (DEPTH, E), 0.01),
        "w_exp_gate": nrm(ks[14], (DEPTH, E, D, F), D ** -0.5),
        "w_exp_up": nrm(ks[15], (DEPTH, E, D, F), D ** -0.5),
        "w_exp_down": nrm(ks[16], (DEPTH, E, F, D), F ** -0.5),
        "final_norm_gain": 1.0 + nrm(ks[17], (D,), 0.02),
    }


def reference(x, norm1_gain, w_in, hg_lb_logits, hg_norm_gain, rel_bias, w_branch_a, w_branch_b,
              w_out, norm2_gain, w_router_group, b_router_group, w_router_expert, b_router_expert,
              w_exp_gate, w_exp_up, w_exp_down, final_norm_gain):
    lower_bounds = jnp.cumsum(jax.nn.softmax(hg_lb_logits.astype(jnp.float32), axis=0), axis=0)
    for layer in range(DEPTH):
        h = rms_norm(x, norm1_gain[layer])
        x = x + hybrid_mixer(h, w_in[layer], lower_bounds[layer], hg_norm_gain[layer], rel_bias,
                             w_branch_a[layer], w_branch_b[layer], w_out[layer])
        h = rms_norm(x, norm2_gain[layer])
        x = x + hierarchical_moe(h, w_router_group[layer], b_router_group[layer],
                                 w_router_expert[layer], b_router_expert[layer],
                                 w_exp_gate[layer], w_exp_up[layer], w_exp_down[layer])
    return rms_norm(x, final_norm_gain)
```

```python
import functools
import math

import numpy as np
import jax
import jax.numpy as jnp
from jax import lax
from jax.experimental import pallas as pl
from jax.experimental.pallas import tpu as pltpu

F32 = jnp.float32
BF16 = jnp.bfloat16

LANES = 128
NORM_EPS = 1e-6
HEAD_DIM = 128
HG_HEADS = 16
ATT_GROUPS = ((128, 1), (512, 4), (2048, 16))
ATT_HEADS_PER_GROUP = 4
ATT_HEADS = len(ATT_GROUPS) * ATT_HEADS_PER_GROUP
ATT_BLOCK = 128
REL_BUCKETS = 32
REL_MAX_DIST = 2048
MOE_GROUPS = 8
MOE_EXPERTS_PER_GROUP = 8
MOE_EXPERTS = MOE_GROUPS * MOE_EXPERTS_PER_GROUP
MOE_TOPK = 2
MOE_ROWS = 256
NEG_BIG = -1e30
VMEM_LIMIT = 56 * 1024 * 1024


def _cparams(sem):
    return pltpu.CompilerParams(dimension_semantics=sem, vmem_limit_bytes=VMEM_LIMIT)


def _rmsnorm_kernel(x_ref, g_ref, o_ref):
    x = x_ref[...]
    ms = jnp.mean(x * x, axis=-1, keepdims=True)
    o_ref[...] = (x * lax.rsqrt(ms + NORM_EPS) * g_ref[...]).astype(o_ref.dtype)


def _rmsnorm(x2d, gain, out_dtype, tm=512):
    T, D = x2d.shape
    return pl.pallas_call(
        _rmsnorm_kernel,
        grid=(T // tm,),
        in_specs=[pl.BlockSpec((tm, D), lambda i: (i, 0)), pl.BlockSpec((1, D), lambda i: (0, 0))],
        out_specs=pl.BlockSpec((tm, D), lambda i: (i, 0)),
        out_shape=jax.ShapeDtypeStruct((T, D), out_dtype),
        compiler_params=_cparams(("parallel",)),
        name="rmsnorm",
    )(x2d, gain.reshape(1, D).astype(F32))


def _inproj_kernel(h_ref, w_ref, lb_ref, o_ref, *, mode, head_major):
    acc = jnp.dot(h_ref[...], w_ref[...], preferred_element_type=F32)
    if mode == "logf":
        lb = lb_ref[...]
        acc = jnp.log(lb + (1.0 - lb) * jax.nn.sigmoid(acc))
    elif mode == "sigmoid":
        acc = jax.nn.sigmoid(acc)
    if head_major:
        for hh in range(acc.shape[1] // HEAD_DIM):
            o_ref[0, hh] = acc[:, hh * HEAD_DIM:(hh + 1) * HEAD_DIM].astype(o_ref.dtype)
    else:
        o_ref[...] = acc.astype(o_ref.dtype)


def _inproj(h, w, lb, *, mode, head_major, out_dtype, B, S, tm=512, tn=1024):
    T, D = h.shape
    N = w.shape[1]
    assert N % tn == 0 and T % tm == 0 and S % tm == 0
    spb = S // tm
    if head_major:
        out_shape = jax.ShapeDtypeStruct((B, N // HEAD_DIM, S, HEAD_DIM), out_dtype)
        out_spec = pl.BlockSpec((1, tn // HEAD_DIM, tm, HEAD_DIM), lambda j, i: (i // spb, j, i % spb, 0))
    else:
        out_shape = jax.ShapeDtypeStruct((T, N), out_dtype)
        out_spec = pl.BlockSpec((tm, tn), lambda j, i: (i, j))
    return pl.pallas_call(
        functools.partial(_inproj_kernel, mode=mode, head_major=head_major),
        grid=(N // tn, T // tm),
        in_specs=[
            pl.BlockSpec((tm, D), lambda j, i: (i, 0)),
            pl.BlockSpec((D, tn), lambda j, i: (0, j)),
            pl.BlockSpec((1, tn), lambda j, i: (0, j)),
        ],
        out_specs=out_spec,
        out_shape=out_shape,
        compiler_params=_cparams(("parallel", "parallel")),
        name="inproj_" + mode,
    )(h, w, lb)


HG_CHUNK = 128
HG_DIAG = 16
HG_HEADS_PER_STEP = 2


def _hgrn_levels(C):
    out, m = [], C // 2
    while m >= HG_DIAG:
        out.append(m)
        m //= 2
    return out


def _hgrn_masks(C):
    t = np.arange(C)[:, None]
    s = np.arange(C)[None, :]
    masks = []
    for m in _hgrn_levels(C):
        masks.append((t // (2 * m) == s // (2 * m)) & ((t // m) % 2 == 1) & ((s // m) % 2 == 0))
    masks.append((t // HG_DIAG == s // HG_DIAG) & (t >= s))
    total = np.sum(np.stack(masks).astype(np.int32), axis=0)
    assert np.array_equal(total, (t >= s).astype(np.int32))
    return np.stack(masks).astype(np.float32)


def _split3(x):
    hi = x.astype(BF16)
    r1 = x - hi.astype(F32)
    mid = r1.astype(BF16)
    lo = (r1 - mid.astype(F32)).astype(BF16)
    return hi, mid, lo


def _dot_nt(a, b):
    return lax.dot_general(a, b, (((1,), (1,)), ((), ())), preferred_element_type=F32)


def _dot_tn(a, b):
    return lax.dot_general(a, b, (((0,), (0,)), ((), ())), preferred_element_type=F32)


def _hgrn_kernel(q_ref, lf_ref, v_ref, sg_ref, gain_ref, tril_ref, mask_ref, o_ref, st_ref, b_ref, *, C):
    S = q_ref.shape[2]
    HP = q_ref.shape[1]
    levels = _hgrn_levels(C)
    st_ref[...] = jnp.zeros_like(st_ref)
    tril = tril_ref[...]

    def ref_rows(h, block, row_of_block):
        parts = [jnp.broadcast_to(b_ref[h, pl.ds(row_of_block(p), 1), :], (block, HEAD_DIM))
                 for p in range(C // block)]
        return jnp.concatenate(parts, axis=0)

    def chunk(c, carry):
        r0 = pl.multiple_of(c * C, C)
        for h in range(HP):
            lf = lf_ref[0, h, pl.ds(r0, C), :]
            q = q_ref[0, h, pl.ds(r0, C), :].astype(F32)
            v = v_ref[0, h, pl.ds(r0, C), :]
            kk = 1.0 - jnp.exp(lf)
            hi, mid, lo = _split3(lf)
            b = (jnp.dot(tril, hi, preferred_element_type=F32)
                 + jnp.dot(tril, mid, preferred_element_type=F32)
                 + jnp.dot(tril, lo, preferred_element_type=F32))
            b_ref[h] = b
            b_last = b_ref[h, pl.ds(C - 1, 1), :]

            scores = jnp.zeros((C, C), F32)
            for li, m in enumerate(levels):
                d = b - ref_rows(h, 2 * m, lambda p, m=m: 2 * m * p + m - 1)
                a_l = (q * jnp.exp(jnp.minimum(d, 0.0))).astype(BF16)
                b_l = (kk * jnp.exp(jnp.minimum(-d, 0.0))).astype(BF16)
                scores = scores + mask_ref[li] * _dot_nt(a_l, b_l)
            d = b - ref_rows(h, HG_DIAG, lambda p: HG_DIAG * p + HG_DIAG // 2 - 1)
            a_l = (q * jnp.exp(d)).astype(BF16)
            b_l = (kk * jnp.exp(-d)).astype(BF16)
            scores = scores + mask_ref[len(levels)] * _dot_nt(a_l, b_l)

            st = st_ref[h]
            qe = (q * jnp.exp(b)).astype(BF16)
            o = _dot_nt(qe, st.astype(BF16)) + jnp.dot(scores.astype(BF16), v, preferred_element_type=F32)
            kd = (kk * jnp.exp(b_last - b)).astype(BF16)
            st_ref[h] = st * jnp.exp(b_last) + _dot_tn(v, kd)

            ms = jnp.mean(o * o, axis=-1, keepdims=True)
            y = o * lax.rsqrt(ms + NORM_EPS) * gain_ref[:, h * HEAD_DIM:(h + 1) * HEAD_DIM]
            y = y * sg_ref[0, h, pl.ds(r0, C), :].astype(F32)
            o_ref[0, pl.ds(r0, C), h * HEAD_DIM:(h + 1) * HEAD_DIM] = y.astype(o_ref.dtype)
        return carry

    lax.fori_loop(0, S // C, chunk, 0)


def _hgrn(q, lf, v, sg, gain):
    B, H, S, E = q.shape
    C, HP = HG_CHUNK, HG_HEADS_PER_STEP
    masks = jnp.asarray(_hgrn_masks(C))
    tril = jnp.asarray(np.tril(np.ones((C, C), np.float32)), dtype=BF16)
    head_spec = pl.BlockSpec((1, HP, S, E), lambda b, h: (b, h, 0, 0))
    return pl.pallas_call(
        functools.partial(_hgrn_kernel, C=C),
        grid=(B, H // HP),
        in_specs=[
            head_spec, head_spec, head_spec, head_spec,
            pl.BlockSpec((1, HP * E), lambda b, h: (0, h)),
            pl.BlockSpec((C, C), lambda b, h: (0, 0)),
            pl.BlockSpec(masks.shape, lambda b, h: (0, 0, 0)),
        ],
        out_specs=pl.BlockSpec((1, S, HP * E), lambda b, h: (b, 0, h)),
        out_shape=jax.ShapeDtypeStruct((B, S, H * E), BF16),
        scratch_shapes=[pltpu.VMEM((HP, E, E), F32), pltpu.VMEM((HP, C, E), F32)],
        compiler_params=_cparams(("parallel", "parallel")),
        name="hgrn2",
    )(q, lf, v, sg, gain.reshape(1, H * E).astype(F32), tril, masks)


def _t5_bucket_np(dist):
    exact = REL_BUCKETS // 2
    d_f = np.maximum(dist, 1).astype(np.float32)
    log_b = exact + (np.log(d_f / np.float32(exact)) / np.float32(math.log(REL_MAX_DIST / exact))
                     * np.float32(REL_BUCKETS - exact)).astype(np.int32)
    return np.where(dist < exact, dist, np.minimum(log_b, REL_BUCKETS - 1))


def _attn_bias(rel_bias):
    i = np.arange(ATT_BLOCK)[:, None]
    c = np.arange(2 * ATT_BLOCK)[None, :]
    delta = i + ATT_BLOCK - c
    out = []
    for gi, (window, dilation) in enumerate(ATT_GROUPS):
        valid = (delta >= 0) & (delta <= window // dilation)
        bucket = _t5_bucket_np(np.maximum(delta, 0) * dilation)
        hs = slice(gi * ATT_HEADS_PER_GROUP, (gi + 1) * ATT_HEADS_PER_GROUP)
        bias = jnp.transpose(rel_bias[:, hs][bucket], (2, 0, 1)).astype(F32)
        out.append(jnp.where(valid[None], bias, NEG_BIG))
    return jnp.stack(out, axis=0)


def _attn_kernel(q_ref, k_ref, v_ref, bias_ref, o_ref, qf, kf, vf, og, lg):
    S = q_ref.shape[3]
    scale = HEAD_DIM ** -0.5
    blk = ATT_BLOCK

    for g, (window, d) in enumerate(ATT_GROUPS):
        L = S // d
        nb = L // blk
        qf[...] = q_ref[0, g, 0].astype(F32)
        kf[...] = k_ref[0, g, 0].astype(F32)
        vf[...] = v_ref[0, g, 0].astype(F32)

        def block(r, n, g=g, d=d):
            start = r + n * blk * d
            rows = pl.ds(start, blk, stride=d) if d > 1 else pl.ds(start, blk)
            q = qf[rows, :].astype(BF16)
            if n == 0:
                k = kf[rows, :].astype(BF16)
                v = vf[rows, :].astype(BF16)
                s = _dot_nt(q, k) * scale + bias_ref[g, 0, :, blk:]
            else:
                first = start - blk * d
                rows2 = pl.ds(first, 2 * blk, stride=d) if d > 1 else pl.ds(first, 2 * blk)
                k = kf[rows2, :].astype(BF16)
                v = vf[rows2, :].astype(BF16)
                s = _dot_nt(q, k) * scale + bias_ref[g, 0]
            m = jnp.max(s, axis=-1, keepdims=True)
            p = jnp.exp(s - m)
            den = jnp.sum(p, axis=-1, keepdims=True)
            o = jnp.dot(p.astype(BF16), v, preferred_element_type=F32) / den
            og[g, rows, :] = o
            lg[g, rows, :] = jnp.broadcast_to(m + jnp.log(den), (blk, HEAD_DIM))

        if d == 1:
            for n in range(nb):
                block(0, n)
        else:
            def residue(r, carry, nb=nb, block=block):
                for n in range(nb):
                    block(r, n)
                return carry
            lax.fori_loop(0, d, residue, 0)

    rows_per_step = 256

    def mix(i, carry):
        rows = pl.ds(pl.multiple_of(i * rows_per_step, rows_per_step), rows_per_step)
        l0, l1, l2 = lg[0, rows, :], lg[1, rows, :], lg[2, rows, :]
        mx = jnp.maximum(jnp.maximum(l0, l1), l2)
        e0, e1, e2 = jnp.exp(l0 - mx), jnp.exp(l1 - mx), jnp.exp(l2 - mx)
        num = e0 * og[0, rows, :] + e1 * og[1, rows, :] + e2 * og[2, rows, :]
        o_ref[0, rows, :] = (num / (e0 + e1 + e2)).astype(o_ref.dtype)
        return carry

    lax.fori_loop(0, S // rows_per_step, mix, 0)


def _attention(qkv, bias):
    B, _, S, E = qkv.shape
    G, HG = len(ATT_GROUPS), ATT_HEADS_PER_GROUP
    x = qkv.reshape(B, 3, G, HG, S, E)

    def spec(which):
        return pl.BlockSpec((1, None, G, 1, S, E), lambda b, h, which=which: (b, which, 0, h, 0, 0))

    def kern(q_ref, k_ref, v_ref, bias_ref, o_ref, *scratch):
        _attn_kernel(q_ref, k_ref, v_ref, bias_ref, o_ref, *scratch)

    return pl.pallas_call(
        kern,
        grid=(B, HG),
        in_specs=[spec(0), spec(1), spec(2),
                  pl.BlockSpec((G, 1, ATT_BLOCK, 2 * ATT_BLOCK), lambda b, h: (0, h, 0, 0))],
        out_specs=pl.BlockSpec((1, S, E), lambda b, h: (b, 0, h)),
        out_shape=jax.ShapeDtypeStruct((B, S, HG * E), BF16),
        scratch_shapes=[pltpu.VMEM((S, E), F32)] * 3 + [pltpu.VMEM((G, S, E), F32)] * 2,
        compiler_params=_cparams(("parallel", "parallel")),
        name="dilated_attn",
    )(x, x, x, bias)


def _merge_kernel(ya_ref, yb_ref, wa_ref, wb_ref, ga_ref, gb_ref, o_ref):
    a = jnp.dot(ya_ref[...], wa_ref[...], preferred_element_type=F32)
    b = jnp.dot(yb_ref[...], wb_ref[...], preferred_element_type=F32)
    o_ref[...] = (ga_ref[...].astype(F32) * a + gb_ref[...].astype(F32) * b).astype(o_ref.dtype)


def _merge(ya, yb, wa, wb, gates, tm=512, tn=1024):
    T, Ka = ya.shape
    Kb = yb.shape[1]
    N = wa.shape[1]
    nj = N // tn
    return pl.pallas_call(
        _merge_kernel,
        grid=(nj, T // tm),
        in_specs=[
            pl.BlockSpec((tm, Ka), lambda j, i: (i, 0)),
            pl.BlockSpec((tm, Kb), lambda j, i: (i, 0)),
            pl.BlockSpec((Ka, tn), lambda j, i: (0, j)),
            pl.BlockSpec((Kb, tn), lambda j, i: (0, j)),
            pl.BlockSpec((tm, tn), lambda j, i: (i, j)),
            pl.BlockSpec((tm, tn), lambda j, i, nj=nj: (i, j + nj)),
        ],
        out_specs=pl.BlockSpec((tm, tn), lambda j, i: (i, j)),
        out_shape=jax.ShapeDtypeStruct((T, N), BF16),
        compiler_params=_cparams(("parallel", "parallel")),
        name="branch_merge",
    )(ya, yb, wa, wb, gates, gates)


ROUTE_LANES = LANES


def _outproj_kernel(m_ref, w_ref, x_ref, g_ref, wrh_ref, wrl_ref, br_ref, x1_ref, h2_ref, rt_ref):
    x1 = x_ref[...] + jnp.dot(m_ref[...], w_ref[...], preferred_element_type=F32)
    x1_ref[...] = x1
    ms = jnp.mean(x1 * x1, axis=-1, keepdims=True)
    h2 = x1 * lax.rsqrt(ms + NORM_EPS) * g_ref[...]
    h2_ref[...] = h2
    h_hi = h2.astype(BF16)
    h_lo = (h2 - h_hi.astype(F32)).astype(BF16)
    wrh = wrh_ref[...]
    lg = (jnp.dot(h_hi, wrh, preferred_element_type=F32) + jnp.dot(h_lo, wrh, preferred_element_type=F32)
          + jnp.dot(h_hi, wrl_ref[...], preferred_element_type=F32)) + br_ref[...]

    lane = lax.broadcasted_iota(jnp.int32, lg.shape, 1)
    lane_f = lane.astype(F32)
    big = float(ROUTE_LANES)
    is_group = (lane >= MOE_EXPERTS) & (lane < MOE_EXPERTS + MOE_GROUPS)
    lgg = jnp.where(is_group, lg, -jnp.inf)
    gmax = jnp.max(lgg, axis=-1, keepdims=True)
    gsel = jnp.min(jnp.where(lgg == gmax, lane_f - MOE_EXPERTS, big), axis=-1, keepdims=True)
    pg = 1.0 / jnp.sum(jnp.where(is_group, jnp.exp(lg - gmax), 0.0), axis=-1, keepdims=True)

    in_group = (lane < MOE_EXPERTS) & ((lane // MOE_EXPERTS_PER_GROUP).astype(F32) == gsel)
    le = jnp.where(in_group, lg, -jnp.inf)
    t1 = jnp.max(le, axis=-1, keepdims=True)
    i1 = jnp.min(jnp.where(le == t1, lane_f, big), axis=-1, keepdims=True)
    le2 = jnp.where(lane_f == i1, -jnp.inf, le)
    t2 = jnp.max(le2, axis=-1, keepdims=True)
    i2 = jnp.min(jnp.where(le2 == t2, lane_f, big), axis=-1, keepdims=True)
    e2 = jnp.exp(t2 - t1)
    w1 = pg / (1.0 + e2)
    w2 = pg * e2 / (1.0 + e2)
    rt_ref[...] = jnp.where(lane == 0, i1, jnp.where(lane == 1, i2, jnp.where(lane == 2, w1,
                            jnp.where(lane == 3, w2, 0.0))))


def _outproj(merged, w_out, x2d, gain, wr_hi, wr_lo, br, tm=256):
    T, D = x2d.shape
    row = lambda i: (i, 0)
    const = lambda i: (0, 0)
    return pl.pallas_call(
        _outproj_kernel,
        grid=(T // tm,),
        in_specs=[
            pl.BlockSpec((tm, D), row), pl.BlockSpec((D, D), const), pl.BlockSpec((tm, D), row),
            pl.BlockSpec((1, D), const), pl.BlockSpec((D, ROUTE_LANES), const),
            pl.BlockSpec((D, ROUTE_LANES), const), pl.BlockSpec((1, ROUTE_LANES), const),
        ],
        out_specs=[pl.BlockSpec((tm, D), row), pl.BlockSpec((tm, D), row), pl.BlockSpec((tm, ROUTE_LANES), row)],
        out_shape=[jax.ShapeDtypeStruct((T, D), F32), jax.ShapeDtypeStruct((T, D), F32),
                   jax.ShapeDtypeStruct((T, ROUTE_LANES), F32)],
        compiler_params=_cparams(("parallel",)),
        name="outproj_router",
    )(merged, w_out, x2d, gain.reshape(1, D).astype(F32), wr_hi, wr_lo, br)


def _dispatch_kernel(dest_ref, h_ref, xs_in_ref, xs_ref, sem, *, tq):
    del xs_in_ref
    base = pl.program_id(0) * tq

    def copy(t, k):
        return pltpu.make_async_copy(h_ref.at[pl.ds(t, 1)], xs_ref.at[pl.ds(dest_ref[2 * t + k], 1)], sem)

    def issue(j, carry):
        copy(base + j, 0).start()
        copy(base + j, 1).start()
        return carry

    def drain(j, carry):
        copy(base + j, 0).wait()
        copy(base + j, 1).wait()
        return carry

    lax.fori_loop(0, tq, issue, 0)
    lax.fori_loop(0, tq, drain, 0)


def _dispatch(h2, dest_flat, n_rows, tq=512):
    T, D = h2.shape
    zeros = jnp.zeros((n_rows, D), h2.dtype)
    return pl.pallas_call(
        functools.partial(_dispatch_kernel, tq=tq),
        grid_spec=pltpu.PrefetchScalarGridSpec(
            num_scalar_prefetch=1,
            grid=(T // tq,),
            in_specs=[pl.BlockSpec(memory_space=pl.ANY), pl.BlockSpec(memory_space=pl.ANY)],
            out_specs=pl.BlockSpec(memory_space=pl.ANY),
            scratch_shapes=[pltpu.SemaphoreType.DMA(())],
        ),
        out_shape=jax.ShapeDtypeStruct((n_rows, D), h2.dtype),
        input_output_aliases={2: 0},
        compiler_params=_cparams(("arbitrary",)),
        name="moe_dispatch",
    )(dest_flat, h2, zeros)


def _expert_kernel(be_ref, nu_ref, xs_ref, wg_ref, wu_ref, wd_ref, o_ref, wg_s, wu_s, wd_s):
    i = pl.program_id(0)
    prev = be_ref[jnp.maximum(i - 1, 0)]
    new_expert = (i == 0) | (be_ref[i] != prev)

    @pl.when(new_expert)
    def _():
        wg_s[...] = wg_ref[0].astype(BF16)
        wu_s[...] = wu_ref[0].astype(BF16)
        wd_s[...] = wd_ref[0].astype(BF16)

    @pl.when(i < nu_ref[0])
    def _():
        x = xs_ref[...].astype(BF16)
        g = jnp.dot(x, wg_s[...], preferred_element_type=F32)
        u = jnp.dot(x, wu_s[...], preferred_element_type=F32)
        a = (g * jax.nn.sigmoid(g) * u).astype(BF16)
        o_ref[...] = jnp.dot(a, wd_s[...], preferred_element_type=F32)

    @pl.when(i >= nu_ref[0])
    def _():
        o_ref[...] = jnp.zeros_like(o_ref)


def _experts(xs, block_expert, n_used, w_gate, w_up, w_down):
    P, D = xs.shape
    F = w_gate.shape[2]
    nblk = P // MOE_ROWS
    rows = lambda i, be, nu: (jnp.minimum(i, nu[0] - 1), 0)
    return pl.pallas_call(
        _expert_kernel,
        grid_spec=pltpu.PrefetchScalarGridSpec(
            num_scalar_prefetch=2,
            grid=(nblk,),
            in_specs=[
                pl.BlockSpec((MOE_ROWS, D), rows),
                pl.BlockSpec((1, D, F), lambda i, be, nu: (be[i], 0, 0)),
                pl.BlockSpec((1, D, F), lambda i, be, nu: (be[i], 0, 0)),
                pl.BlockSpec((1, F, D), lambda i, be, nu: (be[i], 0, 0)),
            ],
            out_specs=pl.BlockSpec((MOE_ROWS, D), lambda i, be, nu: (i, 0)),
            scratch_shapes=[pltpu.VMEM((D, F), BF16), pltpu.VMEM((D, F), BF16), pltpu.VMEM((F, D), BF16)],
        ),
        out_shape=jax.ShapeDtypeStruct((P, D), F32),
        compiler_params=_cparams(("arbitrary",)),
        name="moe_experts",
    )(block_expert, n_used, xs, w_gate, w_up, w_down)


def _combine_kernel(dest_ref, x1_ref, rt_ref, g_ref, ys_ref, o_ref, buf, sem, *, tq):
    base = pl.program_id(0) * tq

    def copy(j, k):
        return pltpu.make_async_copy(ys_ref.at[pl.ds(dest_ref[2 * (base + j) + k], 1)],
                                     buf.at[k, pl.ds(j, 1)], sem)

    def issue(j, carry):
        copy(j, 0).start()
        copy(j, 1).start()
        return carry

    def drain(j, carry):
        copy(j, 0).wait()
        copy(j, 1).wait()
        return carry

    lax.fori_loop(0, tq, issue, 0)
    lax.fori_loop(0, tq, drain, 0)
    rt = rt_ref[...]
    x = x1_ref[...] + rt[:, 2:3] * buf[0] + rt[:, 3:4] * buf[1]
    ms = jnp.mean(x * x, axis=-1, keepdims=True)
    o_ref[...] = x * lax.rsqrt(ms + NORM_EPS) * g_ref[...]


def _combine(x1, route, gain, ys, dest_flat, tq=256):
    T, D = x1.shape
    return pl.pallas_call(
        functools.partial(_combine_kernel, tq=tq),
        grid_spec=pltpu.PrefetchScalarGridSpec(
            num_scalar_prefetch=1,
            grid=(T // tq,),
            in_specs=[
                pl.BlockSpec((tq, D), lambda i, d: (i, 0)),
                pl.BlockSpec((tq, ROUTE_LANES), lambda i, d: (i, 0)),
                pl.BlockSpec((1, D), lambda i, d: (0, 0)),
                pl.BlockSpec(memory_space=pl.ANY),
            ],
            out_specs=pl.BlockSpec((tq, D), lambda i, d: (i, 0)),
            scratch_shapes=[pltpu.VMEM((MOE_TOPK, tq, D), F32), pltpu.SemaphoreType.DMA(())],
        ),
        out_shape=jax.ShapeDtypeStruct((T, D), F32),
        compiler_params=_cparams(("arbitrary",)),
        name="moe_combine",
    )(dest_flat, x1, route, gain.reshape(1, D).astype(F32), ys)


def _route_metadata(route, T):
    e = route[:, :MOE_TOPK].astype(jnp.int32)
    ids = jnp.arange(MOE_EXPERTS, dtype=jnp.int32)
    onehot = ((e[:, 0, None] == ids) | (e[:, 1, None] == ids)).astype(jnp.int32)
    incl = jnp.cumsum(onehot, axis=0)
    counts = incl[-1]
    rank = jnp.take_along_axis(incl - onehot, e, axis=1)
    pcounts = (counts + MOE_ROWS - 1) // MOE_ROWS * MOE_ROWS
    pends = jnp.cumsum(pcounts)
    poffs = pends - pcounts
    dest = poffs[e] + rank
    nblk = (T * MOE_TOPK) // MOE_ROWS + MOE_EXPERTS
    block_expert = jnp.minimum(
        jnp.searchsorted(pends, jnp.arange(nblk, dtype=jnp.int32) * MOE_ROWS, side="right"),
        MOE_EXPERTS - 1).astype(jnp.int32)
    n_used = (pends[-1:] // MOE_ROWS).astype(jnp.int32)
    return dest.reshape(-1).astype(jnp.int32), block_expert, n_used, nblk


def kernel(x, norm1_gain, w_in, hg_lb_logits, hg_norm_gain, rel_bias, w_branch_a, w_branch_b, w_out,
           norm2_gain, w_router_group, b_router_group, w_router_expert, b_router_expert,
           w_exp_gate, w_exp_up, w_exp_down, final_norm_gain):
    B, S, D = x.shape
    T = B * S
    depth = norm1_gain.shape[0]
    lower_bounds = jnp.cumsum(jax.nn.softmax(hg_lb_logits.astype(F32), axis=0), axis=0)
    att_w = ATT_HEADS * HEAD_DIM
    bias = _attn_bias(rel_bias)
    x2d = x.reshape(T, D)
    for layer in range(depth):
        w = w_in[layer]
        h = _rmsnorm(x2d, norm1_gain[layer], BF16)
        zeros_lb = jnp.zeros((1, D), F32)
        seg = lambda a, n: w[:, a:a + n].astype(BF16)
        hm = dict(head_major=True, B=B, S=S)
        q_a = _inproj(h, seg(0, D), zeros_lb, mode="none", out_dtype=BF16, **hm)
        lf_a = _inproj(h, seg(D, D), lower_bounds[layer].reshape(1, D), mode="logf", out_dtype=F32, **hm)
        i_a = _inproj(h, seg(2 * D, D), zeros_lb, mode="none", out_dtype=BF16, **hm)
        sg_a = _inproj(h, seg(3 * D, D), zeros_lb, mode="sigmoid", out_dtype=BF16, **hm)
        qkv_b = _inproj(h, seg(4 * D, 3 * att_w), jnp.zeros((1, 3 * att_w), F32), mode="none",
                        out_dtype=BF16, tn=att_w, **hm)
        gates = _inproj(h, seg(4 * D + 3 * att_w, 2 * D), jnp.zeros((1, 2 * D), F32), mode="sigmoid",
                        head_major=False, out_dtype=BF16, B=B, S=S)

        y_a = _hgrn(q_a, lf_a, i_a, sg_a, hg_norm_gain[layer]).reshape(T, D)
        y_b = _attention(qkv_b, bias).reshape(T, ATT_HEADS_PER_GROUP * HEAD_DIM)
        merged = _merge(y_a, y_b, w_branch_a[layer].astype(BF16), w_branch_b[layer].astype(BF16), gates)

        wr = jnp.zeros((D, ROUTE_LANES), F32)
        wr = wr.at[:, :MOE_EXPERTS].set(w_router_expert[layer].astype(F32))
        wr = wr.at[:, MOE_EXPERTS:MOE_EXPERTS + MOE_GROUPS].set(w_router_group[layer].astype(F32))
        br = jnp.zeros((1, ROUTE_LANES), F32)
        br = br.at[0, :MOE_EXPERTS].set(b_router_expert[layer].astype(F32))
        br = br.at[0, MOE_EXPERTS:MOE_EXPERTS + MOE_GROUPS].set(b_router_group[layer].astype(F32))
        wr_hi = wr.astype(BF16)
        wr_lo = (wr - wr_hi.astype(F32)).astype(BF16)
        x1, h2, route = _outproj(merged, w_out[layer].astype(BF16), x2d, norm2_gain[layer], wr_hi, wr_lo, br)

        dest, block_expert, n_used, nblk = _route_metadata(route, T)
        xs = _dispatch(h2, dest, nblk * MOE_ROWS)
        ys = _experts(xs, block_expert, n_used, w_exp_gate[layer], w_exp_up[layer], w_exp_down[layer])
        last = layer == depth - 1
        assert last, "the fused combine applies the final norm; deeper stacks need an un-normalised combine"
        x2d = _combine(x1, route, final_norm_gain, ys, dest)
    return x2d.reshape(B, S, D)
```

```python
import functools
import math

import numpy as np
import jax
import jax.numpy as jnp
from jax import lax
from jax.experimental import pallas as pl
from jax.experimental.pallas import tpu as pltpu

F32 = jnp.float32
BF16 = jnp.bfloat16

LANES = 128
NORM_EPS = 1e-6
HEAD_DIM = 128
HG_HEADS = 16
ATT_GROUPS = ((128, 1), (512, 4), (2048, 16))
ATT_HEADS_PER_GROUP = 4
ATT_HEADS = len(ATT_GROUPS) * ATT_HEADS_PER_GROUP
ATT_BLOCK = 128
REL_BUCKETS = 32
REL_MAX_DIST = 2048
MOE_GROUPS = 8
MOE_EXPERTS_PER_GROUP = 8
MOE_EXPERTS = MOE_GROUPS * MOE_EXPERTS_PER_GROUP
MOE_TOPK = 2
MOE_ROWS = 256
NEG_BIG = -1e30
VMEM_LIMIT = 56 * 1024 * 1024


def _cparams(sem):
    return pltpu.CompilerParams(dimension_semantics=sem, vmem_limit_bytes=VMEM_LIMIT)


def _rmsnorm_kernel(x_ref, g_ref, o_ref):
    x = x_ref[...]
    ms = jnp.mean(x * x, axis=-1, keepdims=True)
    o_ref[...] = (x * lax.rsqrt(ms + NORM_EPS) * g_ref[...]).astype(o_ref.dtype)


def _rmsnorm(x2d, gain, out_dtype, tm=512):
    T, D = x2d.shape
    return pl.pallas_call(
        _rmsnorm_kernel,
        grid=(T // tm,),
        in_specs=[pl.BlockSpec((tm, D), lambda i: (i, 0)), pl.BlockSpec((1, D), lambda i: (0, 0))],
        out_specs=pl.BlockSpec((tm, D), lambda i: (i, 0)),
        out_shape=jax.ShapeDtypeStruct((T, D), out_dtype),
        compiler_params=_cparams(("parallel",)),
        name="rmsnorm",
    )(x2d, gain.reshape(1, D).astype(F32))


def _inproj_kernel(h_ref, w_ref, lb_ref, o_ref, *, mode, head_major):
    acc = jnp.dot(h_ref[...], w_ref[...], preferred_element_type=F32)
    if mode == "logf":
        lb = lb_ref[...]
        acc = jnp.log(lb + (1.0 - lb) * jax.nn.sigmoid(acc))
    elif mode == "sigmoid":
        acc = jax.nn.sigmoid(acc)
    if head_major:
        for hh in range(acc.shape[1] // HEAD_DIM):
            o_ref[0, hh] = acc[:, hh * HEAD_DIM:(hh + 1) * HEAD_DIM].astype(o_ref.dtype)
    else:
        o_ref[...] = acc.astype(o_ref.dtype)


def _inproj(h, w, lb, *, mode, head_major, out_dtype, B, S, tm=512, tn=1024):
    T, D = h.shape
    N = w.shape[1]
    assert N % tn == 0 and T % tm == 0 and S % tm == 0
    spb = S // tm
    if head_major:
        out_shape = jax.ShapeDtypeStruct((B, N // HEAD_DIM, S, HEAD_DIM), out_dtype)
        out_spec = pl.BlockSpec((1, tn // HEAD_DIM, tm, HEAD_DIM), lambda j, i: (i // spb, j, i % spb, 0))
    else:
        out_shape = jax.ShapeDtypeStruct((T, N), out_dtype)
        out_spec = pl.BlockSpec((tm, tn), lambda j, i: (i, j))
    return pl.pallas_call(
        functools.partial(_inproj_kernel, mode=mode, head_major=head_major),
        grid=(N // tn, T // tm),
        in_specs=[
            pl.BlockSpec((tm, D), lambda j, i: (i, 0)),
            pl.BlockSpec((D, tn), lambda j, i: (0, j)),
            pl.BlockSpec((1, tn), lambda j, i: (0, j)),
        ],
        out_specs=out_spec,
        out_shape=out_shape,
        compiler_params=_cparams(("parallel", "parallel")),
        name="inproj_" + mode,
    )(h, w, lb)


HG_CHUNK = 128
HG_DIAG = 16
HG_HEADS_PER_STEP = 2


def _hgrn_levels(C):
    out, m = [], C // 2
    while m >= HG_DIAG:
        out.append(m)
        m //= 2
    return out


def _hgrn_masks(C):
    t = np.arange(C)[:, None]
    s = np.arange(C)[None, :]
    masks = []
    for m in _hgrn_levels(C):
        masks.append((t // (2 * m) == s // (2 * m)) & ((t // m) % 2 == 1) & ((s // m) % 2 == 0))
    masks.append((t // HG_DIAG == s // HG_DIAG) & (t >= s))
    total = np.sum(np.stack(masks).astype(np.int32), axis=0)
    assert np.array_equal(total, (t >= s).astype(np.int32))
    return np.stack(masks).astype(np.float32)


def _split3(x):
    hi = x.astype(BF16)
    r1 = x - hi.astype(F32)
    mid = r1.astype(BF16)
    lo = (r1 - mid.astype(F32)).astype(BF16)
    return hi, mid, lo


def _dot_nt(a, b):
    return lax.dot_general(a, b, (((1,), (1,)), ((), ())), preferred_element_type=F32)


def _dot_tn(a, b):
    return lax.dot_general(a, b, (((0,), (0,)), ((), ())), preferred_element_type=F32)


def _hgrn_kernel(q_ref, lf_ref, v_ref, sg_ref, gain_ref, tril_ref, mask_ref, o_ref, st_ref, b_ref, *, C):
    S = q_ref.shape[2]
    HP = q_ref.shape[1]
    levels = _hgrn_levels(C)
    st_ref[...] = jnp.zeros_like(st_ref)
    tril = tril_ref[...]

    def ref_rows(h, block, row_of_block):
        parts = [jnp.broadcast_to(b_ref[h, pl.ds(row_of_block(p), 1), :], (block, HEAD_DIM))
                 for p in range(C // block)]
        return jnp.concatenate(parts, axis=0)

    def chunk(c, carry):
        r0 = pl.multiple_of(c * C, C)
        for h in range(HP):
            lf = lf_ref[0, h, pl.ds(r0, C), :]
            q = q_ref[0, h, pl.ds(r0, C), :].astype(F32)
            v = v_ref[0, h, pl.ds(r0, C), :]
            kk = 1.0 - jnp.exp(lf)
            hi, mid, lo = _split3(lf)
            b = (jnp.dot(tril, hi, preferred_element_type=F32)
                 + jnp.dot(tril, mid, preferred_element_type=F32)
                 + jnp.dot(tril, lo, preferred_element_type=F32))
            b_ref[h] = b
            b_last = b_ref[h, pl.ds(C - 1, 1), :]

            scores = jnp.zeros((C, C), F32)
            for li, m in enumerate(levels):
                d = b - ref_rows(h, 2 * m, lambda p, m=m: 2 * m * p + m - 1)
                a_l = (q * jnp.exp(jnp.minimum(d, 0.0))).astype(BF16)
                b_l = (kk * jnp.exp(jnp.minimum(-d, 0.0))).astype(BF16)
                scores = scores + mask_ref[li] * _dot_nt(a_l, b_l)
            d = b - ref_rows(h, HG_DIAG, lambda p: HG_DIAG * p + HG_DIAG // 2 - 1)
            a_l = (q * jnp.exp(d)).astype(BF16)
            b_l = (kk * jnp.exp(-d)).astype(BF16)
            scores = scores + mask_ref[len(levels)] * _dot_nt(a_l, b_l)

            st = st_ref[h]
            qe = (q * jnp.exp(b)).astype(BF16)
            o = _dot_nt(qe, st.astype(BF16)) + jnp.dot(scores.astype(BF16), v, preferred_element_type=F32)
            kd = (kk * jnp.exp(b_last - b)).astype(BF16)
            st_ref[h] = st * jnp.exp(b_last) + _dot_tn(v, kd)

            ms = jnp.mean(o * o, axis=-1, keepdims=True)
            y = o * lax.rsqrt(ms + NORM_EPS) * gain_ref[:, h * HEAD_DIM:(h + 1) * HEAD_DIM]
            y = y * sg_ref[0, h, pl.ds(r0, C), :].astype(F32)
            o_ref[0, pl.ds(r0, C), h * HEAD_DIM:(h + 1) * HEAD_DIM] = y.astype(o_ref.dtype)
        return carry

    lax.fori_loop(0, S // C, chunk, 0)


def _hgrn(q, lf, v, sg, gain):
    B, H, S, E = q.shape
    C, HP = HG_CHUNK, HG_HEADS_PER_STEP
    masks = jnp.asarray(_hgrn_masks(C))
    tril = jnp.asarray(np.tril(np.ones((C, C), np.float32)), dtype=BF16)
    head_spec = pl.BlockSpec((1, HP, S, E), lambda b, h: (b, h, 0, 0))
    return pl.pallas_call(
        functools.partial(_hgrn_kernel, C=C),
        grid=(B, H // HP),
        in_specs=[
            head_spec, head_spec, head_spec, head_spec,
            pl.BlockSpec((1, HP * E), lambda b, h: (0, h)),
            pl.BlockSpec((C, C), lambda b, h: (0, 0)),
            pl.BlockSpec(masks.shape, lambda b, h: (0, 0, 0)),
        ],
        out_specs=pl.BlockSpec((1, S, HP * E), lambda b, h: (b, 0, h)),
        out_shape=jax.ShapeDtypeStruct((B, S, H * E), BF16),
        scratch_shapes=[pltpu.VMEM((HP, E, E), F32), pltpu.VMEM((HP, C, E), F32)],
        compiler_params=_cparams(("parallel", "parallel")),
        name="hgrn2",
    )(q, lf, v, sg, gain.reshape(1, H * E).astype(F32), tril, masks)


def _t5_bucket_np(dist):
    exact = REL_BUCKETS // 2
    d_f = np.maximum(dist, 1).astype(np.float32)
    log_b = exact + (np.log(d_f / np.float32(exact)) / np.float32(math.log(REL_MAX_DIST / exact))
                     * np.float32(REL_BUCKETS - exact)).astype(np.int32)
    return np.where(dist < exact, dist, np.minimum(log_b, REL_BUCKETS - 1))


def _attn_bias(rel_bias):
    i = np.arange(ATT_BLOCK)[:, None]
    c = np.arange(2 * ATT_BLOCK)[None, :]
    delta = i + ATT_BLOCK - c
    out = []
    for gi, (window, dilation) in enumerate(ATT_GROUPS):
        valid = (delta >= 0) & (delta <= window // dilation)
        bucket = _t5_bucket_np(np.maximum(delta, 0) * dilation)
        hs = slice(gi * ATT_HEADS_PER_GROUP, (gi + 1) * ATT_HEADS_PER_GROUP)
        bias = jnp.transpose(rel_bias[:, hs][bucket], (2, 0, 1)).astype(F32)
        out.append(jnp.where(valid[None], bias, NEG_BIG))
    return jnp.stack(out, axis=0)


def _attn_kernel(q_ref, k_ref, v_ref, bias_ref, o_ref, qf, kf, vf, og, lg):
    S = q_ref.shape[3]
    scale = HEAD_DIM ** -0.5
    blk = ATT_BLOCK

    for g, (window, d) in enumerate(ATT_GROUPS):
        L = S // d
        nb = L // blk
        qf[...] = q_ref[0, g, 0].astype(F32)
        kf[...] = k_ref[0, g, 0].astype(F32)
        vf[...] = v_ref[0, g, 0].astype(F32)

        def block(r, n, g=g, d=d):
            start = r + n * blk * d
            rows = pl.ds(start, blk, stride=d) if d > 1 else pl.ds(start, blk)
            q = qf[rows, :].astype(BF16)
            if n == 0:
                k = kf[rows, :].astype(BF16)
                v = vf[rows, :].astype(BF16)
                s = _dot_nt(q, k) * scale + bias_ref[g, 0, :, blk:]
            else:
                first = start - blk * d
                rows2 = pl.ds(first, 2 * blk, stride=d) if d > 1 else pl.ds(first, 2 * blk)
                k = kf[rows2, :].astype(BF16)
                v = vf[rows2, :].astype(BF16)
                s = _dot_nt(q, k) * scale + bias_ref[g, 0]
            m = jnp.max(s, axis=-1, keepdims=True)
            p = jnp.exp(s - m)
            den = jnp.sum(p, axis=-1, keepdims=True)
            o = jnp.dot(p.astype(BF16), v, preferred_element_type=F32) / den
            og[g, rows, :] = o
            lg[g, rows, :] = jnp.broadcast_to(m + jnp.log(den), (blk, HEAD_DIM))

        if d == 1:
            for n in range(nb):
                block(0, n)
        else:
            def residue(r, carry, nb=nb, block=block):
                for n in range(nb):
                    block(r, n)
                return carry
            lax.fori_loop(0, d, residue, 0)

    rows_per_step = 256

    def mix(i, carry):
        rows = pl.ds(pl.multiple_of(i * rows_per_step, rows_per_step), rows_per_step)
        l0, l1, l2 = lg[0, rows, :], lg[1, rows, :], lg[2, rows, :]
        mx = jnp.maximum(jnp.maximum(l0, l1), l2)
        e0, e1, e2 = jnp.exp(l0 - mx), jnp.exp(l1 - mx), jnp.exp(l2 - mx)
        num = e0 * og[0, rows, :] + e1 * og[1, rows, :] + e2 * og[2, rows, :]
        o_ref[0, rows, :] = (num / (e0 + e1 + e2)).astype(o_ref.dtype)
        return carry

    lax.fori_loop(0, S // rows_per_step, mix, 0)


def _attention(qkv, bias):
    B, _, S, E = qkv.shape
    G, HG = len(ATT_GROUPS), ATT_HEADS_PER_GROUP
    x = qkv.reshape(B, 3, G, HG, S, E)

    def spec(which):
        return pl.BlockSpec((1, None, G, 1, S, E), lambda b, h, which=which: (b, which, 0, h, 0, 0))

    def kern(q_ref, k_ref, v_ref, bias_ref, o_ref, *scratch):
        _attn_kernel(q_ref, k_ref, v_ref, bias_ref, o_ref, *scratch)

    return pl.pallas_call(
        kern,
        grid=(B, HG),
        in_specs=[spec(0), spec(1), spec(2),
                  pl.BlockSpec((G, 1, ATT_BLOCK, 2 * ATT_BLOCK), lambda b, h: (0, h, 0, 0))],
        out_specs=pl.BlockSpec((1, S, E), lambda b, h: (b, 0, h)),
        out_shape=jax.ShapeDtypeStruct((B, S, HG * E), BF16),
        scratch_shapes=[pltpu.VMEM((S, E), F32)] * 3 + [pltpu.VMEM((G, S, E), F32)] * 2,
        compiler_params=_cparams(("parallel", "parallel")),
        name="dilated_attn",
    )(x, x, x, bias)


def _merge_kernel(ya_ref, yb_ref, wa_ref, wb_ref, ga_ref, gb_ref, o_ref):
    a = jnp.dot(ya_ref[...], wa_ref[...], preferred_element_type=F32)
    b = jnp.dot(yb_ref[...], wb_ref[...], preferred_element_type=F32)
    o_ref[...] = (ga_ref[...].astype(F32) * a + gb_ref[...].astype(F32) * b).astype(o_ref.dtype)


def _merge(ya, yb, wa, wb, gates, tm=512, tn=1024):
    T, Ka = ya.shape
    Kb = yb.shape[1]
    N = wa.shape[1]
    nj = N // tn
    return pl.pallas_call(
        _merge_kernel,
        grid=(nj, T // tm),
        in_specs=[
            pl.BlockSpec((tm, Ka), lambda j, i: (i, 0)),
            pl.BlockSpec((tm, Kb), lambda j, i: (i, 0)),
            pl.BlockSpec((Ka, tn), lambda j, i: (0, j)),
            pl.BlockSpec((Kb, tn), lambda j, i: (0, j)),
            pl.BlockSpec((tm, tn), lambda j, i: (i, j)),
            pl.BlockSpec((tm, tn), lambda j, i, nj=nj: (i, j + nj)),
        ],
        out_specs=pl.BlockSpec((tm, tn), lambda j, i: (i, j)),
        out_shape=jax.ShapeDtypeStruct((T, N), BF16),
        compiler_params=_cparams(("parallel", "parallel")),
        name="branch_merge",
    )(ya, yb, wa, wb, gates, gates)


ROUTE_LANES = LANES


def _outproj_kernel(m_ref, w_ref, x_ref, g_ref, wrh_ref, wrl_ref, br_ref, x1_ref, h2_ref, rt_ref):
    x1 = x_ref[...] + jnp.dot(m_ref[...], w_ref[...], preferred_element_type=F32)
    x1_ref[...] = x1
    ms = jnp.mean(x1 * x1, axis=-1, keepdims=True)
    h2 = x1 * lax.rsqrt(ms + NORM_EPS) * g_ref[...]
    h2_ref[...] = h2
    h_hi = h2.astype(BF16)
    h_lo = (h2 - h_hi.astype(F32)).astype(BF16)
    wrh = wrh_ref[...]
    lg = (jnp.dot(h_hi, wrh, preferred_element_type=F32) + jnp.dot(h_lo, wrh, preferred_element_type=F32)
          + jnp.dot(h_hi, wrl_ref[...], preferred_element_type=F32)) + br_ref[...]

    lane = lax.broadcasted_iota(jnp.int32, lg.shape, 1)
    lane_f = lane.astype(F32)
    big = float(ROUTE_LANES)
    is_group = (lane >= MOE_EXPERTS) & (lane < MOE_EXPERTS + MOE_GROUPS)
    lgg = jnp.where(is_group, lg, -jnp.inf)
    gmax = jnp.max(lgg, axis=-1, keepdims=True)
    gsel = jnp.min(jnp.where(lgg == gmax, lane_f - MOE_EXPERTS, big), axis=-1, keepdims=True)
    pg = 1.0 / jnp.sum(jnp.where(is_group, jnp.exp(lg - gmax), 0.0), axis=-1, keepdims=True)

    in_group = (lane < MOE_EXPERTS) & ((lane // MOE_EXPERTS_PER_GROUP).astype(F32) == gsel)
    le = jnp.where(in_group, lg, -jnp.inf)
    t1 = jnp.max(le, axis=-1, keepdims=True)
    i1 = jnp.min(jnp.where(le == t1, lane_f, big), axis=-1, keepdims=True)
    le2 = jnp.where(lane_f == i1, -jnp.inf, le)
    t2 = jnp.max(le2, axis=-1, keepdims=True)
    i2 = jnp.min(jnp.where(le2 == t2, lane_f, big), axis=-1, keepdims=True)
    e2 = jnp.exp(t2 - t1)
    w1 = pg / (1.0 + e2)
    w2 = pg * e2 / (1.0 + e2)
    rt_ref[...] = jnp.where(lane == 0, i1, jnp.where(lane == 1, i2, jnp.where(lane == 2, w1,
                            jnp.where(lane == 3, w2, 0.0))))


def _outproj(merged, w_out, x2d, gain, wr_hi, wr_lo, br, tm=256):
    T, D = x2d.shape
    row = lambda i: (i, 0)
    const = lambda i: (0, 0)
    return pl.pallas_call(
        _outproj_kernel,
        grid=(T // tm,),
        in_specs=[
            pl.BlockSpec((tm, D), row), pl.BlockSpec((D, D), const), pl.BlockSpec((tm, D), row),
            pl.BlockSpec((1, D), const), pl.BlockSpec((D, ROUTE_LANES), const),
            pl.BlockSpec((D, ROUTE_LANES), const), pl.BlockSpec((1, ROUTE_LANES), const),
        ],
        out_specs=[pl.BlockSpec((tm, D), row), pl.BlockSpec((tm, D), row), pl.BlockSpec((tm, ROUTE_LANES), row)],
        out_shape=[jax.ShapeDtypeStruct((T, D), F32), jax.ShapeDtypeStruct((T, D), F32),
                   jax.ShapeDtypeStruct((T, ROUTE_LANES), F32)],
        compiler_params=_cparams(("parallel",)),
        name="outproj_router",
    )(merged, w_out, x2d, gain.reshape(1, D).astype(F32), wr_hi, wr_lo, br)


def _dispatch_kernel(dest_ref, h_ref, xs_in_ref, xs_ref, sem, *, tq):
    del xs_in_ref
    base = pl.program_id(0) * tq

    def copy(j, k):
        return pltpu.make_async_copy(h_ref.at[pl.ds(j, 1)], xs_ref.at[pl.ds(dest_ref[2 * (base + j) + k], 1)], sem)

    def issue(j, carry):
        copy(j, 0).start()
        copy(j, 1).start()
        return carry

    def drain(j, carry):
        copy(j, 0).wait()
        copy(j, 1).wait()
        return carry

    lax.fori_loop(0, tq, issue, 0)
    lax.fori_loop(0, tq, drain, 0)


def _dispatch(h2, dest_flat, n_rows, tq=512):
    T, D = h2.shape
    zeros = jnp.zeros((n_rows, D), h2.dtype)
    return pl.pallas_call(
        functools.partial(_dispatch_kernel, tq=tq),
        grid_spec=pltpu.PrefetchScalarGridSpec(
            num_scalar_prefetch=1,
            grid=(T // tq,),
            in_specs=[pl.BlockSpec((tq, D), lambda i, d: (i, 0)), pl.BlockSpec(memory_space=pl.ANY)],
            out_specs=pl.BlockSpec(memory_space=pl.ANY),
            scratch_shapes=[pltpu.SemaphoreType.DMA(())],
        ),
        out_shape=jax.ShapeDtypeStruct((n_rows, D), h2.dtype),
        input_output_aliases={2: 0},
        compiler_params=_cparams(("arbitrary",)),
        name="moe_dispatch",
    )(dest_flat, h2, zeros)


def _expert_kernel(be_ref, nu_ref, xs_ref, wg_ref, wu_ref, wd_ref, o_ref, wg_s, wu_s, wd_s):
    i = pl.program_id(0)
    prev = be_ref[jnp.maximum(i - 1, 0)]
    new_expert = (i == 0) | (be_ref[i] != prev)

    @pl.when(new_expert)
    def _():
        wg_s[...] = wg_ref[0].astype(BF16)
        wu_s[...] = wu_ref[0].astype(BF16)
        wd_s[...] = wd_ref[0].astype(BF16)

    @pl.when(i < nu_ref[0])
    def _():
        x = xs_ref[...].astype(BF16)
        g = jnp.dot(x, wg_s[...], preferred_element_type=F32)
        u = jnp.dot(x, wu_s[...], preferred_element_type=F32)
        a = (g * jax.nn.sigmoid(g) * u).astype(BF16)
        o_ref[...] = jnp.dot(a, wd_s[...], preferred_element_type=F32)

    @pl.when(i >= nu_ref[0])
    def _():
        o_ref[...] = jnp.zeros_like(o_ref)


def _experts(xs, block_expert, n_used, w_gate, w_up, w_down):
    P, D = xs.shape
    F = w_gate.shape[2]
    nblk = P // MOE_ROWS
    rows = lambda i, be, nu: (jnp.minimum(i, nu[0] - 1), 0)
    return pl.pallas_call(
        _expert_kernel,
        grid_spec=pltpu.PrefetchScalarGridSpec(
            num_scalar_prefetch=2,
            grid=(nblk,),
            in_specs=[
                pl.BlockSpec((MOE_ROWS, D), rows),
                pl.BlockSpec((1, D, F), lambda i, be, nu: (be[i], 0, 0)),
                pl.BlockSpec((1, D, F), lambda i, be, nu: (be[i], 0, 0)),
                pl.BlockSpec((1, F, D), lambda i, be, nu: (be[i], 0, 0)),
            ],
            out_specs=pl.BlockSpec((MOE_ROWS, D), lambda i, be, nu: (i, 0)),
            scratch_shapes=[pltpu.VMEM((D, F), BF16), pltpu.VMEM((D, F), BF16), pltpu.VMEM((F, D), BF16)],
        ),
        out_shape=jax.ShapeDtypeStruct((P, D), F32),
        compiler_params=_cparams(("arbitrary",)),
        name="moe_experts",
    )(block_expert, n_used, xs, w_gate, w_up, w_down)


def _combine_kernel(dest_ref, x1_ref, rt_ref, g_ref, ys_ref, o_ref, buf, sem, *, tq):
    base = pl.program_id(0) * tq

    def copy(j, k):
        return pltpu.make_async_copy(ys_ref.at[pl.ds(dest_ref[2 * (base + j) + k], 1)],
                                     buf.at[k, pl.ds(j, 1)], sem)

    def issue(j, carry):
        copy(j, 0).start()
        copy(j, 1).start()
        return carry

    def drain(j, carry):
        copy(j, 0).wait()
        copy(j, 1).wait()
        return carry

    lax.fori_loop(0, tq, issue, 0)
    lax.fori_loop(0, tq, drain, 0)
    rt = rt_ref[...]
    x = x1_ref[...] + rt[:, 2:3] * buf[0] + rt[:, 3:4] * buf[1]
    ms = jnp.mean(x * x, axis=-1, keepdims=True)
    o_ref[...] = x * lax.rsqrt(ms + NORM_EPS) * g_ref[...]


def _combine(x1, route, gain, ys, dest_flat, tq=256):
    T, D = x1.shape
    return pl.pallas_call(
        functools.partial(_combine_kernel, tq=tq),
        grid_spec=pltpu.PrefetchScalarGridSpec(
            num_scalar_prefetch=1,
            grid=(T // tq,),
            in_specs=[
                pl.BlockSpec((tq, D), lambda i, d: (i, 0)),
                pl.BlockSpec((tq, ROUTE_LANES), lambda i, d: (i, 0)),
                pl.BlockSpec((1, D), lambda i, d: (0, 0)),
                pl.BlockSpec(memory_space=pl.ANY),
            ],
            out_specs=pl.BlockSpec((tq, D), lambda i, d: (i, 0)),
            scratch_shapes=[pltpu.VMEM((MOE_TOPK, tq, D), F32), pltpu.SemaphoreType.DMA(())],
        ),
        out_shape=jax.ShapeDtypeStruct((T, D), F32),
        compiler_params=_cparams(("arbitrary",)),
        name="moe_combine",
    )(dest_flat, x1, route, gain.reshape(1, D).astype(F32), ys)


def _route_metadata(route, T):
    e = route[:, :MOE_TOPK].astype(jnp.int32)
    ids = jnp.arange(MOE_EXPERTS, dtype=jnp.int32)
    oh0 = e[:, 0, None] == ids
    oh1 = e[:, 1, None] == ids
    onehot = (oh0 | oh1).astype(jnp.int32)
    incl = jnp.cumsum(onehot, axis=0)
    counts = incl[-1]
    pcounts = (counts + MOE_ROWS - 1) // MOE_ROWS * MOE_ROWS
    pends = jnp.cumsum(pcounts)
    poffs = pends - pcounts
    slot = incl - onehot + poffs[None, :]
    dest = jnp.stack([jnp.sum(jnp.where(oh0, slot, 0), axis=1), jnp.sum(jnp.where(oh1, slot, 0), axis=1)], axis=1)
    nblk = (T * MOE_TOPK) // MOE_ROWS + MOE_EXPERTS
    block_expert = jnp.minimum(
        jnp.searchsorted(pends, jnp.arange(nblk, dtype=jnp.int32) * MOE_ROWS, side="right"),
        MOE_EXPERTS - 1).astype(jnp.int32)
    n_used = (pends[-1:] // MOE_ROWS).astype(jnp.int32)
    return dest.reshape(-1).astype(jnp.int32), block_expert, n_used, nblk


def kernel(x, norm1_gain, w_in, hg_lb_logits, hg_norm_gain, rel_bias, w_branch_a, w_branch_b, w_out,
           norm2_gain, w_router_group, b_router_group, w_router_expert, b_router_expert,
           w_exp_gate, w_exp_up, w_exp_down, final_norm_gain):
    B, S, D = x.shape
    T = B * S
    depth = norm1_gain.shape[0]
    lower_bounds = jnp.cumsum(jax.nn.softmax(hg_lb_logits.astype(F32), axis=0), axis=0)
    att_w = ATT_HEADS * HEAD_DIM
    bias = _attn_bias(rel_bias)
    x2d = x.reshape(T, D)
    for layer in range(depth):
        w = w_in[layer]
        h = _rmsnorm(x2d, norm1_gain[layer], BF16)
        zeros_lb = jnp.zeros((1, D), F32)
        seg = lambda a, n: w[:, a:a + n].astype(BF16)
        hm = dict(head_major=True, B=B, S=S)
        q_a = _inproj(h, seg(0, D), zeros_lb, mode="none", out_dtype=BF16, **hm)
        lf_a = _inproj(h, seg(D, D), lower_bounds[layer].reshape(1, D), mode="logf", out_dtype=F32, **hm)
        i_a = _inproj(h, seg(2 * D, D), zeros_lb, mode="none", out_dtype=BF16, **hm)
        sg_a = _inproj(h, seg(3 * D, D), zeros_lb, mode="sigmoid", out_dtype=BF16, **hm)
        qkv_b = _inproj(h, seg(4 * D, 3 * att_w), jnp.zeros((1, 3 * att_w), F32), mode="none",
                        out_dtype=BF16, tn=att_w, **hm)
        gates = _inproj(h, seg(4 * D + 3 * att_w, 2 * D), jnp.zeros((1, 2 * D), F32), mode="sigmoid",
                        head_major=False, out_dtype=BF16, B=B, S=S)

        y_a = _hgrn(q_a, lf_a, i_a, sg_a, hg_norm_gain[layer]).reshape(T, D)
        y_b = _attention(qkv_b, bias).reshape(T, ATT_HEADS_PER_GROUP * HEAD_DIM)
        merged = _merge(y_a, y_b, w_branch_a[layer].astype(BF16), w_branch_b[layer].astype(BF16), gates)

        wr = jnp.zeros((D, ROUTE_LANES), F32)
        wr = wr.at[:, :MOE_EXPERTS].set(w_router_expert[layer].astype(F32))
        wr = wr.at[:, MOE_EXPERTS:MOE_EXPERTS + MOE_GROUPS].set(w_router_group[layer].astype(F32))
        br = jnp.zeros((1, ROUTE_LANES), F32)
        br = br.at[0, :MOE_EXPERTS].set(b_router_expert[layer].astype(F32))
        br = br.at[0, MOE_EXPERTS:MOE_EXPERTS + MOE_GROUPS].set(b_router_group[layer].astype(F32))
        wr_hi = wr.astype(BF16)
        wr_lo = (wr - wr_hi.astype(F32)).astype(BF16)
        x1, h2, route = _outproj(merged, w_out[layer].astype(BF16), x2d, norm2_gain[layer], wr_hi, wr_lo, br)

        dest, block_expert, n_used, nblk = _route_metadata(route, T)
        xs = _dispatch(h2, dest, nblk * MOE_ROWS)
        ys = _experts(xs, block_expert, n_used, w_exp_gate[layer], w_exp_up[layer], w_exp_down[layer])
        last = layer == depth - 1
        assert last, "the fused combine applies the final norm; deeper stacks need an un-normalised combine"
        x2d = _combine(x1, route, final_norm_gain, ys, dest)
    return x2d.reshape(B, S, D)
```

```python
import functools
import math

import numpy as np
import jax
import jax.numpy as jnp
from jax import lax
from jax.experimental import pallas as pl
from jax.experimental.pallas import tpu as pltpu

F32 = jnp.float32
BF16 = jnp.bfloat16

LANES = 128
NORM_EPS = 1e-6
HEAD_DIM = 128
HG_HEADS = 16
ATT_GROUPS = ((128, 1), (512, 4), (2048, 16))
ATT_HEADS_PER_GROUP = 4
ATT_HEADS = len(ATT_GROUPS) * ATT_HEADS_PER_GROUP
ATT_BLOCK = 128
REL_BUCKETS = 32
REL_MAX_DIST = 2048
MOE_GROUPS = 8
MOE_EXPERTS_PER_GROUP = 8
MOE_EXPERTS = MOE_GROUPS * MOE_EXPERTS_PER_GROUP
MOE_TOPK = 2
MOE_ROWS = 256
NEG_BIG = -1e30
LOG2E = 1.4426950408889634
VMEM_LIMIT = 56 * 1024 * 1024


def _cparams(sem):
    return pltpu.CompilerParams(dimension_semantics=sem, vmem_limit_bytes=VMEM_LIMIT)


def _rmsnorm_kernel(x_ref, g_ref, o_ref):
    x = x_ref[...]
    ms = jnp.mean(x * x, axis=-1, keepdims=True)
    o_ref[...] = (x * lax.rsqrt(ms + NORM_EPS) * g_ref[...]).astype(o_ref.dtype)


def _rmsnorm(x2d, gain, out_dtype, tm=512):
    T, D = x2d.shape
    return pl.pallas_call(
        _rmsnorm_kernel,
        grid=(T // tm,),
        in_specs=[pl.BlockSpec((tm, D), lambda i: (i, 0)), pl.BlockSpec((1, D), lambda i: (0, 0))],
        out_specs=pl.BlockSpec((tm, D), lambda i: (i, 0)),
        out_shape=jax.ShapeDtypeStruct((T, D), out_dtype),
        compiler_params=_cparams(("parallel",)),
        name="rmsnorm",
    )(x2d, gain.reshape(1, D).astype(F32))


INPROJ_COLS = 256


def _inproj_kernel(h_ref, w_ref, lb_ref, o_ref, w_bf, *, mode, head_major):
    @pl.when(pl.program_id(1) == 0)
    def _():
        w_bf[...] = w_ref[...].astype(BF16)

    h = h_ref[...]
    for c0 in range(0, w_bf.shape[1], INPROJ_COLS):
        cols = slice(c0, c0 + INPROJ_COLS)
        acc = jnp.dot(h, w_bf[:, cols], preferred_element_type=F32)
        if mode == "logf":
            lb = lb_ref[:, cols]
            acc = jnp.log(lb + (1.0 - lb) * jax.nn.sigmoid(acc))
        elif mode == "sigmoid":
            acc = jax.nn.sigmoid(acc)
        if head_major:
            for hh in range(INPROJ_COLS // HEAD_DIM):
                o_ref[0, c0 // HEAD_DIM + hh] = acc[:, hh * HEAD_DIM:(hh + 1) * HEAD_DIM].astype(o_ref.dtype)
        else:
            o_ref[:, cols] = acc.astype(o_ref.dtype)


def _inproj(h, w_all, layer, col0, N, lb, *, mode, head_major, out_dtype, B, S, tm=512, tn=1024):
    T, D = h.shape
    assert N % tn == 0 and col0 % tn == 0 and T % tm == 0 and S % tm == 0 and tn % INPROJ_COLS == 0
    spb = S // tm
    j0 = col0 // tn
    if head_major:
        out_shape = jax.ShapeDtypeStruct((B, N // HEAD_DIM, S, HEAD_DIM), out_dtype)
        out_spec = pl.BlockSpec((1, tn // HEAD_DIM, tm, HEAD_DIM), lambda j, i: (i // spb, j, i % spb, 0))
    else:
        out_shape = jax.ShapeDtypeStruct((T, N), out_dtype)
        out_spec = pl.BlockSpec((tm, tn), lambda j, i: (i, j))
    return pl.pallas_call(
        functools.partial(_inproj_kernel, mode=mode, head_major=head_major),
        grid=(N // tn, T // tm),
        in_specs=[
            pl.BlockSpec((tm, D), lambda j, i: (i, 0)),
            pl.BlockSpec((None, D, tn), lambda j, i: (layer, 0, j0 + j)),
            pl.BlockSpec((1, tn), lambda j, i: (0, j)),
        ],
        out_specs=out_spec,
        out_shape=out_shape,
        scratch_shapes=[pltpu.VMEM((D, tn), BF16)],
        compiler_params=_cparams(("parallel", "arbitrary")),
        name="inproj_" + mode,
    )(h, w_all, lb)


HG_CHUNK = 256
HG_DIAG = 16
HG_HEADS_PER_STEP = 4


def _hgrn_levels(C):
    out, m = [], C // 2
    while m >= HG_DIAG:
        out.append(m)
        m //= 2
    return out


def _hgrn_masks(C):
    t = np.arange(C)[:, None]
    s = np.arange(C)[None, :]
    masks = []
    for m in _hgrn_levels(C):
        masks.append((t // (2 * m) == s // (2 * m)) & ((t // m) % 2 == 1) & ((s // m) % 2 == 0))
    masks.append((t // HG_DIAG == s // HG_DIAG) & (t >= s))
    total = np.sum(np.stack(masks).astype(np.int32), axis=0)
    assert np.array_equal(total, (t >= s).astype(np.int32))
    return np.stack(masks).astype(np.float32)


def _split3(x):
    hi = x.astype(BF16)
    r1 = x - hi.astype(F32)
    mid = r1.astype(BF16)
    lo = (r1 - mid.astype(F32)).astype(BF16)
    return hi, mid, lo


def _dot_nt(a, b):
    return lax.dot_general(a, b, (((1,), (1,)), ((), ())), preferred_element_type=F32)


def _dot_tn(a, b):
    return lax.dot_general(a, b, (((0,), (0,)), ((), ())), preferred_element_type=F32)


def _hgrn_kernel(q_ref, lf_ref, v_ref, sg_ref, gain_ref, tril_ref, mask_ref, o_ref,
                 st_ref, b_ref, oi_ref, qe_ref, kd_ref, *, C):
    S = q_ref.shape[2]
    HP = q_ref.shape[1]
    levels = _hgrn_levels(C)
    nchunks = S // C
    tril = tril_ref[...]

    def ref_rows(h, r0, block, row_of_block):
        parts = [jnp.broadcast_to(b_ref[h, pl.ds(r0 + row_of_block(p), 1), :], (block, HEAD_DIM))
                 for p in range(C // block)]
        return jnp.concatenate(parts, axis=0)

    def cumsum_chunk(c, carry):
        r0 = pl.multiple_of(c * C, C)
        for h in range(HP):
            hi, mid, lo = _split3(lf_ref[0, h, pl.ds(r0, C), :])
            r = jnp.dot(tril, jnp.concatenate([hi, mid, lo], axis=1), preferred_element_type=F32)
            b_ref[h, pl.ds(r0, C), :] = r[:, :HEAD_DIM] + r[:, HEAD_DIM:2 * HEAD_DIM] + r[:, 2 * HEAD_DIM:]
        return carry

    lax.fori_loop(0, nchunks, cumsum_chunk, 0)

    def intra_chunk(c, carry):
        r0 = pl.multiple_of(c * C, C)
        for h in range(HP):
            rows = pl.ds(r0, C)
            q = q_ref[0, h, rows, :].astype(F32)
            kk = 1.0 - jnp.exp(lf_ref[0, h, rows, :])
            b = b_ref[h, rows, :]
            b_last = b_ref[h, pl.ds(r0 + C - 1, 1), :]
            scores = jnp.zeros((C, C), F32)
            for li, m in enumerate(levels):
                d = b - ref_rows(h, r0, 2 * m, lambda p, m=m: 2 * m * p + m - 1)
                e = jnp.exp2(jnp.abs(d) * (-LOG2E))
                scores = scores + mask_ref[li] * _dot_nt((q * e).astype(BF16), (kk * e).astype(BF16))
            d = b - ref_rows(h, r0, HG_DIAG, lambda p: HG_DIAG * p + HG_DIAG // 2 - 1)
            scores = scores + mask_ref[len(levels)] * _dot_nt((q * jnp.exp(d)).astype(BF16),
                                                               (kk * jnp.exp(-d)).astype(BF16))
            oi_ref[h, rows, :] = jnp.dot(scores.astype(BF16), v_ref[0, h, rows, :], preferred_element_type=F32)
            qe_ref[h, rows, :] = (q * jnp.exp(b)).astype(BF16)
            kd_ref[h, rows, :] = (kk * jnp.exp(b_last - b)).astype(BF16)
        return carry

    lax.fori_loop(0, nchunks, intra_chunk, 0)

    st_ref[...] = jnp.zeros_like(st_ref)

    def recur_chunk(c, carry):
        r0 = pl.multiple_of(c * C, C)
        for h in range(HP):
            rows = pl.ds(r0, C)
            st = st_ref[h]
            o = oi_ref[h, rows, :] + _dot_nt(qe_ref[h, rows, :], st.astype(BF16))
            b_last = b_ref[h, pl.ds(r0 + C - 1, 1), :]
            st_ref[h] = st * jnp.exp(b_last) + _dot_tn(v_ref[0, h, rows, :], kd_ref[h, rows, :])
            ms = jnp.mean(o * o, axis=-1, keepdims=True)
            y = o * lax.rsqrt(ms + NORM_EPS) * gain_ref[:, h * HEAD_DIM:(h + 1) * HEAD_DIM]
            y = y * sg_ref[0, h, rows, :].astype(F32)
            o_ref[0, rows, h * HEAD_DIM:(h + 1) * HEAD_DIM] = y.astype(o_ref.dtype)
        return carry

    lax.fori_loop(0, nchunks, recur_chunk, 0)


def _hgrn(q, lf, v, sg, gain):
    B, H, S, E = q.shape
    C, HP = HG_CHUNK, HG_HEADS_PER_STEP
    masks = jnp.asarray(_hgrn_masks(C))
    tril = jnp.asarray(np.tril(np.ones((C, C), np.float32)), dtype=BF16)
    head_spec = pl.BlockSpec((1, HP, S, E), lambda b, h: (b, h, 0, 0))
    return pl.pallas_call(
        functools.partial(_hgrn_kernel, C=C),
        grid=(B, H // HP),
        in_specs=[
            head_spec, head_spec, head_spec, head_spec,
            pl.BlockSpec((1, HP * E), lambda b, h: (0, h)),
            pl.BlockSpec((C, C), lambda b, h: (0, 0)),
            pl.BlockSpec(masks.shape, lambda b, h: (0, 0, 0)),
        ],
        out_specs=pl.BlockSpec((1, S, HP * E), lambda b, h: (b, 0, h)),
        out_shape=jax.ShapeDtypeStruct((B, S, H * E), BF16),
        scratch_shapes=[pltpu.VMEM((HP, E, E), F32), pltpu.VMEM((HP, S, E), F32), pltpu.VMEM((HP, S, E), F32),
                        pltpu.VMEM((HP, S, E), BF16), pltpu.VMEM((HP, S, E), BF16)],
        compiler_params=_cparams(("parallel", "parallel")),
        name="hgrn2",
    )(q, lf, v, sg, gain.reshape(1, H * E).astype(F32), tril, masks)


def _t5_bucket_np(dist):
    exact = REL_BUCKETS // 2
    d_f = np.maximum(dist, 1).astype(np.float32)
    log_b = exact + (np.log(d_f / np.float32(exact)) / np.float32(math.log(REL_MAX_DIST / exact))
                     * np.float32(REL_BUCKETS - exact)).astype(np.int32)
    return np.where(dist < exact, dist, np.minimum(log_b, REL_BUCKETS - 1))


def _attn_bias(rel_bias):
    blk = ATT_BLOCK
    period = 3 * blk
    out = []
    for gi, (window, dilation) in enumerate(ATT_GROUPS):
        n_back = window // dilation
        assert n_back <= blk
        hs = slice(gi * ATT_HEADS_PER_GROUP, (gi + 1) * ATT_HEADS_PER_GROUP)
        bucket = _t5_bucket_np(np.arange(n_back + 1) * dilation)
        by_delta = rel_bias[:, hs][bucket].astype(F32).T
        u = jnp.full((ATT_HEADS_PER_GROUP, period), NEG_BIG, F32)
        u = u.at[:, 2 * blk - 1 - n_back:2 * blk].set(by_delta[:, ::-1])
        flat = jnp.tile(u, (1, blk))[:, :blk * (period - 1)]
        out.append(flat.reshape(ATT_HEADS_PER_GROUP, blk, period - 1)[:, :, blk - 1:3 * blk - 1])
    return jnp.stack(out, axis=0)


def _attn_kernel(q_ref, k_ref, v_ref, bias_ref, o_ref, qf, kf, vf, og, lg):
    S = q_ref.shape[3]
    scale = HEAD_DIM ** -0.5
    blk = ATT_BLOCK

    for g, (window, d) in enumerate(ATT_GROUPS):
        L = S // d
        nb = L // blk
        qf[...] = q_ref[0, g, 0].astype(F32)
        kf[...] = k_ref[0, g, 0].astype(F32)
        vf[...] = v_ref[0, g, 0].astype(F32)

        def block(r, n, g=g, d=d):
            start = r + n * blk * d
            rows = pl.ds(start, blk, stride=d) if d > 1 else pl.ds(start, blk)
            q = qf[rows, :].astype(BF16)
            if n == 0:
                k = kf[rows, :].astype(BF16)
                v = vf[rows, :].astype(BF16)
                s = _dot_nt(q, k) * scale + bias_ref[g, 0, :, blk:]
            else:
                first = start - blk * d
                rows2 = pl.ds(first, 2 * blk, stride=d) if d > 1 else pl.ds(first, 2 * blk)
                k = kf[rows2, :].astype(BF16)
                v = vf[rows2, :].astype(BF16)
                s = _dot_nt(q, k) * scale + bias_ref[g, 0]
            m = jnp.max(s, axis=-1, keepdims=True)
            p = jnp.exp(s - m)
            den = jnp.sum(p, axis=-1, keepdims=True)
            o = jnp.dot(p.astype(BF16), v, preferred_element_type=F32) / den
            og[g, rows, :] = o
            lg[g, rows, :] = jnp.broadcast_to(m + jnp.log(den), (blk, HEAD_DIM))

        if d == 1:
            for n in range(nb):
                block(0, n)
        else:
            def residue(r, carry, nb=nb, block=block):
                for n in range(nb):
                    block(r, n)
                return carry
            lax.fori_loop(0, d, residue, 0)

    rows_per_step = 256

    def mix(i, carry):
        rows = pl.ds(pl.multiple_of(i * rows_per_step, rows_per_step), rows_per_step)
        l0, l1, l2 = lg[0, rows, :], lg[1, rows, :], lg[2, rows, :]
        mx = jnp.maximum(jnp.maximum(l0, l1), l2)
        e0, e1, e2 = jnp.exp(l0 - mx), jnp.exp(l1 - mx), jnp.exp(l2 - mx)
        num = e0 * og[0, rows, :] + e1 * og[1, rows, :] + e2 * og[2, rows, :]
        o_ref[0, rows, :] = (num / (e0 + e1 + e2)).astype(o_ref.dtype)
        return carry

    lax.fori_loop(0, S // rows_per_step, mix, 0)


def _attention(qkv, bias):
    B, _, S, E = qkv.shape
    G, HG = len(ATT_GROUPS), ATT_HEADS_PER_GROUP
    x = qkv.reshape(B, 3, G, HG, S, E)

    def spec(which):
        return pl.BlockSpec((1, None, G, 1, S, E), lambda b, h, which=which: (b, which, 0, h, 0, 0))

    def kern(q_ref, k_ref, v_ref, bias_ref, o_ref, *scratch):
        _attn_kernel(q_ref, k_ref, v_ref, bias_ref, o_ref, *scratch)

    return pl.pallas_call(
        kern,
        grid=(B, HG),
        in_specs=[spec(0), spec(1), spec(2),
                  pl.BlockSpec((G, 1, ATT_BLOCK, 2 * ATT_BLOCK), lambda b, h: (0, h, 0, 0))],
        out_specs=pl.BlockSpec((1, S, E), lambda b, h: (b, 0, h)),
        out_shape=jax.ShapeDtypeStruct((B, S, HG * E), BF16),
        scratch_shapes=[pltpu.VMEM((S, E), F32)] * 3 + [pltpu.VMEM((G, S, E), F32)] * 2,
        compiler_params=_cparams(("parallel", "parallel")),
        name="dilated_attn",
    )(x, x, x, bias)


def _merge_kernel(ya_ref, yb_ref, wa_ref, wb_ref, ga_ref, gb_ref, o_ref):
    a = jnp.dot(ya_ref[...], wa_ref[...], preferred_element_type=F32)
    b = jnp.dot(yb_ref[...], wb_ref[...], preferred_element_type=F32)
    o_ref[...] = (ga_ref[...].astype(F32) * a + gb_ref[...].astype(F32) * b).astype(o_ref.dtype)


def _merge(ya, yb, wa, wb, gates, tm=512, tn=1024):
    T, Ka = ya.shape
    Kb = yb.shape[1]
    N = wa.shape[1]
    nj = N // tn
    return pl.pallas_call(
        _merge_kernel,
        grid=(nj, T // tm),
        in_specs=[
            pl.BlockSpec((tm, Ka), lambda j, i: (i, 0)),
            pl.BlockSpec((tm, Kb), lambda j, i: (i, 0)),
            pl.BlockSpec((Ka, tn), lambda j, i: (0, j)),
            pl.BlockSpec((Kb, tn), lambda j, i: (0, j)),
            pl.BlockSpec((tm, tn), lambda j, i: (i, j)),
            pl.BlockSpec((tm, tn), lambda j, i, nj=nj: (i, j + nj)),
        ],
        out_specs=pl.BlockSpec((tm, tn), lambda j, i: (i, j)),
        out_shape=jax.ShapeDtypeStruct((T, N), BF16),
        compiler_params=_cparams(("parallel", "parallel")),
        name="branch_merge",
    )(ya, yb, wa, wb, gates, gates)


ROUTE_LANES = LANES


def _outproj_kernel(m_ref, w_ref, x_ref, g_ref, wrh_ref, wrl_ref, br_ref, x1_ref, h2_ref, rt_ref):
    x1 = x_ref[...] + jnp.dot(m_ref[...], w_ref[...], preferred_element_type=F32)
    x1_ref[...] = x1
    ms = jnp.mean(x1 * x1, axis=-1, keepdims=True)
    h2 = x1 * lax.rsqrt(ms + NORM_EPS) * g_ref[...]
    h2_ref[...] = h2
    h_hi = h2.astype(BF16)
    h_lo = (h2 - h_hi.astype(F32)).astype(BF16)
    wrh = wrh_ref[...]
    lg = (jnp.dot(h_hi, wrh, preferred_element_type=F32) + jnp.dot(h_lo, wrh, preferred_element_type=F32)
          + jnp.dot(h_hi, wrl_ref[...], preferred_element_type=F32)) + br_ref[...]

    lane = lax.broadcasted_iota(jnp.int32, lg.shape, 1)
    lane_f = lane.astype(F32)
    big = float(ROUTE_LANES)
    is_group = (lane >= MOE_EXPERTS) & (lane < MOE_EXPERTS + MOE_GROUPS)
    lgg = jnp.where(is_group, lg, -jnp.inf)
    gmax = jnp.max(lgg, axis=-1, keepdims=True)
    gsel = jnp.min(jnp.where(lgg == gmax, lane_f - MOE_EXPERTS, big), axis=-1, keepdims=True)
    pg = 1.0 / jnp.sum(jnp.where(is_group, jnp.exp(lg - gmax), 0.0), axis=-1, keepdims=True)

    in_group = (lane < MOE_EXPERTS) & ((lane // MOE_EXPERTS_PER_GROUP).astype(F32) == gsel)
    le = jnp.where(in_group, lg, -jnp.inf)
    t1 = jnp.max(le, axis=-1, keepdims=True)
    i1 = jnp.min(jnp.where(le == t1, lane_f, big), axis=-1, keepdims=True)
    le2 = jnp.where(lane_f == i1, -jnp.inf, le)
    t2 = jnp.max(le2, axis=-1, keepdims=True)
    i2 = jnp.min(jnp.where(le2 == t2, lane_f, big), axis=-1, keepdims=True)
    e2 = jnp.exp(t2 - t1)
    w1 = pg / (1.0 + e2)
    w2 = pg * e2 / (1.0 + e2)
    rt_ref[...] = jnp.where(lane == 0, i1, jnp.where(lane == 1, i2, jnp.where(lane == 2, w1,
                            jnp.where(lane == 3, w2, 0.0))))


def _outproj(merged, w_out, x2d, gain, wr_hi, wr_lo, br, tm=256):
    T, D = x2d.shape
    row = lambda i: (i, 0)
    const = lambda i: (0, 0)
    return pl.pallas_call(
        _outproj_kernel,
        grid=(T // tm,),
        in_specs=[
            pl.BlockSpec((tm, D), row), pl.BlockSpec((D, D), const), pl.BlockSpec((tm, D), row),
            pl.BlockSpec((1, D), const), pl.BlockSpec((D, ROUTE_LANES), const),
            pl.BlockSpec((D, ROUTE_LANES), const), pl.BlockSpec((1, ROUTE_LANES), const),
        ],
        out_specs=[pl.BlockSpec((tm, D), row), pl.BlockSpec((tm, D), row), pl.BlockSpec((tm, ROUTE_LANES), row)],
        out_shape=[jax.ShapeDtypeStruct((T, D), F32), jax.ShapeDtypeStruct((T, D), F32),
                   jax.ShapeDtypeStruct((T, ROUTE_LANES), F32)],
        compiler_params=_cparams(("parallel",)),
        name="outproj_router",
    )(merged, w_out, x2d, gain.reshape(1, D).astype(F32), wr_hi, wr_lo, br)


def _dispatch_kernel(dest_ref, h_ref, xs_in_ref, xs_ref, sem, *, tq):
    del xs_in_ref
    base = pl.program_id(0) * tq

    def copy(j, k):
        return pltpu.make_async_copy(h_ref.at[pl.ds(j, 1)], xs_ref.at[pl.ds(dest_ref[2 * (base + j) + k], 1)], sem)

    def issue(j, carry):
        copy(j, 0).start()
        copy(j, 1).start()
        return carry

    def drain(j, carry):
        copy(j, 0).wait()
        copy(j, 1).wait()
        return carry

    lax.fori_loop(0, tq, issue, 0)
    lax.fori_loop(0, tq, drain, 0)


def _dispatch(h2, dest_flat, n_rows, tq=512):
    T, D = h2.shape
    zeros = jnp.zeros((n_rows, D), h2.dtype)
    return pl.pallas_call(
        functools.partial(_dispatch_kernel, tq=tq),
        grid_spec=pltpu.PrefetchScalarGridSpec(
            num_scalar_prefetch=1,
            grid=(T // tq,),
            in_specs=[pl.BlockSpec((tq, D), lambda i, d: (i, 0)), pl.BlockSpec(memory_space=pl.ANY)],
            out_specs=pl.BlockSpec(memory_space=pl.ANY),
            scratch_shapes=[pltpu.SemaphoreType.DMA(())],
        ),
        out_shape=jax.ShapeDtypeStruct((n_rows, D), h2.dtype),
        input_output_aliases={2: 0},
        compiler_params=_cparams(("arbitrary",)),
        name="moe_dispatch",
    )(dest_flat, h2, zeros)


def _expert_kernel(be_ref, nu_ref, xs_ref, wg_ref, wu_ref, wd_ref, o_ref, wg_s, wu_s, wd_s):
    i = pl.program_id(0)
    prev = be_ref[jnp.maximum(i - 1, 0)]
    new_expert = (i == 0) | (be_ref[i] != prev)

    @pl.when(new_expert)
    def _():
        wg_s[...] = wg_ref[0].astype(BF16)
        wu_s[...] = wu_ref[0].astype(BF16)
        wd_s[...] = wd_ref[0].astype(BF16)

    @pl.when(i < nu_ref[0])
    def _():
        x = xs_ref[...].astype(BF16)
        g = jnp.dot(x, wg_s[...], preferred_element_type=F32)
        u = jnp.dot(x, wu_s[...], preferred_element_type=F32)
        a = (g * jax.nn.sigmoid(g) * u).astype(BF16)
        o_ref[...] = jnp.dot(a, wd_s[...], preferred_element_type=F32)

    @pl.when(i >= nu_ref[0])
    def _():
        o_ref[...] = jnp.zeros_like(o_ref)


def _experts(xs, block_expert, n_used, w_gate, w_up, w_down):
    P, D = xs.shape
    F = w_gate.shape[2]
    nblk = P // MOE_ROWS
    rows = lambda i, be, nu: (jnp.minimum(i, nu[0] - 1), 0)
    return pl.pallas_call(
        _expert_kernel,
        grid_spec=pltpu.PrefetchScalarGridSpec(
            num_scalar_prefetch=2,
            grid=(nblk,),
            in_specs=[
                pl.BlockSpec((MOE_ROWS, D), rows),
                pl.BlockSpec((1, D, F), lambda i, be, nu: (be[i], 0, 0)),
                pl.BlockSpec((1, D, F), lambda i, be, nu: (be[i], 0, 0)),
                pl.BlockSpec((1, F, D), lambda i, be, nu: (be[i], 0, 0)),
            ],
            out_specs=pl.BlockSpec((MOE_ROWS, D), lambda i, be, nu: (i, 0)),
            scratch_shapes=[pltpu.VMEM((D, F), BF16), pltpu.VMEM((D, F), BF16), pltpu.VMEM((F, D), BF16)],
        ),
        out_shape=jax.ShapeDtypeStruct((P, D), F32),
        compiler_params=_cparams(("arbitrary",)),
        name="moe_experts",
    )(block_expert, n_used, xs, w_gate, w_up, w_down)


def _combine_kernel(dest_ref, x1_ref, rt_ref, g_ref, ys_ref, o_ref, buf, sem, *, tq):
    base = pl.program_id(0) * tq

    def copy(j, k):
        return pltpu.make_async_copy(ys_ref.at[pl.ds(dest_ref[2 * (base + j) + k], 1)],
                                     buf.at[k, pl.ds(j, 1)], sem)

    def issue(j, carry):
        copy(j, 0).start()
        copy(j, 1).start()
        return carry

    def drain(j, carry):
        copy(j, 0).wait()
        copy(j, 1).wait()
        return carry

    lax.fori_loop(0, tq, issue, 0)
    lax.fori_loop(0, tq, drain, 0)
    rt = rt_ref[...]
    x = x1_ref[...] + rt[:, 2:3] * buf[0] + rt[:, 3:4] * buf[1]
    ms = jnp.mean(x * x, axis=-1, keepdims=True)
    o_ref[...] = x * lax.rsqrt(ms + NORM_EPS) * g_ref[...]


def _combine(x1, route, gain, ys, dest_flat, tq=256):
    T, D = x1.shape
    return pl.pallas_call(
        functools.partial(_combine_kernel, tq=tq),
        grid_spec=pltpu.PrefetchScalarGridSpec(
            num_scalar_prefetch=1,
            grid=(T // tq,),
            in_specs=[
                pl.BlockSpec((tq, D), lambda i, d: (i, 0)),
                pl.BlockSpec((tq, ROUTE_LANES), lambda i, d: (i, 0)),
                pl.BlockSpec((1, D), lambda i, d: (0, 0)),
                pl.BlockSpec(memory_space=pl.ANY),
            ],
            out_specs=pl.BlockSpec((tq, D), lambda i, d: (i, 0)),
            scratch_shapes=[pltpu.VMEM((MOE_TOPK, tq, D), F32), pltpu.SemaphoreType.DMA(())],
        ),
        out_shape=jax.ShapeDtypeStruct((T, D), F32),
        compiler_params=_cparams(("arbitrary",)),
        name="moe_combine",
    )(dest_flat, x1, route, gain.reshape(1, D).astype(F32), ys)


def _route_metadata(route, T):
    e = route[:, :MOE_TOPK].astype(jnp.int32)
    ids = jnp.arange(MOE_EXPERTS, dtype=jnp.int32)
    oh0 = e[:, 0, None] == ids
    oh1 = e[:, 1, None] == ids
    onehot = (oh0 | oh1).astype(jnp.int32)
    incl = jnp.cumsum(onehot, axis=0)
    counts = incl[-1]
    pcounts = (counts + MOE_ROWS - 1) // MOE_ROWS * MOE_ROWS
    pends = jnp.cumsum(pcounts)
    poffs = pends - pcounts
    slot = incl - onehot + poffs[None, :]
    dest = jnp.stack([jnp.sum(jnp.where(oh0, slot, 0), axis=1), jnp.sum(jnp.where(oh1, slot, 0), axis=1)], axis=1)
    nblk = (T * MOE_TOPK) // MOE_ROWS + MOE_EXPERTS
    block_expert = jnp.minimum(
        jnp.searchsorted(pends, jnp.arange(nblk, dtype=jnp.int32) * MOE_ROWS, side="right"),
        MOE_EXPERTS - 1).astype(jnp.int32)
    n_used = (pends[-1:] // MOE_ROWS).astype(jnp.int32)
    return dest.reshape(-1).astype(jnp.int32), block_expert, n_used, nblk


def kernel(x, norm1_gain, w_in, hg_lb_logits, hg_norm_gain, rel_bias, w_branch_a, w_branch_b, w_out,
           norm2_gain, w_router_group, b_router_group, w_router_expert, b_router_expert,
           w_exp_gate, w_exp_up, w_exp_down, final_norm_gain):
    B, S, D = x.shape
    T = B * S
    depth = norm1_gain.shape[0]
    lower_bounds = jnp.cumsum(jax.nn.softmax(hg_lb_logits.astype(F32), axis=0), axis=0)
    att_w = ATT_HEADS * HEAD_DIM
    bias = _attn_bias(rel_bias)
    x2d = x.reshape(T, D)
    for layer in range(depth):
        h = _rmsnorm(x2d, norm1_gain[layer], BF16)
        zeros_lb = jnp.zeros((1, 2 * D), F32)
        hm = dict(head_major=True, B=B, S=S)
        q_a = _inproj(h, w_in, layer, 0, D, zeros_lb, mode="none", out_dtype=BF16, **hm)
        lf_a = _inproj(h, w_in, layer, D, D, lower_bounds[layer].reshape(1, D), mode="logf", out_dtype=F32, **hm)
        i_a = _inproj(h, w_in, layer, 2 * D, D, zeros_lb, mode="none", out_dtype=BF16, **hm)
        sg_a = _inproj(h, w_in, layer, 3 * D, D, zeros_lb, mode="sigmoid", out_dtype=BF16, **hm)
        qkv_b = _inproj(h, w_in, layer, 4 * D, 3 * att_w, jnp.zeros((1, 3 * att_w), F32), mode="none",
                        out_dtype=BF16, tm=1024, tn=512, **hm)
        gates = _inproj(h, w_in, layer, 4 * D + 3 * att_w, 2 * D, zeros_lb, mode="sigmoid",
                        head_major=False, out_dtype=BF16, B=B, S=S, tm=1024, tn=512)

        y_a = _hgrn(q_a, lf_a, i_a, sg_a, hg_norm_gain[layer]).reshape(T, D)
        y_b = _attention(qkv_b, bias).reshape(T, ATT_HEADS_PER_GROUP * HEAD_DIM)
        merged = _merge(y_a, y_b, w_branch_a[layer].astype(BF16), w_branch_b[layer].astype(BF16), gates)

        wr = jnp.zeros((D, ROUTE_LANES), F32)
        wr = wr.at[:, :MOE_EXPERTS].set(w_router_expert[layer].astype(F32))
        wr = wr.at[:, MOE_EXPERTS:MOE_EXPERTS + MOE_GROUPS].set(w_router_group[layer].astype(F32))
        br = jnp.zeros((1, ROUTE_LANES), F32)
        br = br.at[0, :MOE_EXPERTS].set(b_router_expert[layer].astype(F32))
        br = br.at[0, MOE_EXPERTS:MOE_EXPERTS + MOE_GROUPS].set(b_router_group[layer].astype(F32))
        wr_hi = wr.astype(BF16)
        wr_lo = (wr - wr_hi.astype(F32)).astype(BF16)
        x1, h2, route = _outproj(merged, w_out[layer].astype(BF16), x2d, norm2_gain[layer], wr_hi, wr_lo, br)

        dest, block_expert, n_used, nblk = _route_metadata(route, T)
        xs = _dispatch(h2, dest, nblk * MOE_ROWS)
        ys = _experts(xs, block_expert, n_used, w_exp_gate[layer], w_exp_up[layer], w_exp_down[layer])
        last = layer == depth - 1
        assert last, "the fused combine applies the final norm; deeper stacks need an un-normalised combine"
        x2d = _combine(x1, route, final_norm_gain, ys, dest)
    return x2d.reshape(B, S, D)
```

```python
import functools
import math

import numpy as np
import jax
import jax.numpy as jnp
from jax import lax
from jax.experimental import pallas as pl
from jax.experimental.pallas import tpu as pltpu

F32 = jnp.float32
BF16 = jnp.bfloat16

LANES = 128
NORM_EPS = 1e-6
HEAD_DIM = 128
HG_HEADS = 16
ATT_GROUPS = ((128, 1), (512, 4), (2048, 16))
ATT_HEADS_PER_GROUP = 4
ATT_HEADS = len(ATT_GROUPS) * ATT_HEADS_PER_GROUP
ATT_BLOCK = 128
ATT_UNROLL = 16
REL_BUCKETS = 32
REL_MAX_DIST = 2048
MOE_GROUPS = 8
MOE_EXPERTS_PER_GROUP = 8
MOE_EXPERTS = MOE_GROUPS * MOE_EXPERTS_PER_GROUP
MOE_TOPK = 2
MOE_ROWS = 256
NEG_BIG = -1e30
LOG2E = 1.4426950408889634
VMEM_LIMIT = 56 * 1024 * 1024


def _cparams(sem):
    return pltpu.CompilerParams(dimension_semantics=sem, vmem_limit_bytes=VMEM_LIMIT)


def _rmsnorm_kernel(x_ref, g_ref, o_ref):
    x = x_ref[...]
    ms = jnp.mean(x * x, axis=-1, keepdims=True)
    o_ref[...] = (x * lax.rsqrt(ms + NORM_EPS) * g_ref[...]).astype(o_ref.dtype)


def _rmsnorm(x2d, gain, out_dtype, tm=512):
    T, D = x2d.shape
    return pl.pallas_call(
        _rmsnorm_kernel,
        grid=(T // tm,),
        in_specs=[pl.BlockSpec((tm, D), lambda i: (i, 0)), pl.BlockSpec((1, D), lambda i: (0, 0))],
        out_specs=pl.BlockSpec((tm, D), lambda i: (i, 0)),
        out_shape=jax.ShapeDtypeStruct((T, D), out_dtype),
        compiler_params=_cparams(("parallel",)),
        name="rmsnorm",
    )(x2d, gain.reshape(1, D).astype(F32))


def _inproj_kernel(h_ref, w_ref, lb_ref, o_ref, *, mode, head_major):
    acc = jnp.dot(h_ref[...], w_ref[...], preferred_element_type=F32)
    if mode == "logf":
        lb = lb_ref[...]
        acc = jnp.log(lb + (1.0 - lb) * jax.nn.sigmoid(acc))
    elif mode == "sigmoid":
        acc = jax.nn.sigmoid(acc)
    if head_major:
        for hh in range(acc.shape[1] // HEAD_DIM):
            o_ref[0, hh] = acc[:, hh * HEAD_DIM:(hh + 1) * HEAD_DIM].astype(o_ref.dtype)
    else:
        o_ref[...] = acc.astype(o_ref.dtype)


def _inproj(h, w, lb, *, mode, head_major, out_dtype, B, S, tm=512, tn=1024):
    T, D = h.shape
    N = w.shape[1]
    assert N % tn == 0 and T % tm == 0 and S % tm == 0
    spb = S // tm
    if head_major:
        out_shape = jax.ShapeDtypeStruct((B, N // HEAD_DIM, S, HEAD_DIM), out_dtype)
        out_spec = pl.BlockSpec((1, tn // HEAD_DIM, tm, HEAD_DIM), lambda j, i: (i // spb, j, i % spb, 0))
    else:
        out_shape = jax.ShapeDtypeStruct((T, N), out_dtype)
        out_spec = pl.BlockSpec((tm, tn), lambda j, i: (i, j))
    return pl.pallas_call(
        functools.partial(_inproj_kernel, mode=mode, head_major=head_major),
        grid=(N // tn, T // tm),
        in_specs=[
            pl.BlockSpec((tm, D), lambda j, i: (i, 0)),
            pl.BlockSpec((D, tn), lambda j, i: (0, j)),
            pl.BlockSpec((1, tn), lambda j, i: (0, j)),
        ],
        out_specs=out_spec,
        out_shape=out_shape,
        compiler_params=_cparams(("parallel", "parallel")),
        name="inproj_" + mode,
    )(h, w, lb)


HG_CHUNK = 256
HG_DIAG = 16
HG_HEADS_PER_STEP = 4


def _hgrn_levels(C):
    out, m = [], C // 2
    while m >= HG_DIAG:
        out.append(m)
        m //= 2
    return out


def _hgrn_masks(C):
    t = np.arange(C)[:, None]
    s = np.arange(C)[None, :]
    masks = []
    for m in _hgrn_levels(C):
        masks.append((t // (2 * m) == s // (2 * m)) & ((t // m) % 2 == 1) & ((s // m) % 2 == 0))
    masks.append((t // HG_DIAG == s // HG_DIAG) & (t >= s))
    total = np.sum(np.stack(masks).astype(np.int32), axis=0)
    assert np.array_equal(total, (t >= s).astype(np.int32))
    return np.stack(masks).astype(np.float32)


def _split3(x):
    hi = x.astype(BF16)
    r1 = x - hi.astype(F32)
    mid = r1.astype(BF16)
    lo = (r1 - mid.astype(F32)).astype(BF16)
    return hi, mid, lo


def _dot_nt(a, b):
    return lax.dot_general(a, b, (((1,), (1,)), ((), ())), preferred_element_type=F32)


def _dot_tn(a, b):
    return lax.dot_general(a, b, (((0,), (0,)), ((), ())), preferred_element_type=F32)


def _hgrn_kernel(q_ref, lf_ref, v_ref, sg_ref, gain_ref, tril_ref, mask_ref, o_ref,
                 st_ref, b_ref, oi_ref, qe_ref, kd_ref, *, C):
    S = q_ref.shape[2]
    HP = q_ref.shape[1]
    levels = _hgrn_levels(C)
    nchunks = S // C
    tril = tril_ref[...]

    def ref_rows(h, r0, block, row_of_block):
        parts = [jnp.broadcast_to(b_ref[h, pl.ds(r0 + row_of_block(p), 1), :], (block, HEAD_DIM))
                 for p in range(C // block)]
        return jnp.concatenate(parts, axis=0)

    def cumsum_chunk(c, carry):
        r0 = pl.multiple_of(c * C, C)
        for h in range(HP):
            hi, mid, lo = _split3(lf_ref[0, h, pl.ds(r0, C), :])
            r = jnp.dot(tril, jnp.concatenate([hi, mid, lo], axis=1), preferred_element_type=F32)
            b_ref[h, pl.ds(r0, C), :] = r[:, :HEAD_DIM] + r[:, HEAD_DIM:2 * HEAD_DIM] + r[:, 2 * HEAD_DIM:]
        return carry

    lax.fori_loop(0, nchunks, cumsum_chunk, 0)

    def intra_chunk(c, carry):
        r0 = pl.multiple_of(c * C, C)
        for h in range(HP):
            rows = pl.ds(r0, C)
            q = q_ref[0, h, rows, :].astype(F32)
            kk = 1.0 - jnp.exp(lf_ref[0, h, rows, :])
            b = b_ref[h, rows, :]
            b_last = b_ref[h, pl.ds(r0 + C - 1, 1), :]
            scores = jnp.zeros((C, C), F32)
            for li, m in enumerate(levels):
                d = b - ref_rows(h, r0, 2 * m, lambda p, m=m: 2 * m * p + m - 1)
                e = jnp.exp2(jnp.abs(d) * (-LOG2E))
                scores = scores + mask_ref[li] * _dot_nt((q * e).astype(BF16), (kk * e).astype(BF16))
            d = b - ref_rows(h, r0, HG_DIAG, lambda p: HG_DIAG * p + HG_DIAG // 2 - 1)
            scores = scores + mask_ref[len(levels)] * _dot_nt((q * jnp.exp(d)).astype(BF16),
                                                               (kk * jnp.exp(-d)).astype(BF16))
            oi_ref[h, rows, :] = jnp.dot(scores.astype(BF16), v_ref[0, h, rows, :], preferred_element_type=F32)
            qe_ref[h, rows, :] = (q * jnp.exp(b)).astype(BF16)
            kd_ref[h, rows, :] = (kk * jnp.exp(b_last - b)).astype(BF16)
        return carry

    lax.fori_loop(0, nchunks, intra_chunk, 0)

    st_ref[...] = jnp.zeros_like(st_ref)

    def recur_chunk(c, carry):
        r0 = pl.multiple_of(c * C, C)
        for h in range(HP):
            rows = pl.ds(r0, C)
            st = st_ref[h]
            o = oi_ref[h, rows, :] + _dot_nt(qe_ref[h, rows, :], st.astype(BF16))
            b_last = b_ref[h, pl.ds(r0 + C - 1, 1), :]
            st_ref[h] = st * jnp.exp(b_last) + _dot_tn(v_ref[0, h, rows, :], kd_ref[h, rows, :])
            ms = jnp.mean(o * o, axis=-1, keepdims=True)
            y = o * lax.rsqrt(ms + NORM_EPS) * gain_ref[:, h * HEAD_DIM:(h + 1) * HEAD_DIM]
            y = y * sg_ref[0, h, rows, :].astype(F32)
            o_ref[0, rows, h * HEAD_DIM:(h + 1) * HEAD_DIM] = y.astype(o_ref.dtype)
        return carry

    lax.fori_loop(0, nchunks, recur_chunk, 0)


def _hgrn(q, lf, v, sg, gain):
    B, H, S, E = q.shape
    C, HP = HG_CHUNK, HG_HEADS_PER_STEP
    masks = jnp.asarray(_hgrn_masks(C))
    tril = jnp.asarray(np.tril(np.ones((C, C), np.float32)), dtype=BF16)
    head_spec = pl.BlockSpec((1, HP, S, E), lambda b, h: (b, h, 0, 0))
    return pl.pallas_call(
        functools.partial(_hgrn_kernel, C=C),
        grid=(B, H // HP),
        in_specs=[
            head_spec, head_spec, head_spec, head_spec,
            pl.BlockSpec((1, HP * E), lambda b, h: (0, h)),
            pl.BlockSpec((C, C), lambda b, h: (0, 0)),
            pl.BlockSpec(masks.shape, lambda b, h: (0, 0, 0)),
        ],
        out_specs=pl.BlockSpec((1, S, HP * E), lambda b, h: (b, 0, h)),
        out_shape=jax.ShapeDtypeStruct((B, S, H * E), BF16),
        scratch_shapes=[pltpu.VMEM((HP, E, E), F32), pltpu.VMEM((HP, S, E), F32), pltpu.VMEM((HP, S, E), F32),
                        pltpu.VMEM((HP, S, E), BF16), pltpu.VMEM((HP, S, E), BF16)],
        compiler_params=_cparams(("parallel", "parallel")),
        name="hgrn2",
    )(q, lf, v, sg, gain.reshape(1, H * E).astype(F32), tril, masks)


def _t5_bucket_np(dist):
    exact = REL_BUCKETS // 2
    d_f = np.maximum(dist, 1).astype(np.float32)
    log_b = exact + (np.log(d_f / np.float32(exact)) / np.float32(math.log(REL_MAX_DIST / exact))
                     * np.float32(REL_BUCKETS - exact)).astype(np.int32)
    return np.where(dist < exact, dist, np.minimum(log_b, REL_BUCKETS - 1))


def _attn_bias(rel_bias):
    blk = ATT_BLOCK
    period = 3 * blk
    out = []
    for gi, (window, dilation) in enumerate(ATT_GROUPS):
        n_back = window // dilation
        assert n_back <= blk
        hs = slice(gi * ATT_HEADS_PER_GROUP, (gi + 1) * ATT_HEADS_PER_GROUP)
        bucket = _t5_bucket_np(np.arange(n_back + 1) * dilation)
        by_delta = rel_bias[:, hs][bucket].astype(F32).T
        u = jnp.full((ATT_HEADS_PER_GROUP, period), NEG_BIG, F32)
        u = u.at[:, 2 * blk - 1 - n_back:2 * blk].set(by_delta[:, ::-1])
        flat = jnp.tile(u, (1, blk))[:, :blk * (period - 1)]
        out.append(flat.reshape(ATT_HEADS_PER_GROUP, blk, period - 1)[:, :, blk - 1:3 * blk - 1])
    return jnp.stack(out, axis=0)


def _attn_kernel(q_ref, k_ref, v_ref, bias_ref, o_ref, qf, kf, vf, og, lg):
    S = q_ref.shape[3]
    scale = HEAD_DIM ** -0.5
    blk = ATT_BLOCK

    for g, (window, d) in enumerate(ATT_GROUPS):
        L = S // d
        nb = L // blk
        if d > 1:
            qf[...] = q_ref[0, g, 0].astype(F32)
            kf[...] = k_ref[0, g, 0].astype(F32)
            vf[...] = v_ref[0, g, 0].astype(F32)

        def load(ref_bf, ref_f32, start, size, g=g, d=d):
            if d == 1:
                return ref_bf[0, g, 0, pl.ds(start, size), :]
            return ref_f32[pl.ds(start, size, stride=d), :].astype(BF16)

        def block(r, n, g=g, d=d, load=load):
            start = r + n * blk * d
            rows = pl.ds(start, blk, stride=d) if d > 1 else pl.ds(start, blk)
            q = load(q_ref, qf, start, blk)
            if n == 0:
                k = load(k_ref, kf, start, blk)
                v = load(v_ref, vf, start, blk)
                s = _dot_nt(q, k) * scale + bias_ref[g, 0, :, blk:]
            else:
                first = start - blk * d
                k = load(k_ref, kf, first, 2 * blk)
                v = load(v_ref, vf, first, 2 * blk)
                s = _dot_nt(q, k) * scale + bias_ref[g, 0]
            m = jnp.max(s, axis=-1, keepdims=True)
            p = jnp.exp(s - m)
            den = jnp.sum(p, axis=-1, keepdims=True)
            o = jnp.dot(p.astype(BF16), v, preferred_element_type=F32) / den
            og[g, rows, :] = o
            lg[g, rows, :] = jnp.broadcast_to(m + jnp.log(den), (blk, HEAD_DIM))

        per_body = max(ATT_UNROLL // nb, 1)

        def residues(i, carry, nb=nb, block=block, per_body=per_body):
            for k in range(per_body):
                for n in range(nb):
                    block(i * per_body + k, n)
            return carry

        if d <= per_body:
            residues(0, 0, per_body=d)
        else:
            lax.fori_loop(0, d // per_body, residues, 0)

    rows_per_step = 256

    def mix(i, carry):
        rows = pl.ds(pl.multiple_of(i * rows_per_step, rows_per_step), rows_per_step)
        l0, l1, l2 = lg[0, rows, :], lg[1, rows, :], lg[2, rows, :]
        mx = jnp.maximum(jnp.maximum(l0, l1), l2)
        e0, e1, e2 = jnp.exp(l0 - mx), jnp.exp(l1 - mx), jnp.exp(l2 - mx)
        num = e0 * og[0, rows, :] + e1 * og[1, rows, :] + e2 * og[2, rows, :]
        o_ref[0, rows, :] = (num / (e0 + e1 + e2)).astype(o_ref.dtype)
        return carry

    lax.fori_loop(0, S // rows_per_step, mix, 0)


def _attention(qkv, bias):
    B, _, S, E = qkv.shape
    G, HG = len(ATT_GROUPS), ATT_HEADS_PER_GROUP
    x = qkv.reshape(B, 3, G, HG, S, E)

    def spec(which):
        return pl.BlockSpec((1, None, G, 1, S, E), lambda b, h, which=which: (b, which, 0, h, 0, 0))

    def kern(q_ref, k_ref, v_ref, bias_ref, o_ref, *scratch):
        _attn_kernel(q_ref, k_ref, v_ref, bias_ref, o_ref, *scratch)

    return pl.pallas_call(
        kern,
        grid=(B, HG),
        in_specs=[spec(0), spec(1), spec(2),
                  pl.BlockSpec((G, 1, ATT_BLOCK, 2 * ATT_BLOCK), lambda b, h: (0, h, 0, 0))],
        out_specs=pl.BlockSpec((1, S, E), lambda b, h: (b, 0, h)),
        out_shape=jax.ShapeDtypeStruct((B, S, HG * E), BF16),
        scratch_shapes=[pltpu.VMEM((S, E), F32)] * 3 + [pltpu.VMEM((G, S, E), F32)] * 2,
        compiler_params=_cparams(("parallel", "parallel")),
        name="dilated_attn",
    )(x, x, x, bias)


def _merge_kernel(ya_ref, yb_ref, wa_ref, wb_ref, ga_ref, gb_ref, o_ref):
    a = jnp.dot(ya_ref[...], wa_ref[...], preferred_element_type=F32)
    b = jnp.dot(yb_ref[...], wb_ref[...], preferred_element_type=F32)
    o_ref[...] = (ga_ref[...].astype(F32) * a + gb_ref[...].astype(F32) * b).astype(o_ref.dtype)


def _merge(ya, yb, wa, wb, gates, tm=512, tn=1024):
    T, Ka = ya.shape
    Kb = yb.shape[1]
    N = wa.shape[1]
    nj = N // tn
    return pl.pallas_call(
        _merge_kernel,
        grid=(nj, T // tm),
        in_specs=[
            pl.BlockSpec((tm, Ka), lambda j, i: (i, 0)),
            pl.BlockSpec((tm, Kb), lambda j, i: (i, 0)),
            pl.BlockSpec((Ka, tn), lambda j, i: (0, j)),
            pl.BlockSpec((Kb, tn), lambda j, i: (0, j)),
            pl.BlockSpec((tm, tn), lambda j, i: (i, j)),
            pl.BlockSpec((tm, tn), lambda j, i, nj=nj: (i, j + nj)),
        ],
        out_specs=pl.BlockSpec((tm, tn), lambda j, i: (i, j)),
        out_shape=jax.ShapeDtypeStruct((T, N), BF16),
        compiler_params=_cparams(("parallel", "parallel")),
        name="branch_merge",
    )(ya, yb, wa, wb, gates, gates)


ROUTE_LANES = LANES


def _pack_halves(x_bf):
    bits = pltpu.bitcast(x_bf.astype(F32), jnp.uint32)
    n = bits.shape[1] // 2
    return (bits[:, :n] >> 16) | (bits[:, n:] & jnp.uint32(0xFFFF0000))


def _unpack_halves(p):
    lo = pltpu.bitcast(p << 16, F32)
    hi = pltpu.bitcast(p & jnp.uint32(0xFFFF0000), F32)
    return jnp.concatenate([lo, hi], axis=1)


def _outproj_kernel(m_ref, w_ref, x_ref, g_ref, wrh_ref, wrl_ref, br_ref, x1_ref, h2_ref, rt_ref):
    x1 = x_ref[...] + jnp.dot(m_ref[...], w_ref[...], preferred_element_type=F32)
    x1_ref[...] = x1
    ms = jnp.mean(x1 * x1, axis=-1, keepdims=True)
    h2 = x1 * lax.rsqrt(ms + NORM_EPS) * g_ref[...]
    h_hi = h2.astype(BF16)
    h2_ref[...] = _pack_halves(h_hi)
    h_lo = (h2 - h_hi.astype(F32)).astype(BF16)
    wrh = wrh_ref[...]
    lg = (jnp.dot(h_hi, wrh, preferred_element_type=F32) + jnp.dot(h_lo, wrh, preferred_element_type=F32)
          + jnp.dot(h_hi, wrl_ref[...], preferred_element_type=F32)) + br_ref[...]

    lane = lax.broadcasted_iota(jnp.int32, lg.shape, 1)
    lane_f = lane.astype(F32)
    big = float(ROUTE_LANES)
    is_group = (lane >= MOE_EXPERTS) & (lane < MOE_EXPERTS + MOE_GROUPS)
    lgg = jnp.where(is_group, lg, -jnp.inf)
    gmax = jnp.max(lgg, axis=-1, keepdims=True)
    gsel = jnp.min(jnp.where(lgg == gmax, lane_f - MOE_EXPERTS, big), axis=-1, keepdims=True)
    pg = 1.0 / jnp.sum(jnp.where(is_group, jnp.exp(lg - gmax), 0.0), axis=-1, keepdims=True)

    in_group = (lane < MOE_EXPERTS) & ((lane // MOE_EXPERTS_PER_GROUP).astype(F32) == gsel)
    le = jnp.where(in_group, lg, -jnp.inf)
    t1 = jnp.max(le, axis=-1, keepdims=True)
    i1 = jnp.min(jnp.where(le == t1, lane_f, big), axis=-1, keepdims=True)
    le2 = jnp.where(lane_f == i1, -jnp.inf, le)
    t2 = jnp.max(le2, axis=-1, keepdims=True)
    i2 = jnp.min(jnp.where(le2 == t2, lane_f, big), axis=-1, keepdims=True)
    e2 = jnp.exp(t2 - t1)
    w1 = pg / (1.0 + e2)
    w2 = pg * e2 / (1.0 + e2)
    rt_ref[...] = jnp.where(lane == 0, i1, jnp.where(lane == 1, i2, jnp.where(lane == 2, w1,
                            jnp.where(lane == 3, w2, 0.0))))


def _outproj(merged, w_out, x2d, gain, wr_hi, wr_lo, br, tm=256):
    T, D = x2d.shape
    row = lambda i: (i, 0)
    const = lambda i: (0, 0)
    return pl.pallas_call(
        _outproj_kernel,
        grid=(T // tm,),
        in_specs=[
            pl.BlockSpec((tm, D), row), pl.BlockSpec((D, D), const), pl.BlockSpec((tm, D), row),
            pl.BlockSpec((1, D), const), pl.BlockSpec((D, ROUTE_LANES), const),
            pl.BlockSpec((D, ROUTE_LANES), const), pl.BlockSpec((1, ROUTE_LANES), const),
        ],
        out_specs=[pl.BlockSpec((tm, D), row), pl.BlockSpec((tm, D // 2), row),
                   pl.BlockSpec((tm, ROUTE_LANES), row)],
        out_shape=[jax.ShapeDtypeStruct((T, D), F32), jax.ShapeDtypeStruct((T, D // 2), jnp.uint32),
                   jax.ShapeDtypeStruct((T, ROUTE_LANES), F32)],
        compiler_params=_cparams(("parallel",)),
        name="outproj_router",
    )(merged, w_out, x2d, gain.reshape(1, D).astype(F32), wr_hi, wr_lo, br)


def _dispatch_kernel(dest_ref, h_ref, xs_in_ref, xs_ref, sem, *, tq):
    del xs_in_ref
    base = pl.program_id(0) * tq

    def copy(j, k):
        return pltpu.make_async_copy(h_ref.at[pl.ds(j, 1)], xs_ref.at[pl.ds(dest_ref[2 * (base + j) + k], 1)], sem)

    def issue(j, carry):
        copy(j, 0).start()
        copy(j, 1).start()
        return carry

    lax.fori_loop(0, tq, issue, 0)
    for _ in range(MOE_TOPK):
        pltpu.make_async_copy(h_ref, xs_ref.at[pl.ds(0, tq)], sem).wait()


def _dispatch(h2, dest_flat, n_rows, tq=512):
    T, D = h2.shape
    zeros = jnp.zeros((n_rows, D), h2.dtype)
    return pl.pallas_call(
        functools.partial(_dispatch_kernel, tq=tq),
        grid_spec=pltpu.PrefetchScalarGridSpec(
            num_scalar_prefetch=1,
            grid=(T // tq,),
            in_specs=[pl.BlockSpec((tq, D), lambda i, d: (i, 0)), pl.BlockSpec(memory_space=pl.ANY)],
            out_specs=pl.BlockSpec(memory_space=pl.ANY),
            scratch_shapes=[pltpu.SemaphoreType.DMA(())],
        ),
        out_shape=jax.ShapeDtypeStruct((n_rows, D), h2.dtype),
        input_output_aliases={2: 0},
        compiler_params=_cparams(("arbitrary",)),
        name="moe_dispatch",
    )(dest_flat, h2, zeros)


def _expert_kernel(be_ref, nu_ref, nxt_ref, slot_ref, xs_ref, wg_hbm, wu_hbm, wd_hbm, o_ref,
                   wg_f, wu_f, wd_f, wg_s, wu_s, wd_s, sem):
    i = pl.program_id(0)
    e = be_ref[i]
    active = i < nu_ref[0]
    new_expert = active & ((i == 0) | (e != be_ref[jnp.maximum(i - 1, 0)]))
    slot = slot_ref[e]

    def weight_copies(expert, s):
        return (pltpu.make_async_copy(wg_hbm.at[expert], wg_f.at[s], sem.at[s, 0]),
                pltpu.make_async_copy(wu_hbm.at[expert], wu_f.at[s], sem.at[s, 1]),
                pltpu.make_async_copy(wd_hbm.at[expert], wd_f.at[s], sem.at[s, 2]))

    @pl.when(active & (i == 0))
    def _():
        for c in weight_copies(e, slot):
            c.start()

    @pl.when(new_expert)
    def _():
        nxt = nxt_ref[e]

        @pl.when(nxt >= 0)
        def _():
            for c in weight_copies(nxt, 1 - slot):
                c.start()

        for c in weight_copies(e, slot):
            c.wait()
        wg_s[...] = wg_f[slot].astype(BF16)
        wu_s[...] = wu_f[slot].astype(BF16)
        wd_s[...] = wd_f[slot].astype(BF16)

    @pl.when(active)
    def _():
        x = _unpack_halves(xs_ref[...]).astype(BF16)
        g = jnp.dot(x, wg_s[...], preferred_element_type=F32)
        u = jnp.dot(x, wu_s[...], preferred_element_type=F32)
        a = (g * jax.nn.sigmoid(g) * u).astype(BF16)
        y = jnp.dot(a, wd_s[...], preferred_element_type=F32)
        o_ref[...] = _pack_halves(y.astype(BF16))

    @pl.when(jnp.logical_not(active))
    def _():
        o_ref[...] = jnp.zeros_like(o_ref)


def _experts(xs, block_expert, n_used, next_expert, slot_of_expert, w_gate, w_up, w_down):
    P, Dh = xs.shape
    D, F = w_gate.shape[1], w_gate.shape[2]
    nblk = P // MOE_ROWS
    rows = lambda i, be, nu, nx, sl: (jnp.minimum(i, nu[0] - 1), 0)
    hbm = pl.BlockSpec(memory_space=pl.ANY)
    return pl.pallas_call(
        _expert_kernel,
        grid_spec=pltpu.PrefetchScalarGridSpec(
            num_scalar_prefetch=4,
            grid=(nblk,),
            in_specs=[pl.BlockSpec((MOE_ROWS, Dh), rows), hbm, hbm, hbm],
            out_specs=pl.BlockSpec((MOE_ROWS, Dh), lambda i, be, nu, nx, sl: (i, 0)),
            scratch_shapes=[pltpu.VMEM((2, D, F), F32), pltpu.VMEM((2, D, F), F32), pltpu.VMEM((2, F, D), F32),
                            pltpu.VMEM((D, F), BF16), pltpu.VMEM((D, F), BF16), pltpu.VMEM((F, D), BF16),
                            pltpu.SemaphoreType.DMA((2, 3))],
        ),
        out_shape=jax.ShapeDtypeStruct((P, Dh), jnp.uint32),
        compiler_params=_cparams(("arbitrary",)),
        name="moe_experts",
    )(block_expert, n_used, next_expert, slot_of_expert, xs, w_gate, w_up, w_down)


def _combine_kernel(dest_ref, x1_ref, rt_ref, g_ref, ys_ref, o_ref, buf, sem, *, tq):
    i = pl.program_id(0)
    n = pl.num_programs(0)

    def issue(step, s):
        def body(j, carry):
            for k in range(MOE_TOPK):
                pltpu.make_async_copy(ys_ref.at[pl.ds(dest_ref[2 * (step * tq + j) + k], 1)],
                                      buf.at[s, k, pl.ds(j, 1)], sem.at[s]).start()
            return carry
        lax.fori_loop(0, tq, body, 0)

    @pl.when(i == 0)
    def _():
        issue(0, 0)

    for s in range(2):
        @pl.when((i + 1 < n) & ((i + 1) % 2 == s))
        def _():
            issue(i + 1, s)

    for s in range(2):
        @pl.when(i % 2 == s)
        def _():
            for k in range(MOE_TOPK):
                pltpu.make_async_copy(ys_ref.at[pl.ds(0, tq)], buf.at[s, k], sem.at[s]).wait()
            rt = rt_ref[...]
            x = x1_ref[...] + rt[:, 2:3] * _unpack_halves(buf[s, 0]) + rt[:, 3:4] * _unpack_halves(buf[s, 1])
            ms = jnp.mean(x * x, axis=-1, keepdims=True)
            o_ref[...] = x * lax.rsqrt(ms + NORM_EPS) * g_ref[...]


def _combine(x1, route, gain, ys, dest_flat, tq=256):
    T, D = x1.shape
    return pl.pallas_call(
        functools.partial(_combine_kernel, tq=tq),
        grid_spec=pltpu.PrefetchScalarGridSpec(
            num_scalar_prefetch=1,
            grid=(T // tq,),
            in_specs=[
                pl.BlockSpec((tq, D), lambda i, d: (i, 0)),
                pl.BlockSpec((tq, ROUTE_LANES), lambda i, d: (i, 0)),
                pl.BlockSpec((1, D), lambda i, d: (0, 0)),
                pl.BlockSpec(memory_space=pl.ANY),
            ],
            out_specs=pl.BlockSpec((tq, D), lambda i, d: (i, 0)),
            scratch_shapes=[pltpu.VMEM((2, MOE_TOPK, tq, D // 2), jnp.uint32), pltpu.SemaphoreType.DMA((2,))],
        ),
        out_shape=jax.ShapeDtypeStruct((T, D), F32),
        compiler_params=_cparams(("arbitrary",)),
        name="moe_combine",
    )(dest_flat, x1, route, gain.reshape(1, D).astype(F32), ys)


def _route_metadata(route, T):
    e = route[:, :MOE_TOPK].astype(jnp.int32)
    ids = jnp.arange(MOE_EXPERTS, dtype=jnp.int32)
    oh0 = e[:, 0, None] == ids
    oh1 = e[:, 1, None] == ids
    onehot = (oh0 | oh1).astype(jnp.int32)
    incl = jnp.cumsum(onehot, axis=0)
    counts = incl[-1]
    pcounts = (counts + MOE_ROWS - 1) // MOE_ROWS * MOE_ROWS
    pends = jnp.cumsum(pcounts)
    poffs = pends - pcounts
    slot = incl - onehot + poffs[None, :]
    dest = jnp.stack([jnp.sum(jnp.where(oh0, slot, 0), axis=1), jnp.sum(jnp.where(oh1, slot, 0), axis=1)], axis=1)
    nblk = (T * MOE_TOPK) // MOE_ROWS + MOE_EXPERTS
    block_expert = jnp.minimum(
        jnp.searchsorted(pends, jnp.arange(nblk, dtype=jnp.int32) * MOE_ROWS, side="right"),
        MOE_EXPERTS - 1).astype(jnp.int32)
    n_used = (pends[-1:] // MOE_ROWS).astype(jnp.int32)
    used = counts > 0
    later = lax.cummin(jnp.where(used, ids, MOE_EXPERTS)[::-1])[::-1]
    next_expert = jnp.concatenate([later[1:], jnp.full((1,), MOE_EXPERTS, jnp.int32)])
    next_expert = jnp.where(next_expert >= MOE_EXPERTS, -1, next_expert).astype(jnp.int32)
    slot_of_expert = ((jnp.cumsum(used.astype(jnp.int32)) - 1) % 2).astype(jnp.int32)
    return dest.reshape(-1).astype(jnp.int32), block_expert, n_used, next_expert, slot_of_expert, nblk


def kernel(x, norm1_gain, w_in, hg_lb_logits, hg_norm_gain, rel_bias, w_branch_a, w_branch_b, w_out,
           norm2_gain, w_router_group, b_router_group, w_router_expert, b_router_expert,
           w_exp_gate, w_exp_up, w_exp_down, final_norm_gain):
    B, S, D = x.shape
    T = B * S
    depth = norm1_gain.shape[0]
    lower_bounds = jnp.cumsum(jax.nn.softmax(hg_lb_logits.astype(F32), axis=0), axis=0)
    att_w = ATT_HEADS * HEAD_DIM
    bias = _attn_bias(rel_bias)
    x2d = x.reshape(T, D)
    for layer in range(depth):
        h = _rmsnorm(x2d, norm1_gain[layer], BF16)
        w = w_in[layer]
        zeros_lb = jnp.zeros((1, D), F32)
        seg = lambda a, n: w[:, a:a + n].astype(BF16)
        hm = dict(head_major=True, B=B, S=S)
        q_a = _inproj(h, seg(0, D), zeros_lb, mode="none", out_dtype=BF16, **hm)
        lf_a = _inproj(h, seg(D, D), lower_bounds[layer].reshape(1, D), mode="logf", out_dtype=F32, **hm)
        i_a = _inproj(h, seg(2 * D, D), zeros_lb, mode="none", out_dtype=BF16, **hm)
        sg_a = _inproj(h, seg(3 * D, D), zeros_lb, mode="sigmoid", out_dtype=BF16, **hm)
        qkv_b = _inproj(h, seg(4 * D, 3 * att_w), jnp.zeros((1, 3 * att_w), F32), mode="none",
                        out_dtype=BF16, tn=att_w, **hm)
        gates = _inproj(h, seg(4 * D + 3 * att_w, 2 * D), jnp.zeros((1, 2 * D), F32), mode="sigmoid",
                        head_major=False, out_dtype=BF16, B=B, S=S)

        y_a = _hgrn(q_a, lf_a, i_a, sg_a, hg_norm_gain[layer]).reshape(T, D)
        y_b = _attention(qkv_b, bias).reshape(T, ATT_HEADS_PER_GROUP * HEAD_DIM)
        merged = _merge(y_a, y_b, w_branch_a[layer].astype(BF16), w_branch_b[layer].astype(BF16), gates)

        wr = jnp.zeros((D, ROUTE_LANES), F32)
        wr = wr.at[:, :MOE_EXPERTS].set(w_router_expert[layer].astype(F32))
        wr = wr.at[:, MOE_EXPERTS:MOE_EXPERTS + MOE_GROUPS].set(w_router_group[layer].astype(F32))
        br = jnp.zeros((1, ROUTE_LANES), F32)
        br = br.at[0, :MOE_EXPERTS].set(b_router_expert[layer].astype(F32))
        br = br.at[0, MOE_EXPERTS:MOE_EXPERTS + MOE_GROUPS].set(b_router_group[layer].astype(F32))
        wr_hi = wr.astype(BF16)
        wr_lo = (wr - wr_hi.astype(F32)).astype(BF16)
        x1, h2, route = _outproj(merged, w_out[layer].astype(BF16), x2d, norm2_gain[layer], wr_hi, wr_lo, br)

        dest, block_expert, n_used, next_expert, slot_of_expert, nblk = _route_metadata(route, T)
        xs = _dispatch(h2, dest, nblk * MOE_ROWS)
        ys = _experts(xs, block_expert, n_used, next_expert, slot_of_expert,
                      w_exp_gate[layer], w_exp_up[layer], w_exp_down[layer])
        last = layer == depth - 1
        assert last, "the fused combine applies the final norm; deeper stacks need an un-normalised combine"
        x2d = _combine(x1, route, final_norm_gain, ys, dest)
    return x2d.reshape(B, S, D)
```

```python
import functools
import math

import numpy as np
import jax
import jax.numpy as jnp
from jax import lax
from jax.experimental import pallas as pl
from jax.experimental.pallas import tpu as pltpu

F32 = jnp.float32
BF16 = jnp.bfloat16

LANES = 128
NORM_EPS = 1e-6
HEAD_DIM = 128
HG_HEADS = 16
ATT_GROUPS = ((128, 1), (512, 4), (2048, 16))
ATT_HEADS_PER_GROUP = 4
ATT_HEADS = len(ATT_GROUPS) * ATT_HEADS_PER_GROUP
ATT_BLOCK = 128
ATT_UNROLL = 16
REL_BUCKETS = 32
REL_MAX_DIST = 2048
MOE_GROUPS = 8
MOE_EXPERTS_PER_GROUP = 8
MOE_EXPERTS = MOE_GROUPS * MOE_EXPERTS_PER_GROUP
MOE_TOPK = 2
MOE_ROWS = 256
NEG_BIG = -1e30
LOG2E = 1.4426950408889634
VMEM_LIMIT = 56 * 1024 * 1024


def _cparams(sem):
    return pltpu.CompilerParams(dimension_semantics=sem, vmem_limit_bytes=VMEM_LIMIT)


def _rmsnorm_kernel(x_ref, g_ref, o_ref):
    x = x_ref[...]
    ms = jnp.mean(x * x, axis=-1, keepdims=True)
    o_ref[...] = (x * lax.rsqrt(ms + NORM_EPS) * g_ref[...]).astype(o_ref.dtype)


def _rmsnorm(x2d, gain, out_dtype, tm=512):
    T, D = x2d.shape
    return pl.pallas_call(
        _rmsnorm_kernel,
        grid=(T // tm,),
        in_specs=[pl.BlockSpec((tm, D), lambda i: (i, 0)), pl.BlockSpec((1, D), lambda i: (0, 0))],
        out_specs=pl.BlockSpec((tm, D), lambda i: (i, 0)),
        out_shape=jax.ShapeDtypeStruct((T, D), out_dtype),
        compiler_params=_cparams(("parallel",)),
        name="rmsnorm",
    )(x2d, gain.reshape(1, D).astype(F32))


def _inproj_kernel(h_ref, w_ref, lb_ref, o_ref, *, mode, head_major):
    acc = jnp.dot(h_ref[...], w_ref[...], preferred_element_type=F32)
    if mode == "logf":
        lb = lb_ref[...]
        acc = jnp.log(lb + (1.0 - lb) * jax.nn.sigmoid(acc))
    elif mode == "sigmoid":
        acc = jax.nn.sigmoid(acc)
    if head_major:
        for hh in range(acc.shape[1] // HEAD_DIM):
            o_ref[0, hh] = acc[:, hh * HEAD_DIM:(hh + 1) * HEAD_DIM].astype(o_ref.dtype)
    else:
        o_ref[...] = acc.astype(o_ref.dtype)


def _inproj(h, w, lb, *, mode, head_major, out_dtype, B, S, tm=512, tn=1024):
    T, D = h.shape
    N = w.shape[1]
    assert N % tn == 0 and T % tm == 0 and S % tm == 0
    spb = S // tm
    if head_major:
        out_shape = jax.ShapeDtypeStruct((B, N // HEAD_DIM, S, HEAD_DIM), out_dtype)
        out_spec = pl.BlockSpec((1, tn // HEAD_DIM, tm, HEAD_DIM), lambda j, i: (i // spb, j, i % spb, 0))
    else:
        out_shape = jax.ShapeDtypeStruct((T, N), out_dtype)
        out_spec = pl.BlockSpec((tm, tn), lambda j, i: (i, j))
    return pl.pallas_call(
        functools.partial(_inproj_kernel, mode=mode, head_major=head_major),
        grid=(N // tn, T // tm),
        in_specs=[
            pl.BlockSpec((tm, D), lambda j, i: (i, 0)),
            pl.BlockSpec((D, tn), lambda j, i: (0, j)),
            pl.BlockSpec((1, tn), lambda j, i: (0, j)),
        ],
        out_specs=out_spec,
        out_shape=out_shape,
        compiler_params=_cparams(("parallel", "parallel")),
        name="inproj_" + mode,
    )(h, w, lb)


HG_CHUNK = 256
HG_DIAG = 16
HG_HEADS_PER_STEP = 4


def _hgrn_levels(C):
    out, m = [], C // 2
    while m >= HG_DIAG:
        out.append(m)
        m //= 2
    return out


def _hgrn_masks(C):
    t = np.arange(C)[:, None]
    s = np.arange(C)[None, :]
    masks = []
    for m in _hgrn_levels(C):
        masks.append((t // (2 * m) == s // (2 * m)) & ((t // m) % 2 == 1) & ((s // m) % 2 == 0))
    masks.append((t // HG_DIAG == s // HG_DIAG) & (t >= s))
    total = np.sum(np.stack(masks).astype(np.int32), axis=0)
    assert np.array_equal(total, (t >= s).astype(np.int32))
    return np.stack(masks).astype(np.float32)


def _dot_nt(a, b):
    return lax.dot_general(a, b, (((1,), (1,)), ((), ())), preferred_element_type=F32)


def _dot_tn(a, b):
    return lax.dot_general(a, b, (((0,), (0,)), ((), ())), preferred_element_type=F32)


def _hgrn_kernel(q_ref, lf_ref, v_ref, sg_ref, gain_ref, tril_ref, mask_ref, o_ref, st_ref, b_ref, *, C):
    S = q_ref.shape[2]
    HP = q_ref.shape[1]
    levels = _hgrn_levels(C)
    nchunks = S // C
    tril = tril_ref[...]

    def ref_rows(h, block, row_of_block):
        parts = [jnp.broadcast_to(b_ref[h, pl.ds(row_of_block(p), 1), :], (block, HEAD_DIM))
                 for p in range(C // block)]
        return jnp.concatenate(parts, axis=0)

    def cumsum(c):
        r0 = pl.multiple_of(c * C, C)
        out = []
        for h in range(HP):
            lf2 = lf_ref[0, h, pl.ds(r0, C), :] * LOG2E
            hi = lf2.astype(BF16)
            lo = (lf2 - hi.astype(F32)).astype(BF16)
            r = jnp.dot(tril, jnp.concatenate([hi, lo], axis=1), preferred_element_type=F32)
            out.append(r[:, :HEAD_DIM] + r[:, HEAD_DIM:])
        return tuple(out)

    st_ref[...] = jnp.zeros_like(st_ref)

    def chunk(c, b_all):
        r0 = pl.multiple_of(c * C, C)
        rows = pl.ds(r0, C)
        b_next = cumsum(jnp.minimum(c + 1, nchunks - 1))
        for h in range(HP):
            b = b_all[h]
            b_ref[h] = b
            q = q_ref[0, h, rows, :].astype(F32)
            kk = 1.0 - jnp.exp(lf_ref[0, h, rows, :])
            b_last = b_ref[h, pl.ds(C - 1, 1), :]
            scores = jnp.zeros((C, C), F32)
            for li, m in enumerate(levels):
                d = b - ref_rows(h, 2 * m, lambda p, m=m: 2 * m * p + m - 1)
                e = jnp.exp2(-jnp.abs(d))
                scores = scores + mask_ref[li] * _dot_nt((q * e).astype(BF16), (kk * e).astype(BF16))
            d = b - ref_rows(h, HG_DIAG, lambda p: HG_DIAG * p + HG_DIAG // 2 - 1)
            scores = scores + mask_ref[len(levels)] * _dot_nt((q * jnp.exp2(d)).astype(BF16),
                                                               (kk * jnp.exp2(-d)).astype(BF16))
            v = v_ref[0, h, rows, :]
            st = st_ref[h]
            qe = (q * jnp.exp2(b)).astype(BF16)
            kd = (kk * jnp.exp2(b_last - b)).astype(BF16)
            o = jnp.dot(scores.astype(BF16), v, preferred_element_type=F32) + _dot_nt(qe, st.astype(BF16))
            st_ref[h] = st * jnp.exp2(b_last) + _dot_tn(v, kd)
            ms = jnp.mean(o * o, axis=-1, keepdims=True)
            y = o * lax.rsqrt(ms + NORM_EPS) * gain_ref[:, h * HEAD_DIM:(h + 1) * HEAD_DIM]
            y = y * sg_ref[0, h, rows, :].astype(F32)
            o_ref[0, rows, h * HEAD_DIM:(h + 1) * HEAD_DIM] = y.astype(o_ref.dtype)
        return b_next

    lax.fori_loop(0, nchunks, chunk, cumsum(0))


def _hgrn(q, lf, v, sg, gain):
    B, H, S, E = q.shape
    C, HP = HG_CHUNK, HG_HEADS_PER_STEP
    masks = jnp.asarray(_hgrn_masks(C))
    tril = jnp.asarray(np.tril(np.ones((C, C), np.float32)), dtype=BF16)
    head_spec = pl.BlockSpec((1, HP, S, E), lambda b, h: (b, h, 0, 0))
    return pl.pallas_call(
        functools.partial(_hgrn_kernel, C=C),
        grid=(B, H // HP),
        in_specs=[
            head_spec, head_spec, head_spec, head_spec,
            pl.BlockSpec((1, HP * E), lambda b, h: (0, h)),
            pl.BlockSpec((C, C), lambda b, h: (0, 0)),
            pl.BlockSpec(masks.shape, lambda b, h: (0, 0, 0)),
        ],
        out_specs=pl.BlockSpec((1, S, HP * E), lambda b, h: (b, 0, h)),
        out_shape=jax.ShapeDtypeStruct((B, S, H * E), BF16),
        scratch_shapes=[pltpu.VMEM((HP, E, E), F32), pltpu.VMEM((HP, C, E), F32)],
        compiler_params=_cparams(("parallel", "parallel")),
        name="hgrn2",
    )(q, lf, v, sg, gain.reshape(1, H * E).astype(F32), tril, masks)


def _t5_bucket_np(dist):
    exact = REL_BUCKETS // 2
    d_f = np.maximum(dist, 1).astype(np.float32)
    log_b = exact + (np.log(d_f / np.float32(exact)) / np.float32(math.log(REL_MAX_DIST / exact))
                     * np.float32(REL_BUCKETS - exact)).astype(np.int32)
    return np.where(dist < exact, dist, np.minimum(log_b, REL_BUCKETS - 1))


def _attn_bias(rel_bias):
    blk = ATT_BLOCK
    period = 3 * blk
    out = []
    for gi, (window, dilation) in enumerate(ATT_GROUPS):
        n_back = window // dilation
        assert n_back <= blk
        hs = slice(gi * ATT_HEADS_PER_GROUP, (gi + 1) * ATT_HEADS_PER_GROUP)
        bucket = _t5_bucket_np(np.arange(n_back + 1) * dilation)
        by_delta = rel_bias[:, hs][bucket].astype(F32).T
        u = jnp.full((ATT_HEADS_PER_GROUP, period), NEG_BIG, F32)
        u = u.at[:, 2 * blk - 1 - n_back:2 * blk].set(by_delta[:, ::-1])
        flat = jnp.tile(u, (1, blk))[:, :blk * (period - 1)]
        out.append(flat.reshape(ATT_HEADS_PER_GROUP, blk, period - 1)[:, :, blk - 1:3 * blk - 1])
    return jnp.stack(out, axis=0)


def _attn_kernel(q_ref, k_ref, v_ref, bias_ref, o_ref, qf, kf, vf, og, lg):
    S = q_ref.shape[3]
    scale = HEAD_DIM ** -0.5
    blk = ATT_BLOCK

    for g, (window, d) in enumerate(ATT_GROUPS):
        L = S // d
        nb = L // blk
        if d > 1:
            qf[...] = q_ref[0, g, 0].astype(F32)
            kf[...] = k_ref[0, g, 0].astype(F32)
            vf[...] = v_ref[0, g, 0].astype(F32)

        def load(ref_bf, ref_f32, start, size, g=g, d=d):
            if d == 1:
                return ref_bf[0, g, 0, pl.ds(start, size), :]
            return ref_f32[pl.ds(start, size, stride=d), :].astype(BF16)

        def block(r, n, g=g, d=d, load=load):
            start = r + n * blk * d
            rows = pl.ds(start, blk, stride=d) if d > 1 else pl.ds(start, blk)
            q = load(q_ref, qf, start, blk)
            if n == 0:
                k = load(k_ref, kf, start, blk)
                v = load(v_ref, vf, start, blk)
                s = _dot_nt(q, k) * scale + bias_ref[g, 0, :, blk:]
            else:
                first = start - blk * d
                k = load(k_ref, kf, first, 2 * blk)
                v = load(v_ref, vf, first, 2 * blk)
                s = _dot_nt(q, k) * scale + bias_ref[g, 0]
            m = jnp.max(s, axis=-1, keepdims=True)
            p = jnp.exp(s - m)
            den = jnp.sum(p, axis=-1, keepdims=True)
            o = jnp.dot(p.astype(BF16), v, preferred_element_type=F32) / den
            og[g, rows, :] = o
            lg[g, rows, :] = jnp.broadcast_to(m + jnp.log(den), (blk, HEAD_DIM))

        per_body = max(ATT_UNROLL // nb, 1)

        def residues(i, carry, nb=nb, block=block, per_body=per_body):
            for k in range(per_body):
                for n in range(nb):
                    block(i * per_body + k, n)
            return carry

        if d <= per_body:
            residues(0, 0, per_body=d)
        else:
            lax.fori_loop(0, d // per_body, residues, 0)

    rows_per_step = 256

    def mix(i, carry):
        rows = pl.ds(pl.multiple_of(i * rows_per_step, rows_per_step), rows_per_step)
        l0, l1, l2 = lg[0, rows, :], lg[1, rows, :], lg[2, rows, :]
        mx = jnp.maximum(jnp.maximum(l0, l1), l2)
        e0, e1, e2 = jnp.exp(l0 - mx), jnp.exp(l1 - mx), jnp.exp(l2 - mx)
        num = e0 * og[0, rows, :] + e1 * og[1, rows, :] + e2 * og[2, rows, :]
        o_ref[0, rows, :] = (num / (e0 + e1 + e2)).astype(o_ref.dtype)
        return carry

    lax.fori_loop(0, S // rows_per_step, mix, 0)


def _attention(qkv, bias):
    B, _, S, E = qkv.shape
    G, HG = len(ATT_GROUPS), ATT_HEADS_PER_GROUP
    x = qkv.reshape(B, 3, G, HG, S, E)

    def spec(which):
        return pl.BlockSpec((1, None, G, 1, S, E), lambda b, h, which=which: (b, which, 0, h, 0, 0))

    def kern(q_ref, k_ref, v_ref, bias_ref, o_ref, *scratch):
        _attn_kernel(q_ref, k_ref, v_ref, bias_ref, o_ref, *scratch)

    return pl.pallas_call(
        kern,
        grid=(B, HG),
        in_specs=[spec(0), spec(1), spec(2),
                  pl.BlockSpec((G, 1, ATT_BLOCK, 2 * ATT_BLOCK), lambda b, h: (0, h, 0, 0))],
        out_specs=pl.BlockSpec((1, S, E), lambda b, h: (b, 0, h)),
        out_shape=jax.ShapeDtypeStruct((B, S, HG * E), BF16),
        scratch_shapes=[pltpu.VMEM((S, E), F32)] * 3 + [pltpu.VMEM((G, S, E), F32)] * 2,
        compiler_params=_cparams(("parallel", "parallel")),
        name="dilated_attn",
    )(x, x, x, bias)


def _merge_kernel(ya_ref, yb_ref, wa_ref, wb_ref, ga_ref, gb_ref, o_ref):
    a = jnp.dot(ya_ref[...], wa_ref[...], preferred_element_type=F32)
    b = jnp.dot(yb_ref[...], wb_ref[...], preferred_element_type=F32)
    o_ref[...] = (ga_ref[...].astype(F32) * a + gb_ref[...].astype(F32) * b).astype(o_ref.dtype)


def _merge(ya, yb, wa, wb, gates, tm=512, tn=1024):
    T, Ka = ya.shape
    Kb = yb.shape[1]
    N = wa.shape[1]
    nj = N // tn
    return pl.pallas_call(
        _merge_kernel,
        grid=(nj, T // tm),
        in_specs=[
            pl.BlockSpec((tm, Ka), lambda j, i: (i, 0)),
            pl.BlockSpec((tm, Kb), lambda j, i: (i, 0)),
            pl.BlockSpec((Ka, tn), lambda j, i: (0, j)),
            pl.BlockSpec((Kb, tn), lambda j, i: (0, j)),
            pl.BlockSpec((tm, tn), lambda j, i: (i, j)),
            pl.BlockSpec((tm, tn), lambda j, i, nj=nj: (i, j + nj)),
        ],
        out_specs=pl.BlockSpec((tm, tn), lambda j, i: (i, j)),
        out_shape=jax.ShapeDtypeStruct((T, N), BF16),
        compiler_params=_cparams(("parallel", "parallel")),
        name="branch_merge",
    )(ya, yb, wa, wb, gates, gates)


ROUTE_LANES = LANES


def _pack_halves(x_bf):
    bits = pltpu.bitcast(x_bf.astype(F32), jnp.uint32)
    n = bits.shape[1] // 2
    return (bits[:, :n] >> 16) | (bits[:, n:] & jnp.uint32(0xFFFF0000))


def _unpack_halves(p):
    lo = pltpu.bitcast(p << 16, F32)
    hi = pltpu.bitcast(p & jnp.uint32(0xFFFF0000), F32)
    return jnp.concatenate([lo, hi], axis=1)


def _outproj_kernel(m_ref, w_ref, x_ref, g_ref, wrh_ref, wrl_ref, br_ref, x1_ref, h2_ref, rt_ref):
    x1 = x_ref[...] + jnp.dot(m_ref[...], w_ref[...], preferred_element_type=F32)
    x1_ref[...] = x1
    ms = jnp.mean(x1 * x1, axis=-1, keepdims=True)
    h2 = x1 * lax.rsqrt(ms + NORM_EPS) * g_ref[...]
    h_hi = h2.astype(BF16)
    h2_ref[...] = _pack_halves(h_hi)
    h_lo = (h2 - h_hi.astype(F32)).astype(BF16)
    wrh = wrh_ref[...]
    lg = (jnp.dot(h_hi, wrh, preferred_element_type=F32) + jnp.dot(h_lo, wrh, preferred_element_type=F32)
          + jnp.dot(h_hi, wrl_ref[...], preferred_element_type=F32)) + br_ref[...]

    lane = lax.broadcasted_iota(jnp.int32, lg.shape, 1)
    lane_f = lane.astype(F32)
    big = float(ROUTE_LANES)
    is_group = (lane >= MOE_EXPERTS) & (lane < MOE_EXPERTS + MOE_GROUPS)
    lgg = jnp.where(is_group, lg, -jnp.inf)
    gmax = jnp.max(lgg, axis=-1, keepdims=True)
    gsel = jnp.min(jnp.where(lgg == gmax, lane_f - MOE_EXPERTS, big), axis=-1, keepdims=True)
    pg = 1.0 / jnp.sum(jnp.where(is_group, jnp.exp(lg - gmax), 0.0), axis=-1, keepdims=True)

    in_group = (lane < MOE_EXPERTS) & ((lane // MOE_EXPERTS_PER_GROUP).astype(F32) == gsel)
    le = jnp.where(in_group, lg, -jnp.inf)
    t1 = jnp.max(le, axis=-1, keepdims=True)
    i1 = jnp.min(jnp.where(le == t1, lane_f, big), axis=-1, keepdims=True)
    le2 = jnp.where(lane_f == i1, -jnp.inf, le)
    t2 = jnp.max(le2, axis=-1, keepdims=True)
    i2 = jnp.min(jnp.where(le2 == t2, lane_f, big), axis=-1, keepdims=True)
    e2 = jnp.exp(t2 - t1)
    w1 = pg / (1.0 + e2)
    w2 = pg * e2 / (1.0 + e2)
    rt_ref[...] = jnp.where(lane == 0, i1, jnp.where(lane == 1, i2, jnp.where(lane == 2, w1,
                            jnp.where(lane == 3, w2, 0.0))))


def _outproj(merged, w_out, x2d, gain, wr_hi, wr_lo, br, tm=256):
    T, D = x2d.shape
    row = lambda i: (i, 0)
    const = lambda i: (0, 0)
    return pl.pallas_call(
        _outproj_kernel,
        grid=(T // tm,),
        in_specs=[
            pl.BlockSpec((tm, D), row), pl.BlockSpec((D, D), const), pl.BlockSpec((tm, D), row),
            pl.BlockSpec((1, D), const), pl.BlockSpec((D, ROUTE_LANES), const),
            pl.BlockSpec((D, ROUTE_LANES), const), pl.BlockSpec((1, ROUTE_LANES), const),
        ],
        out_specs=[pl.BlockSpec((tm, D), row), pl.BlockSpec((tm, D // 2), row),
                   pl.BlockSpec((tm, ROUTE_LANES), row)],
        out_shape=[jax.ShapeDtypeStruct((T, D), F32), jax.ShapeDtypeStruct((T, D // 2), jnp.uint32),
                   jax.ShapeDtypeStruct((T, ROUTE_LANES), F32)],
        compiler_params=_cparams(("parallel",)),
        name="outproj_router",
    )(merged, w_out, x2d, gain.reshape(1, D).astype(F32), wr_hi, wr_lo, br)


def _dispatch_kernel(dest_ref, pad_ref, h_ref, xs_ref, zeros, sem, *, tq):
    base = pl.program_id(0) * tq

    @pl.when(pl.program_id(0) == 0)
    def _():
        zeros[...] = jnp.zeros_like(zeros)

        def pad_copy(e):
            return pltpu.make_async_copy(zeros, xs_ref.at[pl.ds(pl.multiple_of(pad_ref[e], 8), MOE_ROWS)], sem)

        def start(e, carry):
            pad_copy(e).start()
            return carry

        def wait(e, carry):
            pad_copy(e).wait()
            return carry

        lax.fori_loop(0, MOE_EXPERTS, start, 0)
        lax.fori_loop(0, MOE_EXPERTS, wait, 0)

    def copy(j, k):
        return pltpu.make_async_copy(h_ref.at[pl.ds(j, 1)], xs_ref.at[pl.ds(dest_ref[2 * (base + j) + k], 1)], sem)

    def issue(j, carry):
        copy(j, 0).start()
        copy(j, 1).start()
        return carry

    lax.fori_loop(0, tq, issue, 0)
    for _ in range(MOE_TOPK):
        pltpu.make_async_copy(h_ref, xs_ref.at[pl.ds(0, tq)], sem).wait()


def _dispatch(h2, dest_flat, pad_start, n_rows, tq=512):
    T, D = h2.shape
    return pl.pallas_call(
        functools.partial(_dispatch_kernel, tq=tq),
        grid_spec=pltpu.PrefetchScalarGridSpec(
            num_scalar_prefetch=2,
            grid=(T // tq,),
            in_specs=[pl.BlockSpec((tq, D), lambda i, d, p: (i, 0))],
            out_specs=pl.BlockSpec(memory_space=pl.ANY),
            scratch_shapes=[pltpu.VMEM((MOE_ROWS, D), h2.dtype), pltpu.SemaphoreType.DMA(())],
        ),
        out_shape=jax.ShapeDtypeStruct((n_rows + MOE_ROWS, D), h2.dtype),
        compiler_params=_cparams(("arbitrary",)),
        name="moe_dispatch",
    )(dest_flat, pad_start, h2)


def _expert_kernel(be_ref, nu_ref, nxt_ref, slot_ref, xs_ref, wg_hbm, wu_hbm, wd_hbm, o_ref,
                   wg_f, wu_f, wd_f, wg_s, wu_s, wd_s, sem):
    i = pl.program_id(0)
    e = be_ref[i]
    active = i < nu_ref[0]
    new_expert = active & ((i == 0) | (e != be_ref[jnp.maximum(i - 1, 0)]))
    slot = slot_ref[e]

    def weight_copies(expert, s):
        return (pltpu.make_async_copy(wg_hbm.at[expert], wg_f.at[s], sem.at[s, 0]),
                pltpu.make_async_copy(wu_hbm.at[expert], wu_f.at[s], sem.at[s, 1]),
                pltpu.make_async_copy(wd_hbm.at[expert], wd_f.at[s], sem.at[s, 2]))

    @pl.when(active & (i == 0))
    def _():
        for c in weight_copies(e, slot):
            c.start()

    @pl.when(new_expert)
    def _():
        nxt = nxt_ref[e]

        @pl.when(nxt >= 0)
        def _():
            for c in weight_copies(nxt, 1 - slot):
                c.start()

        for c in weight_copies(e, slot):
            c.wait()
        wg_s[...] = wg_f[slot].astype(BF16)
        wu_s[...] = wu_f[slot].astype(BF16)
        wd_s[...] = wd_f[slot].astype(BF16)

    @pl.when(active)
    def _():
        x = _unpack_halves(xs_ref[...]).astype(BF16)
        g = jnp.dot(x, wg_s[...], preferred_element_type=F32)
        u = jnp.dot(x, wu_s[...], preferred_element_type=F32)
        a = (g * jax.nn.sigmoid(g) * u).astype(BF16)
        y = jnp.dot(a, wd_s[...], preferred_element_type=F32)
        o_ref[...] = _pack_halves(y.astype(BF16))

    @pl.when(jnp.logical_not(active))
    def _():
        o_ref[...] = jnp.zeros_like(o_ref)


def _experts(xs, block_expert, n_used, next_expert, slot_of_expert, w_gate, w_up, w_down):
    Dh = xs.shape[1]
    D, F = w_gate.shape[1], w_gate.shape[2]
    nblk = block_expert.shape[0]
    P = nblk * MOE_ROWS
    rows = lambda i, be, nu, nx, sl: (jnp.minimum(i, nu[0] - 1), 0)
    hbm = pl.BlockSpec(memory_space=pl.ANY)
    return pl.pallas_call(
        _expert_kernel,
        grid_spec=pltpu.PrefetchScalarGridSpec(
            num_scalar_prefetch=4,
            grid=(nblk,),
            in_specs=[pl.BlockSpec((MOE_ROWS, Dh), rows), hbm, hbm, hbm],
            out_specs=pl.BlockSpec((MOE_ROWS, Dh), lambda i, be, nu, nx, sl: (i, 0)),
            scratch_shapes=[pltpu.VMEM((2, D, F), F32), pltpu.VMEM((2, D, F), F32), pltpu.VMEM((2, F, D), F32),
                            pltpu.VMEM((D, F), BF16), pltpu.VMEM((D, F), BF16), pltpu.VMEM((F, D), BF16),
                            pltpu.SemaphoreType.DMA((2, 3))],
        ),
        out_shape=jax.ShapeDtypeStruct((P, Dh), jnp.uint32),
        compiler_params=_cparams(("arbitrary",)),
        name="moe_experts",
    )(block_expert, n_used, next_expert, slot_of_expert, xs, w_gate, w_up, w_down)


def _combine_kernel(dest_ref, x1_ref, rt_ref, g_ref, ys_ref, o_ref, buf, sem, *, tq):
    i = pl.program_id(0)
    n = pl.num_programs(0)

    def issue(step, s):
        def body(j, carry):
            for k in range(MOE_TOPK):
                pltpu.make_async_copy(ys_ref.at[pl.ds(dest_ref[2 * (step * tq + j) + k], 1)],
                                      buf.at[s, k, pl.ds(j, 1)], sem.at[s]).start()
            return carry
        lax.fori_loop(0, tq, body, 0)

    @pl.when(i == 0)
    def _():
        issue(0, 0)

    for s in range(2):
        @pl.when((i + 1 < n) & ((i + 1) % 2 == s))
        def _():
            issue(i + 1, s)

    for s in range(2):
        @pl.when(i % 2 == s)
        def _():
            for k in range(MOE_TOPK):
                pltpu.make_async_copy(ys_ref.at[pl.ds(0, tq)], buf.at[s, k], sem.at[s]).wait()
            rt = rt_ref[...]
            x = x1_ref[...] + rt[:, 2:3] * _unpack_halves(buf[s, 0]) + rt[:, 3:4] * _unpack_halves(buf[s, 1])
            ms = jnp.mean(x * x, axis=-1, keepdims=True)
            o_ref[...] = x * lax.rsqrt(ms + NORM_EPS) * g_ref[...]


def _combine(x1, route, gain, ys, dest_flat, tq=256):
    T, D = x1.shape
    return pl.pallas_call(
        functools.partial(_combine_kernel, tq=tq),
        grid_spec=pltpu.PrefetchScalarGridSpec(
            num_scalar_prefetch=1,
            grid=(T // tq,),
            in_specs=[
                pl.BlockSpec((tq, D), lambda i, d: (i, 0)),
                pl.BlockSpec((tq, ROUTE_LANES), lambda i, d: (i, 0)),
                pl.BlockSpec((1, D), lambda i, d: (0, 0)),
                pl.BlockSpec(memory_space=pl.ANY),
            ],
            out_specs=pl.BlockSpec((tq, D), lambda i, d: (i, 0)),
            scratch_shapes=[pltpu.VMEM((2, MOE_TOPK, tq, D // 2), jnp.uint32), pltpu.SemaphoreType.DMA((2,))],
        ),
        out_shape=jax.ShapeDtypeStruct((T, D), F32),
        compiler_params=_cparams(("arbitrary",)),
        name="moe_combine",
    )(dest_flat, x1, route, gain.reshape(1, D).astype(F32), ys)


def _route_metadata(route, T):
    e = route[:, :MOE_TOPK].astype(jnp.int32)
    ids = jnp.arange(MOE_EXPERTS, dtype=jnp.int32)
    oh0 = e[:, 0, None] == ids
    oh1 = e[:, 1, None] == ids
    onehot = (oh0 | oh1).astype(jnp.int32)
    incl = jnp.cumsum(onehot, axis=0)
    counts = incl[-1]
    pcounts = (counts + MOE_ROWS - 1) // MOE_ROWS * MOE_ROWS
    pends = jnp.cumsum(pcounts)
    poffs = pends - pcounts
    slot = incl - onehot + poffs[None, :]
    dest = jnp.stack([jnp.sum(jnp.where(oh0, slot, 0), axis=1), jnp.sum(jnp.where(oh1, slot, 0), axis=1)], axis=1)
    nblk = (T * MOE_TOPK) // MOE_ROWS + MOE_EXPERTS
    block_expert = jnp.minimum(
        jnp.searchsorted(pends, jnp.arange(nblk, dtype=jnp.int32) * MOE_ROWS, side="right"),
        MOE_EXPERTS - 1).astype(jnp.int32)
    n_used = (pends[-1:] // MOE_ROWS).astype(jnp.int32)
    used = counts > 0
    later = lax.cummin(jnp.where(used, ids, MOE_EXPERTS)[::-1])[::-1]
    next_expert = jnp.concatenate([later[1:], jnp.full((1,), MOE_EXPERTS, jnp.int32)])
    next_expert = jnp.where(next_expert >= MOE_EXPERTS, -1, next_expert).astype(jnp.int32)
    slot_of_expert = ((jnp.cumsum(used.astype(jnp.int32)) - 1) % 2).astype(jnp.int32)
    pad_start = ((poffs + counts) // 8 * 8).astype(jnp.int32)
    return dest.reshape(-1).astype(jnp.int32), pad_start, block_expert, n_used, next_expert, slot_of_expert, nblk


def kernel(x, norm1_gain, w_in, hg_lb_logits, hg_norm_gain, rel_bias, w_branch_a, w_branch_b, w_out,
           norm2_gain, w_router_group, b_router_group, w_router_expert, b_router_expert,
           w_exp_gate, w_exp_up, w_exp_down, final_norm_gain):
    B, S, D = x.shape
    T = B * S
    depth = norm1_gain.shape[0]
    lower_bounds = jnp.cumsum(jax.nn.softmax(hg_lb_logits.astype(F32), axis=0), axis=0)
    att_w = ATT_HEADS * HEAD_DIM
    bias = _attn_bias(rel_bias)
    x2d = x.reshape(T, D)
    for layer in range(depth):
        h = _rmsnorm(x2d, norm1_gain[layer], BF16)
        w = w_in[layer]
        zeros_lb = jnp.zeros((1, D), F32)
        seg = lambda a, n: w[:, a:a + n].astype(BF16)
        hm = dict(head_major=True, B=B, S=S)
        q_a = _inproj(h, seg(0, D), zeros_lb, mode="none", out_dtype=BF16, **hm)
        lf_a = _inproj(h, seg(D, D), lower_bounds[layer].reshape(1, D), mode="logf", out_dtype=F32, **hm)
        i_a = _inproj(h, seg(2 * D, D), zeros_lb, mode="none", out_dtype=BF16, **hm)
        sg_a = _inproj(h, seg(3 * D, D), zeros_lb, mode="sigmoid", out_dtype=BF16, **hm)
        qkv_b = _inproj(h, seg(4 * D, 3 * att_w), jnp.zeros((1, 3 * att_w), F32), mode="none",
                        out_dtype=BF16, tn=att_w, **hm)
        gates = _inproj(h, seg(4 * D + 3 * att_w, 2 * D), jnp.zeros((1, 2 * D), F32), mode="sigmoid",
                        head_major=False, out_dtype=BF16, B=B, S=S)

        y_a = _hgrn(q_a, lf_a, i_a, sg_a, hg_norm_gain[layer]).reshape(T, D)
        y_b = _attention(qkv_b, bias).reshape(T, ATT_HEADS_PER_GROUP * HEAD_DIM)
        merged = _merge(y_a, y_b, w_branch_a[layer].astype(BF16), w_branch_b[layer].astype(BF16), gates)

        wr = jnp.zeros((D, ROUTE_LANES), F32)
        wr = wr.at[:, :MOE_EXPERTS].set(w_router_expert[layer].astype(F32))
        wr = wr.at[:, MOE_EXPERTS:MOE_EXPERTS + MOE_GROUPS].set(w_router_group[layer].astype(F32))
        br = jnp.zeros((1, ROUTE_LANES), F32)
        br = br.at[0, :MOE_EXPERTS].set(b_router_expert[layer].astype(F32))
        br = br.at[0, MOE_EXPERTS:MOE_EXPERTS + MOE_GROUPS].set(b_router_group[layer].astype(F32))
        wr_hi = wr.astype(BF16)
        wr_lo = (wr - wr_hi.astype(F32)).astype(BF16)
        x1, h2, route = _outproj(merged, w_out[layer].astype(BF16), x2d, norm2_gain[layer], wr_hi, wr_lo, br)

        dest, pad_start, block_expert, n_used, next_expert, slot_of_expert, nblk = _route_metadata(route, T)
        xs = _dispatch(h2, dest, pad_start, nblk * MOE_ROWS)
        ys = _experts(xs, block_expert, n_used, next_expert, slot_of_expert,
                      w_exp_gate[layer], w_exp_up[layer], w_exp_down[layer])
        last = layer == depth - 1
        assert last, "the fused combine applies the final norm; deeper stacks need an un-normalised combine"
        x2d = _combine(x1, route, final_norm_gain, ys, dest)
    return x2d.reshape(B, S, D)
```

```python
import functools
import math

import numpy as np
import jax
import jax.numpy as jnp
from jax import lax
from jax.experimental import pallas as pl
from jax.experimental.pallas import tpu as pltpu

F32 = jnp.float32
BF16 = jnp.bfloat16

LANES = 128
NORM_EPS = 1e-6
HEAD_DIM = 128
HG_HEADS = 16
ATT_GROUPS = ((128, 1), (512, 4), (2048, 16))
ATT_HEADS_PER_GROUP = 4
ATT_HEADS = len(ATT_GROUPS) * ATT_HEADS_PER_GROUP
ATT_BLOCK = 128
ATT_UNROLL = 16
REL_BUCKETS = 32
REL_MAX_DIST = 2048
MOE_GROUPS = 8
MOE_EXPERTS_PER_GROUP = 8
MOE_EXPERTS = MOE_GROUPS * MOE_EXPERTS_PER_GROUP
MOE_TOPK = 2
MOE_ROWS = 256
NEG_BIG = -1e30
LOG2E = 1.4426950408889634
VMEM_LIMIT = 56 * 1024 * 1024


def _cparams(sem):
    return pltpu.CompilerParams(dimension_semantics=sem, vmem_limit_bytes=VMEM_LIMIT)


def _rmsnorm_kernel(x_ref, g_ref, o_ref):
    x = x_ref[...]
    ms = jnp.mean(x * x, axis=-1, keepdims=True)
    o_ref[...] = (x * lax.rsqrt(ms + NORM_EPS) * g_ref[...]).astype(o_ref.dtype)


def _rmsnorm(x2d, gain, out_dtype, tm=512):
    T, D = x2d.shape
    return pl.pallas_call(
        _rmsnorm_kernel,
        grid=(T // tm,),
        in_specs=[pl.BlockSpec((tm, D), lambda i: (i, 0)), pl.BlockSpec((1, D), lambda i: (0, 0))],
        out_specs=pl.BlockSpec((tm, D), lambda i: (i, 0)),
        out_shape=jax.ShapeDtypeStruct((T, D), out_dtype),
        compiler_params=_cparams(("parallel",)),
        name="rmsnorm",
    )(x2d, gain.reshape(1, D).astype(F32))


def _inproj_kernel(h_ref, w_ref, lb_ref, o_ref, *, mode, head_major):
    acc = jnp.dot(h_ref[...], w_ref[...], preferred_element_type=F32)
    if mode == "logf":
        lb = lb_ref[...]
        acc = jnp.log(lb + (1.0 - lb) * jax.nn.sigmoid(acc))
    elif mode == "sigmoid":
        acc = jax.nn.sigmoid(acc)
    if head_major:
        for hh in range(acc.shape[1] // HEAD_DIM):
            o_ref[0, hh] = acc[:, hh * HEAD_DIM:(hh + 1) * HEAD_DIM].astype(o_ref.dtype)
    else:
        o_ref[...] = acc.astype(o_ref.dtype)


def _inproj(h, w, lb, *, mode, head_major, out_dtype, B, S, tm=1024, tn=1024):
    T, D = h.shape
    N = w.shape[1]
    assert N % tn == 0 and T % tm == 0 and S % tm == 0
    spb = S // tm
    if head_major:
        out_shape = jax.ShapeDtypeStruct((B, N // HEAD_DIM, S, HEAD_DIM), out_dtype)
        out_spec = pl.BlockSpec((1, tn // HEAD_DIM, tm, HEAD_DIM), lambda j, i: (i // spb, j, i % spb, 0))
    else:
        out_shape = jax.ShapeDtypeStruct((T, N), out_dtype)
        out_spec = pl.BlockSpec((tm, tn), lambda j, i: (i, j))
    return pl.pallas_call(
        functools.partial(_inproj_kernel, mode=mode, head_major=head_major),
        grid=(N // tn, T // tm),
        in_specs=[
            pl.BlockSpec((tm, D), lambda j, i: (i, 0)),
            pl.BlockSpec((D, tn), lambda j, i: (0, j)),
            pl.BlockSpec((1, tn), lambda j, i: (0, j)),
        ],
        out_specs=out_spec,
        out_shape=out_shape,
        compiler_params=_cparams(("parallel", "parallel")),
        name="inproj_" + mode,
    )(h, w, lb)


HG_CHUNK = 256
HG_DIAG = 16
HG_HEADS_PER_STEP = 4


def _hgrn_levels(C):
    out, m = [], C // 2
    while m >= HG_DIAG:
        out.append(m)
        m //= 2
    return out


def _hgrn_masks(C):
    t = np.arange(C)[:, None]
    s = np.arange(C)[None, :]
    masks = []
    for m in _hgrn_levels(C):
        masks.append((t // (2 * m) == s // (2 * m)) & ((t // m) % 2 == 1) & ((s // m) % 2 == 0))
    masks.append((t // HG_DIAG == s // HG_DIAG) & (t >= s))
    total = np.sum(np.stack(masks).astype(np.int32), axis=0)
    assert np.array_equal(total, (t >= s).astype(np.int32))
    return np.stack(masks).astype(np.float32)


def _dot_nt(a, b):
    return lax.dot_general(a, b, (((1,), (1,)), ((), ())), preferred_element_type=F32)


def _dot_tn(a, b):
    return lax.dot_general(a, b, (((0,), (0,)), ((), ())), preferred_element_type=F32)


def _hgrn_kernel(q_ref, lf_ref, v_ref, sg_ref, gain_ref, tril_ref, mask_ref, o_ref, st_ref, b_ref, *, C):
    S = q_ref.shape[2]
    HP = q_ref.shape[1]
    levels = _hgrn_levels(C)
    nchunks = S // C
    tril = tril_ref[...]

    def ref_rows(h, block, row_of_block):
        parts = [jnp.broadcast_to(b_ref[h, pl.ds(row_of_block(p), 1), :], (block, HEAD_DIM))
                 for p in range(C // block)]
        return jnp.concatenate(parts, axis=0)

    def cumsum(c):
        r0 = pl.multiple_of(c * C, C)
        out = []
        for h in range(HP):
            lf2 = lf_ref[0, h, pl.ds(r0, C), :] * LOG2E
            hi = lf2.astype(BF16)
            lo = (lf2 - hi.astype(F32)).astype(BF16)
            r = jnp.dot(tril, jnp.concatenate([hi, lo], axis=1), preferred_element_type=F32)
            out.append(r[:, :HEAD_DIM] + r[:, HEAD_DIM:])
        return tuple(out)

    st_ref[...] = jnp.zeros_like(st_ref)

    def chunk(c, b_all):
        r0 = pl.multiple_of(c * C, C)
        rows = pl.ds(r0, C)
        b_next = cumsum(jnp.minimum(c + 1, nchunks - 1))
        for h in range(HP):
            b = b_all[h]
            b_ref[h] = b
            q = q_ref[0, h, rows, :].astype(F32)
            kk = 1.0 - jnp.exp(lf_ref[0, h, rows, :])
            b_last = b_ref[h, pl.ds(C - 1, 1), :]
            scores = jnp.zeros((C, C), F32)
            for li, m in enumerate(levels):
                d = b - ref_rows(h, 2 * m, lambda p, m=m: 2 * m * p + m - 1)
                e = jnp.exp2(-jnp.abs(d))
                scores = scores + mask_ref[li] * _dot_nt((q * e).astype(BF16), (kk * e).astype(BF16))
            d = b - ref_rows(h, HG_DIAG, lambda p: HG_DIAG * p + HG_DIAG // 2 - 1)
            scores = scores + mask_ref[len(levels)] * _dot_nt((q * jnp.exp2(d)).astype(BF16),
                                                               (kk * jnp.exp2(-d)).astype(BF16))
            v = v_ref[0, h, rows, :]
            st = st_ref[h]
            qe = (q * jnp.exp2(b)).astype(BF16)
            kd = (kk * jnp.exp2(b_last - b)).astype(BF16)
            o = jnp.dot(scores.astype(BF16), v, preferred_element_type=F32) + _dot_nt(qe, st.astype(BF16))
            st_ref[h] = st * jnp.exp2(b_last) + _dot_tn(v, kd)
            ms = jnp.mean(o * o, axis=-1, keepdims=True)
            y = o * lax.rsqrt(ms + NORM_EPS) * gain_ref[:, h * HEAD_DIM:(h + 1) * HEAD_DIM]
            y = y * sg_ref[0, h, rows, :].astype(F32)
            o_ref[0, rows, h * HEAD_DIM:(h + 1) * HEAD_DIM] = y.astype(o_ref.dtype)
        return b_next

    lax.fori_loop(0, nchunks, chunk, cumsum(0))


def _hgrn(q, lf, v, sg, gain):
    B, H, S, E = q.shape
    C, HP = HG_CHUNK, HG_HEADS_PER_STEP
    masks = jnp.asarray(_hgrn_masks(C))
    tril = jnp.asarray(np.tril(np.ones((C, C), np.float32)), dtype=BF16)
    head_spec = pl.BlockSpec((1, HP, S, E), lambda b, h: (b, h, 0, 0))
    return pl.pallas_call(
        functools.partial(_hgrn_kernel, C=C),
        grid=(B, H // HP),
        in_specs=[
            head_spec, head_spec, head_spec, head_spec,
            pl.BlockSpec((1, HP * E), lambda b, h: (0, h)),
            pl.BlockSpec((C, C), lambda b, h: (0, 0)),
            pl.BlockSpec(masks.shape, lambda b, h: (0, 0, 0)),
        ],
        out_specs=pl.BlockSpec((1, S, HP * E), lambda b, h: (b, 0, h)),
        out_shape=jax.ShapeDtypeStruct((B, S, H * E), BF16),
        scratch_shapes=[pltpu.VMEM((HP, E, E), F32), pltpu.VMEM((HP, C, E), F32)],
        compiler_params=_cparams(("parallel", "parallel")),
        name="hgrn2",
    )(q, lf, v, sg, gain.reshape(1, H * E).astype(F32), tril, masks)


def _t5_bucket_np(dist):
    exact = REL_BUCKETS // 2
    d_f = np.maximum(dist, 1).astype(np.float32)
    log_b = exact + (np.log(d_f / np.float32(exact)) / np.float32(math.log(REL_MAX_DIST / exact))
                     * np.float32(REL_BUCKETS - exact)).astype(np.int32)
    return np.where(dist < exact, dist, np.minimum(log_b, REL_BUCKETS - 1))


def _attn_bias(rel_bias):
    blk = ATT_BLOCK
    period = 3 * blk
    out = []
    for gi, (window, dilation) in enumerate(ATT_GROUPS):
        n_back = window // dilation
        assert n_back <= blk
        hs = slice(gi * ATT_HEADS_PER_GROUP, (gi + 1) * ATT_HEADS_PER_GROUP)
        bucket = _t5_bucket_np(np.arange(n_back + 1) * dilation)
        by_delta = rel_bias[:, hs][bucket].astype(F32).T
        u = jnp.full((ATT_HEADS_PER_GROUP, period), NEG_BIG, F32)
        u = u.at[:, 2 * blk - 1 - n_back:2 * blk].set(by_delta[:, ::-1])
        flat = jnp.tile(u, (1, blk))[:, :blk * (period - 1)]
        out.append(flat.reshape(ATT_HEADS_PER_GROUP, blk, period - 1)[:, :, blk - 1:3 * blk - 1])
    return jnp.stack(out, axis=0)


def _attn_kernel(q_ref, k_ref, v_ref, bias_ref, o_ref, qf, kf, vf, og, lg):
    S = q_ref.shape[3]
    scale = HEAD_DIM ** -0.5
    blk = ATT_BLOCK

    for g, (window, d) in enumerate(ATT_GROUPS):
        L = S // d
        nb = L // blk
        if d > 1:
            qf[...] = q_ref[0, g, 0].astype(F32)
            kf[...] = k_ref[0, g, 0].astype(F32)
            vf[...] = v_ref[0, g, 0].astype(F32)

        def load(ref_bf, ref_f32, start, size, g=g, d=d):
            if d == 1:
                return ref_bf[0, g, 0, pl.ds(start, size), :]
            return ref_f32[pl.ds(start, size, stride=d), :].astype(BF16)

        def block(r, n, g=g, d=d, load=load):
            start = r + n * blk * d
            rows = pl.ds(start, blk, stride=d) if d > 1 else pl.ds(start, blk)
            q = load(q_ref, qf, start, blk)
            if n == 0:
                k = load(k_ref, kf, start, blk)
                v = load(v_ref, vf, start, blk)
                s = _dot_nt(q, k) * scale + bias_ref[g, 0, :, blk:]
            else:
                first = start - blk * d
                k = load(k_ref, kf, first, 2 * blk)
                v = load(v_ref, vf, first, 2 * blk)
                s = _dot_nt(q, k) * scale + bias_ref[g, 0]
            m = jnp.max(s, axis=-1, keepdims=True)
            p = jnp.exp(s - m)
            den = jnp.sum(p, axis=-1, keepdims=True)
            o = jnp.dot(p.astype(BF16), v, preferred_element_type=F32) / den
            og[g, rows, :] = o
            lg[g, rows, :] = jnp.broadcast_to(m + jnp.log(den), (blk, HEAD_DIM))

        per_body = max(ATT_UNROLL // nb, 1)

        def residues(i, carry, nb=nb, block=block, per_body=per_body):
            for k in range(per_body):
                for n in range(nb):
                    block(i * per_body + k, n)
            return carry

        if d <= per_body:
            residues(0, 0, per_body=d)
        else:
            lax.fori_loop(0, d // per_body, residues, 0)

    rows_per_step = 256

    def mix(i, carry):
        rows = pl.ds(pl.multiple_of(i * rows_per_step, rows_per_step), rows_per_step)
        l0, l1, l2 = lg[0, rows, :], lg[1, rows, :], lg[2, rows, :]
        mx = jnp.maximum(jnp.maximum(l0, l1), l2)
        e0, e1, e2 = jnp.exp(l0 - mx), jnp.exp(l1 - mx), jnp.exp(l2 - mx)
        num = e0 * og[0, rows, :] + e1 * og[1, rows, :] + e2 * og[2, rows, :]
        o_ref[0, rows, :] = (num / (e0 + e1 + e2)).astype(o_ref.dtype)
        return carry

    lax.fori_loop(0, S // rows_per_step, mix, 0)


def _attention(qkv, bias):
    B, _, S, E = qkv.shape
    G, HG = len(ATT_GROUPS), ATT_HEADS_PER_GROUP
    x = qkv.reshape(B, 3, G, HG, S, E)

    def spec(which):
        return pl.BlockSpec((1, None, G, 1, S, E), lambda b, h, which=which: (b, which, 0, h, 0, 0))

    def kern(q_ref, k_ref, v_ref, bias_ref, o_ref, *scratch):
        _attn_kernel(q_ref, k_ref, v_ref, bias_ref, o_ref, *scratch)

    return pl.pallas_call(
        kern,
        grid=(B, HG),
        in_specs=[spec(0), spec(1), spec(2),
                  pl.BlockSpec((G, 1, ATT_BLOCK, 2 * ATT_BLOCK), lambda b, h: (0, h, 0, 0))],
        out_specs=pl.BlockSpec((1, S, E), lambda b, h: (b, 0, h)),
        out_shape=jax.ShapeDtypeStruct((B, S, HG * E), BF16),
        scratch_shapes=[pltpu.VMEM((S, E), F32)] * 3 + [pltpu.VMEM((G, S, E), F32)] * 2,
        compiler_params=_cparams(("parallel", "parallel")),
        name="dilated_attn",
    )(x, x, x, bias)


def _merge_kernel(ya_ref, yb_ref, wa_ref, wb_ref, ga_ref, gb_ref, o_ref):
    a = jnp.dot(ya_ref[...], wa_ref[...], preferred_element_type=F32)
    b = jnp.dot(yb_ref[...], wb_ref[...], preferred_element_type=F32)
    o_ref[...] = (ga_ref[...].astype(F32) * a + gb_ref[...].astype(F32) * b).astype(o_ref.dtype)


def _merge(ya, yb, wa, wb, gates, tm=1024, tn=1024):
    T, Ka = ya.shape
    Kb = yb.shape[1]
    N = wa.shape[1]
    nj = N // tn
    return pl.pallas_call(
        _merge_kernel,
        grid=(nj, T // tm),
        in_specs=[
            pl.BlockSpec((tm, Ka), lambda j, i: (i, 0)),
            pl.BlockSpec((tm, Kb), lambda j, i: (i, 0)),
            pl.BlockSpec((Ka, tn), lambda j, i: (0, j)),
            pl.BlockSpec((Kb, tn), lambda j, i: (0, j)),
            pl.BlockSpec((tm, tn), lambda j, i: (i, j)),
            pl.BlockSpec((tm, tn), lambda j, i, nj=nj: (i, j + nj)),
        ],
        out_specs=pl.BlockSpec((tm, tn), lambda j, i: (i, j)),
        out_shape=jax.ShapeDtypeStruct((T, N), BF16),
        compiler_params=_cparams(("parallel", "parallel")),
        name="branch_merge",
    )(ya, yb, wa, wb, gates, gates)


ROUTE_LANES = LANES


def _pack_halves(x_bf):
    bits = pltpu.bitcast(x_bf.astype(F32), jnp.uint32)
    n = bits.shape[1] // 2
    return (bits[:, :n] >> 16) | (bits[:, n:] & jnp.uint32(0xFFFF0000))


def _unpack_halves(p):
    lo = pltpu.bitcast(p << 16, F32)
    hi = pltpu.bitcast(p & jnp.uint32(0xFFFF0000), F32)
    return jnp.concatenate([lo, hi], axis=1)


def _outproj_kernel(m_ref, w_ref, x_ref, g_ref, wrh_ref, wrl_ref, br_ref, x1_ref, h2_ref, rt_ref):
    x1 = x_ref[...] + jnp.dot(m_ref[...], w_ref[...], preferred_element_type=F32)
    x1_ref[...] = x1
    ms = jnp.mean(x1 * x1, axis=-1, keepdims=True)
    h2 = x1 * lax.rsqrt(ms + NORM_EPS) * g_ref[...]
    h_hi = h2.astype(BF16)
    h2_ref[...] = _pack_halves(h_hi)
    h_lo = (h2 - h_hi.astype(F32)).astype(BF16)
    wrh = wrh_ref[...]
    lg = (jnp.dot(h_hi, wrh, preferred_element_type=F32) + jnp.dot(h_lo, wrh, preferred_element_type=F32)
          + jnp.dot(h_hi, wrl_ref[...], preferred_element_type=F32)) + br_ref[...]

    lane = lax.broadcasted_iota(jnp.int32, lg.shape, 1)
    lane_f = lane.astype(F32)
    big = float(ROUTE_LANES)
    is_group = (lane >= MOE_EXPERTS) & (lane < MOE_EXPERTS + MOE_GROUPS)
    lgg = jnp.where(is_group, lg, -jnp.inf)
    gmax = jnp.max(lgg, axis=-1, keepdims=True)
    gsel = jnp.min(jnp.where(lgg == gmax, lane_f - MOE_EXPERTS, big), axis=-1, keepdims=True)
    pg = 1.0 / jnp.sum(jnp.where(is_group, jnp.exp(lg - gmax), 0.0), axis=-1, keepdims=True)

    in_group = (lane < MOE_EXPERTS) & ((lane // MOE_EXPERTS_PER_GROUP).astype(F32) == gsel)
    le = jnp.where(in_group, lg, -jnp.inf)
    t1 = jnp.max(le, axis=-1, keepdims=True)
    i1 = jnp.min(jnp.where(le == t1, lane_f, big), axis=-1, keepdims=True)
    le2 = jnp.where(lane_f == i1, -jnp.inf, le)
    t2 = jnp.max(le2, axis=-1, keepdims=True)
    i2 = jnp.min(jnp.where(le2 == t2, lane_f, big), axis=-1, keepdims=True)
    e2 = jnp.exp(t2 - t1)
    w1 = pg / (1.0 + e2)
    w2 = pg * e2 / (1.0 + e2)
    rt_ref[...] = jnp.where(lane == 0, i1, jnp.where(lane == 1, i2, jnp.where(lane == 2, w1,
                            jnp.where(lane == 3, w2, 0.0))))


def _outproj(merged, w_out, x2d, gain, wr_hi, wr_lo, br, tm=256):
    T, D = x2d.shape
    row = lambda i: (i, 0)
    const = lambda i: (0, 0)
    return pl.pallas_call(
        _outproj_kernel,
        grid=(T // tm,),
        in_specs=[
            pl.BlockSpec((tm, D), row), pl.BlockSpec((D, D), const), pl.BlockSpec((tm, D), row),
            pl.BlockSpec((1, D), const), pl.BlockSpec((D, ROUTE_LANES), const),
            pl.BlockSpec((D, ROUTE_LANES), const), pl.BlockSpec((1, ROUTE_LANES), const),
        ],
        out_specs=[pl.BlockSpec((tm, D), row), pl.BlockSpec((tm, D // 2), row),
                   pl.BlockSpec((tm, ROUTE_LANES), row)],
        out_shape=[jax.ShapeDtypeStruct((T, D), F32), jax.ShapeDtypeStruct((T, D // 2), jnp.uint32),
                   jax.ShapeDtypeStruct((T, ROUTE_LANES), F32)],
        compiler_params=_cparams(("parallel",)),
        name="outproj_router",
    )(merged, w_out, x2d, gain.reshape(1, D).astype(F32), wr_hi, wr_lo, br)


def _dispatch_kernel(dest_ref, zb_ref, h_ref, xs_ref, zeros, sem, *, tq):
    base = pl.program_id(0) * tq
    n_blocks = xs_ref.shape[0] // MOE_ROWS

    @pl.when(pl.program_id(0) == 0)
    def _():
        zeros[...] = jnp.zeros_like(zeros)

        def zero_copy(b):
            return pltpu.make_async_copy(zeros, xs_ref.at[pl.ds(pl.multiple_of(b * MOE_ROWS, MOE_ROWS), MOE_ROWS)],
                                         sem)

        def start(b, carry):
            @pl.when(zb_ref[b] != 0)
            def _():
                zero_copy(b).start()
            return carry

        def wait(b, carry):
            @pl.when(zb_ref[b] != 0)
            def _():
                zero_copy(b).wait()
            return carry

        lax.fori_loop(0, n_blocks, start, 0)
        lax.fori_loop(0, n_blocks, wait, 0)

    def copy(j, k):
        return pltpu.make_async_copy(h_ref.at[pl.ds(j, 1)], xs_ref.at[pl.ds(dest_ref[2 * (base + j) + k], 1)], sem)

    def issue(j, carry):
        copy(j, 0).start()
        copy(j, 1).start()
        return carry

    lax.fori_loop(0, tq, issue, 0)
    for _ in range(MOE_TOPK):
        pltpu.make_async_copy(h_ref, xs_ref.at[pl.ds(0, tq)], sem).wait()


def _dispatch(h2, dest_flat, zero_block, n_rows, tq=512):
    T, D = h2.shape
    return pl.pallas_call(
        functools.partial(_dispatch_kernel, tq=tq),
        grid_spec=pltpu.PrefetchScalarGridSpec(
            num_scalar_prefetch=2,
            grid=(T // tq,),
            in_specs=[pl.BlockSpec((tq, D), lambda i, d, z: (i, 0))],
            out_specs=pl.BlockSpec(memory_space=pl.ANY),
            scratch_shapes=[pltpu.VMEM((MOE_ROWS, D), h2.dtype), pltpu.SemaphoreType.DMA(())],
        ),
        out_shape=jax.ShapeDtypeStruct((n_rows, D), h2.dtype),
        compiler_params=_cparams(("arbitrary",)),
        name="moe_dispatch",
    )(dest_flat, zero_block, h2)


def _expert_kernel(be_ref, nu_ref, nxt_ref, slot_ref, xs_ref, wg_hbm, wu_hbm, wd_hbm, o_ref,
                   wg_f, wu_f, wd_f, wg_s, wu_s, wd_s, sem):
    i = pl.program_id(0)
    e = be_ref[i]
    active = i < nu_ref[0]
    new_expert = active & ((i == 0) | (e != be_ref[jnp.maximum(i - 1, 0)]))
    slot = slot_ref[e]

    def weight_copies(expert, s):
        return (pltpu.make_async_copy(wg_hbm.at[expert], wg_f.at[s], sem.at[s, 0]),
                pltpu.make_async_copy(wu_hbm.at[expert], wu_f.at[s], sem.at[s, 1]),
                pltpu.make_async_copy(wd_hbm.at[expert], wd_f.at[s], sem.at[s, 2]))

    @pl.when(active & (i == 0))
    def _():
        for c in weight_copies(e, slot):
            c.start()

    @pl.when(new_expert)
    def _():
        nxt = nxt_ref[e]

        @pl.when(nxt >= 0)
        def _():
            for c in weight_copies(nxt, 1 - slot):
                c.start()

        for c in weight_copies(e, slot):
            c.wait()
        wg_s[...] = wg_f[slot].astype(BF16)
        wu_s[...] = wu_f[slot].astype(BF16)
        wd_s[...] = wd_f[slot].astype(BF16)

    @pl.when(active)
    def _():
        x = _unpack_halves(xs_ref[...]).astype(BF16)
        g = jnp.dot(x, wg_s[...], preferred_element_type=F32)
        u = jnp.dot(x, wu_s[...], preferred_element_type=F32)
        a = (g * jax.nn.sigmoid(g) * u).astype(BF16)
        y = jnp.dot(a, wd_s[...], preferred_element_type=F32)
        o_ref[...] = _pack_halves(y.astype(BF16))

    @pl.when(jnp.logical_not(active))
    def _():
        o_ref[...] = jnp.zeros_like(o_ref)


def _experts(xs, block_expert, n_used, next_expert, slot_of_expert, w_gate, w_up, w_down):
    Dh = xs.shape[1]
    D, F = w_gate.shape[1], w_gate.shape[2]
    nblk = block_expert.shape[0]
    P = nblk * MOE_ROWS
    rows = lambda i, be, nu, nx, sl: (jnp.minimum(i, nu[0] - 1), 0)
    hbm = pl.BlockSpec(memory_space=pl.ANY)
    return pl.pallas_call(
        _expert_kernel,
        grid_spec=pltpu.PrefetchScalarGridSpec(
            num_scalar_prefetch=4,
            grid=(nblk,),
            in_specs=[pl.BlockSpec((MOE_ROWS, Dh), rows), hbm, hbm, hbm],
            out_specs=pl.BlockSpec((MOE_ROWS, Dh), lambda i, be, nu, nx, sl: (i, 0)),
            scratch_shapes=[pltpu.VMEM((2, D, F), F32), pltpu.VMEM((2, D, F), F32), pltpu.VMEM((2, F, D), F32),
                            pltpu.VMEM((D, F), BF16), pltpu.VMEM((D, F), BF16), pltpu.VMEM((F, D), BF16),
                            pltpu.SemaphoreType.DMA((2, 3))],
        ),
        out_shape=jax.ShapeDtypeStruct((P, Dh), jnp.uint32),
        compiler_params=_cparams(("arbitrary",)),
        name="moe_experts",
    )(block_expert, n_used, next_expert, slot_of_expert, xs, w_gate, w_up, w_down)


def _combine_kernel(dest_ref, x1_ref, rt_ref, g_ref, ys_ref, o_ref, buf, sem, *, tq):
    i = pl.program_id(0)
    n = pl.num_programs(0)

    def issue(step, s):
        def body(j, carry):
            for k in range(MOE_TOPK):
                pltpu.make_async_copy(ys_ref.at[pl.ds(dest_ref[2 * (step * tq + j) + k], 1)],
                                      buf.at[s, k, pl.ds(j, 1)], sem.at[s]).start()
            return carry
        lax.fori_loop(0, tq, body, 0)

    @pl.when(i == 0)
    def _():
        issue(0, 0)

    for s in range(2):
        @pl.when((i + 1 < n) & ((i + 1) % 2 == s))
        def _():
            issue(i + 1, s)

    for s in range(2):
        @pl.when(i % 2 == s)
        def _():
            for k in range(MOE_TOPK):
                pltpu.make_async_copy(ys_ref.at[pl.ds(0, tq)], buf.at[s, k], sem.at[s]).wait()
            rt = rt_ref[...]
            x = x1_ref[...] + rt[:, 2:3] * _unpack_halves(buf[s, 0]) + rt[:, 3:4] * _unpack_halves(buf[s, 1])
            ms = jnp.mean(x * x, axis=-1, keepdims=True)
            o_ref[...] = x * lax.rsqrt(ms + NORM_EPS) * g_ref[...]


def _combine(x1, route, gain, ys, dest_flat, tq=256):
    T, D = x1.shape
    return pl.pallas_call(
        functools.partial(_combine_kernel, tq=tq),
        grid_spec=pltpu.PrefetchScalarGridSpec(
            num_scalar_prefetch=1,
            grid=(T // tq,),
            in_specs=[
                pl.BlockSpec((tq, D), lambda i, d: (i, 0)),
                pl.BlockSpec((tq, ROUTE_LANES), lambda i, d: (i, 0)),
                pl.BlockSpec((1, D), lambda i, d: (0, 0)),
                pl.BlockSpec(memory_space=pl.ANY),
            ],
            out_specs=pl.BlockSpec((tq, D), lambda i, d: (i, 0)),
            scratch_shapes=[pltpu.VMEM((2, MOE_TOPK, tq, D // 2), jnp.uint32), pltpu.SemaphoreType.DMA((2,))],
        ),
        out_shape=jax.ShapeDtypeStruct((T, D), F32),
        compiler_params=_cparams(("arbitrary",)),
        name="moe_combine",
    )(dest_flat, x1, route, gain.reshape(1, D).astype(F32), ys)


def _route_metadata(route, T):
    e = route[:, :MOE_TOPK].astype(jnp.int32)
    ids = jnp.arange(MOE_EXPERTS, dtype=jnp.int32)
    oh0 = e[:, 0, None] == ids
    oh1 = e[:, 1, None] == ids
    onehot = (oh0 | oh1).astype(jnp.int32)
    incl = jnp.cumsum(onehot, axis=0)
    counts = incl[-1]
    pcounts = (counts + MOE_ROWS - 1) // MOE_ROWS * MOE_ROWS
    pends = jnp.cumsum(pcounts)
    poffs = pends - pcounts
    slot = incl - onehot + poffs[None, :]
    dest = jnp.stack([jnp.sum(jnp.where(oh0, slot, 0), axis=1), jnp.sum(jnp.where(oh1, slot, 0), axis=1)], axis=1)
    nblk = (T * MOE_TOPK) // MOE_ROWS + MOE_EXPERTS
    block_expert = jnp.minimum(
        jnp.searchsorted(pends, jnp.arange(nblk, dtype=jnp.int32) * MOE_ROWS, side="right"),
        MOE_EXPERTS - 1).astype(jnp.int32)
    n_used = (pends[-1:] // MOE_ROWS).astype(jnp.int32)
    used = counts > 0
    later = lax.cummin(jnp.where(used, ids, MOE_EXPERTS)[::-1])[::-1]
    next_expert = jnp.concatenate([later[1:], jnp.full((1,), MOE_EXPERTS, jnp.int32)])
    next_expert = jnp.where(next_expert >= MOE_EXPERTS, -1, next_expert).astype(jnp.int32)
    slot_of_expert = ((jnp.cumsum(used.astype(jnp.int32)) - 1) % 2).astype(jnp.int32)
    blk = jnp.arange(nblk, dtype=jnp.int32)
    following = jnp.concatenate([block_expert[1:], block_expert[-1:]])
    zero_block = ((blk >= n_used[0] - 1) | (block_expert != following)).astype(jnp.int32)
    return dest.reshape(-1).astype(jnp.int32), zero_block, block_expert, n_used, next_expert, slot_of_expert, nblk


def kernel(x, norm1_gain, w_in, hg_lb_logits, hg_norm_gain, rel_bias, w_branch_a, w_branch_b, w_out,
           norm2_gain, w_router_group, b_router_group, w_router_expert, b_router_expert,
           w_exp_gate, w_exp_up, w_exp_down, final_norm_gain):
    B, S, D = x.shape
    T = B * S
    depth = norm1_gain.shape[0]
    lower_bounds = jnp.cumsum(jax.nn.softmax(hg_lb_logits.astype(F32), axis=0), axis=0)
    att_w = ATT_HEADS * HEAD_DIM
    bias = _attn_bias(rel_bias)
    x2d = x.reshape(T, D)
    for layer in range(depth):
        h = _rmsnorm(x2d, norm1_gain[layer], BF16)
        w = w_in[layer]
        zeros_lb = jnp.zeros((1, D), F32)
        seg = lambda a, n: w[:, a:a + n].astype(BF16)
        hm = dict(head_major=True, B=B, S=S)
        q_a = _inproj(h, seg(0, D), zeros_lb, mode="none", out_dtype=BF16, **hm)
        lf_a = _inproj(h, seg(D, D), lower_bounds[layer].reshape(1, D), mode="logf", out_dtype=F32, **hm)
        i_a = _inproj(h, seg(2 * D, D), zeros_lb, mode="none", out_dtype=BF16, **hm)
        sg_a = _inproj(h, seg(3 * D, D), zeros_lb, mode="sigmoid", out_dtype=BF16, **hm)
        qkv_b = _inproj(h, seg(4 * D, 3 * att_w), jnp.zeros((1, 3 * att_w), F32), mode="none",
                        out_dtype=BF16, tn=att_w, **hm)
        gates = _inproj(h, seg(4 * D + 3 * att_w, 2 * D), jnp.zeros((1, 2 * D), F32), mode="sigmoid",
                        head_major=False, out_dtype=BF16, B=B, S=S)

        y_a = _hgrn(q_a, lf_a, i_a, sg_a, hg_norm_gain[layer]).reshape(T, D)
        y_b = _attention(qkv_b, bias).reshape(T, ATT_HEADS_PER_GROUP * HEAD_DIM)
        merged = _merge(y_a, y_b, w_branch_a[layer].astype(BF16), w_branch_b[layer].astype(BF16), gates)

        wr = jnp.zeros((D, ROUTE_LANES), F32)
        wr = wr.at[:, :MOE_EXPERTS].set(w_router_expert[layer].astype(F32))
        wr = wr.at[:, MOE_EXPERTS:MOE_EXPERTS + MOE_GROUPS].set(w_router_group[layer].astype(F32))
        br = jnp.zeros((1, ROUTE_LANES), F32)
        br = br.at[0, :MOE_EXPERTS].set(b_router_expert[layer].astype(F32))
        br = br.at[0, MOE_EXPERTS:MOE_EXPERTS + MOE_GROUPS].set(b_router_group[layer].astype(F32))
        wr_hi = wr.astype(BF16)
        wr_lo = (wr - wr_hi.astype(F32)).astype(BF16)
        x1, h2, route = _outproj(merged, w_out[layer].astype(BF16), x2d, norm2_gain[layer], wr_hi, wr_lo, br)

        dest, zero_block, block_expert, n_used, next_expert, slot_of_expert, nblk = _route_metadata(route, T)
        xs = _dispatch(h2, dest, zero_block, nblk * MOE_ROWS)
        ys = _experts(xs, block_expert, n_used, next_expert, slot_of_expert,
                      w_exp_gate[layer], w_exp_up[layer], w_exp_down[layer])
        last = layer == depth - 1
        assert last, "the fused combine applies the final norm; deeper stacks need an un-normalised combine"
        x2d = _combine(x1, route, final_norm_gain, ys, dest)
    return x2d.reshape(B, S, D)
```

```python
import functools
import math

import numpy as np
import jax
import jax.numpy as jnp
from jax import lax
from jax.experimental import pallas as pl
from jax.experimental.pallas import tpu as pltpu

F32 = jnp.float32
BF16 = jnp.bfloat16

LANES = 128
NORM_EPS = 1e-6
HEAD_DIM = 128
HG_HEADS = 16
ATT_GROUPS = ((128, 1), (512, 4), (2048, 16))
ATT_HEADS_PER_GROUP = 4
ATT_HEADS = len(ATT_GROUPS) * ATT_HEADS_PER_GROUP
ATT_BLOCK = 128
ATT_UNROLL = 16
REL_BUCKETS = 32
REL_MAX_DIST = 2048
MOE_GROUPS = 8
MOE_EXPERTS_PER_GROUP = 8
MOE_EXPERTS = MOE_GROUPS * MOE_EXPERTS_PER_GROUP
MOE_TOPK = 2
MOE_ROWS = 256
NEG_BIG = -1e30
LOG2E = 1.4426950408889634
VMEM_LIMIT = 56 * 1024 * 1024


def _cparams(sem):
    return pltpu.CompilerParams(dimension_semantics=sem, vmem_limit_bytes=VMEM_LIMIT)


def _normproj_kernel(x_ref, g_ref, w_ref, h_ref, o_ref):
    @pl.when(pl.program_id(1) == 0)
    def _():
        x = x_ref[...]
        ms = jnp.mean(x * x, axis=-1, keepdims=True)
        h_ref[...] = (x * lax.rsqrt(ms + NORM_EPS) * g_ref[...]).astype(h_ref.dtype)

    acc = jnp.dot(h_ref[...], w_ref[...], preferred_element_type=F32)
    for hh in range(acc.shape[1] // HEAD_DIM):
        o_ref[0, hh] = acc[:, hh * HEAD_DIM:(hh + 1) * HEAD_DIM].astype(o_ref.dtype)


def _normproj(x2d, gain, w, *, B, S, tm=1024, tn=1024):
    T, D = x2d.shape
    N = w.shape[1]
    assert N % tn == 0 and S % tm == 0
    spb = S // tm
    return pl.pallas_call(
        _normproj_kernel,
        grid=(T // tm, N // tn),
        in_specs=[
            pl.BlockSpec((tm, D), lambda i, j: (i, 0)),
            pl.BlockSpec((1, D), lambda i, j: (0, 0)),
            pl.BlockSpec((D, tn), lambda i, j: (0, j)),
        ],
        out_specs=[
            pl.BlockSpec((tm, D), lambda i, j: (i, 0)),
            pl.BlockSpec((1, tn // HEAD_DIM, tm, HEAD_DIM), lambda i, j: (i // spb, j, i % spb, 0)),
        ],
        out_shape=[jax.ShapeDtypeStruct((T, D), BF16),
                   jax.ShapeDtypeStruct((B, N // HEAD_DIM, S, HEAD_DIM), BF16)],
        compiler_params=_cparams(("parallel", "arbitrary")),
        name="norm_inproj",
    )(x2d, gain.reshape(1, D).astype(F32), w)


def _inproj_kernel(h_ref, w_ref, lb_ref, o_ref, *, mode, head_major):
    acc = jnp.dot(h_ref[...], w_ref[...], preferred_element_type=F32)
    if mode == "logf":
        lb = lb_ref[...]
        acc = jnp.log(lb + (1.0 - lb) * jax.nn.sigmoid(acc))
    elif mode == "sigmoid":
        acc = jax.nn.sigmoid(acc)
    if head_major:
        for hh in range(acc.shape[1] // HEAD_DIM):
            o_ref[0, hh] = acc[:, hh * HEAD_DIM:(hh + 1) * HEAD_DIM].astype(o_ref.dtype)
    else:
        o_ref[...] = acc.astype(o_ref.dtype)


def _inproj(h, w, lb, *, mode, head_major, out_dtype, B, S, tm=1024, tn=1024):
    T, D = h.shape
    N = w.shape[1]
    assert N % tn == 0 and T % tm == 0 and S % tm == 0
    spb = S // tm
    if head_major:
        out_shape = jax.ShapeDtypeStruct((B, N // HEAD_DIM, S, HEAD_DIM), out_dtype)
        out_spec = pl.BlockSpec((1, tn // HEAD_DIM, tm, HEAD_DIM), lambda j, i: (i // spb, j, i % spb, 0))
    else:
        out_shape = jax.ShapeDtypeStruct((T, N), out_dtype)
        out_spec = pl.BlockSpec((tm, tn), lambda j, i: (i, j))
    return pl.pallas_call(
        functools.partial(_inproj_kernel, mode=mode, head_major=head_major),
        grid=(N // tn, T // tm),
        in_specs=[
            pl.BlockSpec((tm, D), lambda j, i: (i, 0)),
            pl.BlockSpec((D, tn), lambda j, i: (0, j)),
            pl.BlockSpec((1, tn), lambda j, i: (0, j)),
        ],
        out_specs=out_spec,
        out_shape=out_shape,
        compiler_params=_cparams(("parallel", "parallel")),
        name="inproj_" + mode,
    )(h, w, lb)


HG_CHUNK = 256
HG_DIAG = 16
HG_HEADS_PER_STEP = 4


def _hgrn_levels(C):
    out, m = [], C // 2
    while m >= HG_DIAG:
        out.append(m)
        m //= 2
    return out


def _hgrn_masks(C):
    t = np.arange(C)[:, None]
    s = np.arange(C)[None, :]
    masks = []
    for m in _hgrn_levels(C):
        masks.append((t // (2 * m) == s // (2 * m)) & ((t // m) % 2 == 1) & ((s // m) % 2 == 0))
    masks.append((t // HG_DIAG == s // HG_DIAG) & (t >= s))
    total = np.sum(np.stack(masks).astype(np.int32), axis=0)
    assert np.array_equal(total, (t >= s).astype(np.int32))
    return np.stack(masks).astype(np.float32)


def _dot_nt(a, b):
    return lax.dot_general(a, b, (((1,), (1,)), ((), ())), preferred_element_type=F32)


def _dot_tn(a, b):
    return lax.dot_general(a, b, (((0,), (0,)), ((), ())), preferred_element_type=F32)


def _hgrn_kernel(q_ref, lf_ref, v_ref, sg_ref, gain_ref, tril_ref, mask_ref, o_ref, st_ref, b_ref, *, C):
    S = q_ref.shape[2]
    HP = q_ref.shape[1]
    levels = _hgrn_levels(C)
    nchunks = S // C
    tril = tril_ref[...]

    def ref_rows(h, block, row_of_block):
        parts = [jnp.broadcast_to(b_ref[h, pl.ds(row_of_block(p), 1), :], (block, HEAD_DIM))
                 for p in range(C // block)]
        return jnp.concatenate(parts, axis=0)

    def cumsum(c):
        r0 = pl.multiple_of(c * C, C)
        out = []
        for h in range(HP):
            lf2 = lf_ref[0, h, pl.ds(r0, C), :] * LOG2E
            hi = lf2.astype(BF16)
            lo = (lf2 - hi.astype(F32)).astype(BF16)
            r = jnp.dot(tril, jnp.concatenate([hi, lo], axis=1), preferred_element_type=F32)
            out.append(r[:, :HEAD_DIM] + r[:, HEAD_DIM:])
        return tuple(out)

    st_ref[...] = jnp.zeros_like(st_ref)

    def chunk(c, b_all):
        r0 = pl.multiple_of(c * C, C)
        rows = pl.ds(r0, C)
        b_next = cumsum(jnp.minimum(c + 1, nchunks - 1))
        for h in range(HP):
            b = b_all[h]
            b_ref[h] = b
            q = q_ref[0, h, rows, :].astype(F32)
            kk = 1.0 - jnp.exp(lf_ref[0, h, rows, :])
            b_last = b_ref[h, pl.ds(C - 1, 1), :]
            scores = jnp.zeros((C, C), F32)
            for li, m in enumerate(levels):
                d = b - ref_rows(h, 2 * m, lambda p, m=m: 2 * m * p + m - 1)
                e = jnp.exp2(-jnp.abs(d))
                scores = scores + mask_ref[li] * _dot_nt((q * e).astype(BF16), (kk * e).astype(BF16))
            d = b - ref_rows(h, HG_DIAG, lambda p: HG_DIAG * p + HG_DIAG // 2 - 1)
            scores = scores + mask_ref[len(levels)] * _dot_nt((q * jnp.exp2(d)).astype(BF16),
                                                               (kk * jnp.exp2(-d)).astype(BF16))
            v = v_ref[0, h, rows, :]
            st = st_ref[h]
            qe = (q * jnp.exp2(b)).astype(BF16)
            kd = (kk * jnp.exp2(b_last - b)).astype(BF16)
            o = jnp.dot(scores.astype(BF16), v, preferred_element_type=F32) + _dot_nt(qe, st.astype(BF16))
            st_ref[h] = st * jnp.exp2(b_last) + _dot_tn(v, kd)
            ms = jnp.mean(o * o, axis=-1, keepdims=True)
            y = o * lax.rsqrt(ms + NORM_EPS) * gain_ref[:, h * HEAD_DIM:(h + 1) * HEAD_DIM]
            y = y * sg_ref[0, h, rows, :].astype(F32)
            o_ref[0, rows, h * HEAD_DIM:(h + 1) * HEAD_DIM] = y.astype(o_ref.dtype)
        return b_next

    lax.fori_loop(0, nchunks, chunk, cumsum(0))


def _hgrn(q, lf, v, sg, gain):
    B, H, S, E = q.shape
    C, HP = HG_CHUNK, HG_HEADS_PER_STEP
    masks = jnp.asarray(_hgrn_masks(C))
    tril = jnp.asarray(np.tril(np.ones((C, C), np.float32)), dtype=BF16)
    head_spec = pl.BlockSpec((1, HP, S, E), lambda b, h: (b, h, 0, 0))
    return pl.pallas_call(
        functools.partial(_hgrn_kernel, C=C),
        grid=(B, H // HP),
        in_specs=[
            head_spec, head_spec, head_spec, head_spec,
            pl.BlockSpec((1, HP * E), lambda b, h: (0, h)),
            pl.BlockSpec((C, C), lambda b, h: (0, 0)),
            pl.BlockSpec(masks.shape, lambda b, h: (0, 0, 0)),
        ],
        out_specs=pl.BlockSpec((1, S, HP * E), lambda b, h: (b, 0, h)),
        out_shape=jax.ShapeDtypeStruct((B, S, H * E), BF16),
        scratch_shapes=[pltpu.VMEM((HP, E, E), F32), pltpu.VMEM((HP, C, E), F32)],
        compiler_params=_cparams(("parallel", "parallel")),
        name="hgrn2",
    )(q, lf, v, sg, gain.reshape(1, H * E).astype(F32), tril, masks)


def _t5_bucket_np(dist):
    exact = REL_BUCKETS // 2
    d_f = np.maximum(dist, 1).astype(np.float32)
    log_b = exact + (np.log(d_f / np.float32(exact)) / np.float32(math.log(REL_MAX_DIST / exact))
                     * np.float32(REL_BUCKETS - exact)).astype(np.int32)
    return np.where(dist < exact, dist, np.minimum(log_b, REL_BUCKETS - 1))


def _attn_bias(rel_bias):
    blk = ATT_BLOCK
    period = 3 * blk
    out = []
    for gi, (window, dilation) in enumerate(ATT_GROUPS):
        n_back = window // dilation
        assert n_back <= blk
        hs = slice(gi * ATT_HEADS_PER_GROUP, (gi + 1) * ATT_HEADS_PER_GROUP)
        bucket = _t5_bucket_np(np.arange(n_back + 1) * dilation)
        by_delta = rel_bias[:, hs][bucket].astype(F32).T
        u = jnp.full((ATT_HEADS_PER_GROUP, period), NEG_BIG, F32)
        u = u.at[:, 2 * blk - 1 - n_back:2 * blk].set(by_delta[:, ::-1])
        flat = jnp.tile(u, (1, blk))[:, :blk * (period - 1)]
        out.append(flat.reshape(ATT_HEADS_PER_GROUP, blk, period - 1)[:, :, blk - 1:3 * blk - 1])
    return jnp.stack(out, axis=0)


def _attn_kernel(q_ref, k_ref, v_ref, bias_ref, o_ref, qf, kf, vf, og, lg):
    S = q_ref.shape[3]
    scale = HEAD_DIM ** -0.5
    blk = ATT_BLOCK

    for g, (window, d) in enumerate(ATT_GROUPS):
        L = S // d
        nb = L // blk
        if d > 1:
            qf[...] = q_ref[0, g, 0].astype(F32)
            kf[...] = k_ref[0, g, 0].astype(F32)
            vf[...] = v_ref[0, g, 0].astype(F32)

        def load(ref_bf, ref_f32, start, size, g=g, d=d):
            if d == 1:
                return ref_bf[0, g, 0, pl.ds(start, size), :]
            return ref_f32[pl.ds(start, size, stride=d), :].astype(BF16)

        def block(r, n, g=g, d=d, load=load):
            start = r + n * blk * d
            rows = pl.ds(start, blk, stride=d) if d > 1 else pl.ds(start, blk)
            q = load(q_ref, qf, start, blk)
            if n == 0:
                k = load(k_ref, kf, start, blk)
                v = load(v_ref, vf, start, blk)
                s = _dot_nt(q, k) * scale + bias_ref[g, 0, :, blk:]
            else:
                first = start - blk * d
                k = load(k_ref, kf, first, 2 * blk)
                v = load(v_ref, vf, first, 2 * blk)
                s = _dot_nt(q, k) * scale + bias_ref[g, 0]
            m = jnp.max(s, axis=-1, keepdims=True)
            p = jnp.exp(s - m)
            den = jnp.sum(p, axis=-1, keepdims=True)
            o = jnp.dot(p.astype(BF16), v, preferred_element_type=F32) / den
            og[g, rows, :] = o
            lg[g, rows, :] = jnp.broadcast_to(m + jnp.log(den), (blk, HEAD_DIM))

        per_body = max(ATT_UNROLL // nb, 1)

        def residues(i, carry, nb=nb, block=block, per_body=per_body):
            for k in range(per_body):
                for n in range(nb):
                    block(i * per_body + k, n)
            return carry

        if d <= per_body:
            residues(0, 0, per_body=d)
        else:
            lax.fori_loop(0, d // per_body, residues, 0)

    rows_per_step = 256

    def mix(i, carry):
        rows = pl.ds(pl.multiple_of(i * rows_per_step, rows_per_step), rows_per_step)
        l0, l1, l2 = lg[0, rows, :], lg[1, rows, :], lg[2, rows, :]
        mx = jnp.maximum(jnp.maximum(l0, l1), l2)
        e0, e1, e2 = jnp.exp(l0 - mx), jnp.exp(l1 - mx), jnp.exp(l2 - mx)
        num = e0 * og[0, rows, :] + e1 * og[1, rows, :] + e2 * og[2, rows, :]
        o_ref[0, rows, :] = (num / (e0 + e1 + e2)).astype(o_ref.dtype)
        return carry

    lax.fori_loop(0, S // rows_per_step, mix, 0)


def _attention(qkv, bias):
    B, _, S, E = qkv.shape
    G, HG = len(ATT_GROUPS), ATT_HEADS_PER_GROUP
    x = qkv.reshape(B, 3, G, HG, S, E)

    def spec(which):
        return pl.BlockSpec((1, None, G, 1, S, E), lambda b, h, which=which: (b, which, 0, h, 0, 0))

    def kern(q_ref, k_ref, v_ref, bias_ref, o_ref, *scratch):
        _attn_kernel(q_ref, k_ref, v_ref, bias_ref, o_ref, *scratch)

    return pl.pallas_call(
        kern,
        grid=(B, HG),
        in_specs=[spec(0), spec(1), spec(2),
                  pl.BlockSpec((G, 1, ATT_BLOCK, 2 * ATT_BLOCK), lambda b, h: (0, h, 0, 0))],
        out_specs=pl.BlockSpec((1, S, E), lambda b, h: (b, 0, h)),
        out_shape=jax.ShapeDtypeStruct((B, S, HG * E), BF16),
        scratch_shapes=[pltpu.VMEM((S, E), F32)] * 3 + [pltpu.VMEM((G, S, E), F32)] * 2,
        compiler_params=_cparams(("parallel", "parallel")),
        name="dilated_attn",
    )(x, x, x, bias)


def _merge_kernel(ya_ref, yb_ref, wa_ref, wb_ref, ga_ref, gb_ref, o_ref):
    a = jnp.dot(ya_ref[...], wa_ref[...], preferred_element_type=F32)
    b = jnp.dot(yb_ref[...], wb_ref[...], preferred_element_type=F32)
    o_ref[...] = (ga_ref[...].astype(F32) * a + gb_ref[...].astype(F32) * b).astype(o_ref.dtype)


def _merge(ya, yb, wa, wb, gates, tm=1024, tn=1024):
    T, Ka = ya.shape
    Kb = yb.shape[1]
    N = wa.shape[1]
    nj = N // tn
    return pl.pallas_call(
        _merge_kernel,
        grid=(nj, T // tm),
        in_specs=[
            pl.BlockSpec((tm, Ka), lambda j, i: (i, 0)),
            pl.BlockSpec((tm, Kb), lambda j, i: (i, 0)),
            pl.BlockSpec((Ka, tn), lambda j, i: (0, j)),
            pl.BlockSpec((Kb, tn), lambda j, i: (0, j)),
            pl.BlockSpec((tm, tn), lambda j, i: (i, j)),
            pl.BlockSpec((tm, tn), lambda j, i, nj=nj: (i, j + nj)),
        ],
        out_specs=pl.BlockSpec((tm, tn), lambda j, i: (i, j)),
        out_shape=jax.ShapeDtypeStruct((T, N), BF16),
        compiler_params=_cparams(("parallel", "parallel")),
        name="branch_merge",
    )(ya, yb, wa, wb, gates, gates)


ROUTE_LANES = LANES


def _pack_halves(x_bf):
    bits = pltpu.bitcast(x_bf.astype(F32), jnp.uint32)
    n = bits.shape[1] // 2
    return (bits[:, :n] >> 16) | (bits[:, n:] & jnp.uint32(0xFFFF0000))


def _unpack_halves(p):
    lo = pltpu.bitcast(p << 16, F32)
    hi = pltpu.bitcast(p & jnp.uint32(0xFFFF0000), F32)
    return jnp.concatenate([lo, hi], axis=1)


def _outproj_kernel(m_ref, w_ref, x_ref, g_ref, wrc_ref, br_ref, x1_ref, h2_ref, rt_ref):
    x1 = x_ref[...] + jnp.dot(m_ref[...], w_ref[...], preferred_element_type=F32)
    x1_ref[...] = x1
    ms = jnp.mean(x1 * x1, axis=-1, keepdims=True)
    h2 = x1 * lax.rsqrt(ms + NORM_EPS) * g_ref[...]
    h_hi = h2.astype(BF16)
    h2_ref[...] = _pack_halves(h_hi)
    h_lo = (h2 - h_hi.astype(F32)).astype(BF16)
    r = jnp.dot(h_hi, wrc_ref[...], preferred_element_type=F32)
    lg = (r[:, :ROUTE_LANES] + r[:, ROUTE_LANES:]
          + jnp.dot(h_lo, wrc_ref[:, :ROUTE_LANES], preferred_element_type=F32)) + br_ref[...]

    lane = lax.broadcasted_iota(jnp.int32, lg.shape, 1)
    lane_f = lane.astype(F32)
    big = float(ROUTE_LANES)
    is_group = (lane >= MOE_EXPERTS) & (lane < MOE_EXPERTS + MOE_GROUPS)
    lgg = jnp.where(is_group, lg, -jnp.inf)
    gmax = jnp.max(lgg, axis=-1, keepdims=True)
    gsel = jnp.min(jnp.where(lgg == gmax, lane_f - MOE_EXPERTS, big), axis=-1, keepdims=True)
    pg = 1.0 / jnp.sum(jnp.where(is_group, jnp.exp(lg - gmax), 0.0), axis=-1, keepdims=True)

    in_group = (lane < MOE_EXPERTS) & ((lane // MOE_EXPERTS_PER_GROUP).astype(F32) == gsel)
    le = jnp.where(in_group, lg, -jnp.inf)
    t1 = jnp.max(le, axis=-1, keepdims=True)
    i1 = jnp.min(jnp.where(le == t1, lane_f, big), axis=-1, keepdims=True)
    le2 = jnp.where(lane_f == i1, -jnp.inf, le)
    t2 = jnp.max(le2, axis=-1, keepdims=True)
    i2 = jnp.min(jnp.where(le2 == t2, lane_f, big), axis=-1, keepdims=True)
    e2 = jnp.exp(t2 - t1)
    w1 = pg / (1.0 + e2)
    w2 = pg * e2 / (1.0 + e2)
    rt_ref[...] = jnp.where(lane == 0, i1, jnp.where(lane == 1, i2, jnp.where(lane == 2, w1,
                            jnp.where(lane == 3, w2, 0.0))))


def _outproj(merged, w_out, x2d, gain, wr_cat, br, tm=512):
    T, D = x2d.shape
    row = lambda i: (i, 0)
    const = lambda i: (0, 0)
    return pl.pallas_call(
        _outproj_kernel,
        grid=(T // tm,),
        in_specs=[
            pl.BlockSpec((tm, D), row), pl.BlockSpec((D, D), const), pl.BlockSpec((tm, D), row),
            pl.BlockSpec((1, D), const), pl.BlockSpec((D, 2 * ROUTE_LANES), const),
            pl.BlockSpec((1, ROUTE_LANES), const),
        ],
        out_specs=[pl.BlockSpec((tm, D), row), pl.BlockSpec((tm, D // 2), row),
                   pl.BlockSpec((tm, ROUTE_LANES), row)],
        out_shape=[jax.ShapeDtypeStruct((T, D), F32), jax.ShapeDtypeStruct((T, D // 2), jnp.uint32),
                   jax.ShapeDtypeStruct((T, ROUTE_LANES), F32)],
        compiler_params=_cparams(("parallel",)),
        name="outproj_router",
    )(merged, w_out, x2d, gain.reshape(1, D).astype(F32), wr_cat, br)


def _dispatch_kernel(dest_ref, zb_ref, h_ref, xs_ref, zeros, sem, *, tq):
    base = pl.program_id(0) * tq
    n_blocks = xs_ref.shape[0] // MOE_ROWS

    @pl.when(pl.program_id(0) == 0)
    def _():
        zeros[...] = jnp.zeros_like(zeros)

        def zero_copy(b):
            return pltpu.make_async_copy(zeros, xs_ref.at[pl.ds(pl.multiple_of(b * MOE_ROWS, MOE_ROWS), MOE_ROWS)],
                                         sem)

        def start(b, carry):
            @pl.when(zb_ref[b] != 0)
            def _():
                zero_copy(b).start()
            return carry

        def wait(b, carry):
            @pl.when(zb_ref[b] != 0)
            def _():
                zero_copy(b).wait()
            return carry

        lax.fori_loop(0, n_blocks, start, 0)
        lax.fori_loop(0, n_blocks, wait, 0)

    def copy(j, k):
        return pltpu.make_async_copy(h_ref.at[pl.ds(j, 1)], xs_ref.at[pl.ds(dest_ref[2 * (base + j) + k], 1)], sem)

    def issue(j, carry):
        copy(j, 0).start()
        copy(j, 1).start()
        return carry

    lax.fori_loop(0, tq, issue, 0)
    for _ in range(MOE_TOPK):
        pltpu.make_async_copy(h_ref, xs_ref.at[pl.ds(0, tq)], sem).wait()


def _dispatch(h2, dest_flat, zero_block, n_rows, tq=512):
    T, D = h2.shape
    return pl.pallas_call(
        functools.partial(_dispatch_kernel, tq=tq),
        grid_spec=pltpu.PrefetchScalarGridSpec(
            num_scalar_prefetch=2,
            grid=(T // tq,),
            in_specs=[pl.BlockSpec((tq, D), lambda i, d, z: (i, 0))],
            out_specs=pl.BlockSpec(memory_space=pl.ANY),
            scratch_shapes=[pltpu.VMEM((MOE_ROWS, D), h2.dtype), pltpu.SemaphoreType.DMA(())],
        ),
        out_shape=jax.ShapeDtypeStruct((n_rows, D), h2.dtype),
        compiler_params=_cparams(("arbitrary",)),
        name="moe_dispatch",
    )(dest_flat, zero_block, h2)


MOE_WEIGHT_SLOTS = 3


def _expert_kernel(be_ref, nu_ref, rank_ref, kth_ref, xs_ref, wg_hbm, wu_hbm, wd_hbm, o_ref,
                   wg_f, wu_f, wd_f, wg_s, wu_s, wd_s, sem):
    i = pl.program_id(0)
    e = be_ref[i]
    active = i < nu_ref[0]
    new_expert = active & ((i == 0) | (e != be_ref[jnp.maximum(i - 1, 0)]))
    k = rank_ref[e]
    slot = k % MOE_WEIGHT_SLOTS
    ahead = MOE_WEIGHT_SLOTS - 1

    def weight_copies(expert, s):
        return (pltpu.make_async_copy(wg_hbm.at[expert], wg_f.at[s], sem.at[s, 0]),
                pltpu.make_async_copy(wu_hbm.at[expert], wu_f.at[s], sem.at[s, 1]),
                pltpu.make_async_copy(wd_hbm.at[expert], wd_f.at[s], sem.at[s, 2]))

    def start_kth(j):
        nxt = kth_ref[j]

        @pl.when(nxt >= 0)
        def _():
            for c in weight_copies(nxt, j % MOE_WEIGHT_SLOTS):
                c.start()

    @pl.when(active & (i == 0))
    def _():
        for j in range(ahead):
            start_kth(j)

    @pl.when(new_expert)
    def _():
        start_kth(k + ahead)
        for c in weight_copies(e, slot):
            c.wait()
        wg_s[...] = wg_f[slot].astype(BF16)
        wu_s[...] = wu_f[slot].astype(BF16)
        wd_s[...] = wd_f[slot].astype(BF16)

    @pl.when(active)
    def _():
        x = _unpack_halves(xs_ref[...]).astype(BF16)
        g = jnp.dot(x, wg_s[...], preferred_element_type=F32)
        u = jnp.dot(x, wu_s[...], preferred_element_type=F32)
        a = (g * jax.nn.sigmoid(g) * u).astype(BF16)
        y = jnp.dot(a, wd_s[...], preferred_element_type=F32)
        o_ref[...] = _pack_halves(y.astype(BF16))

    @pl.when(jnp.logical_not(active))
    def _():
        o_ref[...] = jnp.zeros_like(o_ref)


def _experts(xs, block_expert, n_used, expert_rank, kth_expert, w_gate, w_up, w_down):
    Dh = xs.shape[1]
    D, F = w_gate.shape[1], w_gate.shape[2]
    nblk = block_expert.shape[0]
    P = nblk * MOE_ROWS
    rows = lambda i, be, nu, nx, sl: (jnp.minimum(i, nu[0] - 1), 0)
    hbm = pl.BlockSpec(memory_space=pl.ANY)
    return pl.pallas_call(
        _expert_kernel,
        grid_spec=pltpu.PrefetchScalarGridSpec(
            num_scalar_prefetch=4,
            grid=(nblk,),
            in_specs=[pl.BlockSpec((MOE_ROWS, Dh), rows), hbm, hbm, hbm],
            out_specs=pl.BlockSpec((MOE_ROWS, Dh), lambda i, be, nu, nx, sl: (i, 0)),
            scratch_shapes=[pltpu.VMEM((MOE_WEIGHT_SLOTS, D, F), F32), pltpu.VMEM((MOE_WEIGHT_SLOTS, D, F), F32),
                            pltpu.VMEM((MOE_WEIGHT_SLOTS, F, D), F32),
                            pltpu.VMEM((D, F), BF16), pltpu.VMEM((D, F), BF16), pltpu.VMEM((F, D), BF16),
                            pltpu.SemaphoreType.DMA((MOE_WEIGHT_SLOTS, 3))],
        ),
        out_shape=jax.ShapeDtypeStruct((P, Dh), jnp.uint32),
        compiler_params=_cparams(("arbitrary",)),
        name="moe_experts",
    )(block_expert, n_used, expert_rank, kth_expert, xs, w_gate, w_up, w_down)


def _combine_kernel(dest_ref, x1_ref, rt_ref, g_ref, ys_ref, o_ref, buf, sem, *, tq):
    i = pl.program_id(0)
    n = pl.num_programs(0)

    def issue(step, s):
        def body(j, carry):
            for k in range(MOE_TOPK):
                pltpu.make_async_copy(ys_ref.at[pl.ds(dest_ref[2 * (step * tq + j) + k], 1)],
                                      buf.at[s, k, pl.ds(j, 1)], sem.at[s]).start()
            return carry
        lax.fori_loop(0, tq, body, 0)

    @pl.when(i == 0)
    def _():
        issue(0, 0)

    for s in range(2):
        @pl.when((i + 1 < n) & ((i + 1) % 2 == s))
        def _():
            issue(i + 1, s)

    for s in range(2):
        @pl.when(i % 2 == s)
        def _():
            for k in range(MOE_TOPK):
                pltpu.make_async_copy(ys_ref.at[pl.ds(0, tq)], buf.at[s, k], sem.at[s]).wait()
            rt = rt_ref[...]
            x = x1_ref[...] + rt[:, 2:3] * _unpack_halves(buf[s, 0]) + rt[:, 3:4] * _unpack_halves(buf[s, 1])
            ms = jnp.mean(x * x, axis=-1, keepdims=True)
            o_ref[...] = x * lax.rsqrt(ms + NORM_EPS) * g_ref[...]


def _combine(x1, route, gain, ys, dest_flat, tq=256):
    T, D = x1.shape
    return pl.pallas_call(
        functools.partial(_combine_kernel, tq=tq),
        grid_spec=pltpu.PrefetchScalarGridSpec(
            num_scalar_prefetch=1,
            grid=(T // tq,),
            in_specs=[
                pl.BlockSpec((tq, D), lambda i, d: (i, 0)),
                pl.BlockSpec((tq, ROUTE_LANES), lambda i, d: (i, 0)),
                pl.BlockSpec((1, D), lambda i, d: (0, 0)),
                pl.BlockSpec(memory_space=pl.ANY),
            ],
            out_specs=pl.BlockSpec((tq, D), lambda i, d: (i, 0)),
            scratch_shapes=[pltpu.VMEM((2, MOE_TOPK, tq, D // 2), jnp.uint32), pltpu.SemaphoreType.DMA((2,))],
        ),
        out_shape=jax.ShapeDtypeStruct((T, D), F32),
        compiler_params=_cparams(("arbitrary",)),
        name="moe_combine",
    )(dest_flat, x1, route, gain.reshape(1, D).astype(F32), ys)


def _route_metadata(route, T):
    e = route[:, :MOE_TOPK].astype(jnp.int32)
    ids = jnp.arange(MOE_EXPERTS, dtype=jnp.int32)
    oh0 = e[:, 0, None] == ids
    oh1 = e[:, 1, None] == ids
    onehot = (oh0 | oh1).astype(jnp.int32)
    incl = jnp.cumsum(onehot, axis=0)
    counts = incl[-1]
    pcounts = (counts + MOE_ROWS - 1) // MOE_ROWS * MOE_ROWS
    pends = jnp.cumsum(pcounts)
    poffs = pends - pcounts
    slot = incl - onehot + poffs[None, :]
    dest = jnp.stack([jnp.sum(jnp.where(oh0, slot, 0), axis=1), jnp.sum(jnp.where(oh1, slot, 0), axis=1)], axis=1)
    nblk = (T * MOE_TOPK) // MOE_ROWS + MOE_EXPERTS
    block_expert = jnp.minimum(
        jnp.searchsorted(pends, jnp.arange(nblk, dtype=jnp.int32) * MOE_ROWS, side="right"),
        MOE_EXPERTS - 1).astype(jnp.int32)
    n_used = (pends[-1:] // MOE_ROWS).astype(jnp.int32)
    used = counts > 0
    expert_rank = (jnp.cumsum(used.astype(jnp.int32)) - 1).astype(jnp.int32)
    order = jnp.argsort(jnp.where(used, ids, ids + MOE_EXPERTS)).astype(jnp.int32)
    kth = jnp.where(ids < jnp.sum(used), order, -1)
    kth_expert = jnp.concatenate([kth, jnp.full((MOE_WEIGHT_SLOTS,), -1, jnp.int32)])
    blk = jnp.arange(nblk, dtype=jnp.int32)
    following = jnp.concatenate([block_expert[1:], block_expert[-1:]])
    zero_block = ((blk >= n_used[0] - 1) | (block_expert != following)).astype(jnp.int32)
    return dest.reshape(-1).astype(jnp.int32), zero_block, block_expert, n_used, expert_rank, kth_expert, nblk


def kernel(x, norm1_gain, w_in, hg_lb_logits, hg_norm_gain, rel_bias, w_branch_a, w_branch_b, w_out,
           norm2_gain, w_router_group, b_router_group, w_router_expert, b_router_expert,
           w_exp_gate, w_exp_up, w_exp_down, final_norm_gain):
    B, S, D = x.shape
    T = B * S
    depth = norm1_gain.shape[0]
    lower_bounds = jnp.cumsum(jax.nn.softmax(hg_lb_logits.astype(F32), axis=0), axis=0)
    att_w = ATT_HEADS * HEAD_DIM
    bias = _attn_bias(rel_bias)
    x2d = x.reshape(T, D)
    for layer in range(depth):
        w = w_in[layer]
        zeros_lb = jnp.zeros((1, D), F32)
        seg = lambda a, n: w[:, a:a + n].astype(BF16)
        hm = dict(head_major=True, B=B, S=S)
        h, q_a = _normproj(x2d, norm1_gain[layer], seg(0, D), B=B, S=S)
        lf_a = _inproj(h, seg(D, D), lower_bounds[layer].reshape(1, D), mode="logf", out_dtype=F32, **hm)
        i_a = _inproj(h, seg(2 * D, D), zeros_lb, mode="none", out_dtype=BF16, **hm)
        sg_a = _inproj(h, seg(3 * D, D), zeros_lb, mode="sigmoid", out_dtype=BF16, **hm)
        qkv_b = _inproj(h, seg(4 * D, 3 * att_w), jnp.zeros((1, 3 * att_w), F32), mode="none",
                        out_dtype=BF16, tn=att_w, **hm)
        gates = _inproj(h, seg(4 * D + 3 * att_w, 2 * D), jnp.zeros((1, 2 * D), F32), mode="sigmoid",
                        head_major=False, out_dtype=BF16, B=B, S=S)

        y_a = _hgrn(q_a, lf_a, i_a, sg_a, hg_norm_gain[layer]).reshape(T, D)
        y_b = _attention(qkv_b, bias).reshape(T, ATT_HEADS_PER_GROUP * HEAD_DIM)
        merged = _merge(y_a, y_b, w_branch_a[layer].astype(BF16), w_branch_b[layer].astype(BF16), gates)

        wr = jnp.zeros((D, ROUTE_LANES), F32)
        wr = wr.at[:, :MOE_EXPERTS].set(w_router_expert[layer].astype(F32))
        wr = wr.at[:, MOE_EXPERTS:MOE_EXPERTS + MOE_GROUPS].set(w_router_group[layer].astype(F32))
        br = jnp.zeros((1, ROUTE_LANES), F32)
        br = br.at[0, :MOE_EXPERTS].set(b_router_expert[layer].astype(F32))
        br = br.at[0, MOE_EXPERTS:MOE_EXPERTS + MOE_GROUPS].set(b_router_group[layer].astype(F32))
        wr_hi = wr.astype(BF16)
        wr_lo = (wr - wr_hi.astype(F32)).astype(BF16)
        x1, h2, route = _outproj(merged, w_out[layer].astype(BF16), x2d, norm2_gain[layer],
                                 jnp.concatenate([wr_hi, wr_lo], axis=1), br)

        dest, zero_block, block_expert, n_used, expert_rank, kth_expert, nblk = _route_metadata(route, T)
        xs = _dispatch(h2, dest, zero_block, nblk * MOE_ROWS)
        ys = _experts(xs, block_expert, n_used, expert_rank, kth_expert,
                      w_exp_gate[layer], w_exp_up[layer], w_exp_down[layer])
        last = layer == depth - 1
        assert last, "the fused combine applies the final norm; deeper stacks need an un-normalised combine"
        x2d = _combine(x1, route, final_norm_gain, ys, dest)
    return x2d.reshape(B, S, D)
```

```python
import functools
import math

import numpy as np
import jax
import jax.numpy as jnp
from jax import lax
from jax.experimental import pallas as pl
from jax.experimental.pallas import tpu as pltpu

F32 = jnp.float32
BF16 = jnp.bfloat16

LANES = 128
NORM_EPS = 1e-6
HEAD_DIM = 128
HG_HEADS = 16
ATT_GROUPS = ((128, 1), (512, 4), (2048, 16))
ATT_HEADS_PER_GROUP = 4
ATT_HEADS = len(ATT_GROUPS) * ATT_HEADS_PER_GROUP
ATT_BLOCK = 128
ATT_STACK = 16
REL_BUCKETS = 32
REL_MAX_DIST = 2048
MOE_GROUPS = 8
MOE_EXPERTS_PER_GROUP = 8
MOE_EXPERTS = MOE_GROUPS * MOE_EXPERTS_PER_GROUP
MOE_TOPK = 2
MOE_ROWS = 256
NEG_BIG = -1e30
LOG2E = 1.4426950408889634
VMEM_LIMIT = 56 * 1024 * 1024


def _cparams(sem):
    return pltpu.CompilerParams(dimension_semantics=sem, vmem_limit_bytes=VMEM_LIMIT)


def _normproj_kernel(x_ref, g_ref, w_ref, h_ref, o_ref):
    @pl.when(pl.program_id(1) == 0)
    def _():
        x = x_ref[...]
        ms = jnp.mean(x * x, axis=-1, keepdims=True)
        h_ref[...] = (x * lax.rsqrt(ms + NORM_EPS) * g_ref[...]).astype(h_ref.dtype)

    acc = jnp.dot(h_ref[...], w_ref[...], preferred_element_type=F32)
    for hh in range(acc.shape[1] // HEAD_DIM):
        o_ref[0, hh] = acc[:, hh * HEAD_DIM:(hh + 1) * HEAD_DIM].astype(o_ref.dtype)


def _normproj(x2d, gain, w, *, B, S, tm=1024, tn=1024):
    T, D = x2d.shape
    N = w.shape[1]
    assert N % tn == 0 and S % tm == 0
    spb = S // tm
    return pl.pallas_call(
        _normproj_kernel,
        grid=(T // tm, N // tn),
        in_specs=[
            pl.BlockSpec((tm, D), lambda i, j: (i, 0)),
            pl.BlockSpec((1, D), lambda i, j: (0, 0)),
            pl.BlockSpec((D, tn), lambda i, j: (0, j)),
        ],
        out_specs=[
            pl.BlockSpec((tm, D), lambda i, j: (i, 0)),
            pl.BlockSpec((1, tn // HEAD_DIM, tm, HEAD_DIM), lambda i, j: (i // spb, j, i % spb, 0)),
        ],
        out_shape=[jax.ShapeDtypeStruct((T, D), BF16),
                   jax.ShapeDtypeStruct((B, N // HEAD_DIM, S, HEAD_DIM), BF16)],
        compiler_params=_cparams(("parallel", "arbitrary")),
        name="norm_inproj",
    )(x2d, gain.reshape(1, D).astype(F32), w)


def _inproj_kernel(h_ref, w_ref, lb_ref, o_ref, *, mode, head_major):
    acc = jnp.dot(h_ref[...], w_ref[...], preferred_element_type=F32)
    if mode == "logf":
        lb = lb_ref[...]
        acc = jnp.log(lb + (1.0 - lb) * jax.nn.sigmoid(acc))
    elif mode == "sigmoid":
        acc = jax.nn.sigmoid(acc)
    if head_major:
        for hh in range(acc.shape[1] // HEAD_DIM):
            o_ref[0, hh] = acc[:, hh * HEAD_DIM:(hh + 1) * HEAD_DIM].astype(o_ref.dtype)
    else:
        o_ref[...] = acc.astype(o_ref.dtype)


def _inproj(h, w, lb, *, mode, head_major, out_dtype, B, S, tm=1024, tn=1024):
    T, D = h.shape
    N = w.shape[1]
    assert N % tn == 0 and T % tm == 0 and S % tm == 0
    spb = S // tm
    if head_major:
        out_shape = jax.ShapeDtypeStruct((B, N // HEAD_DIM, S, HEAD_DIM), out_dtype)
        out_spec = pl.BlockSpec((1, tn // HEAD_DIM, tm, HEAD_DIM), lambda j, i: (i // spb, j, i % spb, 0))
    else:
        out_shape = jax.ShapeDtypeStruct((T, N), out_dtype)
        out_spec = pl.BlockSpec((tm, tn), lambda j, i: (i, j))
    return pl.pallas_call(
        functools.partial(_inproj_kernel, mode=mode, head_major=head_major),
        grid=(N // tn, T // tm),
        in_specs=[
            pl.BlockSpec((tm, D), lambda j, i: (i, 0)),
            pl.BlockSpec((D, tn), lambda j, i: (0, j)),
            pl.BlockSpec((1, tn), lambda j, i: (0, j)),
        ],
        out_specs=out_spec,
        out_shape=out_shape,
        compiler_params=_cparams(("parallel", "parallel")),
        name="inproj_" + mode,
    )(h, w, lb)


HG_CHUNK = 256
HG_DIAG = 16
HG_HEADS_PER_STEP = 4


def _hgrn_levels(C):
    out, m = [], C // 2
    while m >= HG_DIAG:
        out.append(m)
        m //= 2
    return out


def _hgrn_masks(C):
    t = np.arange(C)[:, None]
    s = np.arange(C)[None, :]
    masks = []
    for m in _hgrn_levels(C):
        masks.append((t // (2 * m) == s // (2 * m)) & ((t // m) % 2 == 1) & ((s // m) % 2 == 0))
    masks.append((t // HG_DIAG == s // HG_DIAG) & (t >= s))
    total = np.sum(np.stack(masks).astype(np.int32), axis=0)
    assert np.array_equal(total, (t >= s).astype(np.int32))
    return np.stack(masks).astype(np.float32)


def _dot_nt(a, b):
    return lax.dot_general(a, b, (((1,), (1,)), ((), ())), preferred_element_type=F32)


def _dot_tn(a, b):
    return lax.dot_general(a, b, (((0,), (0,)), ((), ())), preferred_element_type=F32)


def _hgrn_kernel(q_ref, lf_ref, v_ref, sg_ref, gain_ref, tril_ref, mask_ref, o_ref, st_ref, b_ref, *, C):
    S = q_ref.shape[2]
    HP = q_ref.shape[1]
    levels = _hgrn_levels(C)
    nchunks = S // C
    tril = tril_ref[...]

    def ref_rows(h, block, row_of_block):
        parts = [jnp.broadcast_to(b_ref[h, pl.ds(row_of_block(p), 1), :], (block, HEAD_DIM))
                 for p in range(C // block)]
        return jnp.concatenate(parts, axis=0)

    def cumsum(c):
        r0 = pl.multiple_of(c * C, C)
        out = []
        for h in range(HP):
            lf2 = lf_ref[0, h, pl.ds(r0, C), :] * LOG2E
            hi = lf2.astype(BF16)
            lo = (lf2 - hi.astype(F32)).astype(BF16)
            r = jnp.dot(tril, jnp.concatenate([hi, lo], axis=1), preferred_element_type=F32)
            out.append(r[:, :HEAD_DIM] + r[:, HEAD_DIM:])
        return tuple(out)

    st_ref[...] = jnp.zeros_like(st_ref)

    def chunk(c, b_all):
        r0 = pl.multiple_of(c * C, C)
        rows = pl.ds(r0, C)
        b_next = cumsum(jnp.minimum(c + 1, nchunks - 1))
        for h in range(HP):
            b = b_all[h]
            b_ref[h] = b
            q = q_ref[0, h, rows, :].astype(F32)
            kk = 1.0 - jnp.exp(lf_ref[0, h, rows, :])
            b_last = b_ref[h, pl.ds(C - 1, 1), :]
            scores = jnp.zeros((C, C), F32)
            for li, m in enumerate(levels):
                d = b - ref_rows(h, 2 * m, lambda p, m=m: 2 * m * p + m - 1)
                e = jnp.exp2(-jnp.abs(d))
                scores = scores + mask_ref[li] * _dot_nt((q * e).astype(BF16), (kk * e).astype(BF16))
            d = b - ref_rows(h, HG_DIAG, lambda p: HG_DIAG * p + HG_DIAG // 2 - 1)
            scores = scores + mask_ref[len(levels)] * _dot_nt((q * jnp.exp2(d)).astype(BF16),
                                                               (kk * jnp.exp2(-d)).astype(BF16))
            v = v_ref[0, h, rows, :]
            st = st_ref[h]
            qe = (q * jnp.exp2(b)).astype(BF16)
            kd = (kk * jnp.exp2(b_last - b)).astype(BF16)
            o = jnp.dot(scores.astype(BF16), v, preferred_element_type=F32) + _dot_nt(qe, st.astype(BF16))
            st_ref[h] = st * jnp.exp2(b_last) + _dot_tn(v, kd)
            ms = jnp.mean(o * o, axis=-1, keepdims=True)
            y = o * lax.rsqrt(ms + NORM_EPS) * gain_ref[:, h * HEAD_DIM:(h + 1) * HEAD_DIM]
            y = y * sg_ref[0, h, rows, :].astype(F32)
            o_ref[0, rows, h * HEAD_DIM:(h + 1) * HEAD_DIM] = y.astype(o_ref.dtype)
        return b_next

    lax.fori_loop(0, nchunks, chunk, cumsum(0))


def _hgrn(q, lf, v, sg, gain):
    B, H, S, E = q.shape
    C, HP = HG_CHUNK, HG_HEADS_PER_STEP
    masks = jnp.asarray(_hgrn_masks(C))
    tril = jnp.asarray(np.tril(np.ones((C, C), np.float32)), dtype=BF16)
    head_spec = pl.BlockSpec((1, HP, S, E), lambda b, h: (b, h, 0, 0))
    return pl.pallas_call(
        functools.partial(_hgrn_kernel, C=C),
        grid=(B, H // HP),
        in_specs=[
            head_spec, head_spec, head_spec, head_spec,
            pl.BlockSpec((1, HP * E), lambda b, h: (0, h)),
            pl.BlockSpec((C, C), lambda b, h: (0, 0)),
            pl.BlockSpec(masks.shape, lambda b, h: (0, 0, 0)),
        ],
        out_specs=pl.BlockSpec((1, S, HP * E), lambda b, h: (b, 0, h)),
        out_shape=jax.ShapeDtypeStruct((B, S, H * E), BF16),
        scratch_shapes=[pltpu.VMEM((HP, E, E), F32), pltpu.VMEM((HP, C, E), F32)],
        compiler_params=_cparams(("parallel", "parallel")),
        name="hgrn2",
    )(q, lf, v, sg, gain.reshape(1, H * E).astype(F32), tril, masks)


def _t5_bucket_np(dist):
    exact = REL_BUCKETS // 2
    d_f = np.maximum(dist, 1).astype(np.float32)
    log_b = exact + (np.log(d_f / np.float32(exact)) / np.float32(math.log(REL_MAX_DIST / exact))
                     * np.float32(REL_BUCKETS - exact)).astype(np.int32)
    return np.where(dist < exact, dist, np.minimum(log_b, REL_BUCKETS - 1))


def _attn_bias(rel_bias):
    blk = ATT_BLOCK
    period = 3 * blk
    out = []
    for gi, (window, dilation) in enumerate(ATT_GROUPS):
        n_back = window // dilation
        assert n_back <= blk
        hs = slice(gi * ATT_HEADS_PER_GROUP, (gi + 1) * ATT_HEADS_PER_GROUP)
        bucket = _t5_bucket_np(np.arange(n_back + 1) * dilation)
        by_delta = rel_bias[:, hs][bucket].astype(F32).T
        u = jnp.full((ATT_HEADS_PER_GROUP, period), NEG_BIG, F32)
        u = u.at[:, 2 * blk - 1 - n_back:2 * blk].set(by_delta[:, ::-1])
        flat = jnp.tile(u, (1, blk))[:, :blk * (period - 1)]
        out.append(flat.reshape(ATT_HEADS_PER_GROUP, blk, period - 1)[:, :, blk - 1:3 * blk - 1])
    return jnp.stack(out, axis=0)


def _attn_kernel(q_ref, k_ref, v_ref, bias_ref, o_ref, qf, kf, vf, og, lg):
    S = q_ref.shape[3]
    scale = HEAD_DIM ** -0.5
    blk = ATT_BLOCK

    for g, (window, d) in enumerate(ATT_GROUPS):
        L = S // d
        nb = L // blk
        if d > 1:
            qf[...] = q_ref[0, g, 0].astype(F32)
            kf[...] = k_ref[0, g, 0].astype(F32)
            vf[...] = v_ref[0, g, 0].astype(F32)

        def load(ref_bf, ref_f32, start, size, g=g, d=d):
            if d == 1:
                return ref_bf[0, g, 0, pl.ds(start, size), :]
            return ref_f32[pl.ds(start, size, stride=d), :].astype(BF16)

        def scores(r, n, g=g, d=d, load=load):
            start = r + n * blk * d
            q = load(q_ref, qf, start, blk)
            if n == 0:
                k = load(k_ref, kf, start, blk)
                v = load(v_ref, vf, start, blk)
                s = _dot_nt(q, k) * scale + bias_ref[g, 0, :, blk:]
            else:
                first = start - blk * d
                k = load(k_ref, kf, first, 2 * blk)
                v = load(v_ref, vf, first, 2 * blk)
                s = _dot_nt(q, k) * scale + bias_ref[g, 0]
            return start, s, v

        def softmax_pv(items, g=g, d=d):
            s = jnp.concatenate([it[1] for it in items], axis=0)
            m = jnp.max(s, axis=-1, keepdims=True)
            p = jnp.exp(s - m)
            den = jnp.sum(p, axis=-1, keepdims=True)
            pb = p.astype(BF16)
            lse = jnp.broadcast_to(m + jnp.log(den), (s.shape[0], HEAD_DIM))
            for j, (start, _, v) in enumerate(items):
                sl = slice(j * blk, (j + 1) * blk)
                rows = pl.ds(start, blk, stride=d) if d > 1 else pl.ds(start, blk)
                og[g, rows, :] = jnp.dot(pb[sl], v, preferred_element_type=F32) / den[sl]
                lg[g, rows, :] = lse[sl]

        blocks = [(r, n) for r in range(d) for n in range(nb)]
        first_blocks = [bn for bn in blocks if bn[1] == 0]
        later_blocks = [bn for bn in blocks if bn[1] > 0]
        for group in (first_blocks, later_blocks):
            for j0 in range(0, len(group), ATT_STACK):
                softmax_pv([scores(r, n) for r, n in group[j0:j0 + ATT_STACK]])

    rows_per_step = 256

    def mix(i, carry):
        rows = pl.ds(pl.multiple_of(i * rows_per_step, rows_per_step), rows_per_step)
        l0, l1, l2 = lg[0, rows, :], lg[1, rows, :], lg[2, rows, :]
        mx = jnp.maximum(jnp.maximum(l0, l1), l2)
        e0, e1, e2 = jnp.exp(l0 - mx), jnp.exp(l1 - mx), jnp.exp(l2 - mx)
        num = e0 * og[0, rows, :] + e1 * og[1, rows, :] + e2 * og[2, rows, :]
        o_ref[0, rows, :] = (num / (e0 + e1 + e2)).astype(o_ref.dtype)
        return carry

    lax.fori_loop(0, S // rows_per_step, mix, 0)


def _attention(qkv, bias):
    B, _, S, E = qkv.shape
    G, HG = len(ATT_GROUPS), ATT_HEADS_PER_GROUP
    x = qkv.reshape(B, 3, G, HG, S, E)

    def spec(which):
        return pl.BlockSpec((1, None, G, 1, S, E), lambda b, h, which=which: (b, which, 0, h, 0, 0))

    def kern(q_ref, k_ref, v_ref, bias_ref, o_ref, *scratch):
        _attn_kernel(q_ref, k_ref, v_ref, bias_ref, o_ref, *scratch)

    return pl.pallas_call(
        kern,
        grid=(B, HG),
        in_specs=[spec(0), spec(1), spec(2),
                  pl.BlockSpec((G, 1, ATT_BLOCK, 2 * ATT_BLOCK), lambda b, h: (0, h, 0, 0))],
        out_specs=pl.BlockSpec((1, S, E), lambda b, h: (b, 0, h)),
        out_shape=jax.ShapeDtypeStruct((B, S, HG * E), BF16),
        scratch_shapes=[pltpu.VMEM((S, E), F32)] * 3 + [pltpu.VMEM((G, S, E), F32)] * 2,
        compiler_params=_cparams(("parallel", "parallel")),
        name="dilated_attn",
    )(x, x, x, bias)


def _merge_kernel(ya_ref, yb_ref, wa_ref, wb_ref, ga_ref, gb_ref, o_ref):
    a = jnp.dot(ya_ref[...], wa_ref[...], preferred_element_type=F32)
    b = jnp.dot(yb_ref[...], wb_ref[...], preferred_element_type=F32)
    o_ref[...] = (ga_ref[...].astype(F32) * a + gb_ref[...].astype(F32) * b).astype(o_ref.dtype)


def _merge(ya, yb, wa, wb, gates, tm=1024, tn=1024):
    T, Ka = ya.shape
    Kb = yb.shape[1]
    N = wa.shape[1]
    nj = N // tn
    return pl.pallas_call(
        _merge_kernel,
        grid=(nj, T // tm),
        in_specs=[
            pl.BlockSpec((tm, Ka), lambda j, i: (i, 0)),
            pl.BlockSpec((tm, Kb), lambda j, i: (i, 0)),
            pl.BlockSpec((Ka, tn), lambda j, i: (0, j)),
            pl.BlockSpec((Kb, tn), lambda j, i: (0, j)),
            pl.BlockSpec((tm, tn), lambda j, i: (i, j)),
            pl.BlockSpec((tm, tn), lambda j, i, nj=nj: (i, j + nj)),
        ],
        out_specs=pl.BlockSpec((tm, tn), lambda j, i: (i, j)),
        out_shape=jax.ShapeDtypeStruct((T, N), BF16),
        compiler_params=_cparams(("parallel", "parallel")),
        name="branch_merge",
    )(ya, yb, wa, wb, gates, gates)


ROUTE_LANES = LANES


def _pack_halves(x_bf):
    bits = pltpu.bitcast(x_bf.astype(F32), jnp.uint32)
    n = bits.shape[1] // 2
    return (bits[:, :n] >> 16) | (bits[:, n:] & jnp.uint32(0xFFFF0000))


def _unpack_halves(p):
    lo = pltpu.bitcast(p << 16, F32)
    hi = pltpu.bitcast(p & jnp.uint32(0xFFFF0000), F32)
    return jnp.concatenate([lo, hi], axis=1)


def _outproj_kernel(m_ref, w_ref, x_ref, g_ref, wrc_ref, br_ref, x1_ref, h2_ref, rt_ref):
    x1 = x_ref[...] + jnp.dot(m_ref[...], w_ref[...], preferred_element_type=F32)
    x1_ref[...] = x1
    ms = jnp.mean(x1 * x1, axis=-1, keepdims=True)
    h2 = x1 * lax.rsqrt(ms + NORM_EPS) * g_ref[...]
    h_hi = h2.astype(BF16)
    h2_ref[...] = _pack_halves(h_hi)
    h_lo = (h2 - h_hi.astype(F32)).astype(BF16)
    r = jnp.dot(h_hi, wrc_ref[...], preferred_element_type=F32)
    lg = (r[:, :ROUTE_LANES] + r[:, ROUTE_LANES:]
          + jnp.dot(h_lo, wrc_ref[:, :ROUTE_LANES], preferred_element_type=F32)) + br_ref[...]

    lane = lax.broadcasted_iota(jnp.int32, lg.shape, 1)
    lane_f = lane.astype(F32)
    big = float(ROUTE_LANES)
    is_group = (lane >= MOE_EXPERTS) & (lane < MOE_EXPERTS + MOE_GROUPS)
    lgg = jnp.where(is_group, lg, -jnp.inf)
    gmax = jnp.max(lgg, axis=-1, keepdims=True)
    gsel = jnp.min(jnp.where(lgg == gmax, lane_f - MOE_EXPERTS, big), axis=-1, keepdims=True)
    pg = 1.0 / jnp.sum(jnp.where(is_group, jnp.exp(lg - gmax), 0.0), axis=-1, keepdims=True)

    in_group = (lane < MOE_EXPERTS) & ((lane // MOE_EXPERTS_PER_GROUP).astype(F32) == gsel)
    le = jnp.where(in_group, lg, -jnp.inf)
    t1 = jnp.max(le, axis=-1, keepdims=True)
    i1 = jnp.min(jnp.where(le == t1, lane_f, big), axis=-1, keepdims=True)
    le2 = jnp.where(lane_f == i1, -jnp.inf, le)
    t2 = jnp.max(le2, axis=-1, keepdims=True)
    i2 = jnp.min(jnp.where(le2 == t2, lane_f, big), axis=-1, keepdims=True)
    e2 = jnp.exp(t2 - t1)
    w1 = pg / (1.0 + e2)
    w2 = pg * e2 / (1.0 + e2)
    rt_ref[...] = jnp.where(lane == 0, i1, jnp.where(lane == 1, i2, jnp.where(lane == 2, w1,
                            jnp.where(lane == 3, w2, 0.0))))


def _outproj(merged, w_out, x2d, gain, wr_cat, br, tm=512):
    T, D = x2d.shape
    row = lambda i: (i, 0)
    const = lambda i: (0, 0)
    return pl.pallas_call(
        _outproj_kernel,
        grid=(T // tm,),
        in_specs=[
            pl.BlockSpec((tm, D), row), pl.BlockSpec((D, D), const), pl.BlockSpec((tm, D), row),
            pl.BlockSpec((1, D), const), pl.BlockSpec((D, 2 * ROUTE_LANES), const),
            pl.BlockSpec((1, ROUTE_LANES), const),
        ],
        out_specs=[pl.BlockSpec((tm, D), row), pl.BlockSpec((tm, D // 2), row),
                   pl.BlockSpec((tm, ROUTE_LANES), row)],
        out_shape=[jax.ShapeDtypeStruct((T, D), F32), jax.ShapeDtypeStruct((T, D // 2), jnp.uint32),
                   jax.ShapeDtypeStruct((T, ROUTE_LANES), F32)],
        compiler_params=_cparams(("parallel",)),
        name="outproj_router",
    )(merged, w_out, x2d, gain.reshape(1, D).astype(F32), wr_cat, br)


def _dispatch_kernel(dest_ref, zb_ref, h_ref, xs_ref, zeros, sem, *, tq):
    base = pl.program_id(0) * tq
    n_blocks = xs_ref.shape[0] // MOE_ROWS

    @pl.when(pl.program_id(0) == 0)
    def _():
        zeros[...] = jnp.zeros_like(zeros)

        def zero_copy(b):
            return pltpu.make_async_copy(zeros, xs_ref.at[pl.ds(pl.multiple_of(b * MOE_ROWS, MOE_ROWS), MOE_ROWS)],
                                         sem)

        def start(b, carry):
            @pl.when(zb_ref[b] != 0)
            def _():
                zero_copy(b).start()
            return carry

        def wait(b, carry):
            @pl.when(zb_ref[b] != 0)
            def _():
                zero_copy(b).wait()
            return carry

        lax.fori_loop(0, n_blocks, start, 0)
        lax.fori_loop(0, n_blocks, wait, 0)

    def copy(j, k):
        return pltpu.make_async_copy(h_ref.at[pl.ds(j, 1)], xs_ref.at[pl.ds(dest_ref[2 * (base + j) + k], 1)], sem)

    def issue(j, carry):
        copy(j, 0).start()
        copy(j, 1).start()
        return carry

    lax.fori_loop(0, tq, issue, 0)
    for _ in range(MOE_TOPK):
        pltpu.make_async_copy(h_ref, xs_ref.at[pl.ds(0, tq)], sem).wait()


def _dispatch(h2, dest_flat, zero_block, n_rows, tq=512):
    T, D = h2.shape
    return pl.pallas_call(
        functools.partial(_dispatch_kernel, tq=tq),
        grid_spec=pltpu.PrefetchScalarGridSpec(
            num_scalar_prefetch=2,
            grid=(T // tq,),
            in_specs=[pl.BlockSpec((tq, D), lambda i, d, z: (i, 0))],
            out_specs=pl.BlockSpec(memory_space=pl.ANY),
            scratch_shapes=[pltpu.VMEM((MOE_ROWS, D), h2.dtype), pltpu.SemaphoreType.DMA(())],
        ),
        out_shape=jax.ShapeDtypeStruct((n_rows, D), h2.dtype),
        compiler_params=_cparams(("arbitrary",)),
        name="moe_dispatch",
    )(dest_flat, zero_block, h2)


MOE_WEIGHT_SLOTS = 3


def _expert_kernel(be_ref, nu_ref, rank_ref, kth_ref, xs_ref, wg_hbm, wu_hbm, wd_hbm, o_ref,
                   wg_f, wu_f, wd_f, wg_s, wu_s, wd_s, sem):
    i = pl.program_id(0)
    e = be_ref[i]
    active = i < nu_ref[0]
    new_expert = active & ((i == 0) | (e != be_ref[jnp.maximum(i - 1, 0)]))
    k = rank_ref[e]
    slot = k % MOE_WEIGHT_SLOTS
    ahead = MOE_WEIGHT_SLOTS - 1

    def weight_copies(expert, s):
        return (pltpu.make_async_copy(wg_hbm.at[expert], wg_f.at[s], sem.at[s, 0]),
                pltpu.make_async_copy(wu_hbm.at[expert], wu_f.at[s], sem.at[s, 1]),
                pltpu.make_async_copy(wd_hbm.at[expert], wd_f.at[s], sem.at[s, 2]))

    def start_kth(j):
        nxt = kth_ref[j]

        @pl.when(nxt >= 0)
        def _():
            for c in weight_copies(nxt, j % MOE_WEIGHT_SLOTS):
                c.start()

    @pl.when(active & (i == 0))
    def _():
        for j in range(ahead):
            start_kth(j)

    @pl.when(new_expert)
    def _():
        start_kth(k + ahead)
        for c in weight_copies(e, slot):
            c.wait()
        wg_s[...] = wg_f[slot].astype(BF16)
        wu_s[...] = wu_f[slot].astype(BF16)
        wd_s[...] = wd_f[slot].astype(BF16)

    @pl.when(active)
    def _():
        x = _unpack_halves(xs_ref[...]).astype(BF16)
        g = jnp.dot(x, wg_s[...], preferred_element_type=F32)
        u = jnp.dot(x, wu_s[...], preferred_element_type=F32)
        a = (g * jax.nn.sigmoid(g) * u).astype(BF16)
        y = jnp.dot(a, wd_s[...], preferred_element_type=F32)
        o_ref[...] = _pack_halves(y.astype(BF16))

    @pl.when(jnp.logical_not(active))
    def _():
        o_ref[...] = jnp.zeros_like(o_ref)


def _experts(xs, block_expert, n_used, expert_rank, kth_expert, w_gate, w_up, w_down):
    Dh = xs.shape[1]
    D, F = w_gate.shape[1], w_gate.shape[2]
    nblk = block_expert.shape[0]
    P = nblk * MOE_ROWS
    rows = lambda i, be, nu, nx, sl: (jnp.minimum(i, nu[0] - 1), 0)
    hbm = pl.BlockSpec(memory_space=pl.ANY)
    return pl.pallas_call(
        _expert_kernel,
        grid_spec=pltpu.PrefetchScalarGridSpec(
            num_scalar_prefetch=4,
            grid=(nblk,),
            in_specs=[pl.BlockSpec((MOE_ROWS, Dh), rows), hbm, hbm, hbm],
            out_specs=pl.BlockSpec((MOE_ROWS, Dh), lambda i, be, nu, nx, sl: (i, 0)),
            scratch_shapes=[pltpu.VMEM((MOE_WEIGHT_SLOTS, D, F), F32), pltpu.VMEM((MOE_WEIGHT_SLOTS, D, F), F32),
                            pltpu.VMEM((MOE_WEIGHT_SLOTS, F, D), F32),
                            pltpu.VMEM((D, F), BF16), pltpu.VMEM((D, F), BF16), pltpu.VMEM((F, D), BF16),
                            pltpu.SemaphoreType.DMA((MOE_WEIGHT_SLOTS, 3))],
        ),
        out_shape=jax.ShapeDtypeStruct((P, Dh), jnp.uint32),
        compiler_params=_cparams(("arbitrary",)),
        name="moe_experts",
    )(block_expert, n_used, expert_rank, kth_expert, xs, w_gate, w_up, w_down)


def _combine_kernel(dest_ref, x1_ref, rt_ref, g_ref, ys_ref, o_ref, buf, sem, *, tq):
    i = pl.program_id(0)
    n = pl.num_programs(0)

    def issue(step, s):
        def body(j, carry):
            for k in range(MOE_TOPK):
                pltpu.make_async_copy(ys_ref.at[pl.ds(dest_ref[2 * (step * tq + j) + k], 1)],
                                      buf.at[s, k, pl.ds(j, 1)], sem.at[s]).start()
            return carry
        lax.fori_loop(0, tq, body, 0)

    @pl.when(i == 0)
    def _():
        issue(0, 0)

    for s in range(2):
        @pl.when((i + 1 < n) & ((i + 1) % 2 == s))
        def _():
            issue(i + 1, s)

    for s in range(2):
        @pl.when(i % 2 == s)
        def _():
            for k in range(MOE_TOPK):
                pltpu.make_async_copy(ys_ref.at[pl.ds(0, tq)], buf.at[s, k], sem.at[s]).wait()
            rt = rt_ref[...]
            x = x1_ref[...] + rt[:, 2:3] * _unpack_halves(buf[s, 0]) + rt[:, 3:4] * _unpack_halves(buf[s, 1])
            ms = jnp.mean(x * x, axis=-1, keepdims=True)
            o_ref[...] = x * lax.rsqrt(ms + NORM_EPS) * g_ref[...]


def _combine(x1, route, gain, ys, dest_flat, tq=256):
    T, D = x1.shape
    return pl.pallas_call(
        functools.partial(_combine_kernel, tq=tq),
        grid_spec=pltpu.PrefetchScalarGridSpec(
            num_scalar_prefetch=1,
            grid=(T // tq,),
            in_specs=[
                pl.BlockSpec((tq, D), lambda i, d: (i, 0)),
                pl.BlockSpec((tq, ROUTE_LANES), lambda i, d: (i, 0)),
                pl.BlockSpec((1, D), lambda i, d: (0, 0)),
                pl.BlockSpec(memory_space=pl.ANY),
            ],
            out_specs=pl.BlockSpec((tq, D), lambda i, d: (i, 0)),
            scratch_shapes=[pltpu.VMEM((2, MOE_TOPK, tq, D // 2), jnp.uint32), pltpu.SemaphoreType.DMA((2,))],
        ),
        out_shape=jax.ShapeDtypeStruct((T, D), F32),
        compiler_params=_cparams(("arbitrary",)),
        name="moe_combine",
    )(dest_flat, x1, route, gain.reshape(1, D).astype(F32), ys)


def _route_metadata(route, T):
    e = route[:, :MOE_TOPK].astype(jnp.int32)
    ids = jnp.arange(MOE_EXPERTS, dtype=jnp.int32)
    oh0 = e[:, 0, None] == ids
    oh1 = e[:, 1, None] == ids
    onehot = (oh0 | oh1).astype(jnp.int32)
    incl = jnp.cumsum(onehot, axis=0)
    counts = incl[-1]
    pcounts = (counts + MOE_ROWS - 1) // MOE_ROWS * MOE_ROWS
    pends = jnp.cumsum(pcounts)
    poffs = pends - pcounts
    slot = incl - onehot + poffs[None, :]
    dest = jnp.stack([jnp.sum(jnp.where(oh0, slot, 0), axis=1), jnp.sum(jnp.where(oh1, slot, 0), axis=1)], axis=1)
    nblk = (T * MOE_TOPK) // MOE_ROWS + MOE_EXPERTS
    block_expert = jnp.minimum(
        jnp.searchsorted(pends, jnp.arange(nblk, dtype=jnp.int32) * MOE_ROWS, side="right"),
        MOE_EXPERTS - 1).astype(jnp.int32)
    n_used = (pends[-1:] // MOE_ROWS).astype(jnp.int32)
    used = counts > 0
    expert_rank = (jnp.cumsum(used.astype(jnp.int32)) - 1).astype(jnp.int32)
    order = jnp.argsort(jnp.where(used, ids, ids + MOE_EXPERTS)).astype(jnp.int32)
    kth = jnp.where(ids < jnp.sum(used), order, -1)
    kth_expert = jnp.concatenate([kth, jnp.full((MOE_WEIGHT_SLOTS,), -1, jnp.int32)])
    blk = jnp.arange(nblk, dtype=jnp.int32)
    following = jnp.concatenate([block_expert[1:], block_expert[-1:]])
    zero_block = ((blk >= n_used[0] - 1) | (block_expert != following)).astype(jnp.int32)
    return dest.reshape(-1).astype(jnp.int32), zero_block, block_expert, n_used, expert_rank, kth_expert, nblk


def kernel(x, norm1_gain, w_in, hg_lb_logits, hg_norm_gain, rel_bias, w_branch_a, w_branch_b, w_out,
           norm2_gain, w_router_group, b_router_group, w_router_expert, b_router_expert,
           w_exp_gate, w_exp_up, w_exp_down, final_norm_gain):
    B, S, D = x.shape
    T = B * S
    depth = norm1_gain.shape[0]
    lower_bounds = jnp.cumsum(jax.nn.softmax(hg_lb_logits.astype(F32), axis=0), axis=0)
    att_w = ATT_HEADS * HEAD_DIM
    bias = _attn_bias(rel_bias)
    x2d = x.reshape(T, D)
    for layer in range(depth):
        w = w_in[layer]
        zeros_lb = jnp.zeros((1, D), F32)
        seg = lambda a, n: w[:, a:a + n].astype(BF16)
        hm = dict(head_major=True, B=B, S=S)
        h, q_a = _normproj(x2d, norm1_gain[layer], seg(0, D), B=B, S=S)
        lf_a = _inproj(h, seg(D, D), lower_bounds[layer].reshape(1, D), mode="logf", out_dtype=F32, **hm)
        i_a = _inproj(h, seg(2 * D, D), zeros_lb, mode="none", out_dtype=BF16, **hm)
        sg_a = _inproj(h, seg(3 * D, D), zeros_lb, mode="sigmoid", out_dtype=BF16, **hm)
        qkv_b = _inproj(h, seg(4 * D, 3 * att_w), jnp.zeros((1, 3 * att_w), F32), mode="none",
                        out_dtype=BF16, tn=att_w, **hm)
        gates = _inproj(h, seg(4 * D + 3 * att_w, 2 * D), jnp.zeros((1, 2 * D), F32), mode="sigmoid",
                        head_major=False, out_dtype=BF16, B=B, S=S)

        y_a = _hgrn(q_a, lf_a, i_a, sg_a, hg_norm_gain[layer]).reshape(T, D)
        y_b = _attention(qkv_b, bias).reshape(T, ATT_HEADS_PER_GROUP * HEAD_DIM)
        merged = _merge(y_a, y_b, w_branch_a[layer].astype(BF16), w_branch_b[layer].astype(BF16), gates)

        wr = jnp.zeros((D, ROUTE_LANES), F32)
        wr = wr.at[:, :MOE_EXPERTS].set(w_router_expert[layer].astype(F32))
        wr = wr.at[:, MOE_EXPERTS:MOE_EXPERTS + MOE_GROUPS].set(w_router_group[layer].astype(F32))
        br = jnp.zeros((1, ROUTE_LANES), F32)
        br = br.at[0, :MOE_EXPERTS].set(b_router_expert[layer].astype(F32))
        br = br.at[0, MOE_EXPERTS:MOE_EXPERTS + MOE_GROUPS].set(b_router_group[layer].astype(F32))
        wr_hi = wr.astype(BF16)
        wr_lo = (wr - wr_hi.astype(F32)).astype(BF16)
        x1, h2, route = _outproj(merged, w_out[layer].astype(BF16), x2d, norm2_gain[layer],
                                 jnp.concatenate([wr_hi, wr_lo], axis=1), br)

        dest, zero_block, block_expert, n_used, expert_rank, kth_expert, nblk = _route_metadata(route, T)
        xs = _dispatch(h2, dest, zero_block, nblk * MOE_ROWS)
        ys = _experts(xs, block_expert, n_used, expert_rank, kth_expert,
                      w_exp_gate[layer], w_exp_up[layer], w_exp_down[layer])
        last = layer == depth - 1
        assert last, "the fused combine applies the final norm; deeper stacks need an un-normalised combine"
        x2d = _combine(x1, route, final_norm_gain, ys, dest)
    return x2d.reshape(B, S, D)
```

```python
import functools
import math

import numpy as np
import jax
import jax.numpy as jnp
from jax import lax
from jax.experimental import pallas as pl
from jax.experimental.pallas import tpu as pltpu

F32 = jnp.float32
BF16 = jnp.bfloat16

LANES = 128
NORM_EPS = 1e-6
HEAD_DIM = 128
HG_HEADS = 16
ATT_GROUPS = ((128, 1), (512, 4), (2048, 16))
ATT_HEADS_PER_GROUP = 4
ATT_HEADS = len(ATT_GROUPS) * ATT_HEADS_PER_GROUP
ATT_BLOCK = 128
ATT_STACK = 16
REL_BUCKETS = 32
REL_MAX_DIST = 2048
MOE_GROUPS = 8
MOE_EXPERTS_PER_GROUP = 8
MOE_EXPERTS = MOE_GROUPS * MOE_EXPERTS_PER_GROUP
MOE_TOPK = 2
MOE_ROWS = 256
NEG_BIG = -1e30
LOG2E = 1.4426950408889634
VMEM_LIMIT = 56 * 1024 * 1024


def _cparams(sem):
    return pltpu.CompilerParams(dimension_semantics=sem, vmem_limit_bytes=VMEM_LIMIT)


def _normproj_kernel(x_ref, g_ref, w_ref, h_ref, o_ref):
    @pl.when(pl.program_id(1) == 0)
    def _():
        x = x_ref[...]
        ms = jnp.mean(x * x, axis=-1, keepdims=True)
        h_ref[...] = (x * lax.rsqrt(ms + NORM_EPS) * g_ref[...]).astype(h_ref.dtype)

    acc = jnp.dot(h_ref[...], w_ref[...], preferred_element_type=F32)
    for hh in range(acc.shape[1] // HEAD_DIM):
        o_ref[0, hh] = acc[:, hh * HEAD_DIM:(hh + 1) * HEAD_DIM].astype(o_ref.dtype)


def _normproj(x2d, gain, w, *, B, S, tm=1024, tn=1024):
    T, D = x2d.shape
    N = w.shape[1]
    assert N % tn == 0 and S % tm == 0
    spb = S // tm
    return pl.pallas_call(
        _normproj_kernel,
        grid=(T // tm, N // tn),
        in_specs=[
            pl.BlockSpec((tm, D), lambda i, j: (i, 0)),
            pl.BlockSpec((1, D), lambda i, j: (0, 0)),
            pl.BlockSpec((D, tn), lambda i, j: (0, j)),
        ],
        out_specs=[
            pl.BlockSpec((tm, D), lambda i, j: (i, 0)),
            pl.BlockSpec((1, tn // HEAD_DIM, tm, HEAD_DIM), lambda i, j: (i // spb, j, i % spb, 0)),
        ],
        out_shape=[jax.ShapeDtypeStruct((T, D), BF16),
                   jax.ShapeDtypeStruct((B, N // HEAD_DIM, S, HEAD_DIM), BF16)],
        compiler_params=_cparams(("parallel", "arbitrary")),
        name="norm_inproj",
    )(x2d, gain.reshape(1, D).astype(F32), w)


def _inproj_kernel(h_ref, w_ref, lb_ref, o_ref, *, mode, head_major):
    acc = jnp.dot(h_ref[...], w_ref[...], preferred_element_type=F32)
    if mode == "logf":
        lb = lb_ref[...]
        acc = jnp.log(lb + (1.0 - lb) * jax.nn.sigmoid(acc))
    elif mode == "sigmoid":
        acc = jax.nn.sigmoid(acc)
    if head_major:
        for hh in range(acc.shape[1] // HEAD_DIM):
            o_ref[0, hh] = acc[:, hh * HEAD_DIM:(hh + 1) * HEAD_DIM].astype(o_ref.dtype)
    else:
        o_ref[...] = acc.astype(o_ref.dtype)


def _inproj(h, w, lb, *, mode, head_major, out_dtype, B, S, tm=1024, tn=1024):
    T, D = h.shape
    N = w.shape[1]
    assert N % tn == 0 and T % tm == 0 and S % tm == 0
    spb = S // tm
    if head_major:
        out_shape = jax.ShapeDtypeStruct((B, N // HEAD_DIM, S, HEAD_DIM), out_dtype)
        out_spec = pl.BlockSpec((1, tn // HEAD_DIM, tm, HEAD_DIM), lambda j, i: (i // spb, j, i % spb, 0))
    else:
        out_shape = jax.ShapeDtypeStruct((T, N), out_dtype)
        out_spec = pl.BlockSpec((tm, tn), lambda j, i: (i, j))
    return pl.pallas_call(
        functools.partial(_inproj_kernel, mode=mode, head_major=head_major),
        grid=(N // tn, T // tm),
        in_specs=[
            pl.BlockSpec((tm, D), lambda j, i: (i, 0)),
            pl.BlockSpec((D, tn), lambda j, i: (0, j)),
            pl.BlockSpec((1, tn), lambda j, i: (0, j)),
        ],
        out_specs=out_spec,
        out_shape=out_shape,
        compiler_params=_cparams(("parallel", "parallel")),
        name="inproj_" + mode,
    )(h, w, lb)


HG_CHUNK = 256
HG_DIAG = 16
HG_HEADS_PER_STEP = 4


def _hgrn_levels(C):
    out, m = [], C // 2
    while m >= HG_DIAG:
        out.append(m)
        m //= 2
    return out


def _hgrn_masks(C):
    t = np.arange(C)[:, None]
    s = np.arange(C)[None, :]
    masks = []
    for m in _hgrn_levels(C):
        masks.append((t // (2 * m) == s // (2 * m)) & ((t // m) % 2 == 1) & ((s // m) % 2 == 0))
    masks.append((t // HG_DIAG == s // HG_DIAG) & (t >= s))
    total = np.sum(np.stack(masks).astype(np.int32), axis=0)
    assert np.array_equal(total, (t >= s).astype(np.int32))
    return np.stack(masks).astype(np.float32)


def _dot_nt(a, b):
    return lax.dot_general(a, b, (((1,), (1,)), ((), ())), preferred_element_type=F32)


def _dot_tn(a, b):
    return lax.dot_general(a, b, (((0,), (0,)), ((), ())), preferred_element_type=F32)


def _hgrn_kernel(q_ref, lf_ref, v_ref, sg_ref, gain_ref, tril_ref, mask_ref, o_ref, st_ref, b_ref, *, C):
    S = q_ref.shape[2]
    HP = q_ref.shape[1]
    levels = _hgrn_levels(C)
    nchunks = S // C
    tril = tril_ref[...]

    def ref_rows(h, block, row_of_block):
        parts = [jnp.broadcast_to(b_ref[h, pl.ds(row_of_block(p), 1), :], (block, HEAD_DIM))
                 for p in range(C // block)]
        return jnp.concatenate(parts, axis=0)

    def cumsum(c):
        r0 = pl.multiple_of(c * C, C)
        out = []
        for h in range(HP):
            lf2 = lf_ref[0, h, pl.ds(r0, C), :] * LOG2E
            hi = lf2.astype(BF16)
            lo = (lf2 - hi.astype(F32)).astype(BF16)
            r = jnp.dot(tril, jnp.concatenate([hi, lo], axis=1), preferred_element_type=F32)
            out.append(r[:, :HEAD_DIM] + r[:, HEAD_DIM:])
        return tuple(out)

    st_ref[...] = jnp.zeros_like(st_ref)

    def chunk(c, b_all):
        r0 = pl.multiple_of(c * C, C)
        rows = pl.ds(r0, C)
        b_next = cumsum(jnp.minimum(c + 1, nchunks - 1))
        for h in range(HP):
            b = b_all[h]
            b_ref[h] = b
            q = q_ref[0, h, rows, :].astype(F32)
            kk = 1.0 - jnp.exp(lf_ref[0, h, rows, :])
            b_last = b_ref[h, pl.ds(C - 1, 1), :]
            scores = jnp.zeros((C, C), F32)
            for li, m in enumerate(levels):
                d = b - ref_rows(h, 2 * m, lambda p, m=m: 2 * m * p + m - 1)
                e = jnp.exp2(-jnp.abs(d))
                scores = scores + mask_ref[li] * _dot_nt((q * e).astype(BF16), (kk * e).astype(BF16))
            d = b - ref_rows(h, HG_DIAG, lambda p: HG_DIAG * p + HG_DIAG // 2 - 1)
            scores = scores + mask_ref[len(levels)] * _dot_nt((q * jnp.exp2(d)).astype(BF16),
                                                               (kk * jnp.exp2(-d)).astype(BF16))
            v = v_ref[0, h, rows, :]
            st = st_ref[h]
            qe = (q * jnp.exp2(b)).astype(BF16)
            kd = (kk * jnp.exp2(b_last - b)).astype(BF16)
            o = jnp.dot(scores.astype(BF16), v, preferred_element_type=F32) + _dot_nt(qe, st.astype(BF16))
            st_ref[h] = st * jnp.exp2(b_last) + _dot_tn(v, kd)
            ms = jnp.mean(o * o, axis=-1, keepdims=True)
            y = o * lax.rsqrt(ms + NORM_EPS) * gain_ref[:, h * HEAD_DIM:(h + 1) * HEAD_DIM]
            y = y * sg_ref[0, h, rows, :].astype(F32)
            o_ref[0, rows, h * HEAD_DIM:(h + 1) * HEAD_DIM] = y.astype(o_ref.dtype)
        return b_next

    lax.fori_loop(0, nchunks, chunk, cumsum(0))


def _hgrn(q, lf, v, sg, gain):
    B, H, S, E = q.shape
    C, HP = HG_CHUNK, HG_HEADS_PER_STEP
    masks = jnp.asarray(_hgrn_masks(C))
    tril = jnp.asarray(np.tril(np.ones((C, C), np.float32)), dtype=BF16)
    head_spec = pl.BlockSpec((1, HP, S, E), lambda b, h: (b, h, 0, 0))
    return pl.pallas_call(
        functools.partial(_hgrn_kernel, C=C),
        grid=(B, H // HP),
        in_specs=[
            head_spec, head_spec, head_spec, head_spec,
            pl.BlockSpec((1, HP * E), lambda b, h: (0, h)),
            pl.BlockSpec((C, C), lambda b, h: (0, 0)),
            pl.BlockSpec(masks.shape, lambda b, h: (0, 0, 0)),
        ],
        out_specs=pl.BlockSpec((1, S, HP * E), lambda b, h: (b, 0, h)),
        out_shape=jax.ShapeDtypeStruct((B, S, H * E), BF16),
        scratch_shapes=[pltpu.VMEM((HP, E, E), F32), pltpu.VMEM((HP, C, E), F32)],
        compiler_params=_cparams(("parallel", "parallel")),
        name="hgrn2",
    )(q, lf, v, sg, gain.reshape(1, H * E).astype(F32), tril, masks)


def _t5_bucket_np(dist):
    exact = REL_BUCKETS // 2
    d_f = np.maximum(dist, 1).astype(np.float32)
    log_b = exact + (np.log(d_f / np.float32(exact)) / np.float32(math.log(REL_MAX_DIST / exact))
                     * np.float32(REL_BUCKETS - exact)).astype(np.int32)
    return np.where(dist < exact, dist, np.minimum(log_b, REL_BUCKETS - 1))


def _attn_bias(rel_bias):
    blk = ATT_BLOCK
    period = 3 * blk
    out = []
    for gi, (window, dilation) in enumerate(ATT_GROUPS):
        n_back = window // dilation
        assert n_back <= blk
        hs = slice(gi * ATT_HEADS_PER_GROUP, (gi + 1) * ATT_HEADS_PER_GROUP)
        bucket = _t5_bucket_np(np.arange(n_back + 1) * dilation)
        by_delta = rel_bias[:, hs][bucket].astype(F32).T
        u = jnp.full((ATT_HEADS_PER_GROUP, period), NEG_BIG, F32)
        u = u.at[:, 2 * blk - 1 - n_back:2 * blk].set(by_delta[:, ::-1])
        flat = jnp.tile(u, (1, blk))[:, :blk * (period - 1)]
        out.append(flat.reshape(ATT_HEADS_PER_GROUP, blk, period - 1)[:, :, blk - 1:3 * blk - 1])
    return jnp.stack(out, axis=0)


def _attn_kernel(q_ref, k_ref, v_ref, bias_ref, o_ref, qf, kf, vf, og, lg):
    S = q_ref.shape[3]
    scale = HEAD_DIM ** -0.5
    blk = ATT_BLOCK

    for g, (window, d) in enumerate(ATT_GROUPS):
        L = S // d
        nb = L // blk
        if d > 1:
            qf[...] = q_ref[0, g, 0].astype(F32)
            kf[...] = k_ref[0, g, 0].astype(F32)
            vf[...] = v_ref[0, g, 0].astype(F32)

        def load(ref_bf, ref_f32, start, size, g=g, d=d):
            if d == 1:
                return ref_bf[0, g, 0, pl.ds(start, size), :]
            return ref_f32[pl.ds(start, size, stride=d), :].astype(BF16)

        def scores(r, n, g=g, d=d, load=load):
            start = r + n * blk * d
            q = load(q_ref, qf, start, blk)
            if n == 0:
                k = load(k_ref, kf, start, blk)
                v = load(v_ref, vf, start, blk)
                s = _dot_nt(q, k) * scale + bias_ref[g, 0, :, blk:]
            else:
                first = start - blk * d
                k = load(k_ref, kf, first, 2 * blk)
                v = load(v_ref, vf, first, 2 * blk)
                s = _dot_nt(q, k) * scale + bias_ref[g, 0]
            return start, s, v

        def softmax_pv(items, g=g, d=d):
            s = jnp.concatenate([it[1] for it in items], axis=0)
            m = jnp.max(s, axis=-1, keepdims=True)
            p = jnp.exp(s - m)
            den = jnp.sum(p, axis=-1, keepdims=True)
            pb = p.astype(BF16)
            lse = jnp.broadcast_to(m + jnp.log(den), (s.shape[0], HEAD_DIM))
            for j, (start, _, v) in enumerate(items):
                sl = slice(j * blk, (j + 1) * blk)
                rows = pl.ds(start, blk, stride=d) if d > 1 else pl.ds(start, blk)
                og[g, rows, :] = jnp.dot(pb[sl], v, preferred_element_type=F32) / den[sl]
                lg[g, rows, :] = lse[sl]

        blocks = [(r, n) for r in range(d) for n in range(nb)]
        first_blocks = [bn for bn in blocks if bn[1] == 0]
        later_blocks = [bn for bn in blocks if bn[1] > 0]
        for group in (first_blocks, later_blocks):
            for j0 in range(0, len(group), ATT_STACK):
                softmax_pv([scores(r, n) for r, n in group[j0:j0 + ATT_STACK]])

    rows_per_step = 256

    def mix(i, carry):
        rows = pl.ds(pl.multiple_of(i * rows_per_step, rows_per_step), rows_per_step)
        l0, l1, l2 = lg[0, rows, :], lg[1, rows, :], lg[2, rows, :]
        mx = jnp.maximum(jnp.maximum(l0, l1), l2)
        e0, e1, e2 = jnp.exp(l0 - mx), jnp.exp(l1 - mx), jnp.exp(l2 - mx)
        num = e0 * og[0, rows, :] + e1 * og[1, rows, :] + e2 * og[2, rows, :]
        o_ref[0, rows, :] = (num / (e0 + e1 + e2)).astype(o_ref.dtype)
        return carry

    lax.fori_loop(0, S // rows_per_step, mix, 0)


def _attention(qkv, bias):
    B, _, S, E = qkv.shape
    G, HG = len(ATT_GROUPS), ATT_HEADS_PER_GROUP
    x = qkv.reshape(B, 3, G, HG, S, E)

    def spec(which):
        return pl.BlockSpec((1, None, G, 1, S, E), lambda b, h, which=which: (b, which, 0, h, 0, 0))

    def kern(q_ref, k_ref, v_ref, bias_ref, o_ref, *scratch):
        _attn_kernel(q_ref, k_ref, v_ref, bias_ref, o_ref, *scratch)

    return pl.pallas_call(
        kern,
        grid=(B, HG),
        in_specs=[spec(0), spec(1), spec(2),
                  pl.BlockSpec((G, 1, ATT_BLOCK, 2 * ATT_BLOCK), lambda b, h: (0, h, 0, 0))],
        out_specs=pl.BlockSpec((1, S, E), lambda b, h: (b, 0, h)),
        out_shape=jax.ShapeDtypeStruct((B, S, HG * E), BF16),
        scratch_shapes=[pltpu.VMEM((S, E), F32)] * 3 + [pltpu.VMEM((G, S, E), F32)] * 2,
        compiler_params=_cparams(("parallel", "parallel")),
        name="dilated_attn",
    )(x, x, x, bias)


def _merge_kernel(ya_ref, yb_ref, wa_ref, wb_ref, ga_ref, gb_ref, o_ref):
    a = jnp.dot(ya_ref[...], wa_ref[...], preferred_element_type=F32)
    b = jnp.dot(yb_ref[...], wb_ref[...], preferred_element_type=F32)
    o_ref[...] = (ga_ref[...].astype(F32) * a + gb_ref[...].astype(F32) * b).astype(o_ref.dtype)


def _merge(ya, yb, wa, wb, gates, tm=1024, tn=1024):
    T, Ka = ya.shape
    Kb = yb.shape[1]
    N = wa.shape[1]
    nj = N // tn
    return pl.pallas_call(
        _merge_kernel,
        grid=(nj, T // tm),
        in_specs=[
            pl.BlockSpec((tm, Ka), lambda j, i: (i, 0)),
            pl.BlockSpec((tm, Kb), lambda j, i: (i, 0)),
            pl.BlockSpec((Ka, tn), lambda j, i: (0, j)),
            pl.BlockSpec((Kb, tn), lambda j, i: (0, j)),
            pl.BlockSpec((tm, tn), lambda j, i: (i, j)),
            pl.BlockSpec((tm, tn), lambda j, i, nj=nj: (i, j + nj)),
        ],
        out_specs=pl.BlockSpec((tm, tn), lambda j, i: (i, j)),
        out_shape=jax.ShapeDtypeStruct((T, N), BF16),
        compiler_params=_cparams(("parallel", "parallel")),
        name="branch_merge",
    )(ya, yb, wa, wb, gates, gates)


ROUTE_LANES = LANES


def _pack_halves(x_bf):
    bits = pltpu.bitcast(x_bf.astype(F32), jnp.uint32)
    n = bits.shape[1] // 2
    return (bits[:, :n] >> 16) | (bits[:, n:] & jnp.uint32(0xFFFF0000))


def _unpack_halves(p):
    lo = pltpu.bitcast(p << 16, F32)
    hi = pltpu.bitcast(p & jnp.uint32(0xFFFF0000), F32)
    return jnp.concatenate([lo, hi], axis=1)


ROW_TILE = 8


def _store_row_tiles(ref, packed):
    m = packed.shape[0]
    assert packed.shape[1] == ROW_TILE * LANES
    for c in range(ROW_TILE):
        ref[pl.ds(c, m, stride=ROW_TILE), :] = packed[:, c * LANES:(c + 1) * LANES]


def _load_row_tiles(ref, m):
    return jnp.concatenate([ref[pl.ds(c, m, stride=ROW_TILE), :] for c in range(ROW_TILE)], axis=1)


def _outproj_kernel(m_ref, w_ref, x_ref, g_ref, wrc_ref, br_ref, x1_ref, h2_ref, rt_ref):
    x1 = x_ref[...] + jnp.dot(m_ref[...], w_ref[...], preferred_element_type=F32)
    x1_ref[...] = x1
    ms = jnp.mean(x1 * x1, axis=-1, keepdims=True)
    h2 = x1 * lax.rsqrt(ms + NORM_EPS) * g_ref[...]
    h_hi = h2.astype(BF16)
    _store_row_tiles(h2_ref, _pack_halves(h_hi))
    h_lo = (h2 - h_hi.astype(F32)).astype(BF16)
    r = jnp.dot(h_hi, wrc_ref[...], preferred_element_type=F32)
    lg = (r[:, :ROUTE_LANES] + r[:, ROUTE_LANES:]
          + jnp.dot(h_lo, wrc_ref[:, :ROUTE_LANES], preferred_element_type=F32)) + br_ref[...]

    lane = lax.broadcasted_iota(jnp.int32, lg.shape, 1)
    lane_f = lane.astype(F32)
    big = float(ROUTE_LANES)
    is_group = (lane >= MOE_EXPERTS) & (lane < MOE_EXPERTS + MOE_GROUPS)
    lgg = jnp.where(is_group, lg, -jnp.inf)
    gmax = jnp.max(lgg, axis=-1, keepdims=True)
    gsel = jnp.min(jnp.where(lgg == gmax, lane_f - MOE_EXPERTS, big), axis=-1, keepdims=True)
    pg = 1.0 / jnp.sum(jnp.where(is_group, jnp.exp(lg - gmax), 0.0), axis=-1, keepdims=True)

    in_group = (lane < MOE_EXPERTS) & ((lane // MOE_EXPERTS_PER_GROUP).astype(F32) == gsel)
    le = jnp.where(in_group, lg, -jnp.inf)
    t1 = jnp.max(le, axis=-1, keepdims=True)
    i1 = jnp.min(jnp.where(le == t1, lane_f, big), axis=-1, keepdims=True)
    le2 = jnp.where(lane_f == i1, -jnp.inf, le)
    t2 = jnp.max(le2, axis=-1, keepdims=True)
    i2 = jnp.min(jnp.where(le2 == t2, lane_f, big), axis=-1, keepdims=True)
    e2 = jnp.exp(t2 - t1)
    w1 = pg / (1.0 + e2)
    w2 = pg * e2 / (1.0 + e2)
    rt_ref[...] = jnp.where(lane == 0, i1, jnp.where(lane == 1, i2, jnp.where(lane == 2, w1,
                            jnp.where(lane == 3, w2, 0.0))))


def _outproj(merged, w_out, x2d, gain, wr_cat, br, tm=512):
    T, D = x2d.shape
    row = lambda i: (i, 0)
    const = lambda i: (0, 0)
    return pl.pallas_call(
        _outproj_kernel,
        grid=(T // tm,),
        in_specs=[
            pl.BlockSpec((tm, D), row), pl.BlockSpec((D, D), const), pl.BlockSpec((tm, D), row),
            pl.BlockSpec((1, D), const), pl.BlockSpec((D, 2 * ROUTE_LANES), const),
            pl.BlockSpec((1, ROUTE_LANES), const),
        ],
        out_specs=[pl.BlockSpec((tm, D), row), pl.BlockSpec((tm * ROW_TILE, LANES), row),
                   pl.BlockSpec((tm, ROUTE_LANES), row)],
        out_shape=[jax.ShapeDtypeStruct((T, D), F32), jax.ShapeDtypeStruct((T * ROW_TILE, LANES), jnp.uint32),
                   jax.ShapeDtypeStruct((T, ROUTE_LANES), F32)],
        compiler_params=_cparams(("parallel",)),
        name="outproj_router",
    )(merged, w_out, x2d, gain.reshape(1, D).astype(F32), wr_cat, br)


def _dispatch_kernel(dest_ref, zb_ref, h_ref, xs_ref, zeros, sem, *, tq):
    base = pl.program_id(0) * tq
    block_rows = MOE_ROWS * ROW_TILE
    n_blocks = xs_ref.shape[0] // block_rows

    @pl.when(pl.program_id(0) == 0)
    def _():
        zeros[...] = jnp.zeros_like(zeros)

        def zero_copy(b):
            return pltpu.make_async_copy(zeros, xs_ref.at[pl.ds(pl.multiple_of(b * block_rows, block_rows), block_rows)],
                                         sem)

        def start(b, carry):
            @pl.when(zb_ref[b] != 0)
            def _():
                zero_copy(b).start()
            return carry

        def wait(b, carry):
            @pl.when(zb_ref[b] != 0)
            def _():
                zero_copy(b).wait()
            return carry

        lax.fori_loop(0, n_blocks, start, 0)
        lax.fori_loop(0, n_blocks, wait, 0)

    def token_rows(t):
        return pl.ds(pl.multiple_of(t * ROW_TILE, ROW_TILE), ROW_TILE)

    def copy(j, k):
        return pltpu.make_async_copy(h_ref.at[token_rows(j)], xs_ref.at[token_rows(dest_ref[2 * (base + j) + k])], sem)

    def issue(j, carry):
        copy(j, 0).start()
        copy(j, 1).start()
        return carry

    lax.fori_loop(0, tq, issue, 0)
    for _ in range(MOE_TOPK):
        pltpu.make_async_copy(h_ref, xs_ref.at[pl.ds(0, tq * ROW_TILE)], sem).wait()


def _dispatch(h2, dest_flat, zero_block, n_rows, tq=512):
    T = h2.shape[0] // ROW_TILE
    return pl.pallas_call(
        functools.partial(_dispatch_kernel, tq=tq),
        grid_spec=pltpu.PrefetchScalarGridSpec(
            num_scalar_prefetch=2,
            grid=(T // tq,),
            in_specs=[pl.BlockSpec((tq * ROW_TILE, LANES), lambda i, d, z: (i, 0))],
            out_specs=pl.BlockSpec(memory_space=pl.ANY),
            scratch_shapes=[pltpu.VMEM((MOE_ROWS * ROW_TILE, LANES), h2.dtype), pltpu.SemaphoreType.DMA(())],
        ),
        out_shape=jax.ShapeDtypeStruct((n_rows * ROW_TILE, LANES), h2.dtype),
        compiler_params=_cparams(("arbitrary",)),
        name="moe_dispatch",
    )(dest_flat, zero_block, h2)


MOE_WEIGHT_SLOTS = 3


def _expert_kernel(be_ref, nu_ref, rank_ref, kth_ref, xs_ref, wg_hbm, wu_hbm, wd_hbm, o_ref,
                   wg_f, wu_f, wd_f, wg_s, wu_s, wd_s, sem):
    i = pl.program_id(0)
    e = be_ref[i]
    active = i < nu_ref[0]
    new_expert = active & ((i == 0) | (e != be_ref[jnp.maximum(i - 1, 0)]))
    k = rank_ref[e]
    slot = k % MOE_WEIGHT_SLOTS
    ahead = MOE_WEIGHT_SLOTS - 1

    def weight_copies(expert, s):
        return (pltpu.make_async_copy(wg_hbm.at[expert], wg_f.at[s], sem.at[s, 0]),
                pltpu.make_async_copy(wu_hbm.at[expert], wu_f.at[s], sem.at[s, 1]),
                pltpu.make_async_copy(wd_hbm.at[expert], wd_f.at[s], sem.at[s, 2]))

    def start_kth(j):
        nxt = kth_ref[j]

        @pl.when(nxt >= 0)
        def _():
            for c in weight_copies(nxt, j % MOE_WEIGHT_SLOTS):
                c.start()

    @pl.when(active & (i == 0))
    def _():
        for j in range(ahead):
            start_kth(j)

    @pl.when(new_expert)
    def _():
        start_kth(k + ahead)
        for c in weight_copies(e, slot):
            c.wait()
        wg_s[...] = wg_f[slot].astype(BF16)
        wu_s[...] = wu_f[slot].astype(BF16)
        wd_s[...] = wd_f[slot].astype(BF16)

    @pl.when(active)
    def _():
        x = _unpack_halves(_load_row_tiles(xs_ref, MOE_ROWS)).astype(BF16)
        g = jnp.dot(x, wg_s[...], preferred_element_type=F32)
        u = jnp.dot(x, wu_s[...], preferred_element_type=F32)
        a = (g * jax.nn.sigmoid(g) * u).astype(BF16)
        y = jnp.dot(a, wd_s[...], preferred_element_type=F32)
        _store_row_tiles(o_ref, _pack_halves(y.astype(BF16)))

    @pl.when(jnp.logical_not(active))
    def _():
        o_ref[...] = jnp.zeros_like(o_ref)


def _experts(xs, block_expert, n_used, expert_rank, kth_expert, w_gate, w_up, w_down):
    Dh = xs.shape[1]
    D, F = w_gate.shape[1], w_gate.shape[2]
    assert Dh == LANES and D == 2 * ROW_TILE * LANES
    nblk = block_expert.shape[0]
    P = nblk * MOE_ROWS * ROW_TILE
    rows = lambda i, be, nu, nx, sl: (jnp.minimum(i, nu[0] - 1), 0)
    hbm = pl.BlockSpec(memory_space=pl.ANY)
    return pl.pallas_call(
        _expert_kernel,
        grid_spec=pltpu.PrefetchScalarGridSpec(
            num_scalar_prefetch=4,
            grid=(nblk,),
            in_specs=[pl.BlockSpec((MOE_ROWS * ROW_TILE, Dh), rows), hbm, hbm, hbm],
            out_specs=pl.BlockSpec((MOE_ROWS * ROW_TILE, Dh), lambda i, be, nu, nx, sl: (i, 0)),
            scratch_shapes=[pltpu.VMEM((MOE_WEIGHT_SLOTS, D, F), F32), pltpu.VMEM((MOE_WEIGHT_SLOTS, D, F), F32),
                            pltpu.VMEM((MOE_WEIGHT_SLOTS, F, D), F32),
                            pltpu.VMEM((D, F), BF16), pltpu.VMEM((D, F), BF16), pltpu.VMEM((F, D), BF16),
                            pltpu.SemaphoreType.DMA((MOE_WEIGHT_SLOTS, 3))],
        ),
        out_shape=jax.ShapeDtypeStruct((P, Dh), jnp.uint32),
        compiler_params=_cparams(("arbitrary",)),
        name="moe_experts",
    )(block_expert, n_used, expert_rank, kth_expert, xs, w_gate, w_up, w_down)


def _combine_kernel(dest_ref, x1_ref, rt_ref, g_ref, ys_ref, o_ref, buf, sem, *, tq):
    i = pl.program_id(0)
    n = pl.num_programs(0)

    def token_rows(t):
        return pl.ds(pl.multiple_of(t * ROW_TILE, ROW_TILE), ROW_TILE)

    def issue(step, s):
        def body(j, carry):
            for k in range(MOE_TOPK):
                pltpu.make_async_copy(ys_ref.at[token_rows(dest_ref[2 * (step * tq + j) + k])],
                                      buf.at[s, k, token_rows(j)], sem.at[s]).start()
            return carry
        lax.fori_loop(0, tq, body, 0)

    @pl.when(i == 0)
    def _():
        issue(0, 0)

    for s in range(2):
        @pl.when((i + 1 < n) & ((i + 1) % 2 == s))
        def _():
            issue(i + 1, s)

    for s in range(2):
        @pl.when(i % 2 == s)
        def _():
            for k in range(MOE_TOPK):
                pltpu.make_async_copy(ys_ref.at[pl.ds(0, tq * ROW_TILE)], buf.at[s, k], sem.at[s]).wait()
            rt = rt_ref[...]
            y0 = _unpack_halves(_load_row_tiles(buf.at[s, 0], tq))
            y1 = _unpack_halves(_load_row_tiles(buf.at[s, 1], tq))
            x = x1_ref[...] + rt[:, 2:3] * y0 + rt[:, 3:4] * y1
            ms = jnp.mean(x * x, axis=-1, keepdims=True)
            o_ref[...] = x * lax.rsqrt(ms + NORM_EPS) * g_ref[...]


def _combine(x1, route, gain, ys, dest_flat, tq=256):
    T, D = x1.shape
    return pl.pallas_call(
        functools.partial(_combine_kernel, tq=tq),
        grid_spec=pltpu.PrefetchScalarGridSpec(
            num_scalar_prefetch=1,
            grid=(T // tq,),
            in_specs=[
                pl.BlockSpec((tq, D), lambda i, d: (i, 0)),
                pl.BlockSpec((tq, ROUTE_LANES), lambda i, d: (i, 0)),
                pl.BlockSpec((1, D), lambda i, d: (0, 0)),
                pl.BlockSpec(memory_space=pl.ANY),
            ],
            out_specs=pl.BlockSpec((tq, D), lambda i, d: (i, 0)),
            scratch_shapes=[pltpu.VMEM((2, MOE_TOPK, tq * ROW_TILE, LANES), jnp.uint32),
                            pltpu.SemaphoreType.DMA((2,))],
        ),
        out_shape=jax.ShapeDtypeStruct((T, D), F32),
        compiler_params=_cparams(("arbitrary",)),
        name="moe_combine",
    )(dest_flat, x1, route, gain.reshape(1, D).astype(F32), ys)


def _route_metadata(route, T):
    e = route[:, :MOE_TOPK].astype(jnp.int32)
    ids = jnp.arange(MOE_EXPERTS, dtype=jnp.int32)
    oh0 = e[:, 0, None] == ids
    oh1 = e[:, 1, None] == ids
    onehot = (oh0 | oh1).astype(jnp.int32)
    incl = jnp.cumsum(onehot, axis=0)
    counts = incl[-1]
    pcounts = (counts + MOE_ROWS - 1) // MOE_ROWS * MOE_ROWS
    pends = jnp.cumsum(pcounts)
    poffs = pends - pcounts
    slot = incl - onehot + poffs[None, :]
    dest = jnp.stack([jnp.sum(jnp.where(oh0, slot, 0), axis=1), jnp.sum(jnp.where(oh1, slot, 0), axis=1)], axis=1)
    nblk = (T * MOE_TOPK) // MOE_ROWS + MOE_EXPERTS
    block_expert = jnp.minimum(
        jnp.searchsorted(pends, jnp.arange(nblk, dtype=jnp.int32) * MOE_ROWS, side="right"),
        MOE_EXPERTS - 1).astype(jnp.int32)
    n_used = (pends[-1:] // MOE_ROWS).astype(jnp.int32)
    used = counts > 0
    expert_rank = (jnp.cumsum(used.astype(jnp.int32)) - 1).astype(jnp.int32)
    order = jnp.argsort(jnp.where(used, ids, ids + MOE_EXPERTS)).astype(jnp.int32)
    kth = jnp.where(ids < jnp.sum(used), order, -1)
    kth_expert = jnp.concatenate([kth, jnp.full((MOE_WEIGHT_SLOTS,), -1, jnp.int32)])
    blk = jnp.arange(nblk, dtype=jnp.int32)
    following = jnp.concatenate([block_expert[1:], block_expert[-1:]])
    zero_block = ((blk >= n_used[0] - 1) | (block_expert != following)).astype(jnp.int32)
    return dest.reshape(-1).astype(jnp.int32), zero_block, block_expert, n_used, expert_rank, kth_expert, nblk


def kernel(x, norm1_gain, w_in, hg_lb_logits, hg_norm_gain, rel_bias, w_branch_a, w_branch_b, w_out,
           norm2_gain, w_router_group, b_router_group, w_router_expert, b_router_expert,
           w_exp_gate, w_exp_up, w_exp_down, final_norm_gain):
    B, S, D = x.shape
    T = B * S
    depth = norm1_gain.shape[0]
    lower_bounds = jnp.cumsum(jax.nn.softmax(hg_lb_logits.astype(F32), axis=0), axis=0)
    att_w = ATT_HEADS * HEAD_DIM
    bias = _attn_bias(rel_bias)
    x2d = x.reshape(T, D)
    for layer in range(depth):
        w = w_in[layer]
        zeros_lb = jnp.zeros((1, D), F32)
        seg = lambda a, n: w[:, a:a + n].astype(BF16)
        hm = dict(head_major=True, B=B, S=S)
        h, q_a = _normproj(x2d, norm1_gain[layer], seg(0, D), B=B, S=S)
        lf_a = _inproj(h, seg(D, D), lower_bounds[layer].reshape(1, D), mode="logf", out_dtype=F32, **hm)
        i_a = _inproj(h, seg(2 * D, D), zeros_lb, mode="none", out_dtype=BF16, **hm)
        sg_a = _inproj(h, seg(3 * D, D), zeros_lb, mode="sigmoid", out_dtype=BF16, **hm)
        qkv_b = _inproj(h, seg(4 * D, 3 * att_w), jnp.zeros((1, 3 * att_w), F32), mode="none",
                        out_dtype=BF16, tn=att_w, **hm)
        gates = _inproj(h, seg(4 * D + 3 * att_w, 2 * D), jnp.zeros((1, 2 * D), F32), mode="sigmoid",
                        head_major=False, out_dtype=BF16, B=B, S=S)

        y_a = _hgrn(q_a, lf_a, i_a, sg_a, hg_norm_gain[layer]).reshape(T, D)
        y_b = _attention(qkv_b, bias).reshape(T, ATT_HEADS_PER_GROUP * HEAD_DIM)
        merged = _merge(y_a, y_b, w_branch_a[layer].astype(BF16), w_branch_b[layer].astype(BF16), gates)

        wr = jnp.zeros((D, ROUTE_LANES), F32)
        wr = wr.at[:, :MOE_EXPERTS].set(w_router_expert[layer].astype(F32))
        wr = wr.at[:, MOE_EXPERTS:MOE_EXPERTS + MOE_GROUPS].set(w_router_group[layer].astype(F32))
        br = jnp.zeros((1, ROUTE_LANES), F32)
        br = br.at[0, :MOE_EXPERTS].set(b_router_expert[layer].astype(F32))
        br = br.at[0, MOE_EXPERTS:MOE_EXPERTS + MOE_GROUPS].set(b_router_group[layer].astype(F32))
        wr_hi = wr.astype(BF16)
        wr_lo = (wr - wr_hi.astype(F32)).astype(BF16)
        x1, h2, route = _outproj(merged, w_out[layer].astype(BF16), x2d, norm2_gain[layer],
                                 jnp.concatenate([wr_hi, wr_lo], axis=1), br)

        dest, zero_block, block_expert, n_used, expert_rank, kth_expert, nblk = _route_metadata(route, T)
        xs = _dispatch(h2, dest, zero_block, nblk * MOE_ROWS)
        ys = _experts(xs, block_expert, n_used, expert_rank, kth_expert,
                      w_exp_gate[layer], w_exp_up[layer], w_exp_down[layer])
        last = layer == depth - 1
        assert last, "the fused combine applies the final norm; deeper stacks need an un-normalised combine"
        x2d = _combine(x1, route, final_norm_gain, ys, dest)
    return x2d.reshape(B, S, D)
```

```python
import functools
import math

import numpy as np
import jax
import jax.numpy as jnp
from jax import lax
from jax.experimental import pallas as pl
from jax.experimental.pallas import tpu as pltpu

F32 = jnp.float32
BF16 = jnp.bfloat16

LANES = 128
NORM_EPS = 1e-6
HEAD_DIM = 128
HG_HEADS = 16
ATT_GROUPS = ((128, 1), (512, 4), (2048, 16))
ATT_HEADS_PER_GROUP = 4
ATT_HEADS = len(ATT_GROUPS) * ATT_HEADS_PER_GROUP
ATT_BLOCK = 128
ATT_STACK = 16
REL_BUCKETS = 32
REL_MAX_DIST = 2048
MOE_GROUPS = 8
MOE_EXPERTS_PER_GROUP = 8
MOE_EXPERTS = MOE_GROUPS * MOE_EXPERTS_PER_GROUP
MOE_TOPK = 2
MOE_ROWS = 256
NEG_BIG = -1e30
LOG2E = 1.4426950408889634
VMEM_LIMIT = 56 * 1024 * 1024


def _cparams(sem):
    return pltpu.CompilerParams(dimension_semantics=sem, vmem_limit_bytes=VMEM_LIMIT)


def _normproj_kernel(x_ref, g_ref, w_ref, h_ref, o_ref):
    @pl.when(pl.program_id(1) == 0)
    def _():
        x = x_ref[...]
        ms = jnp.mean(x * x, axis=-1, keepdims=True)
        h_ref[...] = (x * lax.rsqrt(ms + NORM_EPS) * g_ref[...]).astype(h_ref.dtype)

    acc = jnp.dot(h_ref[...], w_ref[...], preferred_element_type=F32)
    for hh in range(acc.shape[1] // HEAD_DIM):
        o_ref[0, hh] = acc[:, hh * HEAD_DIM:(hh + 1) * HEAD_DIM].astype(o_ref.dtype)


def _normproj(x2d, gain, w, *, B, S, tm=1024, tn=1024):
    T, D = x2d.shape
    N = w.shape[1]
    assert N % tn == 0 and S % tm == 0
    spb = S // tm
    return pl.pallas_call(
        _normproj_kernel,
        grid=(T // tm, N // tn),
        in_specs=[
            pl.BlockSpec((tm, D), lambda i, j: (i, 0)),
            pl.BlockSpec((1, D), lambda i, j: (0, 0)),
            pl.BlockSpec((D, tn), lambda i, j: (0, j)),
        ],
        out_specs=[
            pl.BlockSpec((tm, D), lambda i, j: (i, 0)),
            pl.BlockSpec((1, tn // HEAD_DIM, tm, HEAD_DIM), lambda i, j: (i // spb, j, i % spb, 0)),
        ],
        out_shape=[jax.ShapeDtypeStruct((T, D), BF16),
                   jax.ShapeDtypeStruct((B, N // HEAD_DIM, S, HEAD_DIM), BF16)],
        compiler_params=_cparams(("parallel", "arbitrary")),
        name="norm_inproj",
    )(x2d, gain.reshape(1, D).astype(F32), w)


def _inproj_kernel(h_ref, w_ref, lb_ref, o_ref, w_bf, *, mode, head_major):
    @pl.when(pl.program_id(1) == 0)
    def _():
        w_bf[...] = w_ref[...].astype(BF16)

    acc = jnp.dot(h_ref[...], w_bf[...], preferred_element_type=F32)
    if mode == "logf":
        lb = lb_ref[...]
        acc = jnp.log(lb + (1.0 - lb) * jax.nn.sigmoid(acc))
    elif mode == "sigmoid":
        acc = jax.nn.sigmoid(acc)
    if head_major:
        for hh in range(acc.shape[1] // HEAD_DIM):
            o_ref[0, hh] = acc[:, hh * HEAD_DIM:(hh + 1) * HEAD_DIM].astype(o_ref.dtype)
    else:
        o_ref[...] = acc.astype(o_ref.dtype)


def _inproj(h, w_all, layer, col0, N, lb, *, mode, head_major, out_dtype, B, S, tm=1024, tn=1024):
    T, D = h.shape
    assert N % tn == 0 and col0 % tn == 0 and T % tm == 0 and S % tm == 0
    spb = S // tm
    j0 = col0 // tn
    if head_major:
        out_shape = jax.ShapeDtypeStruct((B, N // HEAD_DIM, S, HEAD_DIM), out_dtype)
        out_spec = pl.BlockSpec((1, tn // HEAD_DIM, tm, HEAD_DIM), lambda j, i: (i // spb, j, i % spb, 0))
    else:
        out_shape = jax.ShapeDtypeStruct((T, N), out_dtype)
        out_spec = pl.BlockSpec((tm, tn), lambda j, i: (i, j))
    return pl.pallas_call(
        functools.partial(_inproj_kernel, mode=mode, head_major=head_major),
        grid=(N // tn, T // tm),
        in_specs=[
            pl.BlockSpec((tm, D), lambda j, i: (i, 0)),
            pl.BlockSpec((None, D, tn), lambda j, i: (layer, 0, j0 + j)),
            pl.BlockSpec((1, tn), lambda j, i: (0, j)),
        ],
        out_specs=out_spec,
        out_shape=out_shape,
        scratch_shapes=[pltpu.VMEM((D, tn), BF16)],
        compiler_params=_cparams(("parallel", "arbitrary")),
        name="inproj_" + mode,
    )(h, w_all, lb)


HG_CHUNK = 256
HG_DIAG = 16
HG_HEADS_PER_STEP = 4


def _hgrn_levels(C):
    out, m = [], C // 2
    while m >= HG_DIAG:
        out.append(m)
        m //= 2
    return out


def _hgrn_masks(C):
    t = np.arange(C)[:, None]
    s = np.arange(C)[None, :]
    masks = []
    for m in _hgrn_levels(C):
        masks.append((t // (2 * m) == s // (2 * m)) & ((t // m) % 2 == 1) & ((s // m) % 2 == 0))
    masks.append((t // HG_DIAG == s // HG_DIAG) & (t >= s))
    total = np.sum(np.stack(masks).astype(np.int32), axis=0)
    assert np.array_equal(total, (t >= s).astype(np.int32))
    return np.stack(masks).astype(np.float32)


def _dot_nt(a, b):
    return lax.dot_general(a, b, (((1,), (1,)), ((), ())), preferred_element_type=F32)


def _dot_tn(a, b):
    return lax.dot_general(a, b, (((0,), (0,)), ((), ())), preferred_element_type=F32)


def _hgrn_kernel(q_ref, lf_ref, v_ref, sg_ref, gain_ref, tril_ref, mask_ref, o_ref, st_ref, b_ref, *, C):
    S = q_ref.shape[2]
    HP = q_ref.shape[1]
    levels = _hgrn_levels(C)
    nchunks = S // C
    tril = tril_ref[...]

    def ref_rows(h, block, row_of_block):
        parts = [jnp.broadcast_to(b_ref[h, pl.ds(row_of_block(p), 1), :], (block, HEAD_DIM))
                 for p in range(C // block)]
        return jnp.concatenate(parts, axis=0)

    def cumsum(c):
        r0 = pl.multiple_of(c * C, C)
        out = []
        for h in range(HP):
            lf2 = lf_ref[0, h, pl.ds(r0, C), :] * LOG2E
            hi = lf2.astype(BF16)
            lo = (lf2 - hi.astype(F32)).astype(BF16)
            r = jnp.dot(tril, jnp.concatenate([hi, lo], axis=1), preferred_element_type=F32)
            out.append(r[:, :HEAD_DIM] + r[:, HEAD_DIM:])
        return tuple(out)

    st_ref[...] = jnp.zeros_like(st_ref)

    def chunk(c, b_all):
        r0 = pl.multiple_of(c * C, C)
        rows = pl.ds(r0, C)
        b_next = cumsum(jnp.minimum(c + 1, nchunks - 1))
        for h in range(HP):
            b = b_all[h]
            b_ref[h] = b
            q = q_ref[0, h, rows, :].astype(F32)
            kk = 1.0 - jnp.exp(lf_ref[0, h, rows, :])
            b_last = b_ref[h, pl.ds(C - 1, 1), :]
            scores = jnp.zeros((C, C), F32)
            for li, m in enumerate(levels):
                d = b - ref_rows(h, 2 * m, lambda p, m=m: 2 * m * p + m - 1)
                e = jnp.exp2(-jnp.abs(d))
                scores = scores + mask_ref[li] * _dot_nt((q * e).astype(BF16), (kk * e).astype(BF16))
            d = b - ref_rows(h, HG_DIAG, lambda p: HG_DIAG * p + HG_DIAG // 2 - 1)
            scores = scores + mask_ref[len(levels)] * _dot_nt((q * jnp.exp2(d)).astype(BF16),
                                                               (kk * jnp.exp2(-d)).astype(BF16))
            v = v_ref[0, h, rows, :]
            st = st_ref[h]
            qe = (q * jnp.exp2(b)).astype(BF16)
            kd = (kk * jnp.exp2(b_last - b)).astype(BF16)
            o = jnp.dot(scores.astype(BF16), v, preferred_element_type=F32) + _dot_nt(qe, st.astype(BF16))
            st_ref[h] = st * jnp.exp2(b_last) + _dot_tn(v, kd)
            ms = jnp.mean(o * o, axis=-1, keepdims=True)
            y = o * lax.rsqrt(ms + NORM_EPS) * gain_ref[:, h * HEAD_DIM:(h + 1) * HEAD_DIM]
            y = y * sg_ref[0, h, rows, :].astype(F32)
            o_ref[0, rows, h * HEAD_DIM:(h + 1) * HEAD_DIM] = y.astype(o_ref.dtype)
        return b_next

    lax.fori_loop(0, nchunks, chunk, cumsum(0))


def _hgrn(q, lf, v, sg, gain):
    B, H, S, E = q.shape
    C, HP = HG_CHUNK, HG_HEADS_PER_STEP
    masks = jnp.asarray(_hgrn_masks(C))
    tril = jnp.asarray(np.tril(np.ones((C, C), np.float32)), dtype=BF16)
    head_spec = pl.BlockSpec((1, HP, S, E), lambda b, h: (b, h, 0, 0))
    return pl.pallas_call(
        functools.partial(_hgrn_kernel, C=C),
        grid=(B, H // HP),
        in_specs=[
            head_spec, head_spec, head_spec, head_spec,
            pl.BlockSpec((1, HP * E), lambda b, h: (0, h)),
            pl.BlockSpec((C, C), lambda b, h: (0, 0)),
            pl.BlockSpec(masks.shape, lambda b, h: (0, 0, 0)),
        ],
        out_specs=pl.BlockSpec((1, S, HP * E), lambda b, h: (b, 0, h)),
        out_shape=jax.ShapeDtypeStruct((B, S, H * E), BF16),
        scratch_shapes=[pltpu.VMEM((HP, E, E), F32), pltpu.VMEM((HP, C, E), F32)],
        compiler_params=_cparams(("parallel", "parallel")),
        name="hgrn2",
    )(q, lf, v, sg, gain.reshape(1, H * E).astype(F32), tril, masks)


def _t5_bucket_np(dist):
    exact = REL_BUCKETS // 2
    d_f = np.maximum(dist, 1).astype(np.float32)
    log_b = exact + (np.log(d_f / np.float32(exact)) / np.float32(math.log(REL_MAX_DIST / exact))
                     * np.float32(REL_BUCKETS - exact)).astype(np.int32)
    return np.where(dist < exact, dist, np.minimum(log_b, REL_BUCKETS - 1))


def _attn_bias(rel_bias):
    blk = ATT_BLOCK
    period = 3 * blk
    out = []
    for gi, (window, dilation) in enumerate(ATT_GROUPS):
        n_back = window // dilation
        assert n_back <= blk
        hs = slice(gi * ATT_HEADS_PER_GROUP, (gi + 1) * ATT_HEADS_PER_GROUP)
        bucket = _t5_bucket_np(np.arange(n_back + 1) * dilation)
        by_delta = rel_bias[:, hs][bucket].astype(F32).T
        u = jnp.full((ATT_HEADS_PER_GROUP, period), NEG_BIG, F32)
        u = u.at[:, 2 * blk - 1 - n_back:2 * blk].set(by_delta[:, ::-1])
        flat = jnp.tile(u, (1, blk))[:, :blk * (period - 1)]
        out.append(flat.reshape(ATT_HEADS_PER_GROUP, blk, period - 1)[:, :, blk - 1:3 * blk - 1])
    return jnp.stack(out, axis=0)


def _attn_kernel(q_ref, k_ref, v_ref, bias_ref, o_ref, qf, kf, vf, og, lg):
    S = q_ref.shape[3]
    scale = HEAD_DIM ** -0.5
    blk = ATT_BLOCK

    for g, (window, d) in enumerate(ATT_GROUPS):
        L = S // d
        nb = L // blk
        if d > 1:
            qf[...] = q_ref[0, g, 0].astype(F32)
            kf[...] = k_ref[0, g, 0].astype(F32)
            vf[...] = v_ref[0, g, 0].astype(F32)

        def load(ref_bf, ref_f32, start, size, g=g, d=d):
            if d == 1:
                return ref_bf[0, g, 0, pl.ds(start, size), :]
            return ref_f32[pl.ds(start, size, stride=d), :].astype(BF16)

        def scores(r, n, g=g, d=d, load=load):
            start = r + n * blk * d
            q = load(q_ref, qf, start, blk)
            if n == 0:
                k = load(k_ref, kf, start, blk)
                v = load(v_ref, vf, start, blk)
                s = _dot_nt(q, k) * scale + bias_ref[g, 0, :, blk:]
            else:
                first = start - blk * d
                k = load(k_ref, kf, first, 2 * blk)
                v = load(v_ref, vf, first, 2 * blk)
                s = _dot_nt(q, k) * scale + bias_ref[g, 0]
            return start, s, v

        def softmax_pv(items, g=g, d=d):
            s = jnp.concatenate([it[1] for it in items], axis=0)
            m = jnp.max(s, axis=-1, keepdims=True)
            p = jnp.exp(s - m)
            den = jnp.sum(p, axis=-1, keepdims=True)
            pb = p.astype(BF16)
            lse = jnp.broadcast_to(m + jnp.log(den), (s.shape[0], HEAD_DIM))
            for j, (start, _, v) in enumerate(items):
                sl = slice(j * blk, (j + 1) * blk)
                rows = pl.ds(start, blk, stride=d) if d > 1 else pl.ds(start, blk)
                og[g, rows, :] = jnp.dot(pb[sl], v, preferred_element_type=F32) / den[sl]
                lg[g, rows, :] = lse[sl]

        blocks = [(r, n) for r in range(d) for n in range(nb)]
        first_blocks = [bn for bn in blocks if bn[1] == 0]
        later_blocks = [bn for bn in blocks if bn[1] > 0]
        for group in (first_blocks, later_blocks):
            for j0 in range(0, len(group), ATT_STACK):
                softmax_pv([scores(r, n) for r, n in group[j0:j0 + ATT_STACK]])

    rows_per_step = 256

    def mix(i, carry):
        rows = pl.ds(pl.multiple_of(i * rows_per_step, rows_per_step), rows_per_step)
        l0, l1, l2 = lg[0, rows, :], lg[1, rows, :], lg[2, rows, :]
        mx = jnp.maximum(jnp.maximum(l0, l1), l2)
        e0, e1, e2 = jnp.exp(l0 - mx), jnp.exp(l1 - mx), jnp.exp(l2 - mx)
        num = e0 * og[0, rows, :] + e1 * og[1, rows, :] + e2 * og[2, rows, :]
        o_ref[0, rows, :] = (num / (e0 + e1 + e2)).astype(o_ref.dtype)
        return carry

    lax.fori_loop(0, S // rows_per_step, mix, 0)


def _attention(qkv, bias):
    B, _, S, E = qkv.shape
    G, HG = len(ATT_GROUPS), ATT_HEADS_PER_GROUP
    x = qkv.reshape(B, 3, G, HG, S, E)

    def spec(which):
        return pl.BlockSpec((1, None, G, 1, S, E), lambda b, h, which=which: (b, which, 0, h, 0, 0))

    def kern(q_ref, k_ref, v_ref, bias_ref, o_ref, *scratch):
        _attn_kernel(q_ref, k_ref, v_ref, bias_ref, o_ref, *scratch)

    return pl.pallas_call(
        kern,
        grid=(B, HG),
        in_specs=[spec(0), spec(1), spec(2),
                  pl.BlockSpec((G, 1, ATT_BLOCK, 2 * ATT_BLOCK), lambda b, h: (0, h, 0, 0))],
        out_specs=pl.BlockSpec((1, S, E), lambda b, h: (b, 0, h)),
        out_shape=jax.ShapeDtypeStruct((B, S, HG * E), BF16),
        scratch_shapes=[pltpu.VMEM((S, E), F32)] * 3 + [pltpu.VMEM((G, S, E), F32)] * 2,
        compiler_params=_cparams(("parallel", "parallel")),
        name="dilated_attn",
    )(x, x, x, bias)


def _merge_kernel(ya_ref, yb_ref, wa_ref, wb_ref, ga_ref, gb_ref, o_ref):
    a = jnp.dot(ya_ref[...], wa_ref[...], preferred_element_type=F32)
    b = jnp.dot(yb_ref[...], wb_ref[...], preferred_element_type=F32)
    o_ref[...] = (ga_ref[...].astype(F32) * a + gb_ref[...].astype(F32) * b).astype(o_ref.dtype)


def _merge(ya, yb, wa, wb, gates, tm=1024, tn=1024):
    T, Ka = ya.shape
    Kb = yb.shape[1]
    N = wa.shape[1]
    nj = N // tn
    return pl.pallas_call(
        _merge_kernel,
        grid=(nj, T // tm),
        in_specs=[
            pl.BlockSpec((tm, Ka), lambda j, i: (i, 0)),
            pl.BlockSpec((tm, Kb), lambda j, i: (i, 0)),
            pl.BlockSpec((Ka, tn), lambda j, i: (0, j)),
            pl.BlockSpec((Kb, tn), lambda j, i: (0, j)),
            pl.BlockSpec((tm, tn), lambda j, i: (i, j)),
            pl.BlockSpec((tm, tn), lambda j, i, nj=nj: (i, j + nj)),
        ],
        out_specs=pl.BlockSpec((tm, tn), lambda j, i: (i, j)),
        out_shape=jax.ShapeDtypeStruct((T, N), BF16),
        compiler_params=_cparams(("parallel", "parallel")),
        name="branch_merge",
    )(ya, yb, wa, wb, gates, gates)


ROUTE_LANES = LANES


def _pack_halves(x_bf):
    bits = pltpu.bitcast(x_bf.astype(F32), jnp.uint32)
    n = bits.shape[1] // 2
    return (bits[:, :n] >> 16) | (bits[:, n:] & jnp.uint32(0xFFFF0000))


def _unpack_halves(p):
    lo = pltpu.bitcast(p << 16, F32)
    hi = pltpu.bitcast(p & jnp.uint32(0xFFFF0000), F32)
    return jnp.concatenate([lo, hi], axis=1)


ROW_TILE = 8


def _store_row_tiles(ref, packed):
    m = packed.shape[0]
    assert packed.shape[1] == ROW_TILE * LANES
    for c in range(ROW_TILE):
        ref[pl.ds(c, m, stride=ROW_TILE), :] = packed[:, c * LANES:(c + 1) * LANES]


def _load_row_tiles(ref, m):
    return jnp.concatenate([ref[pl.ds(c, m, stride=ROW_TILE), :] for c in range(ROW_TILE)], axis=1)


def _outproj_kernel(m_ref, w_ref, x_ref, g_ref, wrc_ref, br_ref, x1_ref, h2_ref, rt_ref):
    x1 = x_ref[...] + jnp.dot(m_ref[...], w_ref[...], preferred_element_type=F32)
    x1_ref[...] = x1
    ms = jnp.mean(x1 * x1, axis=-1, keepdims=True)
    h2 = x1 * lax.rsqrt(ms + NORM_EPS) * g_ref[...]
    h_hi = h2.astype(BF16)
    _store_row_tiles(h2_ref, _pack_halves(h_hi))
    h_lo = (h2 - h_hi.astype(F32)).astype(BF16)
    r = jnp.dot(h_hi, wrc_ref[...], preferred_element_type=F32)
    lg = (r[:, :ROUTE_LANES] + r[:, ROUTE_LANES:]
          + jnp.dot(h_lo, wrc_ref[:, :ROUTE_LANES], preferred_element_type=F32)) + br_ref[...]

    lane = lax.broadcasted_iota(jnp.int32, lg.shape, 1)
    lane_f = lane.astype(F32)
    big = float(ROUTE_LANES)
    is_group = (lane >= MOE_EXPERTS) & (lane < MOE_EXPERTS + MOE_GROUPS)
    lgg = jnp.where(is_group, lg, -jnp.inf)
    gmax = jnp.max(lgg, axis=-1, keepdims=True)
    gsel = jnp.min(jnp.where(lgg == gmax, lane_f - MOE_EXPERTS, big), axis=-1, keepdims=True)
    pg = 1.0 / jnp.sum(jnp.where(is_group, jnp.exp(lg - gmax), 0.0), axis=-1, keepdims=True)

    in_group = (lane < MOE_EXPERTS) & ((lane // MOE_EXPERTS_PER_GROUP).astype(F32) == gsel)
    le = jnp.where(in_group, lg, -jnp.inf)
    t1 = jnp.max(le, axis=-1, keepdims=True)
    i1 = jnp.min(jnp.where(le == t1, lane_f, big), axis=-1, keepdims=True)
    le2 = jnp.where(lane_f == i1, -jnp.inf, le)
    t2 = jnp.max(le2, axis=-1, keepdims=True)
    i2 = jnp.min(jnp.where(le2 == t2, lane_f, big), axis=-1, keepdims=True)
    e2 = jnp.exp(t2 - t1)
    w1 = pg / (1.0 + e2)
    w2 = pg * e2 / (1.0 + e2)
    rt_ref[...] = jnp.where(lane == 0, i1, jnp.where(lane == 1, i2, jnp.where(lane == 2, w1,
                            jnp.where(lane == 3, w2, 0.0))))


def _outproj(merged, w_out, x2d, gain, wr_cat, br, tm=512):
    T, D = x2d.shape
    row = lambda i: (i, 0)
    const = lambda i: (0, 0)
    return pl.pallas_call(
        _outproj_kernel,
        grid=(T // tm,),
        in_specs=[
            pl.BlockSpec((tm, D), row), pl.BlockSpec((D, D), const), pl.BlockSpec((tm, D), row),
            pl.BlockSpec((1, D), const), pl.BlockSpec((D, 2 * ROUTE_LANES), const),
            pl.BlockSpec((1, ROUTE_LANES), const),
        ],
        out_specs=[pl.BlockSpec((tm, D), row), pl.BlockSpec((tm * ROW_TILE, LANES), row),
                   pl.BlockSpec((tm, ROUTE_LANES), row)],
        out_shape=[jax.ShapeDtypeStruct((T, D), F32), jax.ShapeDtypeStruct((T * ROW_TILE, LANES), jnp.uint32),
                   jax.ShapeDtypeStruct((T, ROUTE_LANES), F32)],
        compiler_params=_cparams(("parallel",)),
        name="outproj_router",
    )(merged, w_out, x2d, gain.reshape(1, D).astype(F32), wr_cat, br)


def _dispatch_kernel(dest_ref, zb_ref, h_ref, xs_ref, zeros, sem, *, tq):
    base = pl.program_id(0) * tq
    block_rows = MOE_ROWS * ROW_TILE
    n_blocks = xs_ref.shape[0] // block_rows

    @pl.when(pl.program_id(0) == 0)
    def _():
        zeros[...] = jnp.zeros_like(zeros)

        def zero_copy(b):
            return pltpu.make_async_copy(zeros, xs_ref.at[pl.ds(pl.multiple_of(b * block_rows, block_rows), block_rows)],
                                         sem)

        def start(b, carry):
            @pl.when(zb_ref[b] != 0)
            def _():
                zero_copy(b).start()
            return carry

        def wait(b, carry):
            @pl.when(zb_ref[b] != 0)
            def _():
                zero_copy(b).wait()
            return carry

        lax.fori_loop(0, n_blocks, start, 0)
        lax.fori_loop(0, n_blocks, wait, 0)

    def token_rows(t):
        return pl.ds(pl.multiple_of(t * ROW_TILE, ROW_TILE), ROW_TILE)

    def copy(j, k):
        return pltpu.make_async_copy(h_ref.at[token_rows(j)], xs_ref.at[token_rows(dest_ref[2 * (base + j) + k])], sem)

    def issue(j, carry):
        copy(j, 0).start()
        copy(j, 1).start()
        return carry

    lax.fori_loop(0, tq, issue, 0)
    for _ in range(MOE_TOPK):
        pltpu.make_async_copy(h_ref, xs_ref.at[pl.ds(0, tq * ROW_TILE)], sem).wait()


def _dispatch(h2, dest_flat, zero_block, n_rows, tq=512):
    T = h2.shape[0] // ROW_TILE
    return pl.pallas_call(
        functools.partial(_dispatch_kernel, tq=tq),
        grid_spec=pltpu.PrefetchScalarGridSpec(
            num_scalar_prefetch=2,
            grid=(T // tq,),
            in_specs=[pl.BlockSpec((tq * ROW_TILE, LANES), lambda i, d, z: (i, 0))],
            out_specs=pl.BlockSpec(memory_space=pl.ANY),
            scratch_shapes=[pltpu.VMEM((MOE_ROWS * ROW_TILE, LANES), h2.dtype), pltpu.SemaphoreType.DMA(())],
        ),
        out_shape=jax.ShapeDtypeStruct((n_rows * ROW_TILE, LANES), h2.dtype),
        compiler_params=_cparams(("arbitrary",)),
        name="moe_dispatch",
    )(dest_flat, zero_block, h2)


MOE_WEIGHT_SLOTS = 3


def _expert_kernel(be_ref, nu_ref, rank_ref, kth_ref, xs_ref, wg_hbm, wu_hbm, wd_hbm, o_ref,
                   wg_f, wu_f, wd_f, wg_s, wu_s, wd_s, sem):
    i = pl.program_id(0)
    e = be_ref[i]
    active = i < nu_ref[0]
    new_expert = active & ((i == 0) | (e != be_ref[jnp.maximum(i - 1, 0)]))
    k = rank_ref[e]
    slot = k % MOE_WEIGHT_SLOTS
    ahead = MOE_WEIGHT_SLOTS - 1

    def weight_copies(expert, s):
        return (pltpu.make_async_copy(wg_hbm.at[expert], wg_f.at[s], sem.at[s, 0]),
                pltpu.make_async_copy(wu_hbm.at[expert], wu_f.at[s], sem.at[s, 1]),
                pltpu.make_async_copy(wd_hbm.at[expert], wd_f.at[s], sem.at[s, 2]))

    def start_kth(j):
        nxt = kth_ref[j]

        @pl.when(nxt >= 0)
        def _():
            for c in weight_copies(nxt, j % MOE_WEIGHT_SLOTS):
                c.start()

    @pl.when(active & (i == 0))
    def _():
        for j in range(ahead):
            start_kth(j)

    @pl.when(new_expert)
    def _():
        start_kth(k + ahead)
        for c in weight_copies(e, slot):
            c.wait()
        wg_s[...] = wg_f[slot].astype(BF16)
        wu_s[...] = wu_f[slot].astype(BF16)
        wd_s[...] = wd_f[slot].astype(BF16)

    @pl.when(active)
    def _():
        x = _unpack_halves(_load_row_tiles(xs_ref, MOE_ROWS)).astype(BF16)
        g = jnp.dot(x, wg_s[...], preferred_element_type=F32)
        u = jnp.dot(x, wu_s[...], preferred_element_type=F32)
        a = (g * jax.nn.sigmoid(g) * u).astype(BF16)
        y = jnp.dot(a, wd_s[...], preferred_element_type=F32)
        _store_row_tiles(o_ref, _pack_halves(y.astype(BF16)))

    @pl.when(jnp.logical_not(active))
    def _():
        o_ref[...] = jnp.zeros_like(o_ref)


def _experts(xs, block_expert, n_used, expert_rank, kth_expert, w_gate, w_up, w_down):
    Dh = xs.shape[1]
    D, F = w_gate.shape[1], w_gate.shape[2]
    assert Dh == LANES and D == 2 * ROW_TILE * LANES
    nblk = block_expert.shape[0]
    P = nblk * MOE_ROWS * ROW_TILE
    rows = lambda i, be, nu, nx, sl: (jnp.minimum(i, nu[0] - 1), 0)
    hbm = pl.BlockSpec(memory_space=pl.ANY)
    return pl.pallas_call(
        _expert_kernel,
        grid_spec=pltpu.PrefetchScalarGridSpec(
            num_scalar_prefetch=4,
            grid=(nblk,),
            in_specs=[pl.BlockSpec((MOE_ROWS * ROW_TILE, Dh), rows), hbm, hbm, hbm],
            out_specs=pl.BlockSpec((MOE_ROWS * ROW_TILE, Dh), lambda i, be, nu, nx, sl: (i, 0)),
            scratch_shapes=[pltpu.VMEM((MOE_WEIGHT_SLOTS, D, F), F32), pltpu.VMEM((MOE_WEIGHT_SLOTS, D, F), F32),
                            pltpu.VMEM((MOE_WEIGHT_SLOTS, F, D), F32),
                            pltpu.VMEM((D, F), BF16), pltpu.VMEM((D, F), BF16), pltpu.VMEM((F, D), BF16),
                            pltpu.SemaphoreType.DMA((MOE_WEIGHT_SLOTS, 3))],
        ),
        out_shape=jax.ShapeDtypeStruct((P, Dh), jnp.uint32),
        compiler_params=_cparams(("arbitrary",)),
        name="moe_experts",
    )(block_expert, n_used, expert_rank, kth_expert, xs, w_gate, w_up, w_down)


def _combine_kernel(dest_ref, x1_ref, rt_ref, g_ref, ys_ref, o_ref, buf, sem, *, tq):
    i = pl.program_id(0)
    n = pl.num_programs(0)

    def token_rows(t):
        return pl.ds(pl.multiple_of(t * ROW_TILE, ROW_TILE), ROW_TILE)

    def issue(step, s):
        def body(j, carry):
            for k in range(MOE_TOPK):
                pltpu.make_async_copy(ys_ref.at[token_rows(dest_ref[2 * (step * tq + j) + k])],
                                      buf.at[s, k, token_rows(j)], sem.at[s]).start()
            return carry
        lax.fori_loop(0, tq, body, 0)

    @pl.when(i == 0)
    def _():
        issue(0, 0)

    for s in range(2):
        @pl.when((i + 1 < n) & ((i + 1) % 2 == s))
        def _():
            issue(i + 1, s)

    for s in range(2):
        @pl.when(i % 2 == s)
        def _():
            for k in range(MOE_TOPK):
                pltpu.make_async_copy(ys_ref.at[pl.ds(0, tq * ROW_TILE)], buf.at[s, k], sem.at[s]).wait()
            rt = rt_ref[...]
            y0 = _unpack_halves(_load_row_tiles(buf.at[s, 0], tq))
            y1 = _unpack_halves(_load_row_tiles(buf.at[s, 1], tq))
            x = x1_ref[...] + rt[:, 2:3] * y0 + rt[:, 3:4] * y1
            ms = jnp.mean(x * x, axis=-1, keepdims=True)
            o_ref[...] = x * lax.rsqrt(ms + NORM_EPS) * g_ref[...]


def _combine(x1, route, gain, ys, dest_flat, tq=256):
    T, D = x1.shape
    return pl.pallas_call(
        functools.partial(_combine_kernel, tq=tq),
        grid_spec=pltpu.PrefetchScalarGridSpec(
            num_scalar_prefetch=1,
            grid=(T // tq,),
            in_specs=[
                pl.BlockSpec((tq, D), lambda i, d: (i, 0)),
                pl.BlockSpec((tq, ROUTE_LANES), lambda i, d: (i, 0)),
                pl.BlockSpec((1, D), lambda i, d: (0, 0)),
                pl.BlockSpec(memory_space=pl.ANY),
            ],
            out_specs=pl.BlockSpec((tq, D), lambda i, d: (i, 0)),
            scratch_shapes=[pltpu.VMEM((2, MOE_TOPK, tq * ROW_TILE, LANES), jnp.uint32),
                            pltpu.SemaphoreType.DMA((2,))],
        ),
        out_shape=jax.ShapeDtypeStruct((T, D), F32),
        compiler_params=_cparams(("arbitrary",)),
        name="moe_combine",
    )(dest_flat, x1, route, gain.reshape(1, D).astype(F32), ys)


def _route_metadata(route, T):
    e = route[:, :MOE_TOPK].astype(jnp.int32)
    ids = jnp.arange(MOE_EXPERTS, dtype=jnp.int32)
    oh0 = e[:, 0, None] == ids
    oh1 = e[:, 1, None] == ids
    onehot = (oh0 | oh1).astype(jnp.int32)
    incl = jnp.cumsum(onehot, axis=0)
    counts = incl[-1]
    pcounts = (counts + MOE_ROWS - 1) // MOE_ROWS * MOE_ROWS
    pends = jnp.cumsum(pcounts)
    poffs = pends - pcounts
    slot = incl - onehot + poffs[None, :]
    dest = jnp.stack([jnp.sum(jnp.where(oh0, slot, 0), axis=1), jnp.sum(jnp.where(oh1, slot, 0), axis=1)], axis=1)
    nblk = (T * MOE_TOPK) // MOE_ROWS + MOE_EXPERTS
    block_expert = jnp.minimum(
        jnp.searchsorted(pends, jnp.arange(nblk, dtype=jnp.int32) * MOE_ROWS, side="right"),
        MOE_EXPERTS - 1).astype(jnp.int32)
    n_used = (pends[-1:] // MOE_ROWS).astype(jnp.int32)
    used = counts > 0
    expert_rank = (jnp.cumsum(used.astype(jnp.int32)) - 1).astype(jnp.int32)
    order = jnp.argsort(jnp.where(used, ids, ids + MOE_EXPERTS)).astype(jnp.int32)
    kth = jnp.where(ids < jnp.sum(used), order, -1)
    kth_expert = jnp.concatenate([kth, jnp.full((MOE_WEIGHT_SLOTS,), -1, jnp.int32)])
    blk = jnp.arange(nblk, dtype=jnp.int32)
    following = jnp.concatenate([block_expert[1:], block_expert[-1:]])
    zero_block = ((blk >= n_used[0] - 1) | (block_expert != following)).astype(jnp.int32)
    return dest.reshape(-1).astype(jnp.int32), zero_block, block_expert, n_used, expert_rank, kth_expert, nblk


def kernel(x, norm1_gain, w_in, hg_lb_logits, hg_norm_gain, rel_bias, w_branch_a, w_branch_b, w_out,
           norm2_gain, w_router_group, b_router_group, w_router_expert, b_router_expert,
           w_exp_gate, w_exp_up, w_exp_down, final_norm_gain):
    B, S, D = x.shape
    T = B * S
    depth = norm1_gain.shape[0]
    lower_bounds = jnp.cumsum(jax.nn.softmax(hg_lb_logits.astype(F32), axis=0), axis=0)
    att_w = ATT_HEADS * HEAD_DIM
    bias = _attn_bias(rel_bias)
    x2d = x.reshape(T, D)
    for layer in range(depth):
        zeros_lb = jnp.zeros((1, 3 * att_w), F32)
        hm = dict(head_major=True, B=B, S=S)
        h, q_a = _normproj(x2d, norm1_gain[layer], w_in[layer][:, :D].astype(BF16), B=B, S=S)
        lf_a = _inproj(h, w_in, layer, D, D, lower_bounds[layer].reshape(1, D), mode="logf", out_dtype=F32, **hm)
        i_a = _inproj(h, w_in, layer, 2 * D, D, zeros_lb, mode="none", out_dtype=BF16, **hm)
        sg_a = _inproj(h, w_in, layer, 3 * D, D, zeros_lb, mode="sigmoid", out_dtype=BF16, **hm)
        qkv_b = _inproj(h, w_in, layer, 4 * D, 3 * att_w, zeros_lb, mode="none", out_dtype=BF16,
                        tm=2048, tn=512, **hm)
        gates = _inproj(h, w_in, layer, 4 * D + 3 * att_w, 2 * D, zeros_lb, mode="sigmoid",
                        head_major=False, out_dtype=BF16, B=B, S=S, tm=2048, tn=512)

        y_a = _hgrn(q_a, lf_a, i_a, sg_a, hg_norm_gain[layer]).reshape(T, D)
        y_b = _attention(qkv_b, bias).reshape(T, ATT_HEADS_PER_GROUP * HEAD_DIM)
        merged = _merge(y_a, y_b, w_branch_a[layer].astype(BF16), w_branch_b[layer].astype(BF16), gates)

        wr = jnp.zeros((D, ROUTE_LANES), F32)
        wr = wr.at[:, :MOE_EXPERTS].set(w_router_expert[layer].astype(F32))
        wr = wr.at[:, MOE_EXPERTS:MOE_EXPERTS + MOE_GROUPS].set(w_router_group[layer].astype(F32))
        br = jnp.zeros((1, ROUTE_LANES), F32)
        br = br.at[0, :MOE_EXPERTS].set(b_router_expert[layer].astype(F32))
        br = br.at[0, MOE_EXPERTS:MOE_EXPERTS + MOE_GROUPS].set(b_router_group[layer].astype(F32))
        wr_hi = wr.astype(BF16)
        wr_lo = (wr - wr_hi.astype(F32)).astype(BF16)
        x1, h2, route = _outproj(merged, w_out[layer].astype(BF16), x2d, norm2_gain[layer],
                                 jnp.concatenate([wr_hi, wr_lo], axis=1), br)

        dest, zero_block, block_expert, n_used, expert_rank, kth_expert, nblk = _route_metadata(route, T)
        xs = _dispatch(h2, dest, zero_block, nblk * MOE_ROWS)
        ys = _experts(xs, block_expert, n_used, expert_rank, kth_expert,
                      w_exp_gate[layer], w_exp_up[layer], w_exp_down[layer])
        last = layer == depth - 1
        assert last, "the fused combine applies the final norm; deeper stacks need an un-normalised combine"
        x2d = _combine(x1, route, final_norm_gain, ys, dest)
    return x2d.reshape(B, S, D)
```

```python
import functools
import math

import numpy as np
import jax
import jax.numpy as jnp
from jax import lax
from jax.experimental import pallas as pl
from jax.experimental.pallas import tpu as pltpu

F32 = jnp.float32
BF16 = jnp.bfloat16

LANES = 128
NORM_EPS = 1e-6
HEAD_DIM = 128
HG_HEADS = 16
ATT_GROUPS = ((128, 1), (512, 4), (2048, 16))
ATT_HEADS_PER_GROUP = 4
ATT_HEADS = len(ATT_GROUPS) * ATT_HEADS_PER_GROUP
ATT_BLOCK = 128
ATT_STACK = 16
REL_BUCKETS = 32
REL_MAX_DIST = 2048
MOE_GROUPS = 8
MOE_EXPERTS_PER_GROUP = 8
MOE_EXPERTS = MOE_GROUPS * MOE_EXPERTS_PER_GROUP
MOE_TOPK = 2
MOE_ROWS = 256
NEG_BIG = -1e30
LOG2E = 1.4426950408889634
VMEM_LIMIT = 56 * 1024 * 1024


def _cparams(sem):
    return pltpu.CompilerParams(dimension_semantics=sem, vmem_limit_bytes=VMEM_LIMIT)


def _normproj_kernel(x_ref, g_ref, w_ref, h_ref, o_ref):
    @pl.when(pl.program_id(1) == 0)
    def _():
        x = x_ref[...]
        ms = jnp.mean(x * x, axis=-1, keepdims=True)
        h_ref[...] = (x * lax.rsqrt(ms + NORM_EPS) * g_ref[...]).astype(h_ref.dtype)

    acc = jnp.dot(h_ref[...], w_ref[...].astype(BF16), preferred_element_type=F32)
    for hh in range(acc.shape[1] // HEAD_DIM):
        o_ref[0, hh] = acc[:, hh * HEAD_DIM:(hh + 1) * HEAD_DIM].astype(o_ref.dtype)


def _normproj(x2d, gain, w_all, layer, N, *, B, S, tm=512, tn=1024):
    T, D = x2d.shape
    assert N % tn == 0 and S % tm == 0
    spb = S // tm
    return pl.pallas_call(
        _normproj_kernel,
        grid=(T // tm, N // tn),
        in_specs=[
            pl.BlockSpec((tm, D), lambda i, j: (i, 0)),
            pl.BlockSpec((1, D), lambda i, j: (0, 0)),
            pl.BlockSpec((None, D, tn), lambda i, j: (layer, 0, j)),
        ],
        out_specs=[
            pl.BlockSpec((tm, D), lambda i, j: (i, 0)),
            pl.BlockSpec((1, tn // HEAD_DIM, tm, HEAD_DIM), lambda i, j: (i // spb, j, i % spb, 0)),
        ],
        out_shape=[jax.ShapeDtypeStruct((T, D), BF16),
                   jax.ShapeDtypeStruct((B, N // HEAD_DIM, S, HEAD_DIM), BF16)],
        compiler_params=_cparams(("parallel", "arbitrary")),
        name="norm_inproj",
    )(x2d, gain.reshape(1, D).astype(F32), w_all)


def _inproj_kernel(h_ref, w_ref, lb_ref, o_ref, w_bf, *, mode, head_major):
    @pl.when(pl.program_id(1) == 0)
    def _():
        w_bf[...] = w_ref[0].astype(BF16)

    acc = jnp.dot(h_ref[...], w_bf[...], preferred_element_type=F32)
    if mode == "logf":
        lb = lb_ref[...]
        acc = jnp.log(lb + (1.0 - lb) * jax.nn.sigmoid(acc))
    elif mode == "sigmoid":
        acc = jax.nn.sigmoid(acc)
    if head_major:
        for hh in range(acc.shape[1] // HEAD_DIM):
            o_ref[0, hh] = acc[:, hh * HEAD_DIM:(hh + 1) * HEAD_DIM].astype(o_ref.dtype)
    else:
        o_ref[...] = acc.astype(o_ref.dtype)


def _inproj(h, w_all, layer, col0, N, lb, *, mode, head_major, out_dtype, B, S, tm=1024, tn=1024):
    T, D = h.shape
    assert N % tn == 0 and col0 % LANES == 0 and T % tm == 0 and S % tm == 0
    spb = S // tm
    if head_major:
        out_shape = jax.ShapeDtypeStruct((B, N // HEAD_DIM, S, HEAD_DIM), out_dtype)
        out_spec = pl.BlockSpec((1, tn // HEAD_DIM, tm, HEAD_DIM), lambda j, i: (i // spb, j, i % spb, 0))
    else:
        out_shape = jax.ShapeDtypeStruct((T, N), out_dtype)
        out_spec = pl.BlockSpec((tm, tn), lambda j, i: (i, j))
    return pl.pallas_call(
        functools.partial(_inproj_kernel, mode=mode, head_major=head_major),
        grid=(N // tn, T // tm),
        in_specs=[
            pl.BlockSpec((tm, D), lambda j, i: (i, 0)),
            pl.BlockSpec((pl.Element(1), pl.Element(D), pl.Element(tn)), lambda j, i: (layer, 0, pl.multiple_of(col0 + j * tn, LANES))),
            pl.BlockSpec((1, tn), lambda j, i: (0, j)),
        ],
        out_specs=out_spec,
        out_shape=out_shape,
        scratch_shapes=[pltpu.VMEM((D, tn), BF16)],
        compiler_params=_cparams(("parallel", "arbitrary")),
        name="inproj_" + mode,
    )(h, w_all, lb)


HG_CHUNK = 256
HG_DIAG = 16
HG_HEADS_PER_STEP = 4


def _hgrn_levels(C):
    out, m = [], C // 2
    while m >= HG_DIAG:
        out.append(m)
        m //= 2
    return out


def _hgrn_masks(C):
    t = np.arange(C)[:, None]
    s = np.arange(C)[None, :]
    masks = []
    for m in _hgrn_levels(C):
        masks.append((t // (2 * m) == s // (2 * m)) & ((t // m) % 2 == 1) & ((s // m) % 2 == 0))
    masks.append((t // HG_DIAG == s // HG_DIAG) & (t >= s))
    total = np.sum(np.stack(masks).astype(np.int32), axis=0)
    assert np.array_equal(total, (t >= s).astype(np.int32))
    return np.stack(masks).astype(np.float32)


def _dot_nt(a, b):
    return lax.dot_general(a, b, (((1,), (1,)), ((), ())), preferred_element_type=F32)


def _dot_tn(a, b):
    return lax.dot_general(a, b, (((0,), (0,)), ((), ())), preferred_element_type=F32)


def _hgrn_kernel(q_ref, lf_ref, v_ref, sg_ref, gain_ref, tril_ref, mask_ref, o_ref, st_ref, b_ref, *, C):
    S = q_ref.shape[2]
    HP = q_ref.shape[1]
    levels = _hgrn_levels(C)
    nchunks = S // C
    tril = tril_ref[...]

    def ref_rows(h, block, row_of_block):
        parts = [jnp.broadcast_to(b_ref[h, pl.ds(row_of_block(p), 1), :], (block, HEAD_DIM))
                 for p in range(C // block)]
        return jnp.concatenate(parts, axis=0)

    def cumsum(c):
        r0 = pl.multiple_of(c * C, C)
        out = []
        for h in range(HP):
            lf2 = lf_ref[0, h, pl.ds(r0, C), :] * LOG2E
            hi = lf2.astype(BF16)
            lo = (lf2 - hi.astype(F32)).astype(BF16)
            r = jnp.dot(tril, jnp.concatenate([hi, lo], axis=1), preferred_element_type=F32)
            out.append(r[:, :HEAD_DIM] + r[:, HEAD_DIM:])
        return tuple(out)

    st_ref[...] = jnp.zeros_like(st_ref)

    def chunk(c, b_all):
        r0 = pl.multiple_of(c * C, C)
        rows = pl.ds(r0, C)
        b_next = cumsum(jnp.minimum(c + 1, nchunks - 1))
        for h in range(HP):
            b = b_all[h]
            b_ref[h] = b
            q = q_ref[0, h, rows, :].astype(F32)
            kk = 1.0 - jnp.exp(lf_ref[0, h, rows, :])
            b_last = b_ref[h, pl.ds(C - 1, 1), :]
            scores = jnp.zeros((C, C), F32)
            for li, m in enumerate(levels):
                d = b - ref_rows(h, 2 * m, lambda p, m=m: 2 * m * p + m - 1)
                e = jnp.exp2(-jnp.abs(d))
                scores = scores + mask_ref[li] * _dot_nt((q * e).astype(BF16), (kk * e).astype(BF16))
            d = b - ref_rows(h, HG_DIAG, lambda p: HG_DIAG * p + HG_DIAG // 2 - 1)
            scores = scores + mask_ref[len(levels)] * _dot_nt((q * jnp.exp2(d)).astype(BF16),
                                                               (kk * jnp.exp2(-d)).astype(BF16))
            v = v_ref[0, h, rows, :]
            st = st_ref[h]
            qe = (q * jnp.exp2(b)).astype(BF16)
            kd = (kk * jnp.exp2(b_last - b)).astype(BF16)
            o = jnp.dot(scores.astype(BF16), v, preferred_element_type=F32) + _dot_nt(qe, st.astype(BF16))
            st_ref[h] = st * jnp.exp2(b_last) + _dot_tn(v, kd)
            ms = jnp.mean(o * o, axis=-1, keepdims=True)
            y = o * lax.rsqrt(ms + NORM_EPS) * gain_ref[:, h * HEAD_DIM:(h + 1) * HEAD_DIM]
            y = y * sg_ref[0, h, rows, :].astype(F32)
            o_ref[0, rows, h * HEAD_DIM:(h + 1) * HEAD_DIM] = y.astype(o_ref.dtype)
        return b_next

    lax.fori_loop(0, nchunks, chunk, cumsum(0))


def _hgrn(q, lf, v, sg, gain):
    B, H, S, E = q.shape
    C, HP = HG_CHUNK, HG_HEADS_PER_STEP
    masks = jnp.asarray(_hgrn_masks(C))
    tril = jnp.asarray(np.tril(np.ones((C, C), np.float32)), dtype=BF16)
    head_spec = pl.BlockSpec((1, HP, S, E), lambda b, h: (b, h, 0, 0))
    return pl.pallas_call(
        functools.partial(_hgrn_kernel, C=C),
        grid=(B, H // HP),
        in_specs=[
            head_spec, head_spec, head_spec, head_spec,
            pl.BlockSpec((1, HP * E), lambda b, h: (0, h)),
            pl.BlockSpec((C, C), lambda b, h: (0, 0)),
            pl.BlockSpec(masks.shape, lambda b, h: (0, 0, 0)),
        ],
        out_specs=pl.BlockSpec((1, S, HP * E), lambda b, h: (b, 0, h)),
        out_shape=jax.ShapeDtypeStruct((B, S, H * E), BF16),
        scratch_shapes=[pltpu.VMEM((HP, E, E), F32), pltpu.VMEM((HP, C, E), F32)],
        compiler_params=_cparams(("parallel", "parallel")),
        name="hgrn2",
    )(q, lf, v, sg, gain.reshape(1, H * E).astype(F32), tril, masks)


def _t5_bucket_np(dist):
    exact = REL_BUCKETS // 2
    d_f = np.maximum(dist, 1).astype(np.float32)
    log_b = exact + (np.log(d_f / np.float32(exact)) / np.float32(math.log(REL_MAX_DIST / exact))
                     * np.float32(REL_BUCKETS - exact)).astype(np.int32)
    return np.where(dist < exact, dist, np.minimum(log_b, REL_BUCKETS - 1))


def _attn_bias(rel_bias):
    blk = ATT_BLOCK
    period = 3 * blk
    out = []
    for gi, (window, dilation) in enumerate(ATT_GROUPS):
        n_back = window // dilation
        assert n_back <= blk
        hs = slice(gi * ATT_HEADS_PER_GROUP, (gi + 1) * ATT_HEADS_PER_GROUP)
        bucket = _t5_bucket_np(np.arange(n_back + 1) * dilation)
        by_delta = rel_bias[:, hs][bucket].astype(F32).T
        u = jnp.full((ATT_HEADS_PER_GROUP, period), NEG_BIG, F32)
        u = u.at[:, 2 * blk - 1 - n_back:2 * blk].set(by_delta[:, ::-1])
        flat = jnp.tile(u, (1, blk))[:, :blk * (period - 1)]
        out.append(flat.reshape(ATT_HEADS_PER_GROUP, blk, period - 1)[:, :, blk - 1:3 * blk - 1])
    return jnp.stack(out, axis=0)


def _attn_kernel(q_ref, k_ref, v_ref, bias_ref, o_ref, qf, kf, vf, og, lg):
    S = q_ref.shape[3]
    scale = HEAD_DIM ** -0.5
    blk = ATT_BLOCK

    for g, (window, d) in enumerate(ATT_GROUPS):
        L = S // d
        nb = L // blk
        if d > 1:
            qf[...] = q_ref[0, g, 0].astype(F32)
            kf[...] = k_ref[0, g, 0].astype(F32)
            vf[...] = v_ref[0, g, 0].astype(F32)

        def load(ref_bf, ref_f32, start, size, g=g, d=d):
            if d == 1:
                return ref_bf[0, g, 0, pl.ds(start, size), :]
            return ref_f32[pl.ds(start, size, stride=d), :].astype(BF16)

        def scores(r, n, g=g, d=d, load=load):
            start = r + n * blk * d
            q = load(q_ref, qf, start, blk)
            if n == 0:
                k = load(k_ref, kf, start, blk)
                v = load(v_ref, vf, start, blk)
                s = _dot_nt(q, k) * scale + bias_ref[g, 0, :, blk:]
            else:
                first = start - blk * d
                k = load(k_ref, kf, first, 2 * blk)
                v = load(v_ref, vf, first, 2 * blk)
                s = _dot_nt(q, k) * scale + bias_ref[g, 0]
            return start, s, v

        def softmax_pv(items, g=g, d=d):
            s = jnp.concatenate([it[1] for it in items], axis=0)
            m = jnp.max(s, axis=-1, keepdims=True)
            p = jnp.exp(s - m)
            den = jnp.sum(p, axis=-1, keepdims=True)
            pb = p.astype(BF16)
            lse = jnp.broadcast_to(m + jnp.log(den), (s.shape[0], HEAD_DIM))
            for j, (start, _, v) in enumerate(items):
                sl = slice(j * blk, (j + 1) * blk)
                rows = pl.ds(start, blk, stride=d) if d > 1 else pl.ds(start, blk)
                og[g, rows, :] = jnp.dot(pb[sl], v, preferred_element_type=F32) / den[sl]
                lg[g, rows, :] = lse[sl]

        blocks = [(r, n) for r in range(d) for n in range(nb)]
        first_blocks = [bn for bn in blocks if bn[1] == 0]
        later_blocks = [bn for bn in blocks if bn[1] > 0]
        for group in (first_blocks, later_blocks):
            for j0 in range(0, len(group), ATT_STACK):
                softmax_pv([scores(r, n) for r, n in group[j0:j0 + ATT_STACK]])

    rows_per_step = 256

    def mix(i, carry):
        rows = pl.ds(pl.multiple_of(i * rows_per_step, rows_per_step), rows_per_step)
        l0, l1, l2 = lg[0, rows, :], lg[1, rows, :], lg[2, rows, :]
        mx = jnp.maximum(jnp.maximum(l0, l1), l2)
        e0, e1, e2 = jnp.exp(l0 - mx), jnp.exp(l1 - mx), jnp.exp(l2 - mx)
        num = e0 * og[0, rows, :] + e1 * og[1, rows, :] + e2 * og[2, rows, :]
        o_ref[0, rows, :] = (num / (e0 + e1 + e2)).astype(o_ref.dtype)
        return carry

    lax.fori_loop(0, S // rows_per_step, mix, 0)


def _attention(qkv, bias):
    B, _, S, E = qkv.shape
    G, HG = len(ATT_GROUPS), ATT_HEADS_PER_GROUP
    x = qkv.reshape(B, 3, G, HG, S, E)

    def spec(which):
        return pl.BlockSpec((1, None, G, 1, S, E), lambda b, h, which=which: (b, which, 0, h, 0, 0))

    def kern(q_ref, k_ref, v_ref, bias_ref, o_ref, *scratch):
        _attn_kernel(q_ref, k_ref, v_ref, bias_ref, o_ref, *scratch)

    return pl.pallas_call(
        kern,
        grid=(B, HG),
        in_specs=[spec(0), spec(1), spec(2),
                  pl.BlockSpec((G, 1, ATT_BLOCK, 2 * ATT_BLOCK), lambda b, h: (0, h, 0, 0))],
        out_specs=pl.BlockSpec((1, S, E), lambda b, h: (b, 0, h)),
        out_shape=jax.ShapeDtypeStruct((B, S, HG * E), BF16),
        scratch_shapes=[pltpu.VMEM((S, E), F32)] * 3 + [pltpu.VMEM((G, S, E), F32)] * 2,
        compiler_params=_cparams(("parallel", "parallel")),
        name="dilated_attn",
    )(x, x, x, bias)


def _merge_kernel(ya_ref, yb_ref, wa_ref, wb_ref, ga_ref, gb_ref, o_ref):
    a = jnp.dot(ya_ref[...], wa_ref[...], preferred_element_type=F32)
    b = jnp.dot(yb_ref[...], wb_ref[...], preferred_element_type=F32)
    o_ref[...] = (ga_ref[...].astype(F32) * a + gb_ref[...].astype(F32) * b).astype(o_ref.dtype)


def _merge(ya, yb, wa, wb, gates, tm=1024, tn=1024):
    T, Ka = ya.shape
    Kb = yb.shape[1]
    N = wa.shape[1]
    nj = N // tn
    return pl.pallas_call(
        _merge_kernel,
        grid=(nj, T // tm),
        in_specs=[
            pl.BlockSpec((tm, Ka), lambda j, i: (i, 0)),
            pl.BlockSpec((tm, Kb), lambda j, i: (i, 0)),
            pl.BlockSpec((Ka, tn), lambda j, i: (0, j)),
            pl.BlockSpec((Kb, tn), lambda j, i: (0, j)),
            pl.BlockSpec((tm, tn), lambda j, i: (i, j)),
            pl.BlockSpec((tm, tn), lambda j, i, nj=nj: (i, j + nj)),
        ],
        out_specs=pl.BlockSpec((tm, tn), lambda j, i: (i, j)),
        out_shape=jax.ShapeDtypeStruct((T, N), BF16),
        compiler_params=_cparams(("parallel", "parallel")),
        name="branch_merge",
    )(ya, yb, wa, wb, gates, gates)


ROUTE_LANES = LANES


def _pack_halves(x_bf):
    bits = pltpu.bitcast(x_bf.astype(F32), jnp.uint32)
    n = bits.shape[1] // 2
    return (bits[:, :n] >> 16) | (bits[:, n:] & jnp.uint32(0xFFFF0000))


def _unpack_halves(p):
    lo = pltpu.bitcast(p << 16, F32)
    hi = pltpu.bitcast(p & jnp.uint32(0xFFFF0000), F32)
    return jnp.concatenate([lo, hi], axis=1)


ROW_TILE = 8


def _store_row_tiles(ref, packed):
    m = packed.shape[0]
    assert packed.shape[1] == ROW_TILE * LANES
    for c in range(ROW_TILE):
        ref[pl.ds(c, m, stride=ROW_TILE), :] = packed[:, c * LANES:(c + 1) * LANES]


def _load_row_tiles(ref, m):
    return jnp.concatenate([ref[pl.ds(c, m, stride=ROW_TILE), :] for c in range(ROW_TILE)], axis=1)


def _outproj_kernel(m_ref, w_ref, x_ref, g_ref, wrc_ref, br_ref, x1_ref, h2_ref, rt_ref):
    x1 = x_ref[...] + jnp.dot(m_ref[...], w_ref[...], preferred_element_type=F32)
    x1_ref[...] = x1
    ms = jnp.mean(x1 * x1, axis=-1, keepdims=True)
    h2 = x1 * lax.rsqrt(ms + NORM_EPS) * g_ref[...]
    h_hi = h2.astype(BF16)
    _store_row_tiles(h2_ref, _pack_halves(h_hi))
    h_lo = (h2 - h_hi.astype(F32)).astype(BF16)
    r = jnp.dot(h_hi, wrc_ref[...], preferred_element_type=F32)
    lg = (r[:, :ROUTE_LANES] + r[:, ROUTE_LANES:]
          + jnp.dot(h_lo, wrc_ref[:, :ROUTE_LANES], preferred_element_type=F32)) + br_ref[...]

    lane = lax.broadcasted_iota(jnp.int32, lg.shape, 1)
    lane_f = lane.astype(F32)
    big = float(ROUTE_LANES)
    is_group = (lane >= MOE_EXPERTS) & (lane < MOE_EXPERTS + MOE_GROUPS)
    lgg = jnp.where(is_group, lg, -jnp.inf)
    gmax = jnp.max(lgg, axis=-1, keepdims=True)
    gsel = jnp.min(jnp.where(lgg == gmax, lane_f - MOE_EXPERTS, big), axis=-1, keepdims=True)
    pg = 1.0 / jnp.sum(jnp.where(is_group, jnp.exp(lg - gmax), 0.0), axis=-1, keepdims=True)

    in_group = (lane < MOE_EXPERTS) & ((lane // MOE_EXPERTS_PER_GROUP).astype(F32) == gsel)
    le = jnp.where(in_group, lg, -jnp.inf)
    t1 = jnp.max(le, axis=-1, keepdims=True)
    i1 = jnp.min(jnp.where(le == t1, lane_f, big), axis=-1, keepdims=True)
    le2 = jnp.where(lane_f == i1, -jnp.inf, le)
    t2 = jnp.max(le2, axis=-1, keepdims=True)
    i2 = jnp.min(jnp.where(le2 == t2, lane_f, big), axis=-1, keepdims=True)
    e2 = jnp.exp(t2 - t1)
    w1 = pg / (1.0 + e2)
    w2 = pg * e2 / (1.0 + e2)
    rt_ref[...] = jnp.where(lane == 0, i1, jnp.where(lane == 1, i2, jnp.where(lane == 2, w1,
                            jnp.where(lane == 3, w2, 0.0))))


def _outproj(merged, w_out, x2d, gain, wr_cat, br, tm=512):
    T, D = x2d.shape
    row = lambda i: (i, 0)
    const = lambda i: (0, 0)
    return pl.pallas_call(
        _outproj_kernel,
        grid=(T // tm,),
        in_specs=[
            pl.BlockSpec((tm, D), row), pl.BlockSpec((D, D), const), pl.BlockSpec((tm, D), row),
            pl.BlockSpec((1, D), const), pl.BlockSpec((D, 2 * ROUTE_LANES), const),
            pl.BlockSpec((1, ROUTE_LANES), const),
        ],
        out_specs=[pl.BlockSpec((tm, D), row), pl.BlockSpec((tm * ROW_TILE, LANES), row),
                   pl.BlockSpec((tm, ROUTE_LANES), row)],
        out_shape=[jax.ShapeDtypeStruct((T, D), F32), jax.ShapeDtypeStruct((T * ROW_TILE, LANES), jnp.uint32),
                   jax.ShapeDtypeStruct((T, ROUTE_LANES), F32)],
        compiler_params=_cparams(("parallel",)),
        name="outproj_router",
    )(merged, w_out, x2d, gain.reshape(1, D).astype(F32), wr_cat, br)


def _dispatch_kernel(dest_ref, zb_ref, h_ref, xs_ref, zeros, sem, *, tq):
    base = pl.program_id(0) * tq
    block_rows = MOE_ROWS * ROW_TILE
    n_blocks = xs_ref.shape[0] // block_rows

    @pl.when(pl.program_id(0) == 0)
    def _():
        zeros[...] = jnp.zeros_like(zeros)

        def zero_copy(b):
            return pltpu.make_async_copy(zeros, xs_ref.at[pl.ds(pl.multiple_of(b * block_rows, block_rows), block_rows)],
                                         sem)

        def start(b, carry):
            @pl.when(zb_ref[b] != 0)
            def _():
                zero_copy(b).start()
            return carry

        def wait(b, carry):
            @pl.when(zb_ref[b] != 0)
            def _():
                zero_copy(b).wait()
            return carry

        lax.fori_loop(0, n_blocks, start, 0)
        lax.fori_loop(0, n_blocks, wait, 0)

    def token_rows(t):
        return pl.ds(pl.multiple_of(t * ROW_TILE, ROW_TILE), ROW_TILE)

    def copy(j, k):
        return pltpu.make_async_copy(h_ref.at[token_rows(j)], xs_ref.at[token_rows(dest_ref[2 * (base + j) + k])], sem)

    def issue(j, carry):
        copy(j, 0).start()
        copy(j, 1).start()
        return carry

    lax.fori_loop(0, tq, issue, 0)
    for _ in range(MOE_TOPK):
        pltpu.make_async_copy(h_ref, xs_ref.at[pl.ds(0, tq * ROW_TILE)], sem).wait()


def _dispatch(h2, dest_flat, zero_block, n_rows, tq=512):
    T = h2.shape[0] // ROW_TILE
    return pl.pallas_call(
        functools.partial(_dispatch_kernel, tq=tq),
        grid_spec=pltpu.PrefetchScalarGridSpec(
            num_scalar_prefetch=2,
            grid=(T // tq,),
            in_specs=[pl.BlockSpec((tq * ROW_TILE, LANES), lambda i, d, z: (i, 0))],
            out_specs=pl.BlockSpec(memory_space=pl.ANY),
            scratch_shapes=[pltpu.VMEM((MOE_ROWS * ROW_TILE, LANES), h2.dtype), pltpu.SemaphoreType.DMA(())],
        ),
        out_shape=jax.ShapeDtypeStruct((n_rows * ROW_TILE, LANES), h2.dtype),
        compiler_params=_cparams(("arbitrary",)),
        name="moe_dispatch",
    )(dest_flat, zero_block, h2)


MOE_WEIGHT_SLOTS = 3


def _expert_kernel(be_ref, nu_ref, rank_ref, kth_ref, xs_ref, wg_hbm, wu_hbm, wd_hbm, o_ref,
                   wg_f, wu_f, wd_f, wg_s, wu_s, wd_s, sem):
    i = pl.program_id(0)
    e = be_ref[i]
    active = i < nu_ref[0]
    new_expert = active & ((i == 0) | (e != be_ref[jnp.maximum(i - 1, 0)]))
    k = rank_ref[e]
    slot = k % MOE_WEIGHT_SLOTS
    ahead = MOE_WEIGHT_SLOTS - 1

    def weight_copies(expert, s):
        return (pltpu.make_async_copy(wg_hbm.at[expert], wg_f.at[s], sem.at[s, 0]),
                pltpu.make_async_copy(wu_hbm.at[expert], wu_f.at[s], sem.at[s, 1]),
                pltpu.make_async_copy(wd_hbm.at[expert], wd_f.at[s], sem.at[s, 2]))

    def start_kth(j):
        nxt = kth_ref[j]

        @pl.when(nxt >= 0)
        def _():
            for c in weight_copies(nxt, j % MOE_WEIGHT_SLOTS):
                c.start()

    @pl.when(active & (i == 0))
    def _():
        for j in range(ahead):
            start_kth(j)

    @pl.when(new_expert)
    def _():
        start_kth(k + ahead)
        for c in weight_copies(e, slot):
            c.wait()
        wg_s[...] = wg_f[slot].astype(BF16)
        wu_s[...] = wu_f[slot].astype(BF16)
        wd_s[...] = wd_f[slot].astype(BF16)

    @pl.when(active)
    def _():
        x = _unpack_halves(_load_row_tiles(xs_ref, MOE_ROWS)).astype(BF16)
        g = jnp.dot(x, wg_s[...], preferred_element_type=F32)
        u = jnp.dot(x, wu_s[...], preferred_element_type=F32)
        a = (g * jax.nn.sigmoid(g) * u).astype(BF16)
        y = jnp.dot(a, wd_s[...], preferred_element_type=F32)
        _store_row_tiles(o_ref, _pack_halves(y.astype(BF16)))

    @pl.when(jnp.logical_not(active))
    def _():
        o_ref[...] = jnp.zeros_like(o_ref)


def _experts(xs, block_expert, n_used, expert_rank, kth_expert, w_gate, w_up, w_down):
    Dh = xs.shape[1]
    D, F = w_gate.shape[1], w_gate.shape[2]
    assert Dh == LANES and D == 2 * ROW_TILE * LANES
    nblk = block_expert.shape[0]
    P = nblk * MOE_ROWS * ROW_TILE
    rows = lambda i, be, nu, nx, sl: (jnp.minimum(i, nu[0] - 1), 0)
    hbm = pl.BlockSpec(memory_space=pl.ANY)
    return pl.pallas_call(
        _expert_kernel,
        grid_spec=pltpu.PrefetchScalarGridSpec(
            num_scalar_prefetch=4,
            grid=(nblk,),
            in_specs=[pl.BlockSpec((MOE_ROWS * ROW_TILE, Dh), rows), hbm, hbm, hbm],
            out_specs=pl.BlockSpec((MOE_ROWS * ROW_TILE, Dh), lambda i, be, nu, nx, sl: (i, 0)),
            scratch_shapes=[pltpu.VMEM((MOE_WEIGHT_SLOTS, D, F), F32), pltpu.VMEM((MOE_WEIGHT_SLOTS, D, F), F32),
                            pltpu.VMEM((MOE_WEIGHT_SLOTS, F, D), F32),
                            pltpu.VMEM((D, F), BF16), pltpu.VMEM((D, F), BF16), pltpu.VMEM((F, D), BF16),
                            pltpu.SemaphoreType.DMA((MOE_WEIGHT_SLOTS, 3))],
        ),
        out_shape=jax.ShapeDtypeStruct((P, Dh), jnp.uint32),
        compiler_params=_cparams(("arbitrary",)),
        name="moe_experts",
    )(block_expert, n_used, expert_rank, kth_expert, xs, w_gate, w_up, w_down)


def _combine_kernel(dest_ref, x1_ref, rt_ref, g_ref, ys_ref, o_ref, buf, sem, *, tq):
    i = pl.program_id(0)
    n = pl.num_programs(0)

    def token_rows(t):
        return pl.ds(pl.multiple_of(t * ROW_TILE, ROW_TILE), ROW_TILE)

    def issue(step, s):
        def body(j, carry):
            for k in range(MOE_TOPK):
                pltpu.make_async_copy(ys_ref.at[token_rows(dest_ref[2 * (step * tq + j) + k])],
                                      buf.at[s, k, token_rows(j)], sem.at[s]).start()
            return carry
        lax.fori_loop(0, tq, body, 0)

    @pl.when(i == 0)
    def _():
        issue(0, 0)

    for s in range(2):
        @pl.when((i + 1 < n) & ((i + 1) % 2 == s))
        def _():
            issue(i + 1, s)

    for s in range(2):
        @pl.when(i % 2 == s)
        def _():
            for k in range(MOE_TOPK):
                pltpu.make_async_copy(ys_ref.at[pl.ds(0, tq * ROW_TILE)], buf.at[s, k], sem.at[s]).wait()
            rt = rt_ref[...]
            y0 = _unpack_halves(_load_row_tiles(buf.at[s, 0], tq))
            y1 = _unpack_halves(_load_row_tiles(buf.at[s, 1], tq))
            x = x1_ref[...] + rt[:, 2:3] * y0 + rt[:, 3:4] * y1
            ms = jnp.mean(x * x, axis=-1, keepdims=True)
            o_ref[...] = x * lax.rsqrt(ms + NORM_EPS) * g_ref[...]


def _combine(x1, route, gain, ys, dest_flat, tq=256):
    T, D = x1.shape
    return pl.pallas_call(
        functools.partial(_combine_kernel, tq=tq),
        grid_spec=pltpu.PrefetchScalarGridSpec(
            num_scalar_prefetch=1,
            grid=(T // tq,),
            in_specs=[
                pl.BlockSpec((tq, D), lambda i, d: (i, 0)),
                pl.BlockSpec((tq, ROUTE_LANES), lambda i, d: (i, 0)),
                pl.BlockSpec((1, D), lambda i, d: (0, 0)),
                pl.BlockSpec(memory_space=pl.ANY),
            ],
            out_specs=pl.BlockSpec((tq, D), lambda i, d: (i, 0)),
            scratch_shapes=[pltpu.VMEM((2, MOE_TOPK, tq * ROW_TILE, LANES), jnp.uint32),
                            pltpu.SemaphoreType.DMA((2,))],
        ),
        out_shape=jax.ShapeDtypeStruct((T, D), F32),
        compiler_params=_cparams(("arbitrary",)),
        name="moe_combine",
    )(dest_flat, x1, route, gain.reshape(1, D).astype(F32), ys)


def _route_metadata(route, T):
    e = route[:, :MOE_TOPK].astype(jnp.int32)
    ids = jnp.arange(MOE_EXPERTS, dtype=jnp.int32)
    oh0 = e[:, 0, None] == ids
    oh1 = e[:, 1, None] == ids
    onehot = (oh0 | oh1).astype(jnp.int32)
    incl = jnp.cumsum(onehot, axis=0)
    counts = incl[-1]
    pcounts = (counts + MOE_ROWS - 1) // MOE_ROWS * MOE_ROWS
    pends = jnp.cumsum(pcounts)
    poffs = pends - pcounts
    slot = incl - onehot + poffs[None, :]
    dest = jnp.stack([jnp.sum(jnp.where(oh0, slot, 0), axis=1), jnp.sum(jnp.where(oh1, slot, 0), axis=1)], axis=1)
    nblk = (T * MOE_TOPK) // MOE_ROWS + MOE_EXPERTS
    block_expert = jnp.minimum(
        jnp.searchsorted(pends, jnp.arange(nblk, dtype=jnp.int32) * MOE_ROWS, side="right"),
        MOE_EXPERTS - 1).astype(jnp.int32)
    n_used = (pends[-1:] // MOE_ROWS).astype(jnp.int32)
    used = counts > 0
    expert_rank = (jnp.cumsum(used.astype(jnp.int32)) - 1).astype(jnp.int32)
    order = jnp.argsort(jnp.where(used, ids, ids + MOE_EXPERTS)).astype(jnp.int32)
    kth = jnp.where(ids < jnp.sum(used), order, -1)
    kth_expert = jnp.concatenate([kth, jnp.full((MOE_WEIGHT_SLOTS,), -1, jnp.int32)])
    blk = jnp.arange(nblk, dtype=jnp.int32)
    following = jnp.concatenate([block_expert[1:], block_expert[-1:]])
    zero_block = ((blk >= n_used[0] - 1) | (block_expert != following)).astype(jnp.int32)
    return dest.reshape(-1).astype(jnp.int32), zero_block, block_expert, n_used, expert_rank, kth_expert, nblk


def kernel(x, norm1_gain, w_in, hg_lb_logits, hg_norm_gain, rel_bias, w_branch_a, w_branch_b, w_out,
           norm2_gain, w_router_group, b_router_group, w_router_expert, b_router_expert,
           w_exp_gate, w_exp_up, w_exp_down, final_norm_gain):
    B, S, D = x.shape
    T = B * S
    depth = norm1_gain.shape[0]
    lower_bounds = jnp.cumsum(jax.nn.softmax(hg_lb_logits.astype(F32), axis=0), axis=0)
    att_w = ATT_HEADS * HEAD_DIM
    bias = _attn_bias(rel_bias)
    x2d = x.reshape(T, D)
    for layer in range(depth):
        zeros_lb = jnp.zeros((1, 3 * att_w), F32)
        hm = dict(head_major=True, B=B, S=S)
        h, q_a = _normproj(x2d, norm1_gain[layer], w_in, layer, D, B=B, S=S)
        lf_a = _inproj(h, w_in, layer, D, D, lower_bounds[layer].reshape(1, D), mode="logf", out_dtype=F32, **hm)
        i_a = _inproj(h, w_in, layer, 2 * D, D, zeros_lb, mode="none", out_dtype=BF16, **hm)
        sg_a = _inproj(h, w_in, layer, 3 * D, D, zeros_lb, mode="sigmoid", out_dtype=BF16, **hm)
        qkv_b = _inproj(h, w_in, layer, 4 * D, 3 * att_w, zeros_lb, mode="none", out_dtype=BF16, tn=att_w, **hm)
        gates = _inproj(h, w_in, layer, 4 * D + 3 * att_w, 2 * D, zeros_lb, mode="sigmoid",
                        head_major=False, out_dtype=BF16, B=B, S=S)

        y_a = _hgrn(q_a, lf_a, i_a, sg_a, hg_norm_gain[layer]).reshape(T, D)
        y_b = _attention(qkv_b, bias).reshape(T, ATT_HEADS_PER_GROUP * HEAD_DIM)
        merged = _merge(y_a, y_b, w_branch_a[layer].astype(BF16), w_branch_b[layer].astype(BF16), gates)

        wr = jnp.zeros((D, ROUTE_LANES), F32)
        wr = wr.at[:, :MOE_EXPERTS].set(w_router_expert[layer].astype(F32))
        wr = wr.at[:, MOE_EXPERTS:MOE_EXPERTS + MOE_GROUPS].set(w_router_group[layer].astype(F32))
        br = jnp.zeros((1, ROUTE_LANES), F32)
        br = br.at[0, :MOE_EXPERTS].set(b_router_expert[layer].astype(F32))
        br = br.at[0, MOE_EXPERTS:MOE_EXPERTS + MOE_GROUPS].set(b_router_group[layer].astype(F32))
        wr_hi = wr.astype(BF16)
        wr_lo = (wr - wr_hi.astype(F32)).astype(BF16)
        x1, h2, route = _outproj(merged, w_out[layer].astype(BF16), x2d, norm2_gain[layer],
                                 jnp.concatenate([wr_hi, wr_lo], axis=1), br)

        dest, zero_block, block_expert, n_used, expert_rank, kth_expert, nblk = _route_metadata(route, T)
        xs = _dispatch(h2, dest, zero_block, nblk * MOE_ROWS)
        ys = _experts(xs, block_expert, n_used, expert_rank, kth_expert,
                      w_exp_gate[layer], w_exp_up[layer], w_exp_down[layer])
        last = layer == depth - 1
        assert last, "the fused combine applies the final norm; deeper stacks need an un-normalised combine"
        x2d = _combine(x1, route, final_norm_gain, ys, dest)
    return x2d.reshape(B, S, D)
```

```python
import functools
import math

import numpy as np
import jax
import jax.numpy as jnp
from jax import lax
from jax.experimental import pallas as pl
from jax.experimental.pallas import tpu as pltpu

F32 = jnp.float32
BF16 = jnp.bfloat16

LANES = 128
NORM_EPS = 1e-6
HEAD_DIM = 128
HG_HEADS = 16
ATT_GROUPS = ((128, 1), (512, 4), (2048, 16))
ATT_HEADS_PER_GROUP = 4
ATT_HEADS = len(ATT_GROUPS) * ATT_HEADS_PER_GROUP
ATT_BLOCK = 128
ATT_STACK = 16
REL_BUCKETS = 32
REL_MAX_DIST = 2048
MOE_GROUPS = 8
MOE_EXPERTS_PER_GROUP = 8
MOE_EXPERTS = MOE_GROUPS * MOE_EXPERTS_PER_GROUP
MOE_TOPK = 2
MOE_ROWS = 256
NEG_BIG = -1e30
LOG2E = 1.4426950408889634
VMEM_LIMIT = 56 * 1024 * 1024


def _cparams(sem):
    return pltpu.CompilerParams(dimension_semantics=sem, vmem_limit_bytes=VMEM_LIMIT)


def _normproj_kernel(x_ref, g_ref, w_ref, h_ref, o_ref, w_bf):
    @pl.when(pl.program_id(0) == 0)
    def _():
        w_bf[...] = w_ref[...].astype(BF16)

    x = x_ref[...]
    ms = jnp.mean(x * x, axis=-1, keepdims=True)
    h = (x * lax.rsqrt(ms + NORM_EPS) * g_ref[...]).astype(h_ref.dtype)
    h_ref[...] = h
    acc = jnp.dot(h, w_bf[...], preferred_element_type=F32)
    for hh in range(acc.shape[1] // HEAD_DIM):
        o_ref[0, hh] = acc[:, hh * HEAD_DIM:(hh + 1) * HEAD_DIM].astype(o_ref.dtype)


def _normproj(x2d, gain, w_all, layer, N, *, B, S, tm=512):
    T, D = x2d.shape
    assert S % tm == 0
    spb = S // tm
    return pl.pallas_call(
        _normproj_kernel,
        grid=(T // tm,),
        in_specs=[
            pl.BlockSpec((tm, D), lambda i: (i, 0)),
            pl.BlockSpec((1, D), lambda i: (0, 0)),
            pl.BlockSpec((None, D, N), lambda i: (layer, 0, 0), pipeline_mode=pl.Buffered(1)),
        ],
        out_specs=[
            pl.BlockSpec((tm, D), lambda i: (i, 0)),
            pl.BlockSpec((1, N // HEAD_DIM, tm, HEAD_DIM), lambda i: (i // spb, 0, i % spb, 0)),
        ],
        out_shape=[jax.ShapeDtypeStruct((T, D), BF16),
                   jax.ShapeDtypeStruct((B, N // HEAD_DIM, S, HEAD_DIM), BF16)],
        scratch_shapes=[pltpu.VMEM((D, N), BF16)],
        compiler_params=_cparams(("arbitrary",)),
        name="norm_inproj",
    )(x2d, gain.reshape(1, D).astype(F32), w_all)


def _inproj_kernel(h_ref, w_ref, lb_ref, o_ref, w_bf, *, mode, head_major):
    @pl.when(pl.program_id(1) == 0)
    def _():
        w_bf[...] = w_ref[0].astype(BF16)

    acc = jnp.dot(h_ref[...], w_bf[...], preferred_element_type=F32)
    if mode == "logf":
        lb = lb_ref[...]
        acc = jnp.log(lb + (1.0 - lb) * jax.nn.sigmoid(acc))
    elif mode == "sigmoid":
        acc = jax.nn.sigmoid(acc)
    if head_major:
        for hh in range(acc.shape[1] // HEAD_DIM):
            o_ref[0, hh] = acc[:, hh * HEAD_DIM:(hh + 1) * HEAD_DIM].astype(o_ref.dtype)
    else:
        o_ref[...] = acc.astype(o_ref.dtype)


def _inproj(h, w_all, layer, col0, N, lb, *, mode, head_major, out_dtype, B, S, tm=1024, tn=1024):
    T, D = h.shape
    assert N % tn == 0 and col0 % LANES == 0 and T % tm == 0 and S % tm == 0
    spb = S // tm
    if head_major:
        out_shape = jax.ShapeDtypeStruct((B, N // HEAD_DIM, S, HEAD_DIM), out_dtype)
        out_spec = pl.BlockSpec((1, tn // HEAD_DIM, tm, HEAD_DIM), lambda j, i: (i // spb, j, i % spb, 0))
    else:
        out_shape = jax.ShapeDtypeStruct((T, N), out_dtype)
        out_spec = pl.BlockSpec((tm, tn), lambda j, i: (i, j))
    return pl.pallas_call(
        functools.partial(_inproj_kernel, mode=mode, head_major=head_major),
        grid=(N // tn, T // tm),
        in_specs=[
            pl.BlockSpec((tm, D), lambda j, i: (i, 0)),
            pl.BlockSpec((pl.Element(1), pl.Element(D), pl.Element(tn)), lambda j, i: (layer, 0, pl.multiple_of(col0 + j * tn, LANES))),
            pl.BlockSpec((1, tn), lambda j, i: (0, j)),
        ],
        out_specs=out_spec,
        out_shape=out_shape,
        scratch_shapes=[pltpu.VMEM((D, tn), BF16)],
        compiler_params=_cparams(("parallel", "arbitrary")),
        name="inproj_" + mode,
    )(h, w_all, lb)


HG_CHUNK = 256
HG_DIAG = 16
HG_HEADS_PER_STEP = 4


def _hgrn_levels(C):
    out, m = [], C // 2
    while m >= HG_DIAG:
        out.append(m)
        m //= 2
    return out


def _hgrn_masks(C):
    t = np.arange(C)[:, None]
    s = np.arange(C)[None, :]
    masks = []
    for m in _hgrn_levels(C):
        masks.append((t // (2 * m) == s // (2 * m)) & ((t // m) % 2 == 1) & ((s // m) % 2 == 0))
    masks.append((t // HG_DIAG == s // HG_DIAG) & (t >= s))
    total = np.sum(np.stack(masks).astype(np.int32), axis=0)
    assert np.array_equal(total, (t >= s).astype(np.int32))
    return np.stack(masks).astype(np.float32)


def _dot_nt(a, b):
    return lax.dot_general(a, b, (((1,), (1,)), ((), ())), preferred_element_type=F32)


def _dot_tn(a, b):
    return lax.dot_general(a, b, (((0,), (0,)), ((), ())), preferred_element_type=F32)


def _hgrn_kernel(q_ref, lf_ref, v_ref, sg_ref, gain_ref, tril_ref, mask_ref, o_ref, st_ref, b_ref, *, C):
    S = q_ref.shape[2]
    HP = q_ref.shape[1]
    levels = _hgrn_levels(C)
    nchunks = S // C
    tril = tril_ref[...]

    def ref_rows(h, block, row_of_block):
        parts = [jnp.broadcast_to(b_ref[h, pl.ds(row_of_block(p), 1), :], (block, HEAD_DIM))
                 for p in range(C // block)]
        return jnp.concatenate(parts, axis=0)

    def cumsum(c):
        r0 = pl.multiple_of(c * C, C)
        out = []
        for h in range(HP):
            lf2 = lf_ref[0, h, pl.ds(r0, C), :] * LOG2E
            hi = lf2.astype(BF16)
            lo = (lf2 - hi.astype(F32)).astype(BF16)
            r = jnp.dot(tril, jnp.concatenate([hi, lo], axis=1), preferred_element_type=F32)
            out.append(r[:, :HEAD_DIM] + r[:, HEAD_DIM:])
        return tuple(out)

    st_ref[...] = jnp.zeros_like(st_ref)

    def chunk(c, b_all):
        r0 = pl.multiple_of(c * C, C)
        rows = pl.ds(r0, C)
        b_next = cumsum(jnp.minimum(c + 1, nchunks - 1))
        for h in range(HP):
            b = b_all[h]
            b_ref[h] = b
            q = q_ref[0, h, rows, :].astype(F32)
            kk = 1.0 - jnp.exp(lf_ref[0, h, rows, :])
            b_last = b_ref[h, pl.ds(C - 1, 1), :]
            scores = jnp.zeros((C, C), F32)
            for li, m in enumerate(levels):
                d = b - ref_rows(h, 2 * m, lambda p, m=m: 2 * m * p + m - 1)
                e = jnp.exp2(-jnp.abs(d))
                scores = scores + mask_ref[li] * _dot_nt((q * e).astype(BF16), (kk * e).astype(BF16))
            d = b - ref_rows(h, HG_DIAG, lambda p: HG_DIAG * p + HG_DIAG // 2 - 1)
            scores = scores + mask_ref[len(levels)] * _dot_nt((q * jnp.exp2(d)).astype(BF16),
                                                               (kk * jnp.exp2(-d)).astype(BF16))
            v = v_ref[0, h, rows, :]
            st = st_ref[h]
            qe = (q * jnp.exp2(b)).astype(BF16)
            kd = (kk * jnp.exp2(b_last - b)).astype(BF16)
            o = jnp.dot(scores.astype(BF16), v, preferred_element_type=F32) + _dot_nt(qe, st.astype(BF16))
            st_ref[h] = st * jnp.exp2(b_last) + _dot_tn(v, kd)
            ms = jnp.mean(o * o, axis=-1, keepdims=True)
            y = o * lax.rsqrt(ms + NORM_EPS) * gain_ref[:, h * HEAD_DIM:(h + 1) * HEAD_DIM]
            y = y * sg_ref[0, h, rows, :].astype(F32)
            o_ref[0, rows, h * HEAD_DIM:(h + 1) * HEAD_DIM] = y.astype(o_ref.dtype)
        return b_next

    lax.fori_loop(0, nchunks, chunk, cumsum(0))


def _hgrn(q, lf, v, sg, gain):
    B, H, S, E = q.shape
    C, HP = HG_CHUNK, HG_HEADS_PER_STEP
    masks = jnp.asarray(_hgrn_masks(C))
    tril = jnp.asarray(np.tril(np.ones((C, C), np.float32)), dtype=BF16)
    head_spec = pl.BlockSpec((1, HP, S, E), lambda b, h: (b, h, 0, 0))
    return pl.pallas_call(
        functools.partial(_hgrn_kernel, C=C),
        grid=(B, H // HP),
        in_specs=[
            head_spec, head_spec, head_spec, head_spec,
            pl.BlockSpec((1, HP * E), lambda b, h: (0, h)),
            pl.BlockSpec((C, C), lambda b, h: (0, 0)),
            pl.BlockSpec(masks.shape, lambda b, h: (0, 0, 0)),
        ],
        out_specs=pl.BlockSpec((1, S, HP * E), lambda b, h: (b, 0, h)),
        out_shape=jax.ShapeDtypeStruct((B, S, H * E), BF16),
        scratch_shapes=[pltpu.VMEM((HP, E, E), F32), pltpu.VMEM((HP, C, E), F32)],
        compiler_params=_cparams(("parallel", "parallel")),
        name="hgrn2",
    )(q, lf, v, sg, gain.reshape(1, H * E).astype(F32), tril, masks)


def _t5_bucket_np(dist):
    exact = REL_BUCKETS // 2
    d_f = np.maximum(dist, 1).astype(np.float32)
    log_b = exact + (np.log(d_f / np.float32(exact)) / np.float32(math.log(REL_MAX_DIST / exact))
                     * np.float32(REL_BUCKETS - exact)).astype(np.int32)
    return np.where(dist < exact, dist, np.minimum(log_b, REL_BUCKETS - 1))


def _attn_bias(rel_bias):
    blk = ATT_BLOCK
    period = 3 * blk
    out = []
    for gi, (window, dilation) in enumerate(ATT_GROUPS):
        n_back = window // dilation
        assert n_back <= blk
        hs = slice(gi * ATT_HEADS_PER_GROUP, (gi + 1) * ATT_HEADS_PER_GROUP)
        bucket = _t5_bucket_np(np.arange(n_back + 1) * dilation)
        by_delta = rel_bias[:, hs][bucket].astype(F32).T
        u = jnp.full((ATT_HEADS_PER_GROUP, period), NEG_BIG, F32)
        u = u.at[:, 2 * blk - 1 - n_back:2 * blk].set(by_delta[:, ::-1])
        flat = jnp.tile(u, (1, blk))[:, :blk * (period - 1)]
        out.append(flat.reshape(ATT_HEADS_PER_GROUP, blk, period - 1)[:, :, blk - 1:3 * blk - 1])
    return jnp.stack(out, axis=0)


def _attn_kernel(q_ref, k_ref, v_ref, bias_ref, o_ref, qf, kf, vf, og, lg):
    S = q_ref.shape[3]
    scale = HEAD_DIM ** -0.5
    blk = ATT_BLOCK

    for g, (window, d) in enumerate(ATT_GROUPS):
        L = S // d
        nb = L // blk
        if d > 1:
            qf[...] = q_ref[0, g, 0].astype(F32)
            kf[...] = k_ref[0, g, 0].astype(F32)
            vf[...] = v_ref[0, g, 0].astype(F32)

        def load(ref_bf, ref_f32, start, size, g=g, d=d):
            if d == 1:
                return ref_bf[0, g, 0, pl.ds(start, size), :]
            return ref_f32[pl.ds(start, size, stride=d), :].astype(BF16)

        def scores(r, n, g=g, d=d, load=load):
            start = r + n * blk * d
            q = load(q_ref, qf, start, blk)
            if n == 0:
                k = load(k_ref, kf, start, blk)
                v = load(v_ref, vf, start, blk)
                s = _dot_nt(q, k) * scale + bias_ref[g, 0, :, blk:]
            else:
                first = start - blk * d
                k = load(k_ref, kf, first, 2 * blk)
                v = load(v_ref, vf, first, 2 * blk)
                s = _dot_nt(q, k) * scale + bias_ref[g, 0]
            return start, s, v

        def softmax_pv(items, g=g, d=d):
            s = jnp.concatenate([it[1] for it in items], axis=0)
            m = jnp.max(s, axis=-1, keepdims=True)
            p = jnp.exp(s - m)
            den = jnp.sum(p, axis=-1, keepdims=True)
            pb = p.astype(BF16)
            lse = jnp.broadcast_to(m + jnp.log(den), (s.shape[0], HEAD_DIM))
            for j, (start, _, v) in enumerate(items):
                sl = slice(j * blk, (j + 1) * blk)
                rows = pl.ds(start, blk, stride=d) if d > 1 else pl.ds(start, blk)
                og[g, rows, :] = jnp.dot(pb[sl], v, preferred_element_type=F32) / den[sl]
                lg[g, rows, :] = lse[sl]

        blocks = [(r, n) for r in range(d) for n in range(nb)]
        first_blocks = [bn for bn in blocks if bn[1] == 0]
        later_blocks = [bn for bn in blocks if bn[1] > 0]
        for group in (first_blocks, later_blocks):
            for j0 in range(0, len(group), ATT_STACK):
                softmax_pv([scores(r, n) for r, n in group[j0:j0 + ATT_STACK]])

    rows_per_step = 256

    def mix(i, carry):
        rows = pl.ds(pl.multiple_of(i * rows_per_step, rows_per_step), rows_per_step)
        l0, l1, l2 = lg[0, rows, :], lg[1, rows, :], lg[2, rows, :]
        mx = jnp.maximum(jnp.maximum(l0, l1), l2)
        e0, e1, e2 = jnp.exp(l0 - mx), jnp.exp(l1 - mx), jnp.exp(l2 - mx)
        num = e0 * og[0, rows, :] + e1 * og[1, rows, :] + e2 * og[2, rows, :]
        o_ref[0, rows, :] = (num / (e0 + e1 + e2)).astype(o_ref.dtype)
        return carry

    lax.fori_loop(0, S // rows_per_step, mix, 0)


def _attention(qkv, bias):
    B, _, S, E = qkv.shape
    G, HG = len(ATT_GROUPS), ATT_HEADS_PER_GROUP
    x = qkv.reshape(B, 3, G, HG, S, E)

    def spec(which):
        return pl.BlockSpec((1, None, G, 1, S, E), lambda b, h, which=which: (b, which, 0, h, 0, 0))

    def kern(q_ref, k_ref, v_ref, bias_ref, o_ref, *scratch):
        _attn_kernel(q_ref, k_ref, v_ref, bias_ref, o_ref, *scratch)

    return pl.pallas_call(
        kern,
        grid=(B, HG),
        in_specs=[spec(0), spec(1), spec(2),
                  pl.BlockSpec((G, 1, ATT_BLOCK, 2 * ATT_BLOCK), lambda b, h: (0, h, 0, 0))],
        out_specs=pl.BlockSpec((1, S, E), lambda b, h: (b, 0, h)),
        out_shape=jax.ShapeDtypeStruct((B, S, HG * E), BF16),
        scratch_shapes=[pltpu.VMEM((S, E), F32)] * 3 + [pltpu.VMEM((G, S, E), F32)] * 2,
        compiler_params=_cparams(("parallel", "parallel")),
        name="dilated_attn",
    )(x, x, x, bias)


def _merge_kernel(ya_ref, yb_ref, wa_ref, wb_ref, ga_ref, gb_ref, o_ref):
    a = jnp.dot(ya_ref[...], wa_ref[...], preferred_element_type=F32)
    b = jnp.dot(yb_ref[...], wb_ref[...], preferred_element_type=F32)
    o_ref[...] = (ga_ref[...].astype(F32) * a + gb_ref[...].astype(F32) * b).astype(o_ref.dtype)


def _merge(ya, yb, wa, wb, gates, tm=1024, tn=1024):
    T, Ka = ya.shape
    Kb = yb.shape[1]
    N = wa.shape[1]
    nj = N // tn
    return pl.pallas_call(
        _merge_kernel,
        grid=(nj, T // tm),
        in_specs=[
            pl.BlockSpec((tm, Ka), lambda j, i: (i, 0)),
            pl.BlockSpec((tm, Kb), lambda j, i: (i, 0)),
            pl.BlockSpec((Ka, tn), lambda j, i: (0, j)),
            pl.BlockSpec((Kb, tn), lambda j, i: (0, j)),
            pl.BlockSpec((tm, tn), lambda j, i: (i, j)),
            pl.BlockSpec((tm, tn), lambda j, i, nj=nj: (i, j + nj)),
        ],
        out_specs=pl.BlockSpec((tm, tn), lambda j, i: (i, j)),
        out_shape=jax.ShapeDtypeStruct((T, N), BF16),
        compiler_params=_cparams(("parallel", "parallel")),
        name="branch_merge",
    )(ya, yb, wa, wb, gates, gates)


ROUTE_LANES = LANES


def _pack_halves(x_bf):
    bits = pltpu.bitcast(x_bf.astype(F32), jnp.uint32)
    n = bits.shape[1] // 2
    return (bits[:, :n] >> 16) | (bits[:, n:] & jnp.uint32(0xFFFF0000))


def _unpack_halves(p):
    lo = pltpu.bitcast(p << 16, F32)
    hi = pltpu.bitcast(p & jnp.uint32(0xFFFF0000), F32)
    return jnp.concatenate([lo, hi], axis=1)


ROW_TILE = 8


def _store_row_tiles(ref, packed):
    m = packed.shape[0]
    assert packed.shape[1] == ROW_TILE * LANES
    for c in range(ROW_TILE):
        ref[pl.ds(c, m, stride=ROW_TILE), :] = packed[:, c * LANES:(c + 1) * LANES]


def _load_row_tiles(ref, m):
    return jnp.concatenate([ref[pl.ds(c, m, stride=ROW_TILE), :] for c in range(ROW_TILE)], axis=1)


def _outproj_kernel(m_ref, w_ref, x_ref, g_ref, wrc_ref, br_ref, x1_ref, h2_ref, rt_ref):
    x1 = x_ref[...] + jnp.dot(m_ref[...], w_ref[...], preferred_element_type=F32)
    x1_ref[...] = x1
    ms = jnp.mean(x1 * x1, axis=-1, keepdims=True)
    h2 = x1 * lax.rsqrt(ms + NORM_EPS) * g_ref[...]
    h_hi = h2.astype(BF16)
    _store_row_tiles(h2_ref, _pack_halves(h_hi))
    h_lo = (h2 - h_hi.astype(F32)).astype(BF16)
    r = jnp.dot(h_hi, wrc_ref[...], preferred_element_type=F32)
    lg = (r[:, :ROUTE_LANES] + r[:, ROUTE_LANES:]
          + jnp.dot(h_lo, wrc_ref[:, :ROUTE_LANES], preferred_element_type=F32)) + br_ref[...]

    lane = lax.broadcasted_iota(jnp.int32, lg.shape, 1)
    lane_f = lane.astype(F32)
    big = float(ROUTE_LANES)
    is_group = (lane >= MOE_EXPERTS) & (lane < MOE_EXPERTS + MOE_GROUPS)
    lgg = jnp.where(is_group, lg, -jnp.inf)
    gmax = jnp.max(lgg, axis=-1, keepdims=True)
    gsel = jnp.min(jnp.where(lgg == gmax, lane_f - MOE_EXPERTS, big), axis=-1, keepdims=True)
    pg = 1.0 / jnp.sum(jnp.where(is_group, jnp.exp(lg - gmax), 0.0), axis=-1, keepdims=True)

    in_group = (lane < MOE_EXPERTS) & ((lane // MOE_EXPERTS_PER_GROUP).astype(F32) == gsel)
    le = jnp.where(in_group, lg, -jnp.inf)
    t1 = jnp.max(le, axis=-1, keepdims=True)
    i1 = jnp.min(jnp.where(le == t1, lane_f, big), axis=-1, keepdims=True)
    le2 = jnp.where(lane_f == i1, -jnp.inf, le)
    t2 = jnp.max(le2, axis=-1, keepdims=True)
    i2 = jnp.min(jnp.where(le2 == t2, lane_f, big), axis=-1, keepdims=True)
    e2 = jnp.exp(t2 - t1)
    w1 = pg / (1.0 + e2)
    w2 = pg * e2 / (1.0 + e2)
    rt_ref[...] = jnp.where(lane == 0, i1, jnp.where(lane == 1, i2, jnp.where(lane == 2, w1,
                            jnp.where(lane == 3, w2, 0.0))))


def _outproj(merged, w_out, x2d, gain, wr_cat, br, tm=512):
    T, D = x2d.shape
    row = lambda i: (i, 0)
    const = lambda i: (0, 0)
    return pl.pallas_call(
        _outproj_kernel,
        grid=(T // tm,),
        in_specs=[
            pl.BlockSpec((tm, D), row), pl.BlockSpec((D, D), const), pl.BlockSpec((tm, D), row),
            pl.BlockSpec((1, D), const), pl.BlockSpec((D, 2 * ROUTE_LANES), const),
            pl.BlockSpec((1, ROUTE_LANES), const),
        ],
        out_specs=[pl.BlockSpec((tm, D), row), pl.BlockSpec((tm * ROW_TILE, LANES), row),
                   pl.BlockSpec((tm, ROUTE_LANES), row)],
        out_shape=[jax.ShapeDtypeStruct((T, D), F32), jax.ShapeDtypeStruct((T * ROW_TILE, LANES), jnp.uint32),
                   jax.ShapeDtypeStruct((T, ROUTE_LANES), F32)],
        compiler_params=_cparams(("parallel",)),
        name="outproj_router",
    )(merged, w_out, x2d, gain.reshape(1, D).astype(F32), wr_cat, br)


def _dispatch_kernel(dest_ref, zb_ref, h_ref, xs_ref, zeros, sem, *, tq):
    base = pl.program_id(0) * tq
    block_rows = MOE_ROWS * ROW_TILE
    n_blocks = xs_ref.shape[0] // block_rows

    @pl.when(pl.program_id(0) == 0)
    def _():
        zeros[...] = jnp.zeros_like(zeros)

        def zero_copy(b):
            return pltpu.make_async_copy(zeros, xs_ref.at[pl.ds(pl.multiple_of(b * block_rows, block_rows), block_rows)],
                                         sem)

        def start(b, carry):
            @pl.when(zb_ref[b] != 0)
            def _():
                zero_copy(b).start()
            return carry

        def wait(b, carry):
            @pl.when(zb_ref[b] != 0)
            def _():
                zero_copy(b).wait()
            return carry

        lax.fori_loop(0, n_blocks, start, 0)
        lax.fori_loop(0, n_blocks, wait, 0)

    def token_rows(t):
        return pl.ds(pl.multiple_of(t * ROW_TILE, ROW_TILE), ROW_TILE)

    def copy(j, k):
        return pltpu.make_async_copy(h_ref.at[token_rows(j)], xs_ref.at[token_rows(dest_ref[2 * (base + j) + k])], sem)

    def issue(j, carry):
        copy(j, 0).start()
        copy(j, 1).start()
        return carry

    lax.fori_loop(0, tq, issue, 0)
    for _ in range(MOE_TOPK):
        pltpu.make_async_copy(h_ref, xs_ref.at[pl.ds(0, tq * ROW_TILE)], sem).wait()


def _dispatch(h2, dest_flat, zero_block, n_rows, tq=512):
    T = h2.shape[0] // ROW_TILE
    return pl.pallas_call(
        functools.partial(_dispatch_kernel, tq=tq),
        grid_spec=pltpu.PrefetchScalarGridSpec(
            num_scalar_prefetch=2,
            grid=(T // tq,),
            in_specs=[pl.BlockSpec((tq * ROW_TILE, LANES), lambda i, d, z: (i, 0))],
            out_specs=pl.BlockSpec(memory_space=pl.ANY),
            scratch_shapes=[pltpu.VMEM((MOE_ROWS * ROW_TILE, LANES), h2.dtype), pltpu.SemaphoreType.DMA(())],
        ),
        out_shape=jax.ShapeDtypeStruct((n_rows * ROW_TILE, LANES), h2.dtype),
        compiler_params=_cparams(("arbitrary",)),
        name="moe_dispatch",
    )(dest_flat, zero_block, h2)


MOE_WEIGHT_SLOTS = 3


def _expert_kernel(be_ref, nu_ref, rank_ref, kth_ref, xs_ref, wg_hbm, wu_hbm, wd_hbm, o_ref,
                   wg_f, wu_f, wd_f, wg_s, wu_s, wd_s, sem):
    i = pl.program_id(0)
    e = be_ref[i]
    active = i < nu_ref[0]
    new_expert = active & ((i == 0) | (e != be_ref[jnp.maximum(i - 1, 0)]))
    k = rank_ref[e]
    slot = k % MOE_WEIGHT_SLOTS
    ahead = MOE_WEIGHT_SLOTS - 1

    def weight_copies(expert, s):
        return (pltpu.make_async_copy(wg_hbm.at[expert], wg_f.at[s], sem.at[s, 0]),
                pltpu.make_async_copy(wu_hbm.at[expert], wu_f.at[s], sem.at[s, 1]),
                pltpu.make_async_copy(wd_hbm.at[expert], wd_f.at[s], sem.at[s, 2]))

    def start_kth(j):
        nxt = kth_ref[j]

        @pl.when(nxt >= 0)
        def _():
            for c in weight_copies(nxt, j % MOE_WEIGHT_SLOTS):
                c.start()

    @pl.when(active & (i == 0))
    def _():
        for j in range(ahead):
            start_kth(j)

    @pl.when(new_expert)
    def _():
        start_kth(k + ahead)
        for c in weight_copies(e, slot):
            c.wait()
        wg_s[...] = wg_f[slot].astype(BF16)
        wu_s[...] = wu_f[slot].astype(BF16)
        wd_s[...] = wd_f[slot].astype(BF16)

    @pl.when(active)
    def _():
        x = _unpack_halves(_load_row_tiles(xs_ref, MOE_ROWS)).astype(BF16)
        g = jnp.dot(x, wg_s[...], preferred_element_type=F32)
        u = jnp.dot(x, wu_s[...], preferred_element_type=F32)
        a = (g * jax.nn.sigmoid(g) * u).astype(BF16)
        y = jnp.dot(a, wd_s[...], preferred_element_type=F32)
        _store_row_tiles(o_ref, _pack_halves(y.astype(BF16)))

    @pl.when(jnp.logical_not(active))
    def _():
        o_ref[...] = jnp.zeros_like(o_ref)


def _experts(xs, block_expert, n_used, expert_rank, kth_expert, w_gate, w_up, w_down):
    Dh = xs.shape[1]
    D, F = w_gate.shape[1], w_gate.shape[2]
    assert Dh == LANES and D == 2 * ROW_TILE * LANES
    nblk = block_expert.shape[0]
    P = nblk * MOE_ROWS * ROW_TILE
    rows = lambda i, be, nu, nx, sl: (jnp.minimum(i, nu[0] - 1), 0)
    hbm = pl.BlockSpec(memory_space=pl.ANY)
    return pl.pallas_call(
        _expert_kernel,
        grid_spec=pltpu.PrefetchScalarGridSpec(
            num_scalar_prefetch=4,
            grid=(nblk,),
            in_specs=[pl.BlockSpec((MOE_ROWS * ROW_TILE, Dh), rows), hbm, hbm, hbm],
            out_specs=pl.BlockSpec((MOE_ROWS * ROW_TILE, Dh), lambda i, be, nu, nx, sl: (i, 0)),
            scratch_shapes=[pltpu.VMEM((MOE_WEIGHT_SLOTS, D, F), F32), pltpu.VMEM((MOE_WEIGHT_SLOTS, D, F), F32),
                            pltpu.VMEM((MOE_WEIGHT_SLOTS, F, D), F32),
                            pltpu.VMEM((D, F), BF16), pltpu.VMEM((D, F), BF16), pltpu.VMEM((F, D), BF16),
                            pltpu.SemaphoreType.DMA((MOE_WEIGHT_SLOTS, 3))],
        ),
        out_shape=jax.ShapeDtypeStruct((P, Dh), jnp.uint32),
        compiler_params=_cparams(("arbitrary",)),
        name="moe_experts",
    )(block_expert, n_used, expert_rank, kth_expert, xs, w_gate, w_up, w_down)


def _combine_kernel(dest_ref, x1_ref, rt_ref, g_ref, ys_ref, o_ref, buf, sem, *, tq):
    i = pl.program_id(0)
    n = pl.num_programs(0)

    def token_rows(t):
        return pl.ds(pl.multiple_of(t * ROW_TILE, ROW_TILE), ROW_TILE)

    def issue(step, s):
        def body(j, carry):
            for k in range(MOE_TOPK):
                pltpu.make_async_copy(ys_ref.at[token_rows(dest_ref[2 * (step * tq + j) + k])],
                                      buf.at[s, k, token_rows(j)], sem.at[s]).start()
            return carry
        lax.fori_loop(0, tq, body, 0)

    @pl.when(i == 0)
    def _():
        issue(0, 0)

    for s in range(2):
        @pl.when((i + 1 < n) & ((i + 1) % 2 == s))
        def _():
            issue(i + 1, s)

    for s in range(2):
        @pl.when(i % 2 == s)
        def _():
            for k in range(MOE_TOPK):
                pltpu.make_async_copy(ys_ref.at[pl.ds(0, tq * ROW_TILE)], buf.at[s, k], sem.at[s]).wait()
            rt = rt_ref[...]
            y0 = _unpack_halves(_load_row_tiles(buf.at[s, 0], tq))
            y1 = _unpack_halves(_load_row_tiles(buf.at[s, 1], tq))
            x = x1_ref[...] + rt[:, 2:3] * y0 + rt[:, 3:4] * y1
            ms = jnp.mean(x * x, axis=-1, keepdims=True)
            o_ref[...] = x * lax.rsqrt(ms + NORM_EPS) * g_ref[...]


def _combine(x1, route, gain, ys, dest_flat, tq=256):
    T, D = x1.shape
    return pl.pallas_call(
        functools.partial(_combine_kernel, tq=tq),
        grid_spec=pltpu.PrefetchScalarGridSpec(
            num_scalar_prefetch=1,
            grid=(T // tq,),
            in_specs=[
                pl.BlockSpec((tq, D), lambda i, d: (i, 0)),
                pl.BlockSpec((tq, ROUTE_LANES), lambda i, d: (i, 0)),
                pl.BlockSpec((1, D), lambda i, d: (0, 0)),
                pl.BlockSpec(memory_space=pl.ANY),
            ],
            out_specs=pl.BlockSpec((tq, D), lambda i, d: (i, 0)),
            scratch_shapes=[pltpu.VMEM((2, MOE_TOPK, tq * ROW_TILE, LANES), jnp.uint32),
                            pltpu.SemaphoreType.DMA((2,))],
        ),
        out_shape=jax.ShapeDtypeStruct((T, D), F32),
        compiler_params=_cparams(("arbitrary",)),
        name="moe_combine",
    )(dest_flat, x1, route, gain.reshape(1, D).astype(F32), ys)


def _route_metadata(route, T):
    e = route[:, :MOE_TOPK].astype(jnp.int32)
    ids = jnp.arange(MOE_EXPERTS, dtype=jnp.int32)
    oh0 = e[:, 0, None] == ids
    oh1 = e[:, 1, None] == ids
    onehot = (oh0 | oh1).astype(jnp.int32)
    incl = jnp.cumsum(onehot, axis=0)
    counts = incl[-1]
    pcounts = (counts + MOE_ROWS - 1) // MOE_ROWS * MOE_ROWS
    pends = jnp.cumsum(pcounts)
    poffs = pends - pcounts
    slot = incl - onehot + poffs[None, :]
    dest = jnp.stack([jnp.sum(jnp.where(oh0, slot, 0), axis=1), jnp.sum(jnp.where(oh1, slot, 0), axis=1)], axis=1)
    nblk = (T * MOE_TOPK) // MOE_ROWS + MOE_EXPERTS
    block_expert = jnp.minimum(
        jnp.searchsorted(pends, jnp.arange(nblk, dtype=jnp.int32) * MOE_ROWS, side="right"),
        MOE_EXPERTS - 1).astype(jnp.int32)
    n_used = (pends[-1:] // MOE_ROWS).astype(jnp.int32)
    used = counts > 0
    expert_rank = (jnp.cumsum(used.astype(jnp.int32)) - 1).astype(jnp.int32)
    order = jnp.argsort(jnp.where(used, ids, ids + MOE_EXPERTS)).astype(jnp.int32)
    kth = jnp.where(ids < jnp.sum(used), order, -1)
    kth_expert = jnp.concatenate([kth, jnp.full((MOE_WEIGHT_SLOTS,), -1, jnp.int32)])
    blk = jnp.arange(nblk, dtype=jnp.int32)
    following = jnp.concatenate([block_expert[1:], block_expert[-1:]])
    zero_block = ((blk >= n_used[0] - 1) | (block_expert != following)).astype(jnp.int32)
    return dest.reshape(-1).astype(jnp.int32), zero_block, block_expert, n_used, expert_rank, kth_expert, nblk


def kernel(x, norm1_gain, w_in, hg_lb_logits, hg_norm_gain, rel_bias, w_branch_a, w_branch_b, w_out,
           norm2_gain, w_router_group, b_router_group, w_router_expert, b_router_expert,
           w_exp_gate, w_exp_up, w_exp_down, final_norm_gain):
    B, S, D = x.shape
    T = B * S
    depth = norm1_gain.shape[0]
    lower_bounds = jnp.cumsum(jax.nn.softmax(hg_lb_logits.astype(F32), axis=0), axis=0)
    att_w = ATT_HEADS * HEAD_DIM
    bias = _attn_bias(rel_bias)
    x2d = x.reshape(T, D)
    for layer in range(depth):
        zeros_lb = jnp.zeros((1, 3 * att_w), F32)
        hm = dict(head_major=True, B=B, S=S)
        h, q_a = _normproj(x2d, norm1_gain[layer], w_in, layer, D, B=B, S=S)
        lf_a = _inproj(h, w_in, layer, D, D, lower_bounds[layer].reshape(1, D), mode="logf", out_dtype=F32, **hm)
        i_a = _inproj(h, w_in, layer, 2 * D, D, zeros_lb, mode="none", out_dtype=BF16, **hm)
        sg_a = _inproj(h, w_in, layer, 3 * D, D, zeros_lb, mode="sigmoid", out_dtype=BF16, **hm)
        qkv_b = _inproj(h, w_in, layer, 4 * D, 3 * att_w, zeros_lb, mode="none", out_dtype=BF16, tn=att_w, **hm)
        gates = _inproj(h, w_in, layer, 4 * D + 3 * att_w, 2 * D, zeros_lb, mode="sigmoid",
                        head_major=False, out_dtype=BF16, B=B, S=S)

        y_a = _hgrn(q_a, lf_a, i_a, sg_a, hg_norm_gain[layer]).reshape(T, D)
        y_b = _attention(qkv_b, bias).reshape(T, ATT_HEADS_PER_GROUP * HEAD_DIM)
        merged = _merge(y_a, y_b, w_branch_a[layer].astype(BF16), w_branch_b[layer].astype(BF16), gates)

        wr = jnp.zeros((D, ROUTE_LANES), F32)
        wr = wr.at[:, :MOE_EXPERTS].set(w_router_expert[layer].astype(F32))
        wr = wr.at[:, MOE_EXPERTS:MOE_EXPERTS + MOE_GROUPS].set(w_router_group[layer].astype(F32))
        br = jnp.zeros((1, ROUTE_LANES), F32)
        br = br.at[0, :MOE_EXPERTS].set(b_router_expert[layer].astype(F32))
        br = br.at[0, MOE_EXPERTS:MOE_EXPERTS + MOE_GROUPS].set(b_router_group[layer].astype(F32))
        wr_hi = wr.astype(BF16)
        wr_lo = (wr - wr_hi.astype(F32)).astype(BF16)
        x1, h2, route = _outproj(merged, w_out[layer].astype(BF16), x2d, norm2_gain[layer],
                                 jnp.concatenate([wr_hi, wr_lo], axis=1), br)

        dest, zero_block, block_expert, n_used, expert_rank, kth_expert, nblk = _route_metadata(route, T)
        xs = _dispatch(h2, dest, zero_block, nblk * MOE_ROWS)
        ys = _experts(xs, block_expert, n_used, expert_rank, kth_expert,
                      w_exp_gate[layer], w_exp_up[layer], w_exp_down[layer])
        last = layer == depth - 1
        assert last, "the fused combine applies the final norm; deeper stacks need an un-normalised combine"
        x2d = _combine(x1, route, final_norm_gain, ys, dest)
    return x2d.reshape(B, S, D)
```

```python
import functools
import math

import numpy as np
import jax
import jax.numpy as jnp
from jax import lax
from jax.experimental import pallas as pl
from jax.experimental.pallas import tpu as pltpu

F32 = jnp.float32
BF16 = jnp.bfloat16

LANES = 128
NORM_EPS = 1e-6
HEAD_DIM = 128
HG_HEADS = 16
ATT_GROUPS = ((128, 1), (512, 4), (2048, 16))
ATT_HEADS_PER_GROUP = 4
ATT_HEADS = len(ATT_GROUPS) * ATT_HEADS_PER_GROUP
ATT_BLOCK = 128
ATT_STACK = 16
REL_BUCKETS = 32
REL_MAX_DIST = 2048
MOE_GROUPS = 8
MOE_EXPERTS_PER_GROUP = 8
MOE_EXPERTS = MOE_GROUPS * MOE_EXPERTS_PER_GROUP
MOE_TOPK = 2
MOE_ROWS = 256
NEG_BIG = -1e30
LOG2E = 1.4426950408889634
VMEM_LIMIT = 56 * 1024 * 1024


def _cparams(sem):
    return pltpu.CompilerParams(dimension_semantics=sem, vmem_limit_bytes=VMEM_LIMIT)


def _normproj_kernel(x_ref, g_ref, w_ref, h_ref, o_ref, w_bf):
    @pl.when(pl.program_id(0) == 0)
    def _():
        w_bf[...] = w_ref[...].astype(BF16)

    x = x_ref[...]
    ms = jnp.mean(x * x, axis=-1, keepdims=True)
    h = (x * lax.rsqrt(ms + NORM_EPS) * g_ref[...]).astype(h_ref.dtype)
    h_ref[...] = h
    acc = jnp.dot(h, w_bf[...], preferred_element_type=F32)
    for hh in range(acc.shape[1] // HEAD_DIM):
        o_ref[0, hh] = acc[:, hh * HEAD_DIM:(hh + 1) * HEAD_DIM].astype(o_ref.dtype)


def _normproj(x2d, gain, w_all, layer, N, *, B, S, tm=512):
    T, D = x2d.shape
    assert S % tm == 0
    spb = S // tm
    return pl.pallas_call(
        _normproj_kernel,
        grid=(T // tm,),
        in_specs=[
            pl.BlockSpec((tm, D), lambda i: (i, 0)),
            pl.BlockSpec((1, D), lambda i: (0, 0)),
            pl.BlockSpec((None, D, N), lambda i: (layer, 0, 0), pipeline_mode=pl.Buffered(1)),
        ],
        out_specs=[
            pl.BlockSpec((tm, D), lambda i: (i, 0)),
            pl.BlockSpec((1, N // HEAD_DIM, tm, HEAD_DIM), lambda i: (i // spb, 0, i % spb, 0)),
        ],
        out_shape=[jax.ShapeDtypeStruct((T, D), BF16),
                   jax.ShapeDtypeStruct((B, N // HEAD_DIM, S, HEAD_DIM), BF16)],
        scratch_shapes=[pltpu.VMEM((D, N), BF16)],
        compiler_params=_cparams(("arbitrary",)),
        name="norm_inproj",
    )(x2d, gain.reshape(1, D).astype(F32), w_all)


def _inproj_kernel(h_ref, w_ref, lb_ref, o_ref, w_bf, *, mode, head_major):
    @pl.when(pl.program_id(1) == 0)
    def _():
        w_bf[...] = w_ref[0].astype(BF16)

    acc = jnp.dot(h_ref[...], w_bf[...], preferred_element_type=F32)
    if mode == "logf":
        lb = lb_ref[...]
        acc = jnp.log(lb + (1.0 - lb) * jax.nn.sigmoid(acc))
    elif mode == "sigmoid":
        acc = jax.nn.sigmoid(acc)
    if head_major:
        for hh in range(acc.shape[1] // HEAD_DIM):
            o_ref[0, hh] = acc[:, hh * HEAD_DIM:(hh + 1) * HEAD_DIM].astype(o_ref.dtype)
    else:
        o_ref[...] = acc.astype(o_ref.dtype)


def _inproj(h, w_all, layer, col0, N, lb, *, mode, head_major, out_dtype, B, S, tm=1024, tn=1024):
    T, D = h.shape
    assert N % tn == 0 and col0 % LANES == 0 and T % tm == 0 and S % tm == 0
    spb = S // tm
    if head_major:
        out_shape = jax.ShapeDtypeStruct((B, N // HEAD_DIM, S, HEAD_DIM), out_dtype)
        out_spec = pl.BlockSpec((1, tn // HEAD_DIM, tm, HEAD_DIM), lambda j, i: (i // spb, j, i % spb, 0))
    else:
        out_shape = jax.ShapeDtypeStruct((T, N), out_dtype)
        out_spec = pl.BlockSpec((tm, tn), lambda j, i: (i, j))
    return pl.pallas_call(
        functools.partial(_inproj_kernel, mode=mode, head_major=head_major),
        grid=(N // tn, T // tm),
        in_specs=[
            pl.BlockSpec((tm, D), lambda j, i: (i, 0)),
            pl.BlockSpec((pl.Element(1), pl.Element(D), pl.Element(tn)), lambda j, i: (layer, 0, pl.multiple_of(col0 + j * tn, LANES))),
            pl.BlockSpec((1, tn), lambda j, i: (0, j)),
        ],
        out_specs=out_spec,
        out_shape=out_shape,
        scratch_shapes=[pltpu.VMEM((D, tn), BF16)],
        compiler_params=_cparams(("parallel", "arbitrary")),
        name="inproj_" + mode,
    )(h, w_all, lb)


HG_CHUNK = 256
HG_DIAG = 16
HG_HEADS_PER_STEP = 4


def _hgrn_levels(C):
    out, m = [], C // 2
    while m >= HG_DIAG:
        out.append(m)
        m //= 2
    return out


def _hgrn_masks(C):
    t = np.arange(C)[:, None]
    s = np.arange(C)[None, :]
    masks = []
    for m in _hgrn_levels(C):
        masks.append((t // (2 * m) == s // (2 * m)) & ((t // m) % 2 == 1) & ((s // m) % 2 == 0))
    masks.append((t // HG_DIAG == s // HG_DIAG) & (t >= s))
    total = np.sum(np.stack(masks).astype(np.int32), axis=0)
    assert np.array_equal(total, (t >= s).astype(np.int32))
    return np.stack(masks).astype(np.float32)


def _dot_nt(a, b):
    return lax.dot_general(a, b, (((1,), (1,)), ((), ())), preferred_element_type=F32)


def _dot_tn(a, b):
    return lax.dot_general(a, b, (((0,), (0,)), ((), ())), preferred_element_type=F32)


def _hgrn_kernel(q_ref, lf_ref, v_ref, sg_ref, gain_ref, tril_ref, mask_ref, o_ref, st_ref, b_ref, *, C):
    S = q_ref.shape[2]
    HP = q_ref.shape[1]
    levels = _hgrn_levels(C)
    nchunks = S // C
    tril = tril_ref[...]

    def ref_rows(h, block, row_of_block):
        parts = [jnp.broadcast_to(b_ref[h, pl.ds(row_of_block(p), 1), :], (block, HEAD_DIM))
                 for p in range(C // block)]
        return jnp.concatenate(parts, axis=0)

    def cumsum(c):
        r0 = pl.multiple_of(c * C, C)
        out = []
        for h in range(HP):
            lf2 = lf_ref[0, h, pl.ds(r0, C), :] * LOG2E
            hi = lf2.astype(BF16)
            lo = (lf2 - hi.astype(F32)).astype(BF16)
            r = jnp.dot(tril, jnp.concatenate([hi, lo], axis=1), preferred_element_type=F32)
            out.append(r[:, :HEAD_DIM] + r[:, HEAD_DIM:])
        return tuple(out)

    st_ref[...] = jnp.zeros_like(st_ref)

    def chunk(c, b_all):
        r0 = pl.multiple_of(c * C, C)
        rows = pl.ds(r0, C)
        b_next = cumsum(jnp.minimum(c + 1, nchunks - 1))
        for h in range(HP):
            b = b_all[h]
            b_ref[h] = b
            q = q_ref[0, h, rows, :].astype(F32)
            kk = 1.0 - jnp.exp(lf_ref[0, h, rows, :])
            b_last = b_ref[h, pl.ds(C - 1, 1), :]
            scores = jnp.zeros((C, C), BF16)
            sign_bit = jnp.uint32(0x80000000)
            for li, m in enumerate(levels):
                d = b - ref_rows(h, 2 * m, lambda p, m=m: 2 * m * p + m - 1)
                e = jnp.exp2(pltpu.bitcast(pltpu.bitcast(d, jnp.uint32) | sign_bit, F32))
                s_l = _dot_nt((q * e).astype(BF16), (kk * e).astype(BF16)).astype(BF16)
                scores = scores + mask_ref[li] * s_l
            d = b - ref_rows(h, HG_DIAG, lambda p: HG_DIAG * p + HG_DIAG // 2 - 1)
            s_l = _dot_nt((q * jnp.exp2(d)).astype(BF16), (kk * jnp.exp2(-d)).astype(BF16)).astype(BF16)
            scores = scores + mask_ref[len(levels)] * s_l
            v = v_ref[0, h, rows, :]
            st = st_ref[h]
            qe = (q * jnp.exp2(b)).astype(BF16)
            kd = (kk * jnp.exp2(b_last - b)).astype(BF16)
            o = jnp.dot(scores, v, preferred_element_type=F32) + _dot_nt(qe, st.astype(BF16))
            st_ref[h] = st * jnp.exp2(b_last) + _dot_tn(v, kd)
            ms = jnp.mean(o * o, axis=-1, keepdims=True)
            y = o * lax.rsqrt(ms + NORM_EPS) * gain_ref[:, h * HEAD_DIM:(h + 1) * HEAD_DIM]
            y = y * sg_ref[0, h, rows, :].astype(F32)
            o_ref[0, rows, h * HEAD_DIM:(h + 1) * HEAD_DIM] = y.astype(o_ref.dtype)
        return b_next

    lax.fori_loop(0, nchunks, chunk, cumsum(0))


def _hgrn(q, lf, v, sg, gain):
    B, H, S, E = q.shape
    C, HP = HG_CHUNK, HG_HEADS_PER_STEP
    masks = jnp.asarray(_hgrn_masks(C), dtype=BF16)
    tril = jnp.asarray(np.tril(np.ones((C, C), np.float32)), dtype=BF16)
    head_spec = pl.BlockSpec((1, HP, S, E), lambda b, h: (b, h, 0, 0))
    return pl.pallas_call(
        functools.partial(_hgrn_kernel, C=C),
        grid=(B, H // HP),
        in_specs=[
            head_spec, head_spec, head_spec, head_spec,
            pl.BlockSpec((1, HP * E), lambda b, h: (0, h)),
            pl.BlockSpec((C, C), lambda b, h: (0, 0)),
            pl.BlockSpec(masks.shape, lambda b, h: (0, 0, 0)),
        ],
        out_specs=pl.BlockSpec((1, S, HP * E), lambda b, h: (b, 0, h)),
        out_shape=jax.ShapeDtypeStruct((B, S, H * E), BF16),
        scratch_shapes=[pltpu.VMEM((HP, E, E), F32), pltpu.VMEM((HP, C, E), F32)],
        compiler_params=_cparams(("parallel", "parallel")),
        name="hgrn2",
    )(q, lf, v, sg, gain.reshape(1, H * E).astype(F32), tril, masks)


def _t5_bucket_np(dist):
    exact = REL_BUCKETS // 2
    d_f = np.maximum(dist, 1).astype(np.float32)
    log_b = exact + (np.log(d_f / np.float32(exact)) / np.float32(math.log(REL_MAX_DIST / exact))
                     * np.float32(REL_BUCKETS - exact)).astype(np.int32)
    return np.where(dist < exact, dist, np.minimum(log_b, REL_BUCKETS - 1))


def _attn_bias(rel_bias):
    blk = ATT_BLOCK
    period = 3 * blk
    out = []
    for gi, (window, dilation) in enumerate(ATT_GROUPS):
        n_back = window // dilation
        assert n_back <= blk
        hs = slice(gi * ATT_HEADS_PER_GROUP, (gi + 1) * ATT_HEADS_PER_GROUP)
        bucket = _t5_bucket_np(np.arange(n_back + 1) * dilation)
        by_delta = rel_bias[:, hs][bucket].astype(F32).T
        u = jnp.full((ATT_HEADS_PER_GROUP, period), NEG_BIG, F32)
        u = u.at[:, 2 * blk - 1 - n_back:2 * blk].set(by_delta[:, ::-1])
        flat = jnp.tile(u, (1, blk))[:, :blk * (period - 1)]
        out.append(flat.reshape(ATT_HEADS_PER_GROUP, blk, period - 1)[:, :, blk - 1:3 * blk - 1])
    return jnp.stack(out, axis=0)


def _attn_kernel(q_ref, k_ref, v_ref, bias_ref, o_ref, qf, kf, vf, og, lg):
    S = q_ref.shape[3]
    scale = HEAD_DIM ** -0.5
    blk = ATT_BLOCK

    for g, (window, d) in enumerate(ATT_GROUPS):
        L = S // d
        nb = L // blk
        if d > 1:
            qf[...] = q_ref[0, g, 0].astype(F32)
            kf[...] = k_ref[0, g, 0].astype(F32)
            vf[...] = v_ref[0, g, 0].astype(F32)

        def load(ref_bf, ref_f32, start, size, g=g, d=d):
            if d == 1:
                return ref_bf[0, g, 0, pl.ds(start, size), :]
            return ref_f32[pl.ds(start, size, stride=d), :].astype(BF16)

        def scores(r, n, g=g, d=d, load=load):
            start = r + n * blk * d
            q = load(q_ref, qf, start, blk)
            if n == 0:
                k = load(k_ref, kf, start, blk)
                v = load(v_ref, vf, start, blk)
                s = _dot_nt(q, k) * scale + bias_ref[g, 0, :, blk:]
            else:
                first = start - blk * d
                k = load(k_ref, kf, first, 2 * blk)
                v = load(v_ref, vf, first, 2 * blk)
                s = _dot_nt(q, k) * scale + bias_ref[g, 0]
            return start, s, v

        def softmax_pv(items, g=g, d=d):
            s = jnp.concatenate([it[1] for it in items], axis=0)
            m = jnp.max(s, axis=-1, keepdims=True)
            p = jnp.exp(s - m)
            den = jnp.sum(p, axis=-1, keepdims=True)
            pb = p.astype(BF16)
            lse = jnp.broadcast_to(m + jnp.log(den), (s.shape[0], HEAD_DIM))
            for j, (start, _, v) in enumerate(items):
                sl = slice(j * blk, (j + 1) * blk)
                rows = pl.ds(start, blk, stride=d) if d > 1 else pl.ds(start, blk)
                og[g, rows, :] = jnp.dot(pb[sl], v, preferred_element_type=F32) / den[sl]
                lg[g, rows, :] = lse[sl]

        blocks = [(r, n) for r in range(d) for n in range(nb)]
        first_blocks = [bn for bn in blocks if bn[1] == 0]
        later_blocks = [bn for bn in blocks if bn[1] > 0]
        for group in (first_blocks, later_blocks):
            for j0 in range(0, len(group), ATT_STACK):
                softmax_pv([scores(r, n) for r, n in group[j0:j0 + ATT_STACK]])

    rows_per_step = 256

    def mix(i, carry):
        rows = pl.ds(pl.multiple_of(i * rows_per_step, rows_per_step), rows_per_step)
        l0, l1, l2 = lg[0, rows, :], lg[1, rows, :], lg[2, rows, :]
        mx = jnp.maximum(jnp.maximum(l0, l1), l2)
        e0, e1, e2 = jnp.exp(l0 - mx), jnp.exp(l1 - mx), jnp.exp(l2 - mx)
        num = e0 * og[0, rows, :] + e1 * og[1, rows, :] + e2 * og[2, rows, :]
        o_ref[0, rows, :] = (num / (e0 + e1 + e2)).astype(o_ref.dtype)
        return carry

    lax.fori_loop(0, S // rows_per_step, mix, 0)


def _attention(qkv, bias):
    B, _, S, E = qkv.shape
    G, HG = len(ATT_GROUPS), ATT_HEADS_PER_GROUP
    x = qkv.reshape(B, 3, G, HG, S, E)

    def spec(which):
        return pl.BlockSpec((1, None, G, 1, S, E), lambda b, h, which=which: (b, which, 0, h, 0, 0))

    def kern(q_ref, k_ref, v_ref, bias_ref, o_ref, *scratch):
        _attn_kernel(q_ref, k_ref, v_ref, bias_ref, o_ref, *scratch)

    return pl.pallas_call(
        kern,
        grid=(B, HG),
        in_specs=[spec(0), spec(1), spec(2),
                  pl.BlockSpec((G, 1, ATT_BLOCK, 2 * ATT_BLOCK), lambda b, h: (0, h, 0, 0))],
        out_specs=pl.BlockSpec((1, S, E), lambda b, h: (b, 0, h)),
        out_shape=jax.ShapeDtypeStruct((B, S, HG * E), BF16),
        scratch_shapes=[pltpu.VMEM((S, E), F32)] * 3 + [pltpu.VMEM((G, S, E), F32)] * 2,
        compiler_params=_cparams(("parallel", "parallel")),
        name="dilated_attn",
    )(x, x, x, bias)


def _merge_kernel(ya_ref, yb_ref, wa_ref, wb_ref, ga_ref, gb_ref, o_ref):
    a = jnp.dot(ya_ref[...], wa_ref[...], preferred_element_type=F32)
    b = jnp.dot(yb_ref[...], wb_ref[...], preferred_element_type=F32)
    o_ref[...] = (ga_ref[...].astype(F32) * a + gb_ref[...].astype(F32) * b).astype(o_ref.dtype)


def _merge(ya, yb, wa, wb, gates, tm=1024, tn=1024):
    T, Ka = ya.shape
    Kb = yb.shape[1]
    N = wa.shape[1]
    nj = N // tn
    return pl.pallas_call(
        _merge_kernel,
        grid=(nj, T // tm),
        in_specs=[
            pl.BlockSpec((tm, Ka), lambda j, i: (i, 0)),
            pl.BlockSpec((tm, Kb), lambda j, i: (i, 0)),
            pl.BlockSpec((Ka, tn), lambda j, i: (0, j)),
            pl.BlockSpec((Kb, tn), lambda j, i: (0, j)),
            pl.BlockSpec((tm, tn), lambda j, i: (i, j)),
            pl.BlockSpec((tm, tn), lambda j, i, nj=nj: (i, j + nj)),
        ],
        out_specs=pl.BlockSpec((tm, tn), lambda j, i: (i, j)),
        out_shape=jax.ShapeDtypeStruct((T, N), BF16),
        compiler_params=_cparams(("parallel", "parallel")),
        name="branch_merge",
    )(ya, yb, wa, wb, gates, gates)


ROUTE_LANES = LANES


def _pack_halves(x_bf):
    bits = pltpu.bitcast(x_bf.astype(F32), jnp.uint32)
    n = bits.shape[1] // 2
    return (bits[:, :n] >> 16) | (bits[:, n:] & jnp.uint32(0xFFFF0000))


def _unpack_halves(p):
    lo = pltpu.bitcast(p << 16, F32)
    hi = pltpu.bitcast(p & jnp.uint32(0xFFFF0000), F32)
    return jnp.concatenate([lo, hi], axis=1)


ROW_TILE = 8


def _store_row_tiles(ref, packed):
    m = packed.shape[0]
    assert packed.shape[1] == ROW_TILE * LANES
    for c in range(ROW_TILE):
        ref[pl.ds(c, m, stride=ROW_TILE), :] = packed[:, c * LANES:(c + 1) * LANES]


def _load_row_tiles(ref, m):
    return jnp.concatenate([ref[pl.ds(c, m, stride=ROW_TILE), :] for c in range(ROW_TILE)], axis=1)


def _outproj_kernel(m_ref, w_ref, x_ref, g_ref, wrc_ref, br_ref, x1_ref, h2_ref, rt_ref):
    x1 = x_ref[...] + jnp.dot(m_ref[...], w_ref[...], preferred_element_type=F32)
    x1_ref[...] = x1
    ms = jnp.mean(x1 * x1, axis=-1, keepdims=True)
    h2 = x1 * lax.rsqrt(ms + NORM_EPS) * g_ref[...]
    h_hi = h2.astype(BF16)
    _store_row_tiles(h2_ref, _pack_halves(h_hi))
    h_lo = (h2 - h_hi.astype(F32)).astype(BF16)
    r = jnp.dot(h_hi, wrc_ref[...], preferred_element_type=F32)
    lg = (r[:, :ROUTE_LANES] + r[:, ROUTE_LANES:]
          + jnp.dot(h_lo, wrc_ref[:, :ROUTE_LANES], preferred_element_type=F32)) + br_ref[...]

    lane = lax.broadcasted_iota(jnp.int32, lg.shape, 1)
    lane_f = lane.astype(F32)
    big = float(ROUTE_LANES)
    is_group = (lane >= MOE_EXPERTS) & (lane < MOE_EXPERTS + MOE_GROUPS)
    lgg = jnp.where(is_group, lg, -jnp.inf)
    gmax = jnp.max(lgg, axis=-1, keepdims=True)
    gsel = jnp.min(jnp.where(lgg == gmax, lane_f - MOE_EXPERTS, big), axis=-1, keepdims=True)
    pg = 1.0 / jnp.sum(jnp.where(is_group, jnp.exp(lg - gmax), 0.0), axis=-1, keepdims=True)

    in_group = (lane < MOE_EXPERTS) & ((lane // MOE_EXPERTS_PER_GROUP).astype(F32) == gsel)
    le = jnp.where(in_group, lg, -jnp.inf)
    t1 = jnp.max(le, axis=-1, keepdims=True)
    i1 = jnp.min(jnp.where(le == t1, lane_f, big), axis=-1, keepdims=True)
    le2 = jnp.where(lane_f == i1, -jnp.inf, le)
    t2 = jnp.max(le2, axis=-1, keepdims=True)
    i2 = jnp.min(jnp.where(le2 == t2, lane_f, big), axis=-1, keepdims=True)
    e2 = jnp.exp(t2 - t1)
    w1 = pg / (1.0 + e2)
    w2 = pg * e2 / (1.0 + e2)
    rt_ref[...] = jnp.where(lane == 0, i1, jnp.where(lane == 1, i2, jnp.where(lane == 2, w1,
                            jnp.where(lane == 3, w2, 0.0))))


def _outproj(merged, w_out, x2d, gain, wr_cat, br, tm=512):
    T, D = x2d.shape
    row = lambda i: (i, 0)
    const = lambda i: (0, 0)
    return pl.pallas_call(
        _outproj_kernel,
        grid=(T // tm,),
        in_specs=[
            pl.BlockSpec((tm, D), row), pl.BlockSpec((D, D), const), pl.BlockSpec((tm, D), row),
            pl.BlockSpec((1, D), const), pl.BlockSpec((D, 2 * ROUTE_LANES), const),
            pl.BlockSpec((1, ROUTE_LANES), const),
        ],
        out_specs=[pl.BlockSpec((tm, D), row), pl.BlockSpec((tm * ROW_TILE, LANES), row),
                   pl.BlockSpec((tm, ROUTE_LANES), row)],
        out_shape=[jax.ShapeDtypeStruct((T, D), F32), jax.ShapeDtypeStruct((T * ROW_TILE, LANES), jnp.uint32),
                   jax.ShapeDtypeStruct((T, ROUTE_LANES), F32)],
        compiler_params=_cparams(("parallel",)),
        name="outproj_router",
    )(merged, w_out, x2d, gain.reshape(1, D).astype(F32), wr_cat, br)


def _dispatch_kernel(dest_ref, zb_ref, h_ref, xs_ref, zeros, sem, *, tq):
    base = pl.program_id(0) * tq
    block_rows = MOE_ROWS * ROW_TILE
    n_blocks = xs_ref.shape[0] // block_rows

    @pl.when(pl.program_id(0) == 0)
    def _():
        zeros[...] = jnp.zeros_like(zeros)

        def zero_copy(b):
            return pltpu.make_async_copy(zeros, xs_ref.at[pl.ds(pl.multiple_of(b * block_rows, block_rows), block_rows)],
                                         sem)

        def start(b, carry):
            @pl.when(zb_ref[b] != 0)
            def _():
                zero_copy(b).start()
            return carry

        def wait(b, carry):
            @pl.when(zb_ref[b] != 0)
            def _():
                zero_copy(b).wait()
            return carry

        lax.fori_loop(0, n_blocks, start, 0)
        lax.fori_loop(0, n_blocks, wait, 0)

    def token_rows(t):
        return pl.ds(pl.multiple_of(t * ROW_TILE, ROW_TILE), ROW_TILE)

    def copy(j, k):
        return pltpu.make_async_copy(h_ref.at[token_rows(j)], xs_ref.at[token_rows(dest_ref[2 * (base + j) + k])], sem)

    def issue(j, carry):
        copy(j, 0).start()
        copy(j, 1).start()
        return carry

    lax.fori_loop(0, tq, issue, 0)
    for _ in range(MOE_TOPK):
        pltpu.make_async_copy(h_ref, xs_ref.at[pl.ds(0, tq * ROW_TILE)], sem).wait()


def _dispatch(h2, dest_flat, zero_block, n_rows, tq=512):
    T = h2.shape[0] // ROW_TILE
    return pl.pallas_call(
        functools.partial(_dispatch_kernel, tq=tq),
        grid_spec=pltpu.PrefetchScalarGridSpec(
            num_scalar_prefetch=2,
            grid=(T // tq,),
            in_specs=[pl.BlockSpec((tq * ROW_TILE, LANES), lambda i, d, z: (i, 0))],
            out_specs=pl.BlockSpec(memory_space=pl.ANY),
            scratch_shapes=[pltpu.VMEM((MOE_ROWS * ROW_TILE, LANES), h2.dtype), pltpu.SemaphoreType.DMA(())],
        ),
        out_shape=jax.ShapeDtypeStruct((n_rows * ROW_TILE, LANES), h2.dtype),
        compiler_params=_cparams(("arbitrary",)),
        name="moe_dispatch",
    )(dest_flat, zero_block, h2)


MOE_WEIGHT_SLOTS = 3


def _expert_kernel(be_ref, nu_ref, rank_ref, kth_ref, xs_ref, wg_hbm, wu_hbm, wd_hbm, o_ref,
                   wg_f, wu_f, wd_f, wg_s, wu_s, wd_s, sem):
    i = pl.program_id(0)
    e = be_ref[i]
    active = i < nu_ref[0]
    new_expert = active & ((i == 0) | (e != be_ref[jnp.maximum(i - 1, 0)]))
    k = rank_ref[e]
    slot = k % MOE_WEIGHT_SLOTS
    ahead = MOE_WEIGHT_SLOTS - 1

    def weight_copies(expert, s):
        return (pltpu.make_async_copy(wg_hbm.at[expert], wg_f.at[s], sem.at[s, 0]),
                pltpu.make_async_copy(wu_hbm.at[expert], wu_f.at[s], sem.at[s, 1]),
                pltpu.make_async_copy(wd_hbm.at[expert], wd_f.at[s], sem.at[s, 2]))

    def start_kth(j):
        nxt = kth_ref[j]

        @pl.when(nxt >= 0)
        def _():
            for c in weight_copies(nxt, j % MOE_WEIGHT_SLOTS):
                c.start()

    @pl.when(active & (i == 0))
    def _():
        for j in range(ahead):
            start_kth(j)

    @pl.when(new_expert)
    def _():
        start_kth(k + ahead)
        for c in weight_copies(e, slot):
            c.wait()
        wg_s[...] = wg_f[slot].astype(BF16)
        wu_s[...] = wu_f[slot].astype(BF16)
        wd_s[...] = wd_f[slot].astype(BF16)

    @pl.when(active)
    def _():
        x = _unpack_halves(_load_row_tiles(xs_ref, MOE_ROWS)).astype(BF16)
        g = jnp.dot(x, wg_s[...], preferred_element_type=F32)
        u = jnp.dot(x, wu_s[...], preferred_element_type=F32)
        a = (g * jax.nn.sigmoid(g) * u).astype(BF16)
        y = jnp.dot(a, wd_s[...], preferred_element_type=F32)
        _store_row_tiles(o_ref, _pack_halves(y.astype(BF16)))

    @pl.when(jnp.logical_not(active))
    def _():
        o_ref[...] = jnp.zeros_like(o_ref)


def _experts(xs, block_expert, n_used, expert_rank, kth_expert, w_gate, w_up, w_down):
    Dh = xs.shape[1]
    D, F = w_gate.shape[1], w_gate.shape[2]
    assert Dh == LANES and D == 2 * ROW_TILE * LANES
    nblk = block_expert.shape[0]
    P = nblk * MOE_ROWS * ROW_TILE
    rows = lambda i, be, nu, nx, sl: (jnp.minimum(i, nu[0] - 1), 0)
    hbm = pl.BlockSpec(memory_space=pl.ANY)
    return pl.pallas_call(
        _expert_kernel,
        grid_spec=pltpu.PrefetchScalarGridSpec(
            num_scalar_prefetch=4,
            grid=(nblk,),
            in_specs=[pl.BlockSpec((MOE_ROWS * ROW_TILE, Dh), rows), hbm, hbm, hbm],
            out_specs=pl.BlockSpec((MOE_ROWS * ROW_TILE, Dh), lambda i, be, nu, nx, sl: (i, 0)),
            scratch_shapes=[pltpu.VMEM((MOE_WEIGHT_SLOTS, D, F), F32), pltpu.VMEM((MOE_WEIGHT_SLOTS, D, F), F32),
                            pltpu.VMEM((MOE_WEIGHT_SLOTS, F, D), F32),
                            pltpu.VMEM((D, F), BF16), pltpu.VMEM((D, F), BF16), pltpu.VMEM((F, D), BF16),
                            pltpu.SemaphoreType.DMA((MOE_WEIGHT_SLOTS, 3))],
        ),
        out_shape=jax.ShapeDtypeStruct((P, Dh), jnp.uint32),
        compiler_params=_cparams(("arbitrary",)),
        name="moe_experts",
    )(block_expert, n_used, expert_rank, kth_expert, xs, w_gate, w_up, w_down)


def _combine_kernel(dest_ref, x1_ref, rt_ref, g_ref, ys_ref, o_ref, buf, sem, *, tq):
    i = pl.program_id(0)
    n = pl.num_programs(0)

    def token_rows(t):
        return pl.ds(pl.multiple_of(t * ROW_TILE, ROW_TILE), ROW_TILE)

    def issue(step, s):
        def body(j, carry):
            for k in range(MOE_TOPK):
                pltpu.make_async_copy(ys_ref.at[token_rows(dest_ref[2 * (step * tq + j) + k])],
                                      buf.at[s, k, token_rows(j)], sem.at[s]).start()
            return carry
        lax.fori_loop(0, tq, body, 0)

    @pl.when(i == 0)
    def _():
        issue(0, 0)

    for s in range(2):
        @pl.when((i + 1 < n) & ((i + 1) % 2 == s))
        def _():
            issue(i + 1, s)

    for s in range(2):
        @pl.when(i % 2 == s)
        def _():
            for k in range(MOE_TOPK):
                pltpu.make_async_copy(ys_ref.at[pl.ds(0, tq * ROW_TILE)], buf.at[s, k], sem.at[s]).wait()
            rt = rt_ref[...]
            y0 = _unpack_halves(_load_row_tiles(buf.at[s, 0], tq))
            y1 = _unpack_halves(_load_row_tiles(buf.at[s, 1], tq))
            x = x1_ref[...] + rt[:, 2:3] * y0 + rt[:, 3:4] * y1
            ms = jnp.mean(x * x, axis=-1, keepdims=True)
            o_ref[...] = x * lax.rsqrt(ms + NORM_EPS) * g_ref[...]


def _combine(x1, route, gain, ys, dest_flat, tq=256):
    T, D = x1.shape
    return pl.pallas_call(
        functools.partial(_combine_kernel, tq=tq),
        grid_spec=pltpu.PrefetchScalarGridSpec(
            num_scalar_prefetch=1,
            grid=(T // tq,),
            in_specs=[
                pl.BlockSpec((tq, D), lambda i, d: (i, 0)),
                pl.BlockSpec((tq, ROUTE_LANES), lambda i, d: (i, 0)),
                pl.BlockSpec((1, D), lambda i, d: (0, 0)),
                pl.BlockSpec(memory_space=pl.ANY),
            ],
            out_specs=pl.BlockSpec((tq, D), lambda i, d: (i, 0)),
            scratch_shapes=[pltpu.VMEM((2, MOE_TOPK, tq * ROW_TILE, LANES), jnp.uint32),
                            pltpu.SemaphoreType.DMA((2,))],
        ),
        out_shape=jax.ShapeDtypeStruct((T, D), F32),
        compiler_params=_cparams(("arbitrary",)),
        name="moe_combine",
    )(dest_flat, x1, route, gain.reshape(1, D).astype(F32), ys)


def _route_metadata(route, T):
    e = route[:, :MOE_TOPK].astype(jnp.int32)
    ids = jnp.arange(MOE_EXPERTS, dtype=jnp.int32)
    oh0 = e[:, 0, None] == ids
    oh1 = e[:, 1, None] == ids
    onehot = (oh0 | oh1).astype(jnp.int32)
    incl = jnp.cumsum(onehot, axis=0)
    counts = incl[-1]
    pcounts = (counts + MOE_ROWS - 1) // MOE_ROWS * MOE_ROWS
    pends = jnp.cumsum(pcounts)
    poffs = pends - pcounts
    slot = incl - onehot + poffs[None, :]
    dest = jnp.stack([jnp.sum(jnp.where(oh0, slot, 0), axis=1), jnp.sum(jnp.where(oh1, slot, 0), axis=1)], axis=1)
    nblk = (T * MOE_TOPK) // MOE_ROWS + MOE_EXPERTS
    block_expert = jnp.minimum(
        jnp.searchsorted(pends, jnp.arange(nblk, dtype=jnp.int32) * MOE_ROWS, side="right"),
        MOE_EXPERTS - 1).astype(jnp.int32)
    n_used = (pends[-1:] // MOE_ROWS).astype(jnp.int32)
    used = counts > 0
    expert_rank = (jnp.cumsum(used.astype(jnp.int32)) - 1).astype(jnp.int32)
    order = jnp.argsort(jnp.where(used, ids, ids + MOE_EXPERTS)).astype(jnp.int32)
    kth = jnp.where(ids < jnp.sum(used), order, -1)
    kth_expert = jnp.concatenate([kth, jnp.full((MOE_WEIGHT_SLOTS,), -1, jnp.int32)])
    blk = jnp.arange(nblk, dtype=jnp.int32)
    following = jnp.concatenate([block_expert[1:], block_expert[-1:]])
    zero_block = ((blk >= n_used[0] - 1) | (block_expert != following)).astype(jnp.int32)
    return dest.reshape(-1).astype(jnp.int32), zero_block, block_expert, n_used, expert_rank, kth_expert, nblk


def kernel(x, norm1_gain, w_in, hg_lb_logits, hg_norm_gain, rel_bias, w_branch_a, w_branch_b, w_out,
           norm2_gain, w_router_group, b_router_group, w_router_expert, b_router_expert,
           w_exp_gate, w_exp_up, w_exp_down, final_norm_gain):
    B, S, D = x.shape
    T = B * S
    depth = norm1_gain.shape[0]
    lower_bounds = jnp.cumsum(jax.nn.softmax(hg_lb_logits.astype(F32), axis=0), axis=0)
    att_w = ATT_HEADS * HEAD_DIM
    bias = _attn_bias(rel_bias)
    x2d = x.reshape(T, D)
    for layer in range(depth):
        zeros_lb = jnp.zeros((1, 3 * att_w), F32)
        hm = dict(head_major=True, B=B, S=S)
        h, q_a = _normproj(x2d, norm1_gain[layer], w_in, layer, D, B=B, S=S)
        lf_a = _inproj(h, w_in, layer, D, D, lower_bounds[layer].reshape(1, D), mode="logf", out_dtype=F32, **hm)
        i_a = _inproj(h, w_in, layer, 2 * D, D, zeros_lb, mode="none", out_dtype=BF16, **hm)
        sg_a = _inproj(h, w_in, layer, 3 * D, D, zeros_lb, mode="sigmoid", out_dtype=BF16, **hm)
        qkv_b = _inproj(h, w_in, layer, 4 * D, 3 * att_w, zeros_lb, mode="none", out_dtype=BF16, tn=att_w, **hm)
        gates = _inproj(h, w_in, layer, 4 * D + 3 * att_w, 2 * D, zeros_lb, mode="sigmoid",
                        head_major=False, out_dtype=BF16, B=B, S=S)

        y_a = _hgrn(q_a, lf_a, i_a, sg_a, hg_norm_gain[layer]).reshape(T, D)
        y_b = _attention(qkv_b, bias).reshape(T, ATT_HEADS_PER_GROUP * HEAD_DIM)
        merged = _merge(y_a, y_b, w_branch_a[layer].astype(BF16), w_branch_b[layer].astype(BF16), gates)

        wr = jnp.zeros((D, ROUTE_LANES), F32)
        wr = wr.at[:, :MOE_EXPERTS].set(w_router_expert[layer].astype(F32))
        wr = wr.at[:, MOE_EXPERTS:MOE_EXPERTS + MOE_GROUPS].set(w_router_group[layer].astype(F32))
        br = jnp.zeros((1, ROUTE_LANES), F32)
        br = br.at[0, :MOE_EXPERTS].set(b_router_expert[layer].astype(F32))
        br = br.at[0, MOE_EXPERTS:MOE_EXPERTS + MOE_GROUPS].set(b_router_group[layer].astype(F32))
        wr_hi = wr.astype(BF16)
        wr_lo = (wr - wr_hi.astype(F32)).astype(BF16)
        x1, h2, route = _outproj(merged, w_out[layer].astype(BF16), x2d, norm2_gain[layer],
                                 jnp.concatenate([wr_hi, wr_lo], axis=1), br)

        dest, zero_block, block_expert, n_used, expert_rank, kth_expert, nblk = _route_metadata(route, T)
        xs = _dispatch(h2, dest, zero_block, nblk * MOE_ROWS)
        ys = _experts(xs, block_expert, n_used, expert_rank, kth_expert,
                      w_exp_gate[layer], w_exp_up[layer], w_exp_down[layer])
        last = layer == depth - 1
        assert last, "the fused combine applies the final norm; deeper stacks need an un-normalised combine"
        x2d = _combine(x1, route, final_norm_gain, ys, dest)
    return x2d.reshape(B, S, D)
```

```python
import functools
import math

import numpy as np
import jax
import jax.numpy as jnp
from jax import lax
from jax.experimental import pallas as pl
from jax.experimental.pallas import tpu as pltpu

F32 = jnp.float32
BF16 = jnp.bfloat16

LANES = 128
NORM_EPS = 1e-6
HEAD_DIM = 128
HG_HEADS = 16
ATT_GROUPS = ((128, 1), (512, 4), (2048, 16))
ATT_HEADS_PER_GROUP = 4
ATT_HEADS = len(ATT_GROUPS) * ATT_HEADS_PER_GROUP
ATT_BLOCK = 128
ATT_STACK = 16
REL_BUCKETS = 32
REL_MAX_DIST = 2048
MOE_GROUPS = 8
MOE_EXPERTS_PER_GROUP = 8
MOE_EXPERTS = MOE_GROUPS * MOE_EXPERTS_PER_GROUP
MOE_TOPK = 2
MOE_ROWS = 256
NEG_BIG = -1e30
VMEM_LIMIT = 56 * 1024 * 1024


def _cparams(sem):
    return pltpu.CompilerParams(dimension_semantics=sem, vmem_limit_bytes=VMEM_LIMIT)


def _normproj_kernel(x_ref, g_ref, w_ref, h_ref, o_ref, w_bf):
    @pl.when(pl.program_id(0) == 0)
    def _():
        w_bf[...] = w_ref[...].astype(BF16)

    x = x_ref[...]
    ms = jnp.mean(x * x, axis=-1, keepdims=True)
    h = (x * lax.rsqrt(ms + NORM_EPS) * g_ref[...]).astype(h_ref.dtype)
    h_ref[...] = h
    acc = jnp.dot(h, w_bf[...], preferred_element_type=F32)
    for hh in range(acc.shape[1] // HEAD_DIM):
        o_ref[0, hh] = acc[:, hh * HEAD_DIM:(hh + 1) * HEAD_DIM].astype(o_ref.dtype)


def _normproj(x2d, gain, w_all, layer, N, *, B, S, tm=512):
    T, D = x2d.shape
    assert S % tm == 0
    spb = S // tm
    return pl.pallas_call(
        _normproj_kernel,
        grid=(T // tm,),
        in_specs=[
            pl.BlockSpec((tm, D), lambda i: (i, 0)),
            pl.BlockSpec((1, D), lambda i: (0, 0)),
            pl.BlockSpec((None, D, N), lambda i: (layer, 0, 0), pipeline_mode=pl.Buffered(1)),
        ],
        out_specs=[
            pl.BlockSpec((tm, D), lambda i: (i, 0)),
            pl.BlockSpec((1, N // HEAD_DIM, tm, HEAD_DIM), lambda i: (i // spb, 0, i % spb, 0)),
        ],
        out_shape=[jax.ShapeDtypeStruct((T, D), BF16),
                   jax.ShapeDtypeStruct((B, N // HEAD_DIM, S, HEAD_DIM), BF16)],
        scratch_shapes=[pltpu.VMEM((D, N), BF16)],
        compiler_params=_cparams(("arbitrary",)),
        name="norm_inproj",
    )(x2d, gain.reshape(1, D).astype(F32), w_all)


def _inproj_kernel(h_ref, w_ref, lb_ref, o_ref, w_bf, *, mode, head_major):
    @pl.when(pl.program_id(1) == 0)
    def _():
        w_bf[...] = w_ref[0].astype(BF16)

    acc = jnp.dot(h_ref[...], w_bf[...], preferred_element_type=F32)
    if mode == "forget":
        lb = lb_ref[...]
        acc = lb + (1.0 - lb) * jax.nn.sigmoid(acc)
    elif mode == "sigmoid":
        acc = jax.nn.sigmoid(acc)
    if head_major:
        for hh in range(acc.shape[1] // HEAD_DIM):
            o_ref[0, hh] = acc[:, hh * HEAD_DIM:(hh + 1) * HEAD_DIM].astype(o_ref.dtype)
    else:
        o_ref[...] = acc.astype(o_ref.dtype)


def _inproj(h, w_all, layer, col0, N, lb, *, mode, head_major, out_dtype, B, S, tm=1024, tn=1024):
    T, D = h.shape
    assert N % tn == 0 and col0 % LANES == 0 and T % tm == 0 and S % tm == 0
    spb = S // tm
    if head_major:
        out_shape = jax.ShapeDtypeStruct((B, N // HEAD_DIM, S, HEAD_DIM), out_dtype)
        out_spec = pl.BlockSpec((1, tn // HEAD_DIM, tm, HEAD_DIM), lambda j, i: (i // spb, j, i % spb, 0))
    else:
        out_shape = jax.ShapeDtypeStruct((T, N), out_dtype)
        out_spec = pl.BlockSpec((tm, tn), lambda j, i: (i, j))
    return pl.pallas_call(
        functools.partial(_inproj_kernel, mode=mode, head_major=head_major),
        grid=(N // tn, T // tm),
        in_specs=[
            pl.BlockSpec((tm, D), lambda j, i: (i, 0)),
            pl.BlockSpec((pl.Element(1), pl.Element(D), pl.Element(tn)), lambda j, i: (layer, 0, pl.multiple_of(col0 + j * tn, LANES))),
            pl.BlockSpec((1, tn), lambda j, i: (0, j)),
        ],
        out_specs=out_spec,
        out_shape=out_shape,
        scratch_shapes=[pltpu.VMEM((D, tn), BF16)],
        compiler_params=_cparams(("parallel", "arbitrary")),
        name="inproj_" + mode,
    )(h, w_all, lb)


HG_CHUNK = 256
HG_DIAG = 16
HG_HEADS_PER_STEP = 4


def _hgrn_levels(C):
    out, m = [], C // 2
    while m >= HG_DIAG:
        out.append(m)
        m //= 2
    return out


def _hgrn_masks(C):
    t = np.arange(C)[:, None]
    s = np.arange(C)[None, :]
    masks = []
    for m in _hgrn_levels(C):
        masks.append((t // (2 * m) == s // (2 * m)) & ((t // m) % 2 == 1) & ((s // m) % 2 == 0))
    masks.append((t // HG_DIAG == s // HG_DIAG) & (t >= s))
    total = np.sum(np.stack(masks).astype(np.int32), axis=0)
    assert np.array_equal(total, (t >= s).astype(np.int32))
    return np.stack(masks).astype(np.float32)


def _dot_nt(a, b):
    return lax.dot_general(a, b, (((1,), (1,)), ((), ())), preferred_element_type=F32)


def _dot_tn(a, b):
    return lax.dot_general(a, b, (((0,), (0,)), ((), ())), preferred_element_type=F32)


def _hgrn_kernel(q_ref, f_ref, v_ref, sg_ref, gain_ref, tril_ref, mask_ref, o_ref, st_ref, b_ref, *, C):
    S = q_ref.shape[2]
    HP = q_ref.shape[1]
    levels = _hgrn_levels(C)
    nchunks = S // C
    tril = tril_ref[...]

    def ref_rows(h, block, row_of_block):
        parts = [jnp.broadcast_to(b_ref[h, pl.ds(row_of_block(p), 1), :], (block, HEAD_DIM))
                 for p in range(C // block)]
        return jnp.concatenate(parts, axis=0)

    def cumsum(c):
        r0 = pl.multiple_of(c * C, C)
        out = []
        for h in range(HP):
            lf2 = jnp.log2(f_ref[0, h, pl.ds(r0, C), :])
            hi = lf2.astype(BF16)
            lo = (lf2 - hi.astype(F32)).astype(BF16)
            r = jnp.dot(tril, jnp.concatenate([hi, lo], axis=1), preferred_element_type=F32)
            out.append(r[:, :HEAD_DIM] + r[:, HEAD_DIM:])
        return tuple(out)

    st_ref[...] = jnp.zeros_like(st_ref)

    def chunk(c, b_all):
        r0 = pl.multiple_of(c * C, C)
        rows = pl.ds(r0, C)
        b_next = cumsum(jnp.minimum(c + 1, nchunks - 1))
        for h in range(HP):
            b = b_all[h]
            b_ref[h] = b
            q = q_ref[0, h, rows, :].astype(F32)
            kk = 1.0 - f_ref[0, h, rows, :]
            b_last = b_ref[h, pl.ds(C - 1, 1), :]
            scores = jnp.zeros((C, C), BF16)
            sign_bit = jnp.uint32(0x80000000)
            for li, m in enumerate(levels):
                d = b - ref_rows(h, 2 * m, lambda p, m=m: 2 * m * p + m - 1)
                e = jnp.exp2(pltpu.bitcast(pltpu.bitcast(d, jnp.uint32) | sign_bit, F32))
                s_l = _dot_nt((q * e).astype(BF16), (kk * e).astype(BF16)).astype(BF16)
                scores = scores + mask_ref[li] * s_l
            d = b - ref_rows(h, HG_DIAG, lambda p: HG_DIAG * p + HG_DIAG // 2 - 1)
            s_l = _dot_nt((q * jnp.exp2(d)).astype(BF16), (kk * jnp.exp2(-d)).astype(BF16)).astype(BF16)
            scores = scores + mask_ref[len(levels)] * s_l
            v = v_ref[0, h, rows, :]
            st = st_ref[h]
            qe = (q * jnp.exp2(b)).astype(BF16)
            kd = (kk * jnp.exp2(b_last - b)).astype(BF16)
            o = jnp.dot(scores, v, preferred_element_type=F32) + _dot_nt(qe, st.astype(BF16))
            st_ref[h] = st * jnp.exp2(b_last) + _dot_tn(v, kd)
            ms = jnp.mean(o * o, axis=-1, keepdims=True)
            y = o * lax.rsqrt(ms + NORM_EPS) * gain_ref[:, h * HEAD_DIM:(h + 1) * HEAD_DIM]
            y = y * sg_ref[0, h, rows, :].astype(F32)
            o_ref[0, rows, h * HEAD_DIM:(h + 1) * HEAD_DIM] = y.astype(o_ref.dtype)
        return b_next

    lax.fori_loop(0, nchunks, chunk, cumsum(0))


def _hgrn(q, lf, v, sg, gain):
    B, H, S, E = q.shape
    C, HP = HG_CHUNK, HG_HEADS_PER_STEP
    masks = jnp.asarray(_hgrn_masks(C), dtype=BF16)
    tril = jnp.asarray(np.tril(np.ones((C, C), np.float32)), dtype=BF16)
    head_spec = pl.BlockSpec((1, HP, S, E), lambda b, h: (b, h, 0, 0))
    return pl.pallas_call(
        functools.partial(_hgrn_kernel, C=C),
        grid=(B, H // HP),
        in_specs=[
            head_spec, head_spec, head_spec, head_spec,
            pl.BlockSpec((1, HP * E), lambda b, h: (0, h)),
            pl.BlockSpec((C, C), lambda b, h: (0, 0)),
            pl.BlockSpec(masks.shape, lambda b, h: (0, 0, 0)),
        ],
        out_specs=pl.BlockSpec((1, S, HP * E), lambda b, h: (b, 0, h)),
        out_shape=jax.ShapeDtypeStruct((B, S, H * E), BF16),
        scratch_shapes=[pltpu.VMEM((HP, E, E), F32), pltpu.VMEM((HP, C, E), F32)],
        compiler_params=_cparams(("parallel", "parallel")),
        name="hgrn2",
    )(q, lf, v, sg, gain.reshape(1, H * E).astype(F32), tril, masks)


def _t5_bucket_np(dist):
    exact = REL_BUCKETS // 2
    d_f = np.maximum(dist, 1).astype(np.float32)
    log_b = exact + (np.log(d_f / np.float32(exact)) / np.float32(math.log(REL_MAX_DIST / exact))
                     * np.float32(REL_BUCKETS - exact)).astype(np.int32)
    return np.where(dist < exact, dist, np.minimum(log_b, REL_BUCKETS - 1))


def _attn_bias(rel_bias):
    blk = ATT_BLOCK
    period = 3 * blk
    out = []
    for gi, (window, dilation) in enumerate(ATT_GROUPS):
        n_back = window // dilation
        assert n_back <= blk
        hs = slice(gi * ATT_HEADS_PER_GROUP, (gi + 1) * ATT_HEADS_PER_GROUP)
        bucket = _t5_bucket_np(np.arange(n_back + 1) * dilation)
        by_delta = rel_bias[:, hs][bucket].astype(F32).T
        u = jnp.full((ATT_HEADS_PER_GROUP, period), NEG_BIG, F32)
        u = u.at[:, 2 * blk - 1 - n_back:2 * blk].set(by_delta[:, ::-1])
        flat = jnp.tile(u, (1, blk))[:, :blk * (period - 1)]
        out.append(flat.reshape(ATT_HEADS_PER_GROUP, blk, period - 1)[:, :, blk - 1:3 * blk - 1])
    return jnp.stack(out, axis=0)


def _attn_kernel(q_ref, k_ref, v_ref, bias_ref, o_ref, qf, kf, vf, og, lg):
    S = q_ref.shape[3]
    scale = HEAD_DIM ** -0.5
    blk = ATT_BLOCK

    for g, (window, d) in enumerate(ATT_GROUPS):
        L = S // d
        nb = L // blk
        if d > 1:
            qf[...] = q_ref[0, g, 0].astype(F32)
            kf[...] = k_ref[0, g, 0].astype(F32)
            vf[...] = v_ref[0, g, 0].astype(F32)

        def load(ref_bf, ref_f32, start, size, g=g, d=d):
            if d == 1:
                return ref_bf[0, g, 0, pl.ds(start, size), :]
            return ref_f32[pl.ds(start, size, stride=d), :].astype(BF16)

        def scores(r, n, g=g, d=d, load=load):
            start = r + n * blk * d
            q = load(q_ref, qf, start, blk)
            if n == 0:
                k = load(k_ref, kf, start, blk)
                v = load(v_ref, vf, start, blk)
                s = _dot_nt(q, k) * scale + bias_ref[g, 0, :, blk:]
            else:
                first = start - blk * d
                k = load(k_ref, kf, first, 2 * blk)
                v = load(v_ref, vf, first, 2 * blk)
                s = _dot_nt(q, k) * scale + bias_ref[g, 0]
            return start, s, v

        def softmax_pv(items, g=g, d=d):
            s = jnp.concatenate([it[1] for it in items], axis=0)
            m = jnp.max(s, axis=-1, keepdims=True)
            p = jnp.exp(s - m)
            den = jnp.sum(p, axis=-1, keepdims=True)
            pb = p.astype(BF16)
            lse = jnp.broadcast_to(m + jnp.log(den), (s.shape[0], HEAD_DIM))
            for j, (start, _, v) in enumerate(items):
                sl = slice(j * blk, (j + 1) * blk)
                rows = pl.ds(start, blk, stride=d) if d > 1 else pl.ds(start, blk)
                og[g, rows, :] = jnp.dot(pb[sl], v, preferred_element_type=F32) / den[sl]
                lg[g, rows, :] = lse[sl]

        blocks = [(r, n) for r in range(d) for n in range(nb)]
        first_blocks = [bn for bn in blocks if bn[1] == 0]
        later_blocks = [bn for bn in blocks if bn[1] > 0]
        for group in (first_blocks, later_blocks):
            for j0 in range(0, len(group), ATT_STACK):
                softmax_pv([scores(r, n) for r, n in group[j0:j0 + ATT_STACK]])

    rows_per_step = 256

    def mix(i, carry):
        rows = pl.ds(pl.multiple_of(i * rows_per_step, rows_per_step), rows_per_step)
        l0, l1, l2 = lg[0, rows, :], lg[1, rows, :], lg[2, rows, :]
        mx = jnp.maximum(jnp.maximum(l0, l1), l2)
        e0, e1, e2 = jnp.exp(l0 - mx), jnp.exp(l1 - mx), jnp.exp(l2 - mx)
        num = e0 * og[0, rows, :] + e1 * og[1, rows, :] + e2 * og[2, rows, :]
        o_ref[0, rows, :] = (num / (e0 + e1 + e2)).astype(o_ref.dtype)
        return carry

    lax.fori_loop(0, S // rows_per_step, mix, 0)


def _attention(qkv, bias):
    B, _, S, E = qkv.shape
    G, HG = len(ATT_GROUPS), ATT_HEADS_PER_GROUP
    x = qkv.reshape(B, 3, G, HG, S, E)

    def spec(which):
        return pl.BlockSpec((1, None, G, 1, S, E), lambda b, h, which=which: (b, which, 0, h, 0, 0))

    def kern(q_ref, k_ref, v_ref, bias_ref, o_ref, *scratch):
        _attn_kernel(q_ref, k_ref, v_ref, bias_ref, o_ref, *scratch)

    return pl.pallas_call(
        kern,
        grid=(B, HG),
        in_specs=[spec(0), spec(1), spec(2),
                  pl.BlockSpec((G, 1, ATT_BLOCK, 2 * ATT_BLOCK), lambda b, h: (0, h, 0, 0))],
        out_specs=pl.BlockSpec((1, S, E), lambda b, h: (b, 0, h)),
        out_shape=jax.ShapeDtypeStruct((B, S, HG * E), BF16),
        scratch_shapes=[pltpu.VMEM((S, E), F32)] * 3 + [pltpu.VMEM((G, S, E), F32)] * 2,
        compiler_params=_cparams(("parallel", "parallel")),
        name="dilated_attn",
    )(x, x, x, bias)


def _merge_kernel(ya_ref, yb_ref, wa_ref, wb_ref, ga_ref, gb_ref, o_ref, wa_bf, wb_bf):
    @pl.when(pl.program_id(1) == 0)
    def _():
        wa_bf[...] = wa_ref[...].astype(BF16)
        wb_bf[...] = wb_ref[...].astype(BF16)

    a = jnp.dot(ya_ref[...], wa_bf[...], preferred_element_type=F32)
    b = jnp.dot(yb_ref[...], wb_bf[...], preferred_element_type=F32)
    o_ref[...] = (ga_ref[...].astype(F32) * a + gb_ref[...].astype(F32) * b).astype(o_ref.dtype)


def _merge(ya, yb, wa, wb, gates, tm=1024, tn=1024):
    T, Ka = ya.shape
    Kb = yb.shape[1]
    N = wa.shape[1]
    nj = N // tn
    return pl.pallas_call(
        _merge_kernel,
        grid=(nj, T // tm),
        in_specs=[
            pl.BlockSpec((tm, Ka), lambda j, i: (i, 0)),
            pl.BlockSpec((tm, Kb), lambda j, i: (i, 0)),
            pl.BlockSpec((Ka, tn), lambda j, i: (0, j)),
            pl.BlockSpec((Kb, tn), lambda j, i: (0, j)),
            pl.BlockSpec((tm, tn), lambda j, i: (i, j)),
            pl.BlockSpec((tm, tn), lambda j, i, nj=nj: (i, j + nj)),
        ],
        out_specs=pl.BlockSpec((tm, tn), lambda j, i: (i, j)),
        out_shape=jax.ShapeDtypeStruct((T, N), BF16),
        scratch_shapes=[pltpu.VMEM((Ka, tn), BF16), pltpu.VMEM((Kb, tn), BF16)],
        compiler_params=_cparams(("parallel", "arbitrary")),
        name="branch_merge",
    )(ya, yb, wa, wb, gates, gates)


ROUTE_LANES = LANES


def _pack_halves(x_bf):
    bits = pltpu.bitcast(x_bf.astype(F32), jnp.uint32)
    n = bits.shape[1] // 2
    return (bits[:, :n] >> 16) | (bits[:, n:] & jnp.uint32(0xFFFF0000))


def _unpack_halves(p):
    lo = pltpu.bitcast(p << 16, F32)
    hi = pltpu.bitcast(p & jnp.uint32(0xFFFF0000), F32)
    return jnp.concatenate([lo, hi], axis=1)


ROW_TILE = 8


def _store_row_tiles(ref, packed):
    m = packed.shape[0]
    assert packed.shape[1] == ROW_TILE * LANES
    for c in range(ROW_TILE):
        ref[pl.ds(c, m, stride=ROW_TILE), :] = packed[:, c * LANES:(c + 1) * LANES]


def _load_row_tiles(ref, m):
    return jnp.concatenate([ref[pl.ds(c, m, stride=ROW_TILE), :] for c in range(ROW_TILE)], axis=1)


def _outproj_kernel(m_ref, w_ref, x_ref, g_ref, wrc_ref, br_ref, x1_ref, h2_ref, rt_ref):
    x1 = x_ref[...] + jnp.dot(m_ref[...], w_ref[...], preferred_element_type=F32)
    x1_ref[...] = x1
    ms = jnp.mean(x1 * x1, axis=-1, keepdims=True)
    h2 = x1 * lax.rsqrt(ms + NORM_EPS) * g_ref[...]
    h_hi = h2.astype(BF16)
    _store_row_tiles(h2_ref, _pack_halves(h_hi))
    h_lo = (h2 - h_hi.astype(F32)).astype(BF16)
    r = jnp.dot(h_hi, wrc_ref[...], preferred_element_type=F32)
    lg = (r[:, :ROUTE_LANES] + r[:, ROUTE_LANES:]
          + jnp.dot(h_lo, wrc_ref[:, :ROUTE_LANES], preferred_element_type=F32)) + br_ref[...]

    lane = lax.broadcasted_iota(jnp.int32, lg.shape, 1)
    lane_f = lane.astype(F32)
    big = float(ROUTE_LANES)
    is_group = (lane >= MOE_EXPERTS) & (lane < MOE_EXPERTS + MOE_GROUPS)
    lgg = jnp.where(is_group, lg, -jnp.inf)
    gmax = jnp.max(lgg, axis=-1, keepdims=True)
    gsel = jnp.min(jnp.where(lgg == gmax, lane_f - MOE_EXPERTS, big), axis=-1, keepdims=True)
    pg = 1.0 / jnp.sum(jnp.where(is_group, jnp.exp(lg - gmax), 0.0), axis=-1, keepdims=True)

    in_group = (lane < MOE_EXPERTS) & ((lane // MOE_EXPERTS_PER_GROUP).astype(F32) == gsel)
    le = jnp.where(in_group, lg, -jnp.inf)
    t1 = jnp.max(le, axis=-1, keepdims=True)
    i1 = jnp.min(jnp.where(le == t1, lane_f, big), axis=-1, keepdims=True)
    le2 = jnp.where(lane_f == i1, -jnp.inf, le)
    t2 = jnp.max(le2, axis=-1, keepdims=True)
    i2 = jnp.min(jnp.where(le2 == t2, lane_f, big), axis=-1, keepdims=True)
    e2 = jnp.exp(t2 - t1)
    w1 = pg / (1.0 + e2)
    w2 = pg * e2 / (1.0 + e2)
    rt_ref[...] = jnp.where(lane == 0, i1, jnp.where(lane == 1, i2, jnp.where(lane == 2, w1,
                            jnp.where(lane == 3, w2, 0.0))))


def _outproj(merged, w_out, x2d, gain, wr_cat, br, tm=512):
    T, D = x2d.shape
    row = lambda i: (i, 0)
    const = lambda i: (0, 0)
    return pl.pallas_call(
        _outproj_kernel,
        grid=(T // tm,),
        in_specs=[
            pl.BlockSpec((tm, D), row), pl.BlockSpec((D, D), const), pl.BlockSpec((tm, D), row),
            pl.BlockSpec((1, D), const), pl.BlockSpec((D, 2 * ROUTE_LANES), const),
            pl.BlockSpec((1, ROUTE_LANES), const),
        ],
        out_specs=[pl.BlockSpec((tm, D), row), pl.BlockSpec((tm * ROW_TILE, LANES), row),
                   pl.BlockSpec((tm, ROUTE_LANES), row)],
        out_shape=[jax.ShapeDtypeStruct((T, D), F32), jax.ShapeDtypeStruct((T * ROW_TILE, LANES), jnp.uint32),
                   jax.ShapeDtypeStruct((T, ROUTE_LANES), F32)],
        compiler_params=_cparams(("parallel",)),
        name="outproj_router",
    )(merged, w_out, x2d, gain.reshape(1, D).astype(F32), wr_cat, br)


def _dispatch_kernel(dest_ref, zb_ref, h_ref, xs_ref, zeros, sem, *, tq):
    base = pl.program_id(0) * tq
    block_rows = MOE_ROWS * ROW_TILE
    n_blocks = xs_ref.shape[0] // block_rows

    @pl.when(pl.program_id(0) == 0)
    def _():
        zeros[...] = jnp.zeros_like(zeros)

        def zero_copy(b):
            return pltpu.make_async_copy(zeros, xs_ref.at[pl.ds(pl.multiple_of(b * block_rows, block_rows), block_rows)],
                                         sem)

        def start(b, carry):
            @pl.when(zb_ref[b] != 0)
            def _():
                zero_copy(b).start()
            return carry

        def wait(b, carry):
            @pl.when(zb_ref[b] != 0)
            def _():
                zero_copy(b).wait()
            return carry

        lax.fori_loop(0, n_blocks, start, 0)
        lax.fori_loop(0, n_blocks, wait, 0)

    def token_rows(t):
        return pl.ds(pl.multiple_of(t * ROW_TILE, ROW_TILE), ROW_TILE)

    def copy(j, k):
        return pltpu.make_async_copy(h_ref.at[token_rows(j)], xs_ref.at[token_rows(dest_ref[2 * (base + j) + k])], sem)

    def issue(j, carry):
        copy(j, 0).start()
        copy(j, 1).start()
        return carry

    lax.fori_loop(0, tq, issue, 0)
    for _ in range(MOE_TOPK):
        pltpu.make_async_copy(h_ref, xs_ref.at[pl.ds(0, tq * ROW_TILE)], sem).wait()


def _dispatch(h2, dest_flat, zero_block, n_rows, tq=512):
    T = h2.shape[0] // ROW_TILE
    return pl.pallas_call(
        functools.partial(_dispatch_kernel, tq=tq),
        grid_spec=pltpu.PrefetchScalarGridSpec(
            num_scalar_prefetch=2,
            grid=(T // tq,),
            in_specs=[pl.BlockSpec((tq * ROW_TILE, LANES), lambda i, d, z: (i, 0))],
            out_specs=pl.BlockSpec(memory_space=pl.ANY),
            scratch_shapes=[pltpu.VMEM((MOE_ROWS * ROW_TILE, LANES), h2.dtype), pltpu.SemaphoreType.DMA(())],
        ),
        out_shape=jax.ShapeDtypeStruct((n_rows * ROW_TILE, LANES), h2.dtype),
        compiler_params=_cparams(("arbitrary",)),
        name="moe_dispatch",
    )(dest_flat, zero_block, h2)


MOE_WEIGHT_SLOTS = 3


def _expert_kernel(be_ref, nu_ref, rank_ref, kth_ref, xs_ref, wg_hbm, wu_hbm, wd_hbm, o_ref,
                   wg_f, wu_f, wd_f, wg_s, wu_s, wd_s, sem):
    i = pl.program_id(0)
    e = be_ref[i]
    active = i < nu_ref[0]
    new_expert = active & ((i == 0) | (e != be_ref[jnp.maximum(i - 1, 0)]))
    k = rank_ref[e]
    slot = k % MOE_WEIGHT_SLOTS
    ahead = MOE_WEIGHT_SLOTS - 1

    def weight_copies(expert, s):
        return (pltpu.make_async_copy(wg_hbm.at[expert], wg_f.at[s], sem.at[s, 0]),
                pltpu.make_async_copy(wu_hbm.at[expert], wu_f.at[s], sem.at[s, 1]),
                pltpu.make_async_copy(wd_hbm.at[expert], wd_f.at[s], sem.at[s, 2]))

    def start_kth(j):
        nxt = kth_ref[j]

        @pl.when(nxt >= 0)
        def _():
            for c in weight_copies(nxt, j % MOE_WEIGHT_SLOTS):
                c.start()

    @pl.when(active & (i == 0))
    def _():
        for j in range(ahead):
            start_kth(j)

    @pl.when(new_expert)
    def _():
        start_kth(k + ahead)
        for c in weight_copies(e, slot):
            c.wait()
        wg_s[...] = wg_f[slot].astype(BF16)
        wu_s[...] = wu_f[slot].astype(BF16)
        wd_s[...] = wd_f[slot].astype(BF16)

    @pl.when(active)
    def _():
        x = _unpack_halves(_load_row_tiles(xs_ref, MOE_ROWS)).astype(BF16)
        g = jnp.dot(x, wg_s[...], preferred_element_type=F32)
        u = jnp.dot(x, wu_s[...], preferred_element_type=F32)
        a = (g * jax.nn.sigmoid(g) * u).astype(BF16)
        y = jnp.dot(a, wd_s[...], preferred_element_type=F32)
        _store_row_tiles(o_ref, _pack_halves(y.astype(BF16)))

    @pl.when(jnp.logical_not(active))
    def _():
        o_ref[...] = jnp.zeros_like(o_ref)


def _experts(xs, block_expert, n_used, expert_rank, kth_expert, w_gate, w_up, w_down):
    Dh = xs.shape[1]
    D, F = w_gate.shape[1], w_gate.shape[2]
    assert Dh == LANES and D == 2 * ROW_TILE * LANES
    nblk = block_expert.shape[0]
    P = nblk * MOE_ROWS * ROW_TILE
    rows = lambda i, be, nu, nx, sl: (jnp.minimum(i, nu[0] - 1), 0)
    hbm = pl.BlockSpec(memory_space=pl.ANY)
    return pl.pallas_call(
        _expert_kernel,
        grid_spec=pltpu.PrefetchScalarGridSpec(
            num_scalar_prefetch=4,
            grid=(nblk,),
            in_specs=[pl.BlockSpec((MOE_ROWS * ROW_TILE, Dh), rows), hbm, hbm, hbm],
            out_specs=pl.BlockSpec((MOE_ROWS * ROW_TILE, Dh), lambda i, be, nu, nx, sl: (i, 0)),
            scratch_shapes=[pltpu.VMEM((MOE_WEIGHT_SLOTS, D, F), F32), pltpu.VMEM((MOE_WEIGHT_SLOTS, D, F), F32),
                            pltpu.VMEM((MOE_WEIGHT_SLOTS, F, D), F32),
                            pltpu.VMEM((D, F), BF16), pltpu.VMEM((D, F), BF16), pltpu.VMEM((F, D), BF16),
                            pltpu.SemaphoreType.DMA((MOE_WEIGHT_SLOTS, 3))],
        ),
        out_shape=jax.ShapeDtypeStruct((P, Dh), jnp.uint32),
        compiler_params=_cparams(("arbitrary",)),
        name="moe_experts",
    )(block_expert, n_used, expert_rank, kth_expert, xs, w_gate, w_up, w_down)


def _combine_kernel(dest_ref, x1_ref, rt_ref, g_ref, ys_ref, o_ref, buf, sem, *, tq):
    i = pl.program_id(0)
    n = pl.num_programs(0)

    def token_rows(t):
        return pl.ds(pl.multiple_of(t * ROW_TILE, ROW_TILE), ROW_TILE)

    def issue(step, s):
        def body(j, carry):
            for k in range(MOE_TOPK):
                pltpu.make_async_copy(ys_ref.at[token_rows(dest_ref[2 * (step * tq + j) + k])],
                                      buf.at[s, k, token_rows(j)], sem.at[s]).start()
            return carry
        lax.fori_loop(0, tq, body, 0)

    @pl.when(i == 0)
    def _():
        issue(0, 0)

    for s in range(2):
        @pl.when((i + 1 < n) & ((i + 1) % 2 == s))
        def _():
            issue(i + 1, s)

    for s in range(2):
        @pl.when(i % 2 == s)
        def _():
            for k in range(MOE_TOPK):
                pltpu.make_async_copy(ys_ref.at[pl.ds(0, tq * ROW_TILE)], buf.at[s, k], sem.at[s]).wait()
            rt = rt_ref[...]
            y0 = _unpack_halves(_load_row_tiles(buf.at[s, 0], tq))
            y1 = _unpack_halves(_load_row_tiles(buf.at[s, 1], tq))
            x = x1_ref[...] + rt[:, 2:3] * y0 + rt[:, 3:4] * y1
            ms = jnp.mean(x * x, axis=-1, keepdims=True)
            o_ref[...] = x * lax.rsqrt(ms + NORM_EPS) * g_ref[...]


def _combine(x1, route, gain, ys, dest_flat, tq=256):
    T, D = x1.shape
    return pl.pallas_call(
        functools.partial(_combine_kernel, tq=tq),
        grid_spec=pltpu.PrefetchScalarGridSpec(
            num_scalar_prefetch=1,
            grid=(T // tq,),
            in_specs=[
                pl.BlockSpec((tq, D), lambda i, d: (i, 0)),
                pl.BlockSpec((tq, ROUTE_LANES), lambda i, d: (i, 0)),
                pl.BlockSpec((1, D), lambda i, d: (0, 0)),
                pl.BlockSpec(memory_space=pl.ANY),
            ],
            out_specs=pl.BlockSpec((tq, D), lambda i, d: (i, 0)),
            scratch_shapes=[pltpu.VMEM((2, MOE_TOPK, tq * ROW_TILE, LANES), jnp.uint32),
                            pltpu.SemaphoreType.DMA((2,))],
        ),
        out_shape=jax.ShapeDtypeStruct((T, D), F32),
        compiler_params=_cparams(("arbitrary",)),
        name="moe_combine",
    )(dest_flat, x1, route, gain.reshape(1, D).astype(F32), ys)


def _route_metadata(route, T):
    e = route[:, :MOE_TOPK].astype(jnp.int32)
    ids = jnp.arange(MOE_EXPERTS, dtype=jnp.int32)
    oh0 = e[:, 0, None] == ids
    oh1 = e[:, 1, None] == ids
    onehot = (oh0 | oh1).astype(jnp.int32)
    incl = jnp.cumsum(onehot, axis=0)
    counts = incl[-1]
    pcounts = (counts + MOE_ROWS - 1) // MOE_ROWS * MOE_ROWS
    pends = jnp.cumsum(pcounts)
    poffs = pends - pcounts
    slot = incl - onehot + poffs[None, :]
    dest = jnp.stack([jnp.sum(jnp.where(oh0, slot, 0), axis=1), jnp.sum(jnp.where(oh1, slot, 0), axis=1)], axis=1)
    nblk = (T * MOE_TOPK) // MOE_ROWS + MOE_EXPERTS
    first_row = jnp.arange(nblk, dtype=jnp.int32) * MOE_ROWS
    block_expert = jnp.minimum(jnp.sum((pends[None, :] <= first_row[:, None]).astype(jnp.int32), axis=1),
                               MOE_EXPERTS - 1).astype(jnp.int32)
    n_used = (pends[-1:] // MOE_ROWS).astype(jnp.int32)
    used = counts > 0
    expert_rank = (jnp.cumsum(used.astype(jnp.int32)) - 1).astype(jnp.int32)
    order = jnp.argsort(jnp.where(used, ids, ids + MOE_EXPERTS)).astype(jnp.int32)
    kth = jnp.where(ids < jnp.sum(used), order, -1)
    kth_expert = jnp.concatenate([kth, jnp.full((MOE_WEIGHT_SLOTS,), -1, jnp.int32)])
    blk = jnp.arange(nblk, dtype=jnp.int32)
    following = jnp.concatenate([block_expert[1:], block_expert[-1:]])
    zero_block = ((blk >= n_used[0] - 1) | (block_expert != following)).astype(jnp.int32)
    return dest.reshape(-1).astype(jnp.int32), zero_block, block_expert, n_used, expert_rank, kth_expert, nblk


def kernel(x, norm1_gain, w_in, hg_lb_logits, hg_norm_gain, rel_bias, w_branch_a, w_branch_b, w_out,
           norm2_gain, w_router_group, b_router_group, w_router_expert, b_router_expert,
           w_exp_gate, w_exp_up, w_exp_down, final_norm_gain):
    B, S, D = x.shape
    T = B * S
    depth = norm1_gain.shape[0]
    lower_bounds = jnp.cumsum(jax.nn.softmax(hg_lb_logits.astype(F32), axis=0), axis=0)
    att_w = ATT_HEADS * HEAD_DIM
    bias = _attn_bias(rel_bias)
    x2d = x.reshape(T, D)
    for layer in range(depth):
        zeros_lb = jnp.zeros((1, 3 * att_w), F32)
        hm = dict(head_major=True, B=B, S=S)
        h, q_a = _normproj(x2d, norm1_gain[layer], w_in, layer, D, B=B, S=S)
        lf_a = _inproj(h, w_in, layer, D, D, lower_bounds[layer].reshape(1, D), mode="forget", out_dtype=F32, **hm)
        i_a = _inproj(h, w_in, layer, 2 * D, D, zeros_lb, mode="none", out_dtype=BF16, **hm)
        sg_a = _inproj(h, w_in, layer, 3 * D, D, zeros_lb, mode="sigmoid", out_dtype=BF16, **hm)
        qkv_b = _inproj(h, w_in, layer, 4 * D, 3 * att_w, zeros_lb, mode="none", out_dtype=BF16, tn=att_w, **hm)
        gates = _inproj(h, w_in, layer, 4 * D + 3 * att_w, 2 * D, zeros_lb, mode="sigmoid",
                        head_major=False, out_dtype=BF16, B=B, S=S)

        y_a = _hgrn(q_a, lf_a, i_a, sg_a, hg_norm_gain[layer]).reshape(T, D)
        y_b = _attention(qkv_b, bias).reshape(T, ATT_HEADS_PER_GROUP * HEAD_DIM)
        merged = _merge(y_a, y_b, w_branch_a[layer], w_branch_b[layer], gates)

        n_pad = ROUTE_LANES - MOE_EXPERTS - MOE_GROUPS
        wr = jnp.concatenate([w_router_expert[layer].astype(F32), w_router_group[layer].astype(F32),
                              jnp.zeros((D, n_pad), F32)], axis=1)
        br = jnp.concatenate([b_router_expert[layer].astype(F32), b_router_group[layer].astype(F32),
                              jnp.zeros((n_pad,), F32)]).reshape(1, ROUTE_LANES)
        wr_hi = wr.astype(BF16)
        wr_lo = (wr - wr_hi.astype(F32)).astype(BF16)
        x1, h2, route = _outproj(merged, w_out[layer].astype(BF16), x2d, norm2_gain[layer],
                                 jnp.concatenate([wr_hi, wr_lo], axis=1), br)

        dest, zero_block, block_expert, n_used, expert_rank, kth_expert, nblk = _route_metadata(route, T)
        xs = _dispatch(h2, dest, zero_block, nblk * MOE_ROWS)
        ys = _experts(xs, block_expert, n_used, expert_rank, kth_expert,
                      w_exp_gate[layer], w_exp_up[layer], w_exp_down[layer])
        last = layer == depth - 1
        assert last, "the fused combine applies the final norm; deeper stacks need an un-normalised combine"
        x2d = _combine(x1, route, final_norm_gain, ys, dest)
    return x2d.reshape(B, S, D)
```

```python
import functools
import math

import numpy as np
import jax
import jax.numpy as jnp
from jax import lax
from jax.experimental import pallas as pl
from jax.experimental.pallas import tpu as pltpu

F32 = jnp.float32
BF16 = jnp.bfloat16

LANES = 128
NORM_EPS = 1e-6
HEAD_DIM = 128
ATT_GROUPS = ((128, 1), (512, 4), (2048, 16))
ATT_HEADS_PER_GROUP = 4
ATT_HEADS = len(ATT_GROUPS) * ATT_HEADS_PER_GROUP
ATT_BLOCK = 128
ATT_STACK = 16
REL_BUCKETS = 32
REL_MAX_DIST = 2048
MOE_GROUPS = 8
MOE_EXPERTS_PER_GROUP = 8
MOE_EXPERTS = MOE_GROUPS * MOE_EXPERTS_PER_GROUP
MOE_TOPK = 2
MOE_ROWS = 256
NEG_BIG = -1e30
VMEM_LIMIT = 56 * 1024 * 1024


def _cparams(sem):
    return pltpu.CompilerParams(dimension_semantics=sem, vmem_limit_bytes=VMEM_LIMIT)


def _normproj_kernel(x_ref, g_ref, w_ref, h_ref, o_ref, w_bf):
    @pl.when(pl.program_id(0) == 0)
    def _():
        w_bf[...] = w_ref[...].astype(BF16)

    x = x_ref[...]
    ms = jnp.mean(x * x, axis=-1, keepdims=True)
    h = (x * lax.rsqrt(ms + NORM_EPS) * g_ref[...]).astype(h_ref.dtype)
    h_ref[...] = h
    acc = jnp.dot(h, w_bf[...], preferred_element_type=F32)
    for hh in range(acc.shape[1] // HEAD_DIM):
        o_ref[0, hh] = acc[:, hh * HEAD_DIM:(hh + 1) * HEAD_DIM].astype(o_ref.dtype)


def _normproj(x2d, gain, w_all, layer, N, *, B, S, tm=512):
    T, D = x2d.shape
    assert S % tm == 0
    spb = S // tm
    return pl.pallas_call(
        _normproj_kernel,
        grid=(T // tm,),
        in_specs=[
            pl.BlockSpec((tm, D), lambda i: (i, 0)),
            pl.BlockSpec((1, D), lambda i: (0, 0)),
            pl.BlockSpec((None, D, N), lambda i: (layer, 0, 0), pipeline_mode=pl.Buffered(1)),
        ],
        out_specs=[
            pl.BlockSpec((tm, D), lambda i: (i, 0)),
            pl.BlockSpec((1, N // HEAD_DIM, tm, HEAD_DIM), lambda i: (i // spb, 0, i % spb, 0)),
        ],
        out_shape=[jax.ShapeDtypeStruct((T, D), BF16),
                   jax.ShapeDtypeStruct((B, N // HEAD_DIM, S, HEAD_DIM), BF16)],
        scratch_shapes=[pltpu.VMEM((D, N), BF16)],
        compiler_params=_cparams(("arbitrary",)),
        name="norm_inproj",
    )(x2d, gain.reshape(1, D).astype(F32), w_all)


def _inproj_kernel(h_ref, w_ref, lb_ref, o_ref, w_bf, *, mode, head_major):
    @pl.when(pl.program_id(1) == 0)
    def _():
        w_bf[...] = w_ref[0].astype(BF16)

    acc = jnp.dot(h_ref[...], w_bf[...], preferred_element_type=F32)
    if mode == "forget":
        lb = lb_ref[...]
        acc = lb + (1.0 - lb) * jax.nn.sigmoid(acc)
    elif mode == "sigmoid":
        acc = 0.5 * jnp.tanh(0.5 * acc) + 0.5
    if head_major:
        for hh in range(acc.shape[1] // HEAD_DIM):
            o_ref[0, hh] = acc[:, hh * HEAD_DIM:(hh + 1) * HEAD_DIM].astype(o_ref.dtype)
    else:
        o_ref[...] = acc.astype(o_ref.dtype)


def _inproj(h, w_all, layer, col0, N, lb, *, mode, head_major, out_dtype, B, S, tm=1024, tn=1024):
    T, D = h.shape
    assert N % tn == 0 and col0 % LANES == 0 and T % tm == 0 and S % tm == 0
    spb = S // tm
    if head_major:
        out_shape = jax.ShapeDtypeStruct((B, N // HEAD_DIM, S, HEAD_DIM), out_dtype)
        out_spec = pl.BlockSpec((1, tn // HEAD_DIM, tm, HEAD_DIM), lambda j, i: (i // spb, j, i % spb, 0))
    else:
        out_shape = jax.ShapeDtypeStruct((T, N), out_dtype)
        out_spec = pl.BlockSpec((tm, tn), lambda j, i: (i, j))
    return pl.pallas_call(
        functools.partial(_inproj_kernel, mode=mode, head_major=head_major),
        grid=(N // tn, T // tm),
        in_specs=[
            pl.BlockSpec((tm, D), lambda j, i: (i, 0)),
            pl.BlockSpec((pl.Element(1), pl.Element(D), pl.Element(tn)), lambda j, i: (layer, 0, pl.multiple_of(col0 + j * tn, LANES))),
            pl.BlockSpec((1, tn), lambda j, i: (0, j)),
        ],
        out_specs=out_spec,
        out_shape=out_shape,
        scratch_shapes=[pltpu.VMEM((D, tn), BF16)],
        compiler_params=_cparams(("parallel", "arbitrary")),
        name="inproj_" + mode,
    )(h, w_all, lb)


HG_CHUNK = 256
HG_DIAG = 16
HG_HEADS_PER_STEP = 4


def _hgrn_levels(C):
    out, m = [], C // 2
    while m >= HG_DIAG:
        out.append(m)
        m //= 2
    return out


def _hgrn_masks(C):
    t = np.arange(C)[:, None]
    s = np.arange(C)[None, :]
    masks = []
    for m in _hgrn_levels(C):
        masks.append((t // (2 * m) == s // (2 * m)) & ((t // m) % 2 == 1) & ((s // m) % 2 == 0))
    masks.append((t // HG_DIAG == s // HG_DIAG) & (t >= s))
    total = np.sum(np.stack(masks).astype(np.int32), axis=0)
    assert np.array_equal(total, (t >= s).astype(np.int32))
    return np.stack(masks).astype(np.float32)


def _dot_nt(a, b):
    return lax.dot_general(a, b, (((1,), (1,)), ((), ())), preferred_element_type=F32)


def _dot_tn(a, b):
    return lax.dot_general(a, b, (((0,), (0,)), ((), ())), preferred_element_type=F32)


def _hgrn_kernel(q_ref, f_ref, v_ref, sg_ref, gain_ref, tril_ref, mask_ref, o_ref, st_ref, b_ref, *, C):
    S = q_ref.shape[2]
    HP = q_ref.shape[1]
    levels = _hgrn_levels(C)
    nchunks = S // C
    tril = tril_ref[...]

    def ref_rows(h, block, row_of_block):
        parts = [jnp.broadcast_to(b_ref[h, pl.ds(row_of_block(p), 1), :], (block, HEAD_DIM))
                 for p in range(C // block)]
        return jnp.concatenate(parts, axis=0)

    def cumsum(c):
        r0 = pl.multiple_of(c * C, C)
        out = []
        for h in range(HP):
            lf2 = jnp.log2(f_ref[0, h, pl.ds(r0, C), :])
            hi = lf2.astype(BF16)
            lo = (lf2 - hi.astype(F32)).astype(BF16)
            r = jnp.dot(tril, jnp.concatenate([hi, lo], axis=1), preferred_element_type=F32)
            out.append(r[:, :HEAD_DIM] + r[:, HEAD_DIM:])
        return tuple(out)

    st_ref[...] = jnp.zeros_like(st_ref)

    def chunk(c, b_all):
        r0 = pl.multiple_of(c * C, C)
        rows = pl.ds(r0, C)
        b_next = cumsum(jnp.minimum(c + 1, nchunks - 1))
        for h in range(HP):
            b = b_all[h]
            b_ref[h] = b
            q = q_ref[0, h, rows, :].astype(F32)
            kk = 1.0 - f_ref[0, h, rows, :]
            b_last = b_ref[h, pl.ds(C - 1, 1), :]
            scores = jnp.zeros((C, C), BF16)
            sign_bit = jnp.uint32(0x80000000)
            for li, m in enumerate(levels):
                d = b - ref_rows(h, 2 * m, lambda p, m=m: 2 * m * p + m - 1)
                e = jnp.exp2(pltpu.bitcast(pltpu.bitcast(d, jnp.uint32) | sign_bit, F32))
                s_l = _dot_nt((q * e).astype(BF16), (kk * e).astype(BF16)).astype(BF16)
                scores = scores + mask_ref[li] * s_l
            d = b - ref_rows(h, HG_DIAG, lambda p: HG_DIAG * p + HG_DIAG // 2 - 1)
            s_l = _dot_nt((q * jnp.exp2(d)).astype(BF16), (kk * jnp.exp2(-d)).astype(BF16)).astype(BF16)
            scores = scores + mask_ref[len(levels)] * s_l
            v = v_ref[0, h, rows, :]
            st = st_ref[h]
            qe = (q * jnp.exp2(b)).astype(BF16)
            kd = (kk * jnp.exp2(b_last - b)).astype(BF16)
            o = jnp.dot(scores, v, preferred_element_type=F32) + _dot_nt(qe, st.astype(BF16))
            st_ref[h] = st * jnp.exp2(b_last) + _dot_tn(v, kd)
            ms = jnp.mean(o * o, axis=-1, keepdims=True)
            y = o * lax.rsqrt(ms + NORM_EPS) * gain_ref[:, h * HEAD_DIM:(h + 1) * HEAD_DIM]
            y = y * sg_ref[0, h, rows, :].astype(F32)
            o_ref[0, rows, h * HEAD_DIM:(h + 1) * HEAD_DIM] = y.astype(o_ref.dtype)
        return b_next

    lax.fori_loop(0, nchunks, chunk, cumsum(0))


def _hgrn(q, lf, v, sg, gain):
    B, H, S, E = q.shape
    C, HP = HG_CHUNK, HG_HEADS_PER_STEP
    masks = jnp.asarray(_hgrn_masks(C), dtype=BF16)
    tril = jnp.asarray(np.tril(np.ones((C, C), np.float32)), dtype=BF16)
    head_spec = pl.BlockSpec((1, HP, S, E), lambda b, h: (b, h, 0, 0))
    return pl.pallas_call(
        functools.partial(_hgrn_kernel, C=C),
        grid=(B, H // HP),
        in_specs=[
            head_spec, head_spec, head_spec, head_spec,
            pl.BlockSpec((1, HP * E), lambda b, h: (0, h)),
            pl.BlockSpec((C, C), lambda b, h: (0, 0)),
            pl.BlockSpec(masks.shape, lambda b, h: (0, 0, 0)),
        ],
        out_specs=pl.BlockSpec((1, S, HP * E), lambda b, h: (b, 0, h)),
        out_shape=jax.ShapeDtypeStruct((B, S, H * E), BF16),
        scratch_shapes=[pltpu.VMEM((HP, E, E), F32), pltpu.VMEM((HP, C, E), F32)],
        compiler_params=_cparams(("parallel", "parallel")),
        name="hgrn2",
    )(q, lf, v, sg, gain.reshape(1, H * E).astype(F32), tril, masks)


def _t5_bucket_np(dist):
    exact = REL_BUCKETS // 2
    d_f = np.maximum(dist, 1).astype(np.float32)
    log_b = exact + (np.log(d_f / np.float32(exact)) / np.float32(math.log(REL_MAX_DIST / exact))
                     * np.float32(REL_BUCKETS - exact)).astype(np.int32)
    return np.where(dist < exact, dist, np.minimum(log_b, REL_BUCKETS - 1))


def _attn_bias(rel_bias):
    blk = ATT_BLOCK
    period = 3 * blk
    out = []
    for gi, (window, dilation) in enumerate(ATT_GROUPS):
        n_back = window // dilation
        assert n_back <= blk
        hs = slice(gi * ATT_HEADS_PER_GROUP, (gi + 1) * ATT_HEADS_PER_GROUP)
        bucket = _t5_bucket_np(np.arange(n_back + 1) * dilation)
        by_delta = rel_bias[:, hs][bucket].astype(F32).T
        u = jnp.full((ATT_HEADS_PER_GROUP, period), NEG_BIG, F32)
        u = u.at[:, 2 * blk - 1 - n_back:2 * blk].set(by_delta[:, ::-1])
        flat = jnp.tile(u, (1, blk))[:, :blk * (period - 1)]
        out.append(flat.reshape(ATT_HEADS_PER_GROUP, blk, period - 1)[:, :, blk - 1:3 * blk - 1])
    return jnp.stack(out, axis=0)


def _attn_kernel(q_ref, k_ref, v_ref, bias_ref, o_ref, qf, kf, vf, og, lg):
    S = q_ref.shape[3]
    scale = HEAD_DIM ** -0.5
    blk = ATT_BLOCK

    for g, (window, d) in enumerate(ATT_GROUPS):
        L = S // d
        nb = L // blk
        if d > 1:
            qf[...] = q_ref[0, g, 0].astype(F32)
            kf[...] = k_ref[0, g, 0].astype(F32)
            vf[...] = v_ref[0, g, 0].astype(F32)

        def load(ref_bf, ref_f32, start, size, g=g, d=d):
            if d == 1:
                return ref_bf[0, g, 0, pl.ds(start, size), :]
            return ref_f32[pl.ds(start, size, stride=d), :].astype(BF16)

        def scores(r, n, g=g, d=d, load=load):
            start = r + n * blk * d
            q = load(q_ref, qf, start, blk)
            if n == 0:
                k = load(k_ref, kf, start, blk)
                v = load(v_ref, vf, start, blk)
                s = _dot_nt(q, k) * scale + bias_ref[g, 0, :, blk:]
            else:
                first = start - blk * d
                k = load(k_ref, kf, first, 2 * blk)
                v = load(v_ref, vf, first, 2 * blk)
                s = _dot_nt(q, k) * scale + bias_ref[g, 0]
            return start, s, v

        def softmax_pv(items, g=g, d=d):
            s = jnp.concatenate([it[1] for it in items], axis=0)
            m = jnp.max(s, axis=-1, keepdims=True)
            p = jnp.exp(s - m)
            den = jnp.sum(p, axis=-1, keepdims=True)
            pb = p.astype(BF16)
            lse = jnp.broadcast_to(m + jnp.log(den), (s.shape[0], HEAD_DIM))
            for j, (start, _, v) in enumerate(items):
                sl = slice(j * blk, (j + 1) * blk)
                rows = pl.ds(start, blk, stride=d) if d > 1 else pl.ds(start, blk)
                og[g, rows, :] = jnp.dot(pb[sl], v, preferred_element_type=F32) / den[sl]
                lg[g, rows, :] = lse[sl]

        blocks = [(r, n) for r in range(d) for n in range(nb)]
        first_blocks = [bn for bn in blocks if bn[1] == 0]
        later_blocks = [bn for bn in blocks if bn[1] > 0]
        for group in (first_blocks, later_blocks):
            for j0 in range(0, len(group), ATT_STACK):
                softmax_pv([scores(r, n) for r, n in group[j0:j0 + ATT_STACK]])

    rows_per_step = 256

    def mix(i, carry):
        rows = pl.ds(pl.multiple_of(i * rows_per_step, rows_per_step), rows_per_step)
        l0, l1, l2 = lg[0, rows, :], lg[1, rows, :], lg[2, rows, :]
        mx = jnp.maximum(jnp.maximum(l0, l1), l2)
        e0, e1, e2 = jnp.exp(l0 - mx), jnp.exp(l1 - mx), jnp.exp(l2 - mx)
        num = e0 * og[0, rows, :] + e1 * og[1, rows, :] + e2 * og[2, rows, :]
        o_ref[0, rows, :] = (num / (e0 + e1 + e2)).astype(o_ref.dtype)
        return carry

    lax.fori_loop(0, S // rows_per_step, mix, 0)


def _attention(qkv, bias):
    B, _, S, E = qkv.shape
    G, HG = len(ATT_GROUPS), ATT_HEADS_PER_GROUP
    x = qkv.reshape(B, 3, G, HG, S, E)

    def spec(which):
        return pl.BlockSpec((1, None, G, 1, S, E), lambda b, h, which=which: (b, which, 0, h, 0, 0))

    def kern(q_ref, k_ref, v_ref, bias_ref, o_ref, *scratch):
        _attn_kernel(q_ref, k_ref, v_ref, bias_ref, o_ref, *scratch)

    return pl.pallas_call(
        kern,
        grid=(B, HG),
        in_specs=[spec(0), spec(1), spec(2),
                  pl.BlockSpec((G, 1, ATT_BLOCK, 2 * ATT_BLOCK), lambda b, h: (0, h, 0, 0))],
        out_specs=pl.BlockSpec((1, S, E), lambda b, h: (b, 0, h)),
        out_shape=jax.ShapeDtypeStruct((B, S, HG * E), BF16),
        scratch_shapes=[pltpu.VMEM((S, E), F32)] * 3 + [pltpu.VMEM((G, S, E), F32)] * 2,
        compiler_params=_cparams(("parallel", "parallel")),
        name="dilated_attn",
    )(x, x, x, bias)


def _merge_kernel(ya_ref, yb_ref, wa_ref, wb_ref, ga_ref, gb_ref, o_ref, wa_bf, wb_bf):
    @pl.when(pl.program_id(1) == 0)
    def _():
        wa_bf[...] = wa_ref[...].astype(BF16)
        wb_bf[...] = wb_ref[...].astype(BF16)

    a = jnp.dot(ya_ref[...], wa_bf[...], preferred_element_type=F32)
    b = jnp.dot(yb_ref[...], wb_bf[...], preferred_element_type=F32)
    o_ref[...] = (ga_ref[...].astype(F32) * a + gb_ref[...].astype(F32) * b).astype(o_ref.dtype)


def _merge(ya, yb, wa, wb, gates, tm=1024, tn=1024):
    T, Ka = ya.shape
    Kb = yb.shape[1]
    N = wa.shape[1]
    nj = N // tn
    return pl.pallas_call(
        _merge_kernel,
        grid=(nj, T // tm),
        in_specs=[
            pl.BlockSpec((tm, Ka), lambda j, i: (i, 0)),
            pl.BlockSpec((tm, Kb), lambda j, i: (i, 0)),
            pl.BlockSpec((Ka, tn), lambda j, i: (0, j)),
            pl.BlockSpec((Kb, tn), lambda j, i: (0, j)),
            pl.BlockSpec((tm, tn), lambda j, i: (i, j)),
            pl.BlockSpec((tm, tn), lambda j, i, nj=nj: (i, j + nj)),
        ],
        out_specs=pl.BlockSpec((tm, tn), lambda j, i: (i, j)),
        out_shape=jax.ShapeDtypeStruct((T, N), BF16),
        scratch_shapes=[pltpu.VMEM((Ka, tn), BF16), pltpu.VMEM((Kb, tn), BF16)],
        compiler_params=_cparams(("parallel", "arbitrary")),
        name="branch_merge",
    )(ya, yb, wa, wb, gates, gates)


ROUTE_LANES = LANES


def _pack_halves(x_bf):
    bits = pltpu.bitcast(x_bf.astype(F32), jnp.uint32)
    n = bits.shape[1] // 2
    return (bits[:, :n] >> 16) | (bits[:, n:] & jnp.uint32(0xFFFF0000))


def _unpack_halves(p):
    lo = pltpu.bitcast(p << 16, F32)
    hi = pltpu.bitcast(p & jnp.uint32(0xFFFF0000), F32)
    return jnp.concatenate([lo, hi], axis=1)


ROW_TILE = 8


def _store_row_tiles(ref, packed):
    m = packed.shape[0]
    assert packed.shape[1] == ROW_TILE * LANES
    for c in range(ROW_TILE):
        ref[pl.ds(c, m, stride=ROW_TILE), :] = packed[:, c * LANES:(c + 1) * LANES]


def _load_row_tiles(ref, m):
    return jnp.concatenate([ref[pl.ds(c, m, stride=ROW_TILE), :] for c in range(ROW_TILE)], axis=1)


def _outproj_kernel(m_ref, w_ref, x_ref, g_ref, wrc_ref, br_ref, x1_ref, h2_ref, rt_ref):
    x1 = x_ref[...] + jnp.dot(m_ref[...], w_ref[...], preferred_element_type=F32)
    x1_ref[...] = x1
    ms = jnp.mean(x1 * x1, axis=-1, keepdims=True)
    h2 = x1 * lax.rsqrt(ms + NORM_EPS) * g_ref[...]
    h_hi = h2.astype(BF16)
    _store_row_tiles(h2_ref, _pack_halves(h_hi))
    h_lo = (h2 - h_hi.astype(F32)).astype(BF16)
    r = jnp.dot(h_hi, wrc_ref[...], preferred_element_type=F32)
    lg = (r[:, :ROUTE_LANES] + r[:, ROUTE_LANES:]
          + jnp.dot(h_lo, wrc_ref[:, :ROUTE_LANES], preferred_element_type=F32)) + br_ref[...]

    lane = lax.broadcasted_iota(jnp.int32, lg.shape, 1)
    lane_f = lane.astype(F32)
    big = float(ROUTE_LANES)
    is_group = (lane >= MOE_EXPERTS) & (lane < MOE_EXPERTS + MOE_GROUPS)
    lgg = jnp.where(is_group, lg, -jnp.inf)
    gmax = jnp.max(lgg, axis=-1, keepdims=True)
    gsel = jnp.min(jnp.where(lgg == gmax, lane_f - MOE_EXPERTS, big), axis=-1, keepdims=True)
    pg = 1.0 / jnp.sum(jnp.where(is_group, jnp.exp(lg - gmax), 0.0), axis=-1, keepdims=True)

    in_group = (lane < MOE_EXPERTS) & ((lane // MOE_EXPERTS_PER_GROUP).astype(F32) == gsel)
    le = jnp.where(in_group, lg, -jnp.inf)
    t1 = jnp.max(le, axis=-1, keepdims=True)
    i1 = jnp.min(jnp.where(le == t1, lane_f, big), axis=-1, keepdims=True)
    le2 = jnp.where(lane_f == i1, -jnp.inf, le)
    t2 = jnp.max(le2, axis=-1, keepdims=True)
    i2 = jnp.min(jnp.where(le2 == t2, lane_f, big), axis=-1, keepdims=True)
    e2 = jnp.exp(t2 - t1)
    w1 = pg / (1.0 + e2)
    w2 = pg * e2 / (1.0 + e2)
    rt_ref[...] = jnp.where(lane == 0, i1, jnp.where(lane == 1, i2, jnp.where(lane == 2, w1,
                            jnp.where(lane == 3, w2, 0.0))))


def _outproj(merged, w_out, x2d, gain, wr_cat, br, tm=512):
    T, D = x2d.shape
    row = lambda i: (i, 0)
    const = lambda i: (0, 0)
    return pl.pallas_call(
        _outproj_kernel,
        grid=(T // tm,),
        in_specs=[
            pl.BlockSpec((tm, D), row), pl.BlockSpec((D, D), const), pl.BlockSpec((tm, D), row),
            pl.BlockSpec((1, D), const), pl.BlockSpec((D, 2 * ROUTE_LANES), const),
            pl.BlockSpec((1, ROUTE_LANES), const),
        ],
        out_specs=[pl.BlockSpec((tm, D), row), pl.BlockSpec((tm * ROW_TILE, LANES), row),
                   pl.BlockSpec((tm, ROUTE_LANES), row)],
        out_shape=[jax.ShapeDtypeStruct((T, D), F32), jax.ShapeDtypeStruct((T * ROW_TILE, LANES), jnp.uint32),
                   jax.ShapeDtypeStruct((T, ROUTE_LANES), F32)],
        compiler_params=_cparams(("parallel",)),
        name="outproj_router",
    )(merged, w_out, x2d, gain.reshape(1, D).astype(F32), wr_cat, br)


def _dispatch_kernel(dest_ref, zb_ref, h_ref, xs_ref, zeros, sem, *, tq):
    base = pl.program_id(0) * tq
    block_rows = MOE_ROWS * ROW_TILE
    n_blocks = xs_ref.shape[0] // block_rows

    @pl.when(pl.program_id(0) == 0)
    def _():
        zeros[...] = jnp.zeros_like(zeros)

        def zero_copy(b):
            return pltpu.make_async_copy(zeros, xs_ref.at[pl.ds(pl.multiple_of(b * block_rows, block_rows), block_rows)],
                                         sem)

        def start(b, carry):
            @pl.when(zb_ref[b] != 0)
            def _():
                zero_copy(b).start()
            return carry

        def wait(b, carry):
            @pl.when(zb_ref[b] != 0)
            def _():
                zero_copy(b).wait()
            return carry

        lax.fori_loop(0, n_blocks, start, 0)
        lax.fori_loop(0, n_blocks, wait, 0)

    def token_rows(t):
        return pl.ds(pl.multiple_of(t * ROW_TILE, ROW_TILE), ROW_TILE)

    def copy(j, k):
        return pltpu.make_async_copy(h_ref.at[token_rows(j)], xs_ref.at[token_rows(dest_ref[2 * (base + j) + k])], sem)

    def issue(j, carry):
        copy(j, 0).start()
        copy(j, 1).start()
        return carry

    lax.fori_loop(0, tq, issue, 0)
    for _ in range(MOE_TOPK):
        pltpu.make_async_copy(h_ref, xs_ref.at[pl.ds(0, tq * ROW_TILE)], sem).wait()


def _dispatch(h2, dest_flat, zero_block, n_rows, tq=512):
    T = h2.shape[0] // ROW_TILE
    return pl.pallas_call(
        functools.partial(_dispatch_kernel, tq=tq),
        grid_spec=pltpu.PrefetchScalarGridSpec(
            num_scalar_prefetch=2,
            grid=(T // tq,),
            in_specs=[pl.BlockSpec((tq * ROW_TILE, LANES), lambda i, d, z: (i, 0))],
            out_specs=pl.BlockSpec(memory_space=pl.ANY),
            scratch_shapes=[pltpu.VMEM((MOE_ROWS * ROW_TILE, LANES), h2.dtype), pltpu.SemaphoreType.DMA(())],
        ),
        out_shape=jax.ShapeDtypeStruct((n_rows * ROW_TILE, LANES), h2.dtype),
        compiler_params=_cparams(("arbitrary",)),
        name="moe_dispatch",
    )(dest_flat, zero_block, h2)


MOE_WEIGHT_SLOTS = 3


def _expert_kernel(be_ref, nu_ref, rank_ref, kth_ref, xs_ref, wg_hbm, wu_hbm, wd_hbm, o_ref,
                   wg_f, wu_f, wd_f, wg_s, wu_s, wd_s, sem):
    i = pl.program_id(0)
    e = be_ref[i]
    active = i < nu_ref[0]
    new_expert = active & ((i == 0) | (e != be_ref[jnp.maximum(i - 1, 0)]))
    k = rank_ref[e]
    slot = k % MOE_WEIGHT_SLOTS
    ahead = MOE_WEIGHT_SLOTS - 1

    def weight_copies(expert, s):
        return (pltpu.make_async_copy(wg_hbm.at[expert], wg_f.at[s], sem.at[s, 0]),
                pltpu.make_async_copy(wu_hbm.at[expert], wu_f.at[s], sem.at[s, 1]),
                pltpu.make_async_copy(wd_hbm.at[expert], wd_f.at[s], sem.at[s, 2]))

    def start_kth(j):
        nxt = kth_ref[j]

        @pl.when(nxt >= 0)
        def _():
            for c in weight_copies(nxt, j % MOE_WEIGHT_SLOTS):
                c.start()

    @pl.when(active & (i == 0))
    def _():
        for j in range(ahead):
            start_kth(j)

    @pl.when(new_expert)
    def _():
        start_kth(k + ahead)
        for c in weight_copies(e, slot):
            c.wait()
        wg_s[...] = wg_f[slot].astype(BF16)
        wu_s[...] = wu_f[slot].astype(BF16)
        wd_s[...] = wd_f[slot].astype(BF16)

    @pl.when(active)
    def _():
        x = _unpack_halves(_load_row_tiles(xs_ref, MOE_ROWS)).astype(BF16)
        g = jnp.dot(x, wg_s[...], preferred_element_type=F32)
        u = jnp.dot(x, wu_s[...], preferred_element_type=F32)
        a = (g * jax.nn.sigmoid(g) * u).astype(BF16)
        y = jnp.dot(a, wd_s[...], preferred_element_type=F32)
        _store_row_tiles(o_ref, _pack_halves(y.astype(BF16)))

    @pl.when(jnp.logical_not(active))
    def _():
        o_ref[...] = jnp.zeros_like(o_ref)


def _experts(xs, block_expert, n_used, expert_rank, kth_expert, w_gate, w_up, w_down):
    Dh = xs.shape[1]
    D, F = w_gate.shape[1], w_gate.shape[2]
    assert Dh == LANES and D == 2 * ROW_TILE * LANES
    nblk = block_expert.shape[0]
    P = nblk * MOE_ROWS * ROW_TILE
    rows = lambda i, be, nu, nx, sl: (jnp.minimum(i, nu[0] - 1), 0)
    hbm = pl.BlockSpec(memory_space=pl.ANY)
    return pl.pallas_call(
        _expert_kernel,
        grid_spec=pltpu.PrefetchScalarGridSpec(
            num_scalar_prefetch=4,
            grid=(nblk,),
            in_specs=[pl.BlockSpec((MOE_ROWS * ROW_TILE, Dh), rows), hbm, hbm, hbm],
            out_specs=pl.BlockSpec((MOE_ROWS * ROW_TILE, Dh), lambda i, be, nu, nx, sl: (i, 0)),
            scratch_shapes=[pltpu.VMEM((MOE_WEIGHT_SLOTS, D, F), F32), pltpu.VMEM((MOE_WEIGHT_SLOTS, D, F), F32),
                            pltpu.VMEM((MOE_WEIGHT_SLOTS, F, D), F32),
                            pltpu.VMEM((D, F), BF16), pltpu.VMEM((D, F), BF16), pltpu.VMEM((F, D), BF16),
                            pltpu.SemaphoreType.DMA((MOE_WEIGHT_SLOTS, 3))],
        ),
        out_shape=jax.ShapeDtypeStruct((P, Dh), jnp.uint32),
        compiler_params=_cparams(("arbitrary",)),
        name="moe_experts",
    )(block_expert, n_used, expert_rank, kth_expert, xs, w_gate, w_up, w_down)


def _combine_kernel(dest_ref, x1_ref, rt_ref, g_ref, ys_ref, o_ref, buf, sem, *, tq):
    i = pl.program_id(0)
    n = pl.num_programs(0)

    def token_rows(t):
        return pl.ds(pl.multiple_of(t * ROW_TILE, ROW_TILE), ROW_TILE)

    def issue(step, s):
        def body(j, carry):
            for k in range(MOE_TOPK):
                pltpu.make_async_copy(ys_ref.at[token_rows(dest_ref[2 * (step * tq + j) + k])],
                                      buf.at[s, k, token_rows(j)], sem.at[s]).start()
            return carry
        lax.fori_loop(0, tq, body, 0)

    @pl.when(i == 0)
    def _():
        issue(0, 0)

    for s in range(2):
        @pl.when((i + 1 < n) & ((i + 1) % 2 == s))
        def _():
            issue(i + 1, s)

    for s in range(2):
        @pl.when(i % 2 == s)
        def _():
            for k in range(MOE_TOPK):
                pltpu.make_async_copy(ys_ref.at[pl.ds(0, tq * ROW_TILE)], buf.at[s, k], sem.at[s]).wait()
            rt = rt_ref[...]
            y0 = _unpack_halves(_load_row_tiles(buf.at[s, 0], tq))
            y1 = _unpack_halves(_load_row_tiles(buf.at[s, 1], tq))
            x = x1_ref[...] + rt[:, 2:3] * y0 + rt[:, 3:4] * y1
            ms = jnp.mean(x * x, axis=-1, keepdims=True)
            o_ref[...] = x * lax.rsqrt(ms + NORM_EPS) * g_ref[...]


def _combine(x1, route, gain, ys, dest_flat, tq=512):
    T, D = x1.shape
    return pl.pallas_call(
        functools.partial(_combine_kernel, tq=tq),
        grid_spec=pltpu.PrefetchScalarGridSpec(
            num_scalar_prefetch=1,
            grid=(T // tq,),
            in_specs=[
                pl.BlockSpec((tq, D), lambda i, d: (i, 0)),
                pl.BlockSpec((tq, ROUTE_LANES), lambda i, d: (i, 0)),
                pl.BlockSpec((1, D), lambda i, d: (0, 0)),
                pl.BlockSpec(memory_space=pl.ANY),
            ],
            out_specs=pl.BlockSpec((tq, D), lambda i, d: (i, 0)),
            scratch_shapes=[pltpu.VMEM((2, MOE_TOPK, tq * ROW_TILE, LANES), jnp.uint32),
                            pltpu.SemaphoreType.DMA((2,))],
        ),
        out_shape=jax.ShapeDtypeStruct((T, D), F32),
        compiler_params=_cparams(("arbitrary",)),
        name="moe_combine",
    )(dest_flat, x1, route, gain.reshape(1, D).astype(F32), ys)


def _route_metadata(route, T):
    e = route[:, :MOE_TOPK].astype(jnp.int32)
    ids = jnp.arange(MOE_EXPERTS, dtype=jnp.int32)
    oh0 = e[:, 0, None] == ids
    oh1 = e[:, 1, None] == ids
    onehot = (oh0 | oh1).astype(jnp.int32)
    incl = jnp.cumsum(onehot, axis=0)
    counts = incl[-1]
    pcounts = (counts + MOE_ROWS - 1) // MOE_ROWS * MOE_ROWS
    pends = jnp.cumsum(pcounts)
    poffs = pends - pcounts
    slot = incl - onehot + poffs[None, :]
    dest = jnp.stack([jnp.sum(jnp.where(oh0, slot, 0), axis=1), jnp.sum(jnp.where(oh1, slot, 0), axis=1)], axis=1)
    nblk = (T * MOE_TOPK) // MOE_ROWS + MOE_EXPERTS
    first_row = jnp.arange(nblk, dtype=jnp.int32) * MOE_ROWS
    block_expert = jnp.minimum(jnp.sum((pends[None, :] <= first_row[:, None]).astype(jnp.int32), axis=1),
                               MOE_EXPERTS - 1).astype(jnp.int32)
    n_used = (pends[-1:] // MOE_ROWS).astype(jnp.int32)
    used = counts > 0
    expert_rank = (jnp.cumsum(used.astype(jnp.int32)) - 1).astype(jnp.int32)
    order = jnp.argsort(jnp.where(used, ids, ids + MOE_EXPERTS)).astype(jnp.int32)
    kth = jnp.where(ids < jnp.sum(used), order, -1)
    kth_expert = jnp.concatenate([kth, jnp.full((MOE_WEIGHT_SLOTS,), -1, jnp.int32)])
    blk = jnp.arange(nblk, dtype=jnp.int32)
    following = jnp.concatenate([block_expert[1:], block_expert[-1:]])
    zero_block = ((blk >= n_used[0] - 1) | (block_expert != following)).astype(jnp.int32)
    return dest.reshape(-1).astype(jnp.int32), zero_block, block_expert, n_used, expert_rank, kth_expert, nblk


def kernel(x, norm1_gain, w_in, hg_lb_logits, hg_norm_gain, rel_bias, w_branch_a, w_branch_b, w_out,
           norm2_gain, w_router_group, b_router_group, w_router_expert, b_router_expert,
           w_exp_gate, w_exp_up, w_exp_down, final_norm_gain):
    B, S, D = x.shape
    T = B * S
    depth = norm1_gain.shape[0]
    lower_bounds = jnp.cumsum(jax.nn.softmax(hg_lb_logits.astype(F32), axis=0), axis=0)
    att_w = ATT_HEADS * HEAD_DIM
    bias = _attn_bias(rel_bias)
    x2d = x.reshape(T, D)
    for layer in range(depth):
        zeros_lb = jnp.zeros((1, 3 * att_w), F32)
        hm = dict(head_major=True, B=B, S=S)
        h, q_a = _normproj(x2d, norm1_gain[layer], w_in, layer, D, B=B, S=S)
        lf_a = _inproj(h, w_in, layer, D, D, lower_bounds[layer].reshape(1, D), mode="forget", out_dtype=F32, **hm)
        i_a = _inproj(h, w_in, layer, 2 * D, D, zeros_lb, mode="none", out_dtype=BF16, **hm)
        sg_a = _inproj(h, w_in, layer, 3 * D, D, zeros_lb, mode="sigmoid", out_dtype=BF16, **hm)
        qkv_b = _inproj(h, w_in, layer, 4 * D, 3 * att_w, zeros_lb, mode="none", out_dtype=BF16, tn=att_w, **hm)
        gates = _inproj(h, w_in, layer, 4 * D + 3 * att_w, 2 * D, zeros_lb, mode="sigmoid",
                        head_major=False, out_dtype=BF16, B=B, S=S)

        y_a = _hgrn(q_a, lf_a, i_a, sg_a, hg_norm_gain[layer]).reshape(T, D)
        y_b = _attention(qkv_b, bias).reshape(T, ATT_HEADS_PER_GROUP * HEAD_DIM)
        merged = _merge(y_a, y_b, w_branch_a[layer], w_branch_b[layer], gates)

        n_pad = ROUTE_LANES - MOE_EXPERTS - MOE_GROUPS
        wr = jnp.concatenate([w_router_expert[layer].astype(F32), w_router_group[layer].astype(F32),
                              jnp.zeros((D, n_pad), F32)], axis=1)
        br = jnp.concatenate([b_router_expert[layer].astype(F32), b_router_group[layer].astype(F32),
                              jnp.zeros((n_pad,), F32)]).reshape(1, ROUTE_LANES)
        wr_hi = wr.astype(BF16)
        wr_lo = (wr - wr_hi.astype(F32)).astype(BF16)
        x1, h2, route = _outproj(merged, w_out[layer].astype(BF16), x2d, norm2_gain[layer],
                                 jnp.concatenate([wr_hi, wr_lo], axis=1), br)

        dest, zero_block, block_expert, n_used, expert_rank, kth_expert, nblk = _route_metadata(route, T)
        xs = _dispatch(h2, dest, zero_block, nblk * MOE_ROWS)
        ys = _experts(xs, block_expert, n_used, expert_rank, kth_expert,
                      w_exp_gate[layer], w_exp_up[layer], w_exp_down[layer])
        last = layer == depth - 1
        assert last, "the fused combine applies the final norm; deeper stacks need an un-normalised combine"
        x2d = _combine(x1, route, final_norm_gain, ys, dest)
    return x2d.reshape(B, S, D)
```

```python
import functools
import math

import numpy as np
import jax
import jax.numpy as jnp
from jax import lax
from jax.experimental import pallas as pl
from jax.experimental.pallas import tpu as pltpu

F32 = jnp.float32
BF16 = jnp.bfloat16

LANES = 128
NORM_EPS = 1e-6
HEAD_DIM = 128
ATT_GROUPS = ((128, 1), (512, 4), (2048, 16))
ATT_HEADS_PER_GROUP = 4
ATT_HEADS = len(ATT_GROUPS) * ATT_HEADS_PER_GROUP
ATT_BLOCK = 128
ATT_STACK = 16
REL_BUCKETS = 32
REL_MAX_DIST = 2048
MOE_GROUPS = 8
MOE_EXPERTS_PER_GROUP = 8
MOE_EXPERTS = MOE_GROUPS * MOE_EXPERTS_PER_GROUP
MOE_TOPK = 2
MOE_ROWS = 256
NEG_BIG = -1e30
VMEM_LIMIT = 56 * 1024 * 1024


def _cparams(sem):
    return pltpu.CompilerParams(dimension_semantics=sem, vmem_limit_bytes=VMEM_LIMIT)


def _normproj_kernel(x_ref, g_ref, w_ref, h_ref, o_ref, w_bf):
    @pl.when(pl.program_id(0) == 0)
    def _():
        w_bf[...] = w_ref[...].astype(BF16)

    x = x_ref[...]
    ms = jnp.mean(x * x, axis=-1, keepdims=True)
    h = (x * lax.rsqrt(ms + NORM_EPS) * g_ref[...]).astype(h_ref.dtype)
    h_ref[...] = h
    acc = jnp.dot(h, w_bf[...], preferred_element_type=F32)
    for hh in range(acc.shape[1] // HEAD_DIM):
        o_ref[0, hh] = acc[:, hh * HEAD_DIM:(hh + 1) * HEAD_DIM].astype(o_ref.dtype)


def _normproj(x2d, gain, w_all, layer, N, *, B, S, tm=512):
    T, D = x2d.shape
    assert S % tm == 0
    spb = S // tm
    return pl.pallas_call(
        _normproj_kernel,
        grid=(T // tm,),
        in_specs=[
            pl.BlockSpec((tm, D), lambda i: (i, 0)),
            pl.BlockSpec((1, D), lambda i: (0, 0)),
            pl.BlockSpec((None, D, N), lambda i: (layer, 0, 0), pipeline_mode=pl.Buffered(1)),
        ],
        out_specs=[
            pl.BlockSpec((tm, D), lambda i: (i, 0)),
            pl.BlockSpec((1, N // HEAD_DIM, tm, HEAD_DIM), lambda i: (i // spb, 0, i % spb, 0)),
        ],
        out_shape=[jax.ShapeDtypeStruct((T, D), BF16),
                   jax.ShapeDtypeStruct((B, N // HEAD_DIM, S, HEAD_DIM), BF16)],
        scratch_shapes=[pltpu.VMEM((D, N), BF16)],
        compiler_params=_cparams(("arbitrary",)),
        name="norm_inproj",
    )(x2d, gain.reshape(1, D).astype(F32), w_all)


def _inproj_kernel(h_ref, w_ref, lb_ref, o_ref, w_bf, *, mode, head_major):
    @pl.when(pl.program_id(1) == 0)
    def _():
        w_bf[...] = w_ref[0].astype(BF16)

    acc = jnp.dot(h_ref[...], w_bf[...], preferred_element_type=F32)
    if mode == "forget":
        lb = lb_ref[...]
        acc = lb + (1.0 - lb) * jax.nn.sigmoid(acc)
    elif mode == "sigmoid":
        acc = 0.5 * jnp.tanh(0.5 * acc) + 0.5
    if head_major:
        for hh in range(acc.shape[1] // HEAD_DIM):
            o_ref[0, hh] = acc[:, hh * HEAD_DIM:(hh + 1) * HEAD_DIM].astype(o_ref.dtype)
    else:
        o_ref[...] = acc.astype(o_ref.dtype)


def _inproj(h, w_all, layer, col0, N, lb, *, mode, head_major, out_dtype, B, S, tm=1024, tn=1024):
    T, D = h.shape
    assert N % tn == 0 and col0 % LANES == 0 and T % tm == 0 and S % tm == 0
    spb = S // tm
    if head_major:
        out_shape = jax.ShapeDtypeStruct((B, N // HEAD_DIM, S, HEAD_DIM), out_dtype)
        out_spec = pl.BlockSpec((1, tn // HEAD_DIM, tm, HEAD_DIM), lambda j, i: (i // spb, j, i % spb, 0))
    else:
        out_shape = jax.ShapeDtypeStruct((T, N), out_dtype)
        out_spec = pl.BlockSpec((tm, tn), lambda j, i: (i, j))
    return pl.pallas_call(
        functools.partial(_inproj_kernel, mode=mode, head_major=head_major),
        grid=(N // tn, T // tm),
        in_specs=[
            pl.BlockSpec((tm, D), lambda j, i: (i, 0)),
            pl.BlockSpec((pl.Element(1), pl.Element(D), pl.Element(tn)), lambda j, i: (layer, 0, pl.multiple_of(col0 + j * tn, LANES))),
            pl.BlockSpec((1, tn), lambda j, i: (0, j)),
        ],
        out_specs=out_spec,
        out_shape=out_shape,
        scratch_shapes=[pltpu.VMEM((D, tn), BF16)],
        compiler_params=_cparams(("parallel", "arbitrary")),
        name="inproj_" + mode,
    )(h, w_all, lb)


HG_CHUNK = 256
HG_DIAG = 16
HG_HEADS_PER_STEP = 4


def _hgrn_levels(C):
    out, m = [], C // 2
    while m >= HG_DIAG:
        out.append(m)
        m //= 2
    return out


def _hgrn_masks(C):
    t = np.arange(C)[:, None]
    s = np.arange(C)[None, :]
    masks = []
    for m in _hgrn_levels(C):
        masks.append((t // (2 * m) == s // (2 * m)) & ((t // m) % 2 == 1) & ((s // m) % 2 == 0))
    masks.append((t // HG_DIAG == s // HG_DIAG) & (t >= s))
    total = np.sum(np.stack(masks).astype(np.int32), axis=0)
    assert np.array_equal(total, (t >= s).astype(np.int32))
    return np.stack(masks).astype(np.float32)


def _dot_nt(a, b):
    return lax.dot_general(a, b, (((1,), (1,)), ((), ())), preferred_element_type=F32)


def _dot_tn(a, b):
    return lax.dot_general(a, b, (((0,), (0,)), ((), ())), preferred_element_type=F32)


def _hgrn_kernel(q_ref, f_ref, v_ref, sg_ref, gain_ref, tril_ref, mask_ref, o_ref, st_ref, b_ref, *, C):
    S = q_ref.shape[2]
    HP = q_ref.shape[1]
    levels = _hgrn_levels(C)
    nchunks = S // C
    tril = tril_ref[...]

    def ref_rows(h, block, row_of_block):
        parts = [jnp.broadcast_to(b_ref[h, pl.ds(row_of_block(p), 1), :], (block, HEAD_DIM))
                 for p in range(C // block)]
        return jnp.concatenate(parts, axis=0)

    def cumsum(c):
        r0 = pl.multiple_of(c * C, C)
        out = []
        for h in range(HP):
            lf2 = jnp.log2(f_ref[0, h, pl.ds(r0, C), :])
            hi = lf2.astype(BF16)
            lo = (lf2 - hi.astype(F32)).astype(BF16)
            r = jnp.dot(tril, jnp.concatenate([hi, lo], axis=1), preferred_element_type=F32)
            out.append(r[:, :HEAD_DIM] + r[:, HEAD_DIM:])
        return tuple(out)

    st_ref[...] = jnp.zeros_like(st_ref)

    def chunk(c, b_all):
        r0 = pl.multiple_of(c * C, C)
        rows = pl.ds(r0, C)
        b_next = cumsum(jnp.minimum(c + 1, nchunks - 1))
        for h in range(HP):
            b = b_all[h]
            b_ref[h] = b
            q = q_ref[0, h, rows, :].astype(F32)
            kk = 1.0 - f_ref[0, h, rows, :]
            b_last = b_ref[h, pl.ds(C - 1, 1), :]
            scores = jnp.zeros((C, C), BF16)
            for li, m in enumerate(levels):
                parts = []
                for p in range(C // (2 * m)):
                    ref = b_ref[h, pl.ds(2 * m * p + m - 1, 1), :]
                    parts += [ref - b[2 * m * p:2 * m * p + m], b[2 * m * p + m:2 * m * (p + 1)] - ref]
                e = jnp.exp2(jnp.concatenate(parts, axis=0))
                s_l = _dot_nt((q * e).astype(BF16), (kk * e).astype(BF16)).astype(BF16)
                scores = scores + mask_ref[li] * s_l
            d = b - ref_rows(h, HG_DIAG, lambda p: HG_DIAG * p + HG_DIAG // 2 - 1)
            s_l = _dot_nt((q * jnp.exp2(d)).astype(BF16), (kk * jnp.exp2(-d)).astype(BF16)).astype(BF16)
            scores = scores + mask_ref[len(levels)] * s_l
            v = v_ref[0, h, rows, :]
            st = st_ref[h]
            qe = (q * jnp.exp2(b)).astype(BF16)
            kd = (kk * jnp.exp2(b_last - b)).astype(BF16)
            o = jnp.dot(scores, v, preferred_element_type=F32) + _dot_nt(qe, st.astype(BF16))
            st_ref[h] = st * jnp.exp2(b_last) + _dot_tn(v, kd)
            ms = jnp.mean(o * o, axis=-1, keepdims=True)
            y = o * lax.rsqrt(ms + NORM_EPS) * gain_ref[:, h * HEAD_DIM:(h + 1) * HEAD_DIM]
            y = y * sg_ref[0, h, rows, :].astype(F32)
            o_ref[0, rows, h * HEAD_DIM:(h + 1) * HEAD_DIM] = y.astype(o_ref.dtype)
        return b_next

    lax.fori_loop(0, nchunks, chunk, cumsum(0))


def _hgrn(q, lf, v, sg, gain):
    B, H, S, E = q.shape
    C, HP = HG_CHUNK, HG_HEADS_PER_STEP
    masks = jnp.asarray(_hgrn_masks(C), dtype=BF16)
    tril = jnp.asarray(np.tril(np.ones((C, C), np.float32)), dtype=BF16)
    head_spec = pl.BlockSpec((1, HP, S, E), lambda b, h: (b, h, 0, 0))
    return pl.pallas_call(
        functools.partial(_hgrn_kernel, C=C),
        grid=(B, H // HP),
        in_specs=[
            head_spec, head_spec, head_spec, head_spec,
            pl.BlockSpec((1, HP * E), lambda b, h: (0, h)),
            pl.BlockSpec((C, C), lambda b, h: (0, 0)),
            pl.BlockSpec(masks.shape, lambda b, h: (0, 0, 0)),
        ],
        out_specs=pl.BlockSpec((1, S, HP * E), lambda b, h: (b, 0, h)),
        out_shape=jax.ShapeDtypeStruct((B, S, H * E), BF16),
        scratch_shapes=[pltpu.VMEM((HP, E, E), F32), pltpu.VMEM((HP, C, E), F32)],
        compiler_params=_cparams(("parallel", "parallel")),
        name="hgrn2",
    )(q, lf, v, sg, gain.reshape(1, H * E).astype(F32), tril, masks)


def _t5_bucket_np(dist):
    exact = REL_BUCKETS // 2
    d_f = np.maximum(dist, 1).astype(np.float32)
    log_b = exact + (np.log(d_f / np.float32(exact)) / np.float32(math.log(REL_MAX_DIST / exact))
                     * np.float32(REL_BUCKETS - exact)).astype(np.int32)
    return np.where(dist < exact, dist, np.minimum(log_b, REL_BUCKETS - 1))


def _attn_bias(rel_bias):
    blk = ATT_BLOCK
    period = 3 * blk
    out = []
    for gi, (window, dilation) in enumerate(ATT_GROUPS):
        n_back = window // dilation
        assert n_back <= blk
        hs = slice(gi * ATT_HEADS_PER_GROUP, (gi + 1) * ATT_HEADS_PER_GROUP)
        bucket = _t5_bucket_np(np.arange(n_back + 1) * dilation)
        by_delta = rel_bias[:, hs][bucket].astype(F32).T
        u = jnp.full((ATT_HEADS_PER_GROUP, period), NEG_BIG, F32)
        u = u.at[:, 2 * blk - 1 - n_back:2 * blk].set(by_delta[:, ::-1])
        flat = jnp.tile(u, (1, blk))[:, :blk * (period - 1)]
        out.append(flat.reshape(ATT_HEADS_PER_GROUP, blk, period - 1)[:, :, blk - 1:3 * blk - 1])
    return jnp.stack(out, axis=0)


def _attn_kernel(q_ref, k_ref, v_ref, bias_ref, o_ref, qf, kf, vf, og, lg):
    S = q_ref.shape[3]
    scale = HEAD_DIM ** -0.5
    blk = ATT_BLOCK

    for g, (window, d) in enumerate(ATT_GROUPS):
        L = S // d
        nb = L // blk
        if d > 1:
            qf[...] = q_ref[0, g, 0].astype(F32)
            kf[...] = k_ref[0, g, 0].astype(F32)
            vf[...] = v_ref[0, g, 0].astype(F32)

        def load(ref_bf, ref_f32, start, size, g=g, d=d):
            if d == 1:
                return ref_bf[0, g, 0, pl.ds(start, size), :]
            return ref_f32[pl.ds(start, size, stride=d), :].astype(BF16)

        def scores(r, n, g=g, d=d, load=load):
            start = r + n * blk * d
            q = load(q_ref, qf, start, blk)
            if n == 0:
                k = load(k_ref, kf, start, blk)
                v = load(v_ref, vf, start, blk)
                s = _dot_nt(q, k) * scale + bias_ref[g, 0, :, blk:]
            else:
                first = start - blk * d
                k = load(k_ref, kf, first, 2 * blk)
                v = load(v_ref, vf, first, 2 * blk)
                s = _dot_nt(q, k) * scale + bias_ref[g, 0]
            return start, s, v

        def softmax_pv(items, g=g, d=d):
            s = jnp.concatenate([it[1] for it in items], axis=0)
            m = jnp.max(s, axis=-1, keepdims=True)
            p = jnp.exp(s - m)
            den = jnp.sum(p, axis=-1, keepdims=True)
            pb = p.astype(BF16)
            lse = jnp.broadcast_to(m + jnp.log(den), (s.shape[0], HEAD_DIM))
            for j, (start, _, v) in enumerate(items):
                sl = slice(j * blk, (j + 1) * blk)
                rows = pl.ds(start, blk, stride=d) if d > 1 else pl.ds(start, blk)
                og[g, rows, :] = jnp.dot(pb[sl], v, preferred_element_type=F32) / den[sl]
                lg[g, rows, :] = lse[sl]

        blocks = [(r, n) for r in range(d) for n in range(nb)]
        first_blocks = [bn for bn in blocks if bn[1] == 0]
        later_blocks = [bn for bn in blocks if bn[1] > 0]
        for group in (first_blocks, later_blocks):
            for j0 in range(0, len(group), ATT_STACK):
                softmax_pv([scores(r, n) for r, n in group[j0:j0 + ATT_STACK]])

    rows_per_step = 256

    def mix(i, carry):
        rows = pl.ds(pl.multiple_of(i * rows_per_step, rows_per_step), rows_per_step)
        l0, l1, l2 = lg[0, rows, :], lg[1, rows, :], lg[2, rows, :]
        mx = jnp.maximum(jnp.maximum(l0, l1), l2)
        e0, e1, e2 = jnp.exp(l0 - mx), jnp.exp(l1 - mx), jnp.exp(l2 - mx)
        num = e0 * og[0, rows, :] + e1 * og[1, rows, :] + e2 * og[2, rows, :]
        o_ref[0, rows, :] = (num / (e0 + e1 + e2)).astype(o_ref.dtype)
        return carry

    lax.fori_loop(0, S // rows_per_step, mix, 0)


def _attention(qkv, bias):
    B, _, S, E = qkv.shape
    G, HG = len(ATT_GROUPS), ATT_HEADS_PER_GROUP
    x = qkv.reshape(B, 3, G, HG, S, E)

    def spec(which):
        return pl.BlockSpec((1, None, G, 1, S, E), lambda b, h, which=which: (b, which, 0, h, 0, 0))

    def kern(q_ref, k_ref, v_ref, bias_ref, o_ref, *scratch):
        _attn_kernel(q_ref, k_ref, v_ref, bias_ref, o_ref, *scratch)

    return pl.pallas_call(
        kern,
        grid=(B, HG),
        in_specs=[spec(0), spec(1), spec(2),
                  pl.BlockSpec((G, 1, ATT_BLOCK, 2 * ATT_BLOCK), lambda b, h: (0, h, 0, 0))],
        out_specs=pl.BlockSpec((1, S, E), lambda b, h: (b, 0, h)),
        out_shape=jax.ShapeDtypeStruct((B, S, HG * E), BF16),
        scratch_shapes=[pltpu.VMEM((S, E), F32)] * 3 + [pltpu.VMEM((G, S, E), F32)] * 2,
        compiler_params=_cparams(("parallel", "parallel")),
        name="dilated_attn",
    )(x, x, x, bias)


def _merge_kernel(ya_ref, yb_ref, wa_ref, wb_ref, ga_ref, gb_ref, o_ref, wa_bf, wb_bf):
    @pl.when(pl.program_id(1) == 0)
    def _():
        wa_bf[...] = wa_ref[...].astype(BF16)
        wb_bf[...] = wb_ref[...].astype(BF16)

    a = jnp.dot(ya_ref[...], wa_bf[...], preferred_element_type=F32)
    b = jnp.dot(yb_ref[...], wb_bf[...], preferred_element_type=F32)
    o_ref[...] = (ga_ref[...].astype(F32) * a + gb_ref[...].astype(F32) * b).astype(o_ref.dtype)


def _merge(ya, yb, wa, wb, gates, tm=1024, tn=1024):
    T, Ka = ya.shape
    Kb = yb.shape[1]
    N = wa.shape[1]
    nj = N // tn
    return pl.pallas_call(
        _merge_kernel,
        grid=(nj, T // tm),
        in_specs=[
            pl.BlockSpec((tm, Ka), lambda j, i: (i, 0)),
            pl.BlockSpec((tm, Kb), lambda j, i: (i, 0)),
            pl.BlockSpec((Ka, tn), lambda j, i: (0, j)),
            pl.BlockSpec((Kb, tn), lambda j, i: (0, j)),
            pl.BlockSpec((tm, tn), lambda j, i: (i, j)),
            pl.BlockSpec((tm, tn), lambda j, i, nj=nj: (i, j + nj)),
        ],
        out_specs=pl.BlockSpec((tm, tn), lambda j, i: (i, j)),
        out_shape=jax.ShapeDtypeStruct((T, N), BF16),
        scratch_shapes=[pltpu.VMEM((Ka, tn), BF16), pltpu.VMEM((Kb, tn), BF16)],
        compiler_params=_cparams(("parallel", "arbitrary")),
        name="branch_merge",
    )(ya, yb, wa, wb, gates, gates)


ROUTE_LANES = LANES


def _pack_halves(x_bf):
    bits = pltpu.bitcast(x_bf.astype(F32), jnp.uint32)
    n = bits.shape[1] // 2
    return (bits[:, :n] >> 16) | (bits[:, n:] & jnp.uint32(0xFFFF0000))


def _unpack_halves(p):
    lo = pltpu.bitcast(p << 16, F32)
    hi = pltpu.bitcast(p & jnp.uint32(0xFFFF0000), F32)
    return jnp.concatenate([lo, hi], axis=1)


ROW_TILE = 8


def _store_row_tiles(ref, packed):
    m = packed.shape[0]
    assert packed.shape[1] == ROW_TILE * LANES
    for c in range(ROW_TILE):
        ref[pl.ds(c, m, stride=ROW_TILE), :] = packed[:, c * LANES:(c + 1) * LANES]


def _load_row_tiles(ref, m):
    return jnp.concatenate([ref[pl.ds(c, m, stride=ROW_TILE), :] for c in range(ROW_TILE)], axis=1)


def _outproj_kernel(m_ref, w_ref, x_ref, g_ref, wrc_ref, br_ref, x1_ref, h2_ref, rt_ref):
    x1 = x_ref[...] + jnp.dot(m_ref[...], w_ref[...], preferred_element_type=F32)
    x1_ref[...] = x1
    ms = jnp.mean(x1 * x1, axis=-1, keepdims=True)
    h2 = x1 * lax.rsqrt(ms + NORM_EPS) * g_ref[...]
    h_hi = h2.astype(BF16)
    _store_row_tiles(h2_ref, _pack_halves(h_hi))
    h_lo = (h2 - h_hi.astype(F32)).astype(BF16)
    r = jnp.dot(h_hi, wrc_ref[...], preferred_element_type=F32)
    lg = (r[:, :ROUTE_LANES] + r[:, ROUTE_LANES:]
          + jnp.dot(h_lo, wrc_ref[:, :ROUTE_LANES], preferred_element_type=F32)) + br_ref[...]

    lane = lax.broadcasted_iota(jnp.int32, lg.shape, 1)
    lane_f = lane.astype(F32)
    big = float(ROUTE_LANES)
    is_group = (lane >= MOE_EXPERTS) & (lane < MOE_EXPERTS + MOE_GROUPS)
    lgg = jnp.where(is_group, lg, -jnp.inf)
    gmax = jnp.max(lgg, axis=-1, keepdims=True)
    gsel = jnp.min(jnp.where(lgg == gmax, lane_f - MOE_EXPERTS, big), axis=-1, keepdims=True)
    pg = 1.0 / jnp.sum(jnp.where(is_group, jnp.exp(lg - gmax), 0.0), axis=-1, keepdims=True)

    in_group = (lane < MOE_EXPERTS) & ((lane // MOE_EXPERTS_PER_GROUP).astype(F32) == gsel)
    le = jnp.where(in_group, lg, -jnp.inf)
    t1 = jnp.max(le, axis=-1, keepdims=True)
    i1 = jnp.min(jnp.where(le == t1, lane_f, big), axis=-1, keepdims=True)
    le2 = jnp.where(lane_f == i1, -jnp.inf, le)
    t2 = jnp.max(le2, axis=-1, keepdims=True)
    i2 = jnp.min(jnp.where(le2 == t2, lane_f, big), axis=-1, keepdims=True)
    e2 = jnp.exp(t2 - t1)
    w1 = pg / (1.0 + e2)
    w2 = pg * e2 / (1.0 + e2)
    rt_ref[...] = jnp.where(lane == 0, i1, jnp.where(lane == 1, i2, jnp.where(lane == 2, w1,
                            jnp.where(lane == 3, w2, 0.0))))


def _outproj(merged, w_out, x2d, gain, wr_cat, br, tm=512):
    T, D = x2d.shape
    row = lambda i: (i, 0)
    const = lambda i: (0, 0)
    return pl.pallas_call(
        _outproj_kernel,
        grid=(T // tm,),
        in_specs=[
            pl.BlockSpec((tm, D), row), pl.BlockSpec((D, D), const), pl.BlockSpec((tm, D), row),
            pl.BlockSpec((1, D), const), pl.BlockSpec((D, 2 * ROUTE_LANES), const),
            pl.BlockSpec((1, ROUTE_LANES), const),
        ],
        out_specs=[pl.BlockSpec((tm, D), row), pl.BlockSpec((tm * ROW_TILE, LANES), row),
                   pl.BlockSpec((tm, ROUTE_LANES), row)],
        out_shape=[jax.ShapeDtypeStruct((T, D), F32), jax.ShapeDtypeStruct((T * ROW_TILE, LANES), jnp.uint32),
                   jax.ShapeDtypeStruct((T, ROUTE_LANES), F32)],
        compiler_params=_cparams(("parallel",)),
        name="outproj_router",
    )(merged, w_out, x2d, gain.reshape(1, D).astype(F32), wr_cat, br)


def _dispatch_kernel(dest_ref, zb_ref, h_ref, xs_ref, zeros, sem, *, tq):
    base = pl.program_id(0) * tq
    block_rows = MOE_ROWS * ROW_TILE
    n_blocks = xs_ref.shape[0] // block_rows

    @pl.when(pl.program_id(0) == 0)
    def _():
        zeros[...] = jnp.zeros_like(zeros)

        def zero_copy(b):
            return pltpu.make_async_copy(zeros, xs_ref.at[pl.ds(pl.multiple_of(b * block_rows, block_rows), block_rows)],
                                         sem)

        def start(b, carry):
            @pl.when(zb_ref[b] != 0)
            def _():
                zero_copy(b).start()
            return carry

        def wait(b, carry):
            @pl.when(zb_ref[b] != 0)
            def _():
                zero_copy(b).wait()
            return carry

        lax.fori_loop(0, n_blocks, start, 0)
        lax.fori_loop(0, n_blocks, wait, 0)

    def token_rows(t):
        return pl.ds(pl.multiple_of(t * ROW_TILE, ROW_TILE), ROW_TILE)

    def copy(j, k):
        return pltpu.make_async_copy(h_ref.at[token_rows(j)], xs_ref.at[token_rows(dest_ref[2 * (base + j) + k])], sem)

    def issue(j, carry):
        copy(j, 0).start()
        copy(j, 1).start()
        return carry

    lax.fori_loop(0, tq, issue, 0)
    for _ in range(MOE_TOPK):
        pltpu.make_async_copy(h_ref, xs_ref.at[pl.ds(0, tq * ROW_TILE)], sem).wait()


def _dispatch(h2, dest_flat, zero_block, n_rows, tq=512):
    T = h2.shape[0] // ROW_TILE
    return pl.pallas_call(
        functools.partial(_dispatch_kernel, tq=tq),
        grid_spec=pltpu.PrefetchScalarGridSpec(
            num_scalar_prefetch=2,
            grid=(T // tq,),
            in_specs=[pl.BlockSpec((tq * ROW_TILE, LANES), lambda i, d, z: (i, 0))],
            out_specs=pl.BlockSpec(memory_space=pl.ANY),
            scratch_shapes=[pltpu.VMEM((MOE_ROWS * ROW_TILE, LANES), h2.dtype), pltpu.SemaphoreType.DMA(())],
        ),
        out_shape=jax.ShapeDtypeStruct((n_rows * ROW_TILE, LANES), h2.dtype),
        compiler_params=_cparams(("arbitrary",)),
        name="moe_dispatch",
    )(dest_flat, zero_block, h2)


MOE_WEIGHT_SLOTS = 2


def _expert_kernel(be_ref, nu_ref, rank_ref, kth_ref, xs_ref, wg_hbm, wu_hbm, wd_hbm, o_ref,
                   wg_f, wu_f, wd_f, wg_s, wu_s, wd_s, sem):
    i = pl.program_id(0)
    e = be_ref[i]
    active = i < nu_ref[0]
    new_expert = active & ((i == 0) | (e != be_ref[jnp.maximum(i - 1, 0)]))
    k = rank_ref[e]
    slot = k % MOE_WEIGHT_SLOTS
    ahead = MOE_WEIGHT_SLOTS - 1

    def weight_copies(expert, s):
        return (pltpu.make_async_copy(wg_hbm.at[expert], wg_f.at[s], sem.at[s, 0]),
                pltpu.make_async_copy(wu_hbm.at[expert], wu_f.at[s], sem.at[s, 1]),
                pltpu.make_async_copy(wd_hbm.at[expert], wd_f.at[s], sem.at[s, 2]))

    def start_kth(j):
        nxt = kth_ref[j]

        @pl.when(nxt >= 0)
        def _():
            for c in weight_copies(nxt, j % MOE_WEIGHT_SLOTS):
                c.start()

    @pl.when(active & (i == 0))
    def _():
        for j in range(ahead):
            start_kth(j)

    @pl.when(new_expert)
    def _():
        start_kth(k + ahead)
        for c in weight_copies(e, slot):
            c.wait()
        wg_s[...] = wg_f[slot].astype(BF16)
        wu_s[...] = wu_f[slot].astype(BF16)
        wd_s[...] = wd_f[slot].astype(BF16)

    @pl.when(active)
    def _():
        x = _unpack_halves(_load_row_tiles(xs_ref, MOE_ROWS)).astype(BF16)
        g = jnp.dot(x, wg_s[...], preferred_element_type=F32)
        u = jnp.dot(x, wu_s[...], preferred_element_type=F32)
        a = (g * jax.nn.sigmoid(g) * u).astype(BF16)
        y = jnp.dot(a, wd_s[...], preferred_element_type=F32)
        _store_row_tiles(o_ref, _pack_halves(y.astype(BF16)))

    @pl.when(jnp.logical_not(active))
    def _():
        o_ref[...] = jnp.zeros_like(o_ref)


def _experts(xs, block_expert, n_used, expert_rank, kth_expert, w_gate, w_up, w_down):
    Dh = xs.shape[1]
    D, F = w_gate.shape[1], w_gate.shape[2]
    assert Dh == LANES and D == 2 * ROW_TILE * LANES
    nblk = block_expert.shape[0]
    P = nblk * MOE_ROWS * ROW_TILE
    rows = lambda i, be, nu, nx, sl: (jnp.minimum(i, nu[0] - 1), 0)
    hbm = pl.BlockSpec(memory_space=pl.ANY)
    return pl.pallas_call(
        _expert_kernel,
        grid_spec=pltpu.PrefetchScalarGridSpec(
            num_scalar_prefetch=4,
            grid=(nblk,),
            in_specs=[pl.BlockSpec((MOE_ROWS * ROW_TILE, Dh), rows), hbm, hbm, hbm],
            out_specs=pl.BlockSpec((MOE_ROWS * ROW_TILE, Dh), lambda i, be, nu, nx, sl: (i, 0)),
            scratch_shapes=[pltpu.VMEM((MOE_WEIGHT_SLOTS, D, F), F32), pltpu.VMEM((MOE_WEIGHT_SLOTS, D, F), F32),
                            pltpu.VMEM((MOE_WEIGHT_SLOTS, F, D), F32),
                            pltpu.VMEM((D, F), BF16), pltpu.VMEM((D, F), BF16), pltpu.VMEM((F, D), BF16),
                            pltpu.SemaphoreType.DMA((MOE_WEIGHT_SLOTS, 3))],
        ),
        out_shape=jax.ShapeDtypeStruct((P, Dh), jnp.uint32),
        compiler_params=_cparams(("arbitrary",)),
        name="moe_experts",
    )(block_expert, n_used, expert_rank, kth_expert, xs, w_gate, w_up, w_down)


def _combine_kernel(dest_ref, x1_ref, rt_ref, g_ref, ys_ref, o_ref, buf, sem, *, tq):
    i = pl.program_id(0)
    n = pl.num_programs(0)

    def token_rows(t):
        return pl.ds(pl.multiple_of(t * ROW_TILE, ROW_TILE), ROW_TILE)

    def issue(step, s):
        def body(j, carry):
            for k in range(MOE_TOPK):
                pltpu.make_async_copy(ys_ref.at[token_rows(dest_ref[2 * (step * tq + j) + k])],
                                      buf.at[s, k, token_rows(j)], sem.at[s]).start()
            return carry
        lax.fori_loop(0, tq, body, 0)

    @pl.when(i == 0)
    def _():
        issue(0, 0)

    for s in range(2):
        @pl.when((i + 1 < n) & ((i + 1) % 2 == s))
        def _():
            issue(i + 1, s)

    for s in range(2):
        @pl.when(i % 2 == s)
        def _():
            for k in range(MOE_TOPK):
                pltpu.make_async_copy(ys_ref.at[pl.ds(0, tq * ROW_TILE)], buf.at[s, k], sem.at[s]).wait()
            rt = rt_ref[...]
            y0 = _unpack_halves(_load_row_tiles(buf.at[s, 0], tq))
            y1 = _unpack_halves(_load_row_tiles(buf.at[s, 1], tq))
            x = x1_ref[...] + rt[:, 2:3] * y0 + rt[:, 3:4] * y1
            ms = jnp.mean(x * x, axis=-1, keepdims=True)
            o_ref[...] = x * lax.rsqrt(ms + NORM_EPS) * g_ref[...]


def _combine(x1, route, gain, ys, dest_flat, tq=256):
    T, D = x1.shape
    return pl.pallas_call(
        functools.partial(_combine_kernel, tq=tq),
        grid_spec=pltpu.PrefetchScalarGridSpec(
            num_scalar_prefetch=1,
            grid=(T // tq,),
            in_specs=[
                pl.BlockSpec((tq, D), lambda i, d: (i, 0)),
                pl.BlockSpec((tq, ROUTE_LANES), lambda i, d: (i, 0)),
                pl.BlockSpec((1, D), lambda i, d: (0, 0)),
                pl.BlockSpec(memory_space=pl.ANY),
            ],
            out_specs=pl.BlockSpec((tq, D), lambda i, d: (i, 0)),
            scratch_shapes=[pltpu.VMEM((2, MOE_TOPK, tq * ROW_TILE, LANES), jnp.uint32),
                            pltpu.SemaphoreType.DMA((2,))],
        ),
        out_shape=jax.ShapeDtypeStruct((T, D), F32),
        compiler_params=_cparams(("arbitrary",)),
        name="moe_combine",
    )(dest_flat, x1, route, gain.reshape(1, D).astype(F32), ys)


def _route_metadata(route, T):
    e = route[:, :MOE_TOPK].astype(jnp.int32)
    ids = jnp.arange(MOE_EXPERTS, dtype=jnp.int32)
    oh0 = e[:, 0, None] == ids
    oh1 = e[:, 1, None] == ids
    onehot = (oh0 | oh1).astype(jnp.int32)
    incl = jnp.cumsum(onehot, axis=0)
    counts = incl[-1]
    pcounts = (counts + MOE_ROWS - 1) // MOE_ROWS * MOE_ROWS
    pends = jnp.cumsum(pcounts)
    poffs = pends - pcounts
    slot = incl - onehot + poffs[None, :]
    dest = jnp.stack([jnp.sum(jnp.where(oh0, slot, 0), axis=1), jnp.sum(jnp.where(oh1, slot, 0), axis=1)], axis=1)
    nblk = (T * MOE_TOPK) // MOE_ROWS + MOE_EXPERTS
    first_row = jnp.arange(nblk, dtype=jnp.int32) * MOE_ROWS
    block_expert = jnp.minimum(jnp.sum((pends[None, :] <= first_row[:, None]).astype(jnp.int32), axis=1),
                               MOE_EXPERTS - 1).astype(jnp.int32)
    n_used = (pends[-1:] // MOE_ROWS).astype(jnp.int32)
    used = counts > 0
    expert_rank = (jnp.cumsum(used.astype(jnp.int32)) - 1).astype(jnp.int32)
    order = jnp.argsort(jnp.where(used, ids, ids + MOE_EXPERTS)).astype(jnp.int32)
    kth = jnp.where(ids < jnp.sum(used), order, -1)
    kth_expert = jnp.concatenate([kth, jnp.full((MOE_WEIGHT_SLOTS,), -1, jnp.int32)])
    blk = jnp.arange(nblk, dtype=jnp.int32)
    following = jnp.concatenate([block_expert[1:], block_expert[-1:]])
    zero_block = ((blk >= n_used[0] - 1) | (block_expert != following)).astype(jnp.int32)
    return dest.reshape(-1).astype(jnp.int32), zero_block, block_expert, n_used, expert_rank, kth_expert, nblk


def kernel(x, norm1_gain, w_in, hg_lb_logits, hg_norm_gain, rel_bias, w_branch_a, w_branch_b, w_out,
           norm2_gain, w_router_group, b_router_group, w_router_expert, b_router_expert,
           w_exp_gate, w_exp_up, w_exp_down, final_norm_gain):
    B, S, D = x.shape
    T = B * S
    depth = norm1_gain.shape[0]
    lower_bounds = jnp.cumsum(jax.nn.softmax(hg_lb_logits.astype(F32), axis=0), axis=0)
    att_w = ATT_HEADS * HEAD_DIM
    bias = _attn_bias(rel_bias)
    x2d = x.reshape(T, D)
    for layer in range(depth):
        zeros_lb = jnp.zeros((1, 3 * att_w), F32)
        hm = dict(head_major=True, B=B, S=S)
        h, q_a = _normproj(x2d, norm1_gain[layer], w_in, layer, D, B=B, S=S)
        lf_a = _inproj(h, w_in, layer, D, D, lower_bounds[layer].reshape(1, D), mode="forget", out_dtype=F32, **hm)
        i_a = _inproj(h, w_in, layer, 2 * D, D, zeros_lb, mode="none", out_dtype=BF16, **hm)
        sg_a = _inproj(h, w_in, layer, 3 * D, D, zeros_lb, mode="sigmoid", out_dtype=BF16, **hm)
        qkv_b = _inproj(h, w_in, layer, 4 * D, 3 * att_w, zeros_lb, mode="none", out_dtype=BF16, tn=att_w, **hm)
        gates = _inproj(h, w_in, layer, 4 * D + 3 * att_w, 2 * D, zeros_lb, mode="sigmoid",
                        head_major=False, out_dtype=BF16, B=B, S=S)

        y_a = _hgrn(q_a, lf_a, i_a, sg_a, hg_norm_gain[layer]).reshape(T, D)
        y_b = _attention(qkv_b, bias).reshape(T, ATT_HEADS_PER_GROUP * HEAD_DIM)
        merged = _merge(y_a, y_b, w_branch_a[layer], w_branch_b[layer], gates)

        n_pad = ROUTE_LANES - MOE_EXPERTS - MOE_GROUPS
        wr = jnp.concatenate([w_router_expert[layer].astype(F32), w_router_group[layer].astype(F32),
                              jnp.zeros((D, n_pad), F32)], axis=1)
        br = jnp.concatenate([b_router_expert[layer].astype(F32), b_router_group[layer].astype(F32),
                              jnp.zeros((n_pad,), F32)]).reshape(1, ROUTE_LANES)
        wr_hi = wr.astype(BF16)
        wr_lo = (wr - wr_hi.astype(F32)).astype(BF16)
        x1, h2, route = _outproj(merged, w_out[layer].astype(BF16), x2d, norm2_gain[layer],
                                 jnp.concatenate([wr_hi, wr_lo], axis=1), br)

        dest, zero_block, block_expert, n_used, expert_rank, kth_expert, nblk = _route_metadata(route, T)
        xs = _dispatch(h2, dest, zero_block, nblk * MOE_ROWS)
        ys = _experts(xs, block_expert, n_used, expert_rank, kth_expert,
                      w_exp_gate[layer], w_exp_up[layer], w_exp_down[layer])
        last = layer == depth - 1
        assert last, "the fused combine applies the final norm; deeper stacks need an un-normalised combine"
        x2d = _combine(x1, route, final_norm_gain, ys, dest)
    return x2d.reshape(B, S, D)
```

```python
import functools
import math

import numpy as np
import jax
import jax.numpy as jnp
from jax import lax
from jax.experimental import pallas as pl
from jax.experimental.pallas import tpu as pltpu

F32 = jnp.float32
BF16 = jnp.bfloat16

LANES = 128
NORM_EPS = 1e-6
HEAD_DIM = 128
ATT_GROUPS = ((128, 1), (512, 4), (2048, 16))
ATT_HEADS_PER_GROUP = 4
ATT_HEADS = len(ATT_GROUPS) * ATT_HEADS_PER_GROUP
ATT_BLOCK = 128
ATT_STACK = 16
REL_BUCKETS = 32
REL_MAX_DIST = 2048
MOE_GROUPS = 8
MOE_EXPERTS_PER_GROUP = 8
MOE_EXPERTS = MOE_GROUPS * MOE_EXPERTS_PER_GROUP
MOE_TOPK = 2
MOE_ROWS = 256
NEG_BIG = -1e30
VMEM_LIMIT = 56 * 1024 * 1024


def _cparams(sem):
    return pltpu.CompilerParams(dimension_semantics=sem, vmem_limit_bytes=VMEM_LIMIT)


def _normproj_kernel(x_ref, g_ref, w_ref, h_ref, o_ref, w_bf):
    @pl.when(pl.program_id(0) == 0)
    def _():
        w_bf[...] = w_ref[...].astype(BF16)

    x = x_ref[...]
    ms = jnp.mean(x * x, axis=-1, keepdims=True)
    h = (x * lax.rsqrt(ms + NORM_EPS) * g_ref[...]).astype(h_ref.dtype)
    h_ref[...] = h
    acc = jnp.dot(h, w_bf[...], preferred_element_type=F32)
    for hh in range(acc.shape[1] // HEAD_DIM):
        o_ref[0, hh] = acc[:, hh * HEAD_DIM:(hh + 1) * HEAD_DIM].astype(o_ref.dtype)


def _normproj(x2d, gain, w_all, layer, N, *, B, S, tm=512):
    T, D = x2d.shape
    assert S % tm == 0
    spb = S // tm
    return pl.pallas_call(
        _normproj_kernel,
        grid=(T // tm,),
        in_specs=[
            pl.BlockSpec((tm, D), lambda i: (i, 0)),
            pl.BlockSpec((1, D), lambda i: (0, 0)),
            pl.BlockSpec((None, D, N), lambda i: (layer, 0, 0), pipeline_mode=pl.Buffered(1)),
        ],
        out_specs=[
            pl.BlockSpec((tm, D), lambda i: (i, 0)),
            pl.BlockSpec((1, N // HEAD_DIM, tm, HEAD_DIM), lambda i: (i // spb, 0, i % spb, 0)),
        ],
        out_shape=[jax.ShapeDtypeStruct((T, D), BF16),
                   jax.ShapeDtypeStruct((B, N // HEAD_DIM, S, HEAD_DIM), BF16)],
        scratch_shapes=[pltpu.VMEM((D, N), BF16)],
        compiler_params=_cparams(("arbitrary",)),
        name="norm_inproj",
    )(x2d, gain.reshape(1, D).astype(F32), w_all)


def _inproj_kernel(h_ref, w_ref, lb_ref, o_ref, w_bf, *, mode, head_major):
    @pl.when(pl.program_id(1) == 0)
    def _():
        w_bf[...] = w_ref[0].astype(BF16)

    acc = jnp.dot(h_ref[...], w_bf[...], preferred_element_type=F32)
    if mode == "forget":
        lb = lb_ref[...]
        acc = lb + (1.0 - lb) * jax.nn.sigmoid(acc)
    elif mode == "sigmoid":
        acc = 0.5 * jnp.tanh(0.5 * acc) + 0.5
    if head_major:
        for hh in range(acc.shape[1] // HEAD_DIM):
            o_ref[0, hh] = acc[:, hh * HEAD_DIM:(hh + 1) * HEAD_DIM].astype(o_ref.dtype)
    else:
        o_ref[...] = acc.astype(o_ref.dtype)


def _inproj(h, w_all, layer, col0, N, lb, *, mode, head_major, out_dtype, B, S, tm=1024, tn=1024):
    T, D = h.shape
    assert N % tn == 0 and col0 % LANES == 0 and T % tm == 0 and S % tm == 0
    spb = S // tm
    if head_major:
        out_shape = jax.ShapeDtypeStruct((B, N // HEAD_DIM, S, HEAD_DIM), out_dtype)
        out_spec = pl.BlockSpec((1, tn // HEAD_DIM, tm, HEAD_DIM), lambda j, i: (i // spb, j, i % spb, 0))
    else:
        out_shape = jax.ShapeDtypeStruct((T, N), out_dtype)
        out_spec = pl.BlockSpec((tm, tn), lambda j, i: (i, j))
    return pl.pallas_call(
        functools.partial(_inproj_kernel, mode=mode, head_major=head_major),
        grid=(N // tn, T // tm),
        in_specs=[
            pl.BlockSpec((tm, D), lambda j, i: (i, 0)),
            pl.BlockSpec((pl.Element(1), pl.Element(D), pl.Element(tn)), lambda j, i: (layer, 0, pl.multiple_of(col0 + j * tn, LANES))),
            pl.BlockSpec((1, tn), lambda j, i: (0, j)),
        ],
        out_specs=out_spec,
        out_shape=out_shape,
        scratch_shapes=[pltpu.VMEM((D, tn), BF16)],
        compiler_params=_cparams(("parallel", "arbitrary")),
        name="inproj_" + mode,
    )(h, w_all, lb)


HG_CHUNK = 256
HG_DIAG = 16
HG_HEADS_PER_STEP = 4


def _hgrn_levels(C):
    out, m = [], C // 2
    while m >= HG_DIAG:
        out.append(m)
        m //= 2
    return out


def _hgrn_masks(C):
    t = np.arange(C)[:, None]
    s = np.arange(C)[None, :]
    masks = []
    for m in _hgrn_levels(C):
        masks.append((t // (2 * m) == s // (2 * m)) & ((t // m) % 2 == 1) & ((s // m) % 2 == 0))
    masks.append((t // HG_DIAG == s // HG_DIAG) & (t >= s))
    total = np.sum(np.stack(masks).astype(np.int32), axis=0)
    assert np.array_equal(total, (t >= s).astype(np.int32))
    return np.stack(masks).astype(np.float32)


def _dot_nt(a, b):
    return lax.dot_general(a, b, (((1,), (1,)), ((), ())), preferred_element_type=F32)


def _dot_tn(a, b):
    return lax.dot_general(a, b, (((0,), (0,)), ((), ())), preferred_element_type=F32)


def _hgrn_kernel(q_ref, f_ref, v_ref, sg_ref, gain_ref, tril_ref, mask_ref, o_ref, st_ref, b_ref, *, C):
    S = q_ref.shape[2]
    HP = q_ref.shape[1]
    levels = _hgrn_levels(C)
    nchunks = S // C
    tril = tril_ref[...]

    def ref_rows(h, block, row_of_block):
        parts = [jnp.broadcast_to(b_ref[h, pl.ds(row_of_block(p), 1), :], (block, HEAD_DIM))
                 for p in range(C // block)]
        return jnp.concatenate(parts, axis=0)

    def cumsum(c):
        r0 = pl.multiple_of(c * C, C)
        out = []
        for h in range(HP):
            lf2 = jnp.log2(f_ref[0, h, pl.ds(r0, C), :])
            hi = lf2.astype(BF16)
            lo = (lf2 - hi.astype(F32)).astype(BF16)
            r = jnp.dot(tril, jnp.concatenate([hi, lo], axis=1), preferred_element_type=F32)
            out.append(r[:, :HEAD_DIM] + r[:, HEAD_DIM:])
        return tuple(out)

    st_ref[...] = jnp.zeros_like(st_ref)

    def chunk(c, b_all):
        r0 = pl.multiple_of(c * C, C)
        rows = pl.ds(r0, C)
        b_next = cumsum(jnp.minimum(c + 1, nchunks - 1))
        for h in range(HP):
            b = b_all[h]
            b_ref[h] = b
            q = q_ref[0, h, rows, :].astype(F32)
            kk = 1.0 - f_ref[0, h, rows, :]
            b_last = b_ref[h, pl.ds(C - 1, 1), :]
            scores = jnp.zeros((C, C), BF16)
            half = C // 2
            ref_top = b_ref[h, pl.ds(half - 1, 1), :]
            a_top = (q[half:] * jnp.exp2(b[half:] - ref_top)).astype(BF16)
            k_top = (kk[:half] * jnp.exp2(ref_top - b[:half])).astype(BF16)
            s_top = _dot_nt(a_top, k_top).astype(BF16)
            for li, m in enumerate(levels):
                if m == half:
                    continue
                parts = []
                for p in range(C // (2 * m)):
                    ref = b_ref[h, pl.ds(2 * m * p + m - 1, 1), :]
                    parts += [ref - b[2 * m * p:2 * m * p + m], b[2 * m * p + m:2 * m * (p + 1)] - ref]
                e = jnp.exp2(jnp.concatenate(parts, axis=0))
                s_l = _dot_nt((q * e).astype(BF16), (kk * e).astype(BF16)).astype(BF16)
                scores = scores + mask_ref[li] * s_l
            d = b - ref_rows(h, HG_DIAG, lambda p: HG_DIAG * p + HG_DIAG // 2 - 1)
            s_l = _dot_nt((q * jnp.exp2(d)).astype(BF16), (kk * jnp.exp2(-d)).astype(BF16)).astype(BF16)
            scores = scores + mask_ref[len(levels)] * s_l
            v = v_ref[0, h, rows, :]
            st = st_ref[h]
            qe = (q * jnp.exp2(b)).astype(BF16)
            kd = (kk * jnp.exp2(b_last - b)).astype(BF16)
            o = jnp.dot(scores, v, preferred_element_type=F32) + _dot_nt(qe, st.astype(BF16))
            o_top = jnp.dot(s_top, v[:half], preferred_element_type=F32)
            o = jnp.concatenate([o[:half], o[half:] + o_top], axis=0)
            st_ref[h] = st * jnp.exp2(b_last) + _dot_tn(v, kd)
            ms = jnp.mean(o * o, axis=-1, keepdims=True)
            y = o * lax.rsqrt(ms + NORM_EPS) * gain_ref[:, h * HEAD_DIM:(h + 1) * HEAD_DIM]
            y = y * sg_ref[0, h, rows, :].astype(F32)
            o_ref[0, rows, h * HEAD_DIM:(h + 1) * HEAD_DIM] = y.astype(o_ref.dtype)
        return b_next

    lax.fori_loop(0, nchunks, chunk, cumsum(0))


def _hgrn(q, lf, v, sg, gain):
    B, H, S, E = q.shape
    C, HP = HG_CHUNK, HG_HEADS_PER_STEP
    masks = jnp.asarray(_hgrn_masks(C), dtype=BF16)
    tril = jnp.asarray(np.tril(np.ones((C, C), np.float32)), dtype=BF16)
    head_spec = pl.BlockSpec((1, HP, S, E), lambda b, h: (b, h, 0, 0))
    return pl.pallas_call(
        functools.partial(_hgrn_kernel, C=C),
        grid=(B, H // HP),
        in_specs=[
            head_spec, head_spec, head_spec, head_spec,
            pl.BlockSpec((1, HP * E), lambda b, h: (0, h)),
            pl.BlockSpec((C, C), lambda b, h: (0, 0)),
            pl.BlockSpec(masks.shape, lambda b, h: (0, 0, 0)),
        ],
        out_specs=pl.BlockSpec((1, S, HP * E), lambda b, h: (b, 0, h)),
        out_shape=jax.ShapeDtypeStruct((B, S, H * E), BF16),
        scratch_shapes=[pltpu.VMEM((HP, E, E), F32), pltpu.VMEM((HP, C, E), F32)],
        compiler_params=_cparams(("parallel", "parallel")),
        name="hgrn2",
    )(q, lf, v, sg, gain.reshape(1, H * E).astype(F32), tril, masks)


def _t5_bucket_np(dist):
    exact = REL_BUCKETS // 2
    d_f = np.maximum(dist, 1).astype(np.float32)
    log_b = exact + (np.log(d_f / np.float32(exact)) / np.float32(math.log(REL_MAX_DIST / exact))
                     * np.float32(REL_BUCKETS - exact)).astype(np.int32)
    return np.where(dist < exact, dist, np.minimum(log_b, REL_BUCKETS - 1))


def _attn_bias(rel_bias):
    blk = ATT_BLOCK
    period = 3 * blk
    out = []
    for gi, (window, dilation) in enumerate(ATT_GROUPS):
        n_back = window // dilation
        assert n_back <= blk
        hs = slice(gi * ATT_HEADS_PER_GROUP, (gi + 1) * ATT_HEADS_PER_GROUP)
        bucket = _t5_bucket_np(np.arange(n_back + 1) * dilation)
        by_delta = rel_bias[:, hs][bucket].astype(F32).T
        u = jnp.full((ATT_HEADS_PER_GROUP, period), NEG_BIG, F32)
        u = u.at[:, 2 * blk - 1 - n_back:2 * blk].set(by_delta[:, ::-1])
        flat = jnp.tile(u, (1, blk))[:, :blk * (period - 1)]
        out.append(flat.reshape(ATT_HEADS_PER_GROUP, blk, period - 1)[:, :, blk - 1:3 * blk - 1])
    return jnp.stack(out, axis=0)


def _attn_kernel(q_ref, k_ref, v_ref, bias_ref, o_ref, qf, kf, vf, og, lg):
    S = q_ref.shape[3]
    scale = HEAD_DIM ** -0.5
    blk = ATT_BLOCK

    for g, (window, d) in enumerate(ATT_GROUPS):
        L = S // d
        nb = L // blk
        if d > 1:
            qf[...] = q_ref[0, g, 0].astype(F32)
            kf[...] = k_ref[0, g, 0].astype(F32)
            vf[...] = v_ref[0, g, 0].astype(F32)

        def load(ref_bf, ref_f32, start, size, g=g, d=d):
            if d == 1:
                return ref_bf[0, g, 0, pl.ds(start, size), :]
            return ref_f32[pl.ds(start, size, stride=d), :].astype(BF16)

        def scores(r, n, g=g, d=d, load=load):
            start = r + n * blk * d
            q = load(q_ref, qf, start, blk)
            if n == 0:
                k = load(k_ref, kf, start, blk)
                v = load(v_ref, vf, start, blk)
                s = _dot_nt(q, k) * scale + bias_ref[g, 0, :, blk:]
            else:
                first = start - blk * d
                k = load(k_ref, kf, first, 2 * blk)
                v = load(v_ref, vf, first, 2 * blk)
                s = _dot_nt(q, k) * scale + bias_ref[g, 0]
            return start, s, v

        def softmax_pv(items, g=g, d=d):
            s = jnp.concatenate([it[1] for it in items], axis=0)
            m = jnp.max(s, axis=-1, keepdims=True)
            p = jnp.exp(s - m)
            den = jnp.sum(p, axis=-1, keepdims=True)
            pb = p.astype(BF16)
            lse = jnp.broadcast_to(m + jnp.log(den), (s.shape[0], HEAD_DIM))
            for j, (start, _, v) in enumerate(items):
                sl = slice(j * blk, (j + 1) * blk)
                rows = pl.ds(start, blk, stride=d) if d > 1 else pl.ds(start, blk)
                og[g, rows, :] = jnp.dot(pb[sl], v, preferred_element_type=F32) / den[sl]
                lg[g, rows, :] = lse[sl]

        blocks = [(r, n) for r in range(d) for n in range(nb)]
        first_blocks = [bn for bn in blocks if bn[1] == 0]
        later_blocks = [bn for bn in blocks if bn[1] > 0]
        for group in (first_blocks, later_blocks):
            for j0 in range(0, len(group), ATT_STACK):
                softmax_pv([scores(r, n) for r, n in group[j0:j0 + ATT_STACK]])

    rows_per_step = 256

    def mix(i, carry):
        rows = pl.ds(pl.multiple_of(i * rows_per_step, rows_per_step), rows_per_step)
        l0, l1, l2 = lg[0, rows, :], lg[1, rows, :], lg[2, rows, :]
        mx = jnp.maximum(jnp.maximum(l0, l1), l2)
        e0, e1, e2 = jnp.exp(l0 - mx), jnp.exp(l1 - mx), jnp.exp(l2 - mx)
        num = e0 * og[0, rows, :] + e1 * og[1, rows, :] + e2 * og[2, rows, :]
        o_ref[0, rows, :] = (num / (e0 + e1 + e2)).astype(o_ref.dtype)
        return carry

    lax.fori_loop(0, S // rows_per_step, mix, 0)


def _attention(qkv, bias):
    B, _, S, E = qkv.shape
    G, HG = len(ATT_GROUPS), ATT_HEADS_PER_GROUP
    x = qkv.reshape(B, 3, G, HG, S, E)

    def spec(which):
        return pl.BlockSpec((1, None, G, 1, S, E), lambda b, h, which=which: (b, which, 0, h, 0, 0))

    def kern(q_ref, k_ref, v_ref, bias_ref, o_ref, *scratch):
        _attn_kernel(q_ref, k_ref, v_ref, bias_ref, o_ref, *scratch)

    return pl.pallas_call(
        kern,
        grid=(B, HG),
        in_specs=[spec(0), spec(1), spec(2),
                  pl.BlockSpec((G, 1, ATT_BLOCK, 2 * ATT_BLOCK), lambda b, h: (0, h, 0, 0))],
        out_specs=pl.BlockSpec((1, S, E), lambda b, h: (b, 0, h)),
        out_shape=jax.ShapeDtypeStruct((B, S, HG * E), BF16),
        scratch_shapes=[pltpu.VMEM((S, E), F32)] * 3 + [pltpu.VMEM((G, S, E), F32)] * 2,
        compiler_params=_cparams(("parallel", "parallel")),
        name="dilated_attn",
    )(x, x, x, bias)


def _merge_kernel(ya_ref, yb_ref, wa_ref, wb_ref, ga_ref, gb_ref, o_ref, wa_bf, wb_bf):
    @pl.when(pl.program_id(1) == 0)
    def _():
        wa_bf[...] = wa_ref[...].astype(BF16)
        wb_bf[...] = wb_ref[...].astype(BF16)

    a = jnp.dot(ya_ref[...], wa_bf[...], preferred_element_type=F32)
    b = jnp.dot(yb_ref[...], wb_bf[...], preferred_element_type=F32)
    o_ref[...] = (ga_ref[...].astype(F32) * a + gb_ref[...].astype(F32) * b).astype(o_ref.dtype)


def _merge(ya, yb, wa, wb, gates, tm=1024, tn=1024):
    T, Ka = ya.shape
    Kb = yb.shape[1]
    N = wa.shape[1]
    nj = N // tn
    return pl.pallas_call(
        _merge_kernel,
        grid=(nj, T // tm),
        in_specs=[
            pl.BlockSpec((tm, Ka), lambda j, i: (i, 0)),
            pl.BlockSpec((tm, Kb), lambda j, i: (i, 0)),
            pl.BlockSpec((Ka, tn), lambda j, i: (0, j)),
            pl.BlockSpec((Kb, tn), lambda j, i: (0, j)),
            pl.BlockSpec((tm, tn), lambda j, i: (i, j)),
            pl.BlockSpec((tm, tn), lambda j, i, nj=nj: (i, j + nj)),
        ],
        out_specs=pl.BlockSpec((tm, tn), lambda j, i: (i, j)),
        out_shape=jax.ShapeDtypeStruct((T, N), BF16),
        scratch_shapes=[pltpu.VMEM((Ka, tn), BF16), pltpu.VMEM((Kb, tn), BF16)],
        compiler_params=_cparams(("parallel", "arbitrary")),
        name="branch_merge",
    )(ya, yb, wa, wb, gates, gates)


ROUTE_LANES = LANES


def _pack_halves(x_bf):
    bits = pltpu.bitcast(x_bf.astype(F32), jnp.uint32)
    n = bits.shape[1] // 2
    return (bits[:, :n] >> 16) | (bits[:, n:] & jnp.uint32(0xFFFF0000))


def _unpack_halves(p):
    lo = pltpu.bitcast(p << 16, F32)
    hi = pltpu.bitcast(p & jnp.uint32(0xFFFF0000), F32)
    return jnp.concatenate([lo, hi], axis=1)


ROW_TILE = 8


def _store_row_tiles(ref, packed):
    m = packed.shape[0]
    assert packed.shape[1] == ROW_TILE * LANES
    for c in range(ROW_TILE):
        ref[pl.ds(c, m, stride=ROW_TILE), :] = packed[:, c * LANES:(c + 1) * LANES]


def _load_row_tiles(ref, m):
    return jnp.concatenate([ref[pl.ds(c, m, stride=ROW_TILE), :] for c in range(ROW_TILE)], axis=1)


def _outproj_kernel(m_ref, w_ref, x_ref, g_ref, wrc_ref, br_ref, x1_ref, h2_ref, rt_ref):
    x1 = x_ref[...] + jnp.dot(m_ref[...], w_ref[...], preferred_element_type=F32)
    x1_ref[...] = x1
    ms = jnp.mean(x1 * x1, axis=-1, keepdims=True)
    h2 = x1 * lax.rsqrt(ms + NORM_EPS) * g_ref[...]
    h_hi = h2.astype(BF16)
    _store_row_tiles(h2_ref, _pack_halves(h_hi))
    h_lo = (h2 - h_hi.astype(F32)).astype(BF16)
    r = jnp.dot(h_hi, wrc_ref[...], preferred_element_type=F32)
    lg = (r[:, :ROUTE_LANES] + r[:, ROUTE_LANES:]
          + jnp.dot(h_lo, wrc_ref[:, :ROUTE_LANES], preferred_element_type=F32)) + br_ref[...]

    lane = lax.broadcasted_iota(jnp.int32, lg.shape, 1)
    lane_f = lane.astype(F32)
    big = float(ROUTE_LANES)
    is_group = (lane >= MOE_EXPERTS) & (lane < MOE_EXPERTS + MOE_GROUPS)
    lgg = jnp.where(is_group, lg, -jnp.inf)
    gmax = jnp.max(lgg, axis=-1, keepdims=True)
    gsel = jnp.min(jnp.where(lgg == gmax, lane_f - MOE_EXPERTS, big), axis=-1, keepdims=True)
    pg = 1.0 / jnp.sum(jnp.where(is_group, jnp.exp(lg - gmax), 0.0), axis=-1, keepdims=True)

    in_group = (lane < MOE_EXPERTS) & ((lane // MOE_EXPERTS_PER_GROUP).astype(F32) == gsel)
    le = jnp.where(in_group, lg, -jnp.inf)
    t1 = jnp.max(le, axis=-1, keepdims=True)
    i1 = jnp.min(jnp.where(le == t1, lane_f, big), axis=-1, keepdims=True)
    le2 = jnp.where(lane_f == i1, -jnp.inf, le)
    t2 = jnp.max(le2, axis=-1, keepdims=True)
    i2 = jnp.min(jnp.where(le2 == t2, lane_f, big), axis=-1, keepdims=True)
    e2 = jnp.exp(t2 - t1)
    w1 = pg / (1.0 + e2)
    w2 = pg * e2 / (1.0 + e2)
    rt_ref[...] = jnp.where(lane == 0, i1, jnp.where(lane == 1, i2, jnp.where(lane == 2, w1,
                            jnp.where(lane == 3, w2, 0.0))))


def _outproj(merged, w_out, x2d, gain, wr_cat, br, tm=512):
    T, D = x2d.shape
    row = lambda i: (i, 0)
    const = lambda i: (0, 0)
    return pl.pallas_call(
        _outproj_kernel,
        grid=(T // tm,),
        in_specs=[
            pl.BlockSpec((tm, D), row), pl.BlockSpec((D, D), const), pl.BlockSpec((tm, D), row),
            pl.BlockSpec((1, D), const), pl.BlockSpec((D, 2 * ROUTE_LANES), const),
            pl.BlockSpec((1, ROUTE_LANES), const),
        ],
        out_specs=[pl.BlockSpec((tm, D), row), pl.BlockSpec((tm * ROW_TILE, LANES), row),
                   pl.BlockSpec((tm, ROUTE_LANES), row)],
        out_shape=[jax.ShapeDtypeStruct((T, D), F32), jax.ShapeDtypeStruct((T * ROW_TILE, LANES), jnp.uint32),
                   jax.ShapeDtypeStruct((T, ROUTE_LANES), F32)],
        compiler_params=_cparams(("parallel",)),
        name="outproj_router",
    )(merged, w_out, x2d, gain.reshape(1, D).astype(F32), wr_cat, br)


def _dispatch_kernel(dest_ref, zb_ref, h_ref, xs_ref, zeros, sem, *, tq):
    base = pl.program_id(0) * tq
    block_rows = MOE_ROWS * ROW_TILE
    n_blocks = xs_ref.shape[0] // block_rows

    @pl.when(pl.program_id(0) == 0)
    def _():
        zeros[...] = jnp.zeros_like(zeros)

        def zero_copy(b):
            return pltpu.make_async_copy(zeros, xs_ref.at[pl.ds(pl.multiple_of(b * block_rows, block_rows), block_rows)],
                                         sem)

        def start(b, carry):
            @pl.when(zb_ref[b] != 0)
            def _():
                zero_copy(b).start()
            return carry

        def wait(b, carry):
            @pl.when(zb_ref[b] != 0)
            def _():
                zero_copy(b).wait()
            return carry

        lax.fori_loop(0, n_blocks, start, 0)
        lax.fori_loop(0, n_blocks, wait, 0)

    def token_rows(t):
        return pl.ds(pl.multiple_of(t * ROW_TILE, ROW_TILE), ROW_TILE)

    def copy(j, k):
        return pltpu.make_async_copy(h_ref.at[token_rows(j)], xs_ref.at[token_rows(dest_ref[2 * (base + j) + k])], sem)

    def issue(j, carry):
        copy(j, 0).start()
        copy(j, 1).start()
        return carry

    lax.fori_loop(0, tq, issue, 0)
    for _ in range(MOE_TOPK):
        pltpu.make_async_copy(h_ref, xs_ref.at[pl.ds(0, tq * ROW_TILE)], sem).wait()


def _dispatch(h2, dest_flat, zero_block, n_rows, tq=512):
    T = h2.shape[0] // ROW_TILE
    return pl.pallas_call(
        functools.partial(_dispatch_kernel, tq=tq),
        grid_spec=pltpu.PrefetchScalarGridSpec(
            num_scalar_prefetch=2,
            grid=(T // tq,),
            in_specs=[pl.BlockSpec((tq * ROW_TILE, LANES), lambda i, d, z: (i, 0))],
            out_specs=pl.BlockSpec(memory_space=pl.ANY),
            scratch_shapes=[pltpu.VMEM((MOE_ROWS * ROW_TILE, LANES), h2.dtype), pltpu.SemaphoreType.DMA(())],
        ),
        out_shape=jax.ShapeDtypeStruct((n_rows * ROW_TILE, LANES), h2.dtype),
        compiler_params=_cparams(("arbitrary",)),
        name="moe_dispatch",
    )(dest_flat, zero_block, h2)


MOE_WEIGHT_SLOTS = 2


def _expert_kernel(be_ref, nu_ref, rank_ref, kth_ref, xs_ref, wg_hbm, wu_hbm, wd_hbm, o_ref,
                   wg_f, wu_f, wd_f, wg_s, wu_s, wd_s, sem):
    i = pl.program_id(0)
    e = be_ref[i]
    active = i < nu_ref[0]
    new_expert = active & ((i == 0) | (e != be_ref[jnp.maximum(i - 1, 0)]))
    k = rank_ref[e]
    slot = k % MOE_WEIGHT_SLOTS
    ahead = MOE_WEIGHT_SLOTS - 1

    def weight_copies(expert, s):
        return (pltpu.make_async_copy(wg_hbm.at[expert], wg_f.at[s], sem.at[s, 0]),
                pltpu.make_async_copy(wu_hbm.at[expert], wu_f.at[s], sem.at[s, 1]),
                pltpu.make_async_copy(wd_hbm.at[expert], wd_f.at[s], sem.at[s, 2]))

    def start_kth(j):
        nxt = kth_ref[j]

        @pl.when(nxt >= 0)
        def _():
            for c in weight_copies(nxt, j % MOE_WEIGHT_SLOTS):
                c.start()

    @pl.when(active & (i == 0))
    def _():
        for j in range(ahead):
            start_kth(j)

    @pl.when(new_expert)
    def _():
        start_kth(k + ahead)
        for c in weight_copies(e, slot):
            c.wait()
        wg_s[...] = wg_f[slot].astype(BF16)
        wu_s[...] = wu_f[slot].astype(BF16)
        wd_s[...] = wd_f[slot].astype(BF16)

    @pl.when(active)
    def _():
        x = _unpack_halves(_load_row_tiles(xs_ref, MOE_ROWS)).astype(BF16)
        g = jnp.dot(x, wg_s[...], preferred_element_type=F32)
        u = jnp.dot(x, wu_s[...], preferred_element_type=F32)
        a = (g * jax.nn.sigmoid(g) * u).astype(BF16)
        y = jnp.dot(a, wd_s[...], preferred_element_type=F32)
        _store_row_tiles(o_ref, _pack_halves(y.astype(BF16)))

    @pl.when(jnp.logical_not(active))
    def _():
        o_ref[...] = jnp.zeros_like(o_ref)


def _experts(xs, block_expert, n_used, expert_rank, kth_expert, w_gate, w_up, w_down):
    Dh = xs.shape[1]
    D, F = w_gate.shape[1], w_gate.shape[2]
    assert Dh == LANES and D == 2 * ROW_TILE * LANES
    nblk = block_expert.shape[0]
    P = nblk * MOE_ROWS * ROW_TILE
    rows = lambda i, be, nu, nx, sl: (jnp.minimum(i, nu[0] - 1), 0)
    hbm = pl.BlockSpec(memory_space=pl.ANY)
    return pl.pallas_call(
        _expert_kernel,
        grid_spec=pltpu.PrefetchScalarGridSpec(
            num_scalar_prefetch=4,
            grid=(nblk,),
            in_specs=[pl.BlockSpec((MOE_ROWS * ROW_TILE, Dh), rows), hbm, hbm, hbm],
            out_specs=pl.BlockSpec((MOE_ROWS * ROW_TILE, Dh), lambda i, be, nu, nx, sl: (i, 0)),
            scratch_shapes=[pltpu.VMEM((MOE_WEIGHT_SLOTS, D, F), F32), pltpu.VMEM((MOE_WEIGHT_SLOTS, D, F), F32),
                            pltpu.VMEM((MOE_WEIGHT_SLOTS, F, D), F32),
                            pltpu.VMEM((D, F), BF16), pltpu.VMEM((D, F), BF16), pltpu.VMEM((F, D), BF16),
                            pltpu.SemaphoreType.DMA((MOE_WEIGHT_SLOTS, 3))],
        ),
        out_shape=jax.ShapeDtypeStruct((P, Dh), jnp.uint32),
        compiler_params=_cparams(("arbitrary",)),
        name="moe_experts",
    )(block_expert, n_used, expert_rank, kth_expert, xs, w_gate, w_up, w_down)


def _combine_kernel(dest_ref, x1_ref, rt_ref, g_ref, ys_ref, o_ref, buf, sem, *, tq):
    i = pl.program_id(0)
    n = pl.num_programs(0)

    def token_rows(t):
        return pl.ds(pl.multiple_of(t * ROW_TILE, ROW_TILE), ROW_TILE)

    def issue(step, s):
        def body(j, carry):
            for k in range(MOE_TOPK):
                pltpu.make_async_copy(ys_ref.at[token_rows(dest_ref[2 * (step * tq + j) + k])],
                                      buf.at[s, k, token_rows(j)], sem.at[s]).start()
            return carry
        lax.fori_loop(0, tq, body, 0)

    @pl.when(i == 0)
    def _():
        issue(0, 0)

    for s in range(2):
        @pl.when((i + 1 < n) & ((i + 1) % 2 == s))
        def _():
            issue(i + 1, s)

    for s in range(2):
        @pl.when(i % 2 == s)
        def _():
            for k in range(MOE_TOPK):
                pltpu.make_async_copy(ys_ref.at[pl.ds(0, tq * ROW_TILE)], buf.at[s, k], sem.at[s]).wait()
            rt = rt_ref[...]
            y0 = _unpack_halves(_load_row_tiles(buf.at[s, 0], tq))
            y1 = _unpack_halves(_load_row_tiles(buf.at[s, 1], tq))
            x = x1_ref[...] + rt[:, 2:3] * y0 + rt[:, 3:4] * y1
            ms = jnp.mean(x * x, axis=-1, keepdims=True)
            o_ref[...] = x * lax.rsqrt(ms + NORM_EPS) * g_ref[...]


def _combine(x1, route, gain, ys, dest_flat, tq=256):
    T, D = x1.shape
    return pl.pallas_call(
        functools.partial(_combine_kernel, tq=tq),
        grid_spec=pltpu.PrefetchScalarGridSpec(
            num_scalar_prefetch=1,
            grid=(T // tq,),
            in_specs=[
                pl.BlockSpec((tq, D), lambda i, d: (i, 0)),
                pl.BlockSpec((tq, ROUTE_LANES), lambda i, d: (i, 0)),
                pl.BlockSpec((1, D), lambda i, d: (0, 0)),
                pl.BlockSpec(memory_space=pl.ANY),
            ],
            out_specs=pl.BlockSpec((tq, D), lambda i, d: (i, 0)),
            scratch_shapes=[pltpu.VMEM((2, MOE_TOPK, tq * ROW_TILE, LANES), jnp.uint32),
                            pltpu.SemaphoreType.DMA((2,))],
        ),
        out_shape=jax.ShapeDtypeStruct((T, D), F32),
        compiler_params=_cparams(("arbitrary",)),
        name="moe_combine",
    )(dest_flat, x1, route, gain.reshape(1, D).astype(F32), ys)


def _route_metadata(route, T):
    e = route[:, :MOE_TOPK].astype(jnp.int32)
    ids = jnp.arange(MOE_EXPERTS, dtype=jnp.int32)
    oh0 = e[:, 0, None] == ids
    oh1 = e[:, 1, None] == ids
    onehot = (oh0 | oh1).astype(jnp.int32)
    incl = jnp.cumsum(onehot, axis=0)
    counts = incl[-1]
    pcounts = (counts + MOE_ROWS - 1) // MOE_ROWS * MOE_ROWS
    pends = jnp.cumsum(pcounts)
    poffs = pends - pcounts
    slot = incl - onehot + poffs[None, :]
    dest = jnp.stack([jnp.sum(jnp.where(oh0, slot, 0), axis=1), jnp.sum(jnp.where(oh1, slot, 0), axis=1)], axis=1)
    nblk = (T * MOE_TOPK) // MOE_ROWS + MOE_EXPERTS
    first_row = jnp.arange(nblk, dtype=jnp.int32) * MOE_ROWS
    block_expert = jnp.minimum(jnp.sum((pends[None, :] <= first_row[:, None]).astype(jnp.int32), axis=1),
                               MOE_EXPERTS - 1).astype(jnp.int32)
    n_used = (pends[-1:] // MOE_ROWS).astype(jnp.int32)
    used = counts > 0
    expert_rank = (jnp.cumsum(used.astype(jnp.int32)) - 1).astype(jnp.int32)
    order = jnp.argsort(jnp.where(used, ids, ids + MOE_EXPERTS)).astype(jnp.int32)
    kth = jnp.where(ids < jnp.sum(used), order, -1)
    kth_expert = jnp.concatenate([kth, jnp.full((MOE_WEIGHT_SLOTS,), -1, jnp.int32)])
    blk = jnp.arange(nblk, dtype=jnp.int32)
    following = jnp.concatenate([block_expert[1:], block_expert[-1:]])
    zero_block = ((blk >= n_used[0] - 1) | (block_expert != following)).astype(jnp.int32)
    return dest.reshape(-1).astype(jnp.int32), zero_block, block_expert, n_used, expert_rank, kth_expert, nblk


def kernel(x, norm1_gain, w_in, hg_lb_logits, hg_norm_gain, rel_bias, w_branch_a, w_branch_b, w_out,
           norm2_gain, w_router_group, b_router_group, w_router_expert, b_router_expert,
           w_exp_gate, w_exp_up, w_exp_down, final_norm_gain):
    B, S, D = x.shape
    T = B * S
    depth = norm1_gain.shape[0]
    lower_bounds = jnp.cumsum(jax.nn.softmax(hg_lb_logits.astype(F32), axis=0), axis=0)
    att_w = ATT_HEADS * HEAD_DIM
    bias = _attn_bias(rel_bias)
    x2d = x.reshape(T, D)
    for layer in range(depth):
        zeros_lb = jnp.zeros((1, 3 * att_w), F32)
        hm = dict(head_major=True, B=B, S=S)
        h, q_a = _normproj(x2d, norm1_gain[layer], w_in, layer, D, B=B, S=S)
        lf_a = _inproj(h, w_in, layer, D, D, lower_bounds[layer].reshape(1, D), mode="forget", out_dtype=F32, **hm)
        i_a = _inproj(h, w_in, layer, 2 * D, D, zeros_lb, mode="none", out_dtype=BF16, **hm)
        sg_a = _inproj(h, w_in, layer, 3 * D, D, zeros_lb, mode="sigmoid", out_dtype=BF16, **hm)
        qkv_b = _inproj(h, w_in, layer, 4 * D, 3 * att_w, zeros_lb, mode="none", out_dtype=BF16, tn=att_w, **hm)
        gates = _inproj(h, w_in, layer, 4 * D + 3 * att_w, 2 * D, zeros_lb, mode="sigmoid",
                        head_major=False, out_dtype=BF16, B=B, S=S)

        y_a = _hgrn(q_a, lf_a, i_a, sg_a, hg_norm_gain[layer]).reshape(T, D)
        y_b = _attention(qkv_b, bias).reshape(T, ATT_HEADS_PER_GROUP * HEAD_DIM)
        merged = _merge(y_a, y_b, w_branch_a[layer], w_branch_b[layer], gates)

        n_pad = ROUTE_LANES - MOE_EXPERTS - MOE_GROUPS
        wr = jnp.concatenate([w_router_expert[layer].astype(F32), w_router_group[layer].astype(F32),
                              jnp.zeros((D, n_pad), F32)], axis=1)
        br = jnp.concatenate([b_router_expert[layer].astype(F32), b_router_group[layer].astype(F32),
                              jnp.zeros((n_pad,), F32)]).reshape(1, ROUTE_LANES)
        wr_hi = wr.astype(BF16)
        wr_lo = (wr - wr_hi.astype(F32)).astype(BF16)
        x1, h2, route = _outproj(merged, w_out[layer].astype(BF16), x2d, norm2_gain[layer],
                                 jnp.concatenate([wr_hi, wr_lo], axis=1), br)

        dest, zero_block, block_expert, n_used, expert_rank, kth_expert, nblk = _route_metadata(route, T)
        xs = _dispatch(h2, dest, zero_block, nblk * MOE_ROWS)
        ys = _experts(xs, block_expert, n_used, expert_rank, kth_expert,
                      w_exp_gate[layer], w_exp_up[layer], w_exp_down[layer])
        last = layer == depth - 1
        assert last, "the fused combine applies the final norm; deeper stacks need an un-normalised combine"
        x2d = _combine(x1, route, final_norm_gain, ys, dest)
    return x2d.reshape(B, S, D)
```

```python
import functools
import math

import numpy as np
import jax
import jax.numpy as jnp
from jax import lax
from jax.experimental import pallas as pl
from jax.experimental.pallas import tpu as pltpu

F32 = jnp.float32
BF16 = jnp.bfloat16

LANES = 128
NORM_EPS = 1e-6
HEAD_DIM = 128
ATT_GROUPS = ((128, 1), (512, 4), (2048, 16))
ATT_HEADS_PER_GROUP = 4
ATT_HEADS = len(ATT_GROUPS) * ATT_HEADS_PER_GROUP
ATT_BLOCK = 128
ATT_STACK = 16
REL_BUCKETS = 32
REL_MAX_DIST = 2048
MOE_GROUPS = 8
MOE_EXPERTS_PER_GROUP = 8
MOE_EXPERTS = MOE_GROUPS * MOE_EXPERTS_PER_GROUP
MOE_TOPK = 2
MOE_ROWS = 256
NEG_BIG = -1e30
VMEM_LIMIT = 56 * 1024 * 1024


def _cparams(sem):
    return pltpu.CompilerParams(dimension_semantics=sem, vmem_limit_bytes=VMEM_LIMIT)


def _normproj_kernel(x_ref, g_ref, w_ref, h_ref, o_ref, w_bf):
    @pl.when(pl.program_id(0) == 0)
    def _():
        w_bf[...] = w_ref[...].astype(BF16)

    x = x_ref[...]
    ms = jnp.mean(x * x, axis=-1, keepdims=True)
    h = (x * lax.rsqrt(ms + NORM_EPS) * g_ref[...]).astype(h_ref.dtype)
    h_ref[...] = h
    acc = jnp.dot(h, w_bf[...], preferred_element_type=F32)
    for hh in range(acc.shape[1] // HEAD_DIM):
        o_ref[0, hh] = acc[:, hh * HEAD_DIM:(hh + 1) * HEAD_DIM].astype(o_ref.dtype)


def _normproj(x2d, gain, w_all, layer, N, *, B, S, tm=512):
    T, D = x2d.shape
    assert S % tm == 0
    spb = S // tm
    return pl.pallas_call(
        _normproj_kernel,
        grid=(T // tm,),
        in_specs=[
            pl.BlockSpec((tm, D), lambda i: (i, 0)),
            pl.BlockSpec((1, D), lambda i: (0, 0)),
            pl.BlockSpec((None, D, N), lambda i: (layer, 0, 0), pipeline_mode=pl.Buffered(1)),
        ],
        out_specs=[
            pl.BlockSpec((tm, D), lambda i: (i, 0)),
            pl.BlockSpec((1, N // HEAD_DIM, tm, HEAD_DIM), lambda i: (i // spb, 0, i % spb, 0)),
        ],
        out_shape=[jax.ShapeDtypeStruct((T, D), BF16),
                   jax.ShapeDtypeStruct((B, N // HEAD_DIM, S, HEAD_DIM), BF16)],
        scratch_shapes=[pltpu.VMEM((D, N), BF16)],
        compiler_params=_cparams(("arbitrary",)),
        name="norm_inproj",
    )(x2d, gain.reshape(1, D).astype(F32), w_all)


def _inproj_kernel(h_ref, w_ref, lb_ref, o_ref, w_bf, *, mode, head_major):
    @pl.when(pl.program_id(1) == 0)
    def _():
        w_bf[...] = w_ref[0].astype(BF16)

    acc = jnp.dot(h_ref[...], w_bf[...], preferred_element_type=F32)
    if mode == "forget":
        lb = lb_ref[...]
        acc = lb + (1.0 - lb) * jax.nn.sigmoid(acc)
    elif mode == "sigmoid":
        acc = 0.5 * jnp.tanh(0.5 * acc) + 0.5
    if head_major:
        for hh in range(acc.shape[1] // HEAD_DIM):
            o_ref[0, hh] = acc[:, hh * HEAD_DIM:(hh + 1) * HEAD_DIM].astype(o_ref.dtype)
    else:
        o_ref[...] = acc.astype(o_ref.dtype)


def _inproj(h, w_all, layer, col0, N, lb, *, mode, head_major, out_dtype, B, S, tm=1024, tn=1024):
    T, D = h.shape
    assert N % tn == 0 and col0 % LANES == 0 and T % tm == 0 and S % tm == 0
    spb = S // tm
    if head_major:
        out_shape = jax.ShapeDtypeStruct((B, N // HEAD_DIM, S, HEAD_DIM), out_dtype)
        out_spec = pl.BlockSpec((1, tn // HEAD_DIM, tm, HEAD_DIM), lambda j, i: (i // spb, j, i % spb, 0))
    else:
        out_shape = jax.ShapeDtypeStruct((T, N), out_dtype)
        out_spec = pl.BlockSpec((tm, tn), lambda j, i: (i, j))
    return pl.pallas_call(
        functools.partial(_inproj_kernel, mode=mode, head_major=head_major),
        grid=(N // tn, T // tm),
        in_specs=[
            pl.BlockSpec((tm, D), lambda j, i: (i, 0)),
            pl.BlockSpec((pl.Element(1), pl.Element(D), pl.Element(tn)), lambda j, i: (layer, 0, pl.multiple_of(col0 + j * tn, LANES))),
            pl.BlockSpec((1, tn), lambda j, i: (0, j)),
        ],
        out_specs=out_spec,
        out_shape=out_shape,
        scratch_shapes=[pltpu.VMEM((D, tn), BF16)],
        compiler_params=_cparams(("parallel", "arbitrary")),
        name="inproj_" + mode,
    )(h, w_all, lb)


HG_CHUNK = 256
HG_DIAG = 16
HG_HEADS_PER_STEP = 4


def _hgrn_levels(C):
    out, m = [], C // 2
    while m >= HG_DIAG:
        out.append(m)
        m //= 2
    return out


def _hgrn_masks(C):
    t = np.arange(C)[:, None]
    s = np.arange(C)[None, :]
    masks = []
    for m in _hgrn_levels(C):
        masks.append((t // (2 * m) == s // (2 * m)) & ((t // m) % 2 == 1) & ((s // m) % 2 == 0))
    masks.append((t // HG_DIAG == s // HG_DIAG) & (t >= s))
    total = np.sum(np.stack(masks).astype(np.int32), axis=0)
    assert np.array_equal(total, (t >= s).astype(np.int32))
    return np.stack(masks).astype(np.float32)


def _dot_nt(a, b):
    return lax.dot_general(a, b, (((1,), (1,)), ((), ())), preferred_element_type=F32)


def _dot_tn(a, b):
    return lax.dot_general(a, b, (((0,), (0,)), ((), ())), preferred_element_type=F32)


def _hgrn_kernel(q_ref, f_ref, v_ref, sg_ref, gain_ref, tril_ref, mask_ref, o_ref, st_ref, b_ref, *, C):
    S = q_ref.shape[2]
    HP = q_ref.shape[1]
    levels = _hgrn_levels(C)
    nchunks = S // C
    tril = tril_ref[...]

    def ref_rows(h, block, row_of_block):
        parts = [jnp.broadcast_to(b_ref[h, pl.ds(row_of_block(p), 1), :], (block, HEAD_DIM))
                 for p in range(C // block)]
        return jnp.concatenate(parts, axis=0)

    def cumsum(c):
        r0 = pl.multiple_of(c * C, C)
        out = []
        for h in range(HP):
            lf2 = jnp.log2(f_ref[0, h, pl.ds(r0, C), :])
            hi = lf2.astype(BF16)
            lo = (lf2 - hi.astype(F32)).astype(BF16)
            r = jnp.dot(tril, jnp.concatenate([hi, lo], axis=1), preferred_element_type=F32)
            out.append(r[:, :HEAD_DIM] + r[:, HEAD_DIM:])
        return tuple(out)

    st_ref[...] = jnp.zeros_like(st_ref)

    def chunk(c, b_all):
        r0 = pl.multiple_of(c * C, C)
        rows = pl.ds(r0, C)
        b_next = cumsum(jnp.minimum(c + 1, nchunks - 1))
        for h in range(HP):
            b = b_all[h]
            b_ref[h] = b
            q = q_ref[0, h, rows, :].astype(F32)
            kk = 1.0 - f_ref[0, h, rows, :]
            b_last = b_ref[h, pl.ds(C - 1, 1), :]
            scores = jnp.zeros((C, C), BF16)
            half = C // 2
            ref_top = b_ref[h, pl.ds(half - 1, 1), :]
            a_top = (q[half:] * jnp.exp2(b[half:] - ref_top)).astype(BF16)
            k_top = (kk[:half] * jnp.exp2(ref_top - b[:half])).astype(BF16)
            s_top = _dot_nt(a_top, k_top).astype(BF16)
            for li, m in enumerate(levels):
                if m == half:
                    continue
                parts = []
                for p in range(C // (2 * m)):
                    ref = b_ref[h, pl.ds(2 * m * p + m - 1, 1), :]
                    parts += [ref - b[2 * m * p:2 * m * p + m], b[2 * m * p + m:2 * m * (p + 1)] - ref]
                e = jnp.exp2(jnp.concatenate(parts, axis=0))
                s_l = _dot_nt((q * e).astype(BF16), (kk * e).astype(BF16)).astype(BF16)
                scores = scores + mask_ref[li] * s_l
            d = b - ref_rows(h, HG_DIAG, lambda p: HG_DIAG * p + HG_DIAG // 2 - 1)
            s_l = _dot_nt((q * jnp.exp2(d)).astype(BF16), (kk * jnp.exp2(-d)).astype(BF16)).astype(BF16)
            scores = scores + mask_ref[len(levels)] * s_l
            v = v_ref[0, h, rows, :]
            st = st_ref[h]
            qe = (q * jnp.exp2(b)).astype(BF16)
            kd = (kk * jnp.exp2(b_last - b)).astype(BF16)
            o = jnp.dot(scores, v, preferred_element_type=F32) + _dot_nt(qe, st.astype(BF16))
            o_top = jnp.dot(s_top, v[:half], preferred_element_type=F32)
            o = jnp.concatenate([o[:half], o[half:] + o_top], axis=0)
            st_ref[h] = st * jnp.exp2(b_last) + _dot_tn(v, kd)
            ms = jnp.mean(o * o, axis=-1, keepdims=True)
            y = o * lax.rsqrt(ms + NORM_EPS) * gain_ref[:, h * HEAD_DIM:(h + 1) * HEAD_DIM]
            y = y * sg_ref[0, h, rows, :].astype(F32)
            o_ref[0, rows, h * HEAD_DIM:(h + 1) * HEAD_DIM] = y.astype(o_ref.dtype)
        return b_next

    lax.fori_loop(0, nchunks, chunk, cumsum(0))


def _hgrn(q, lf, v, sg, gain):
    B, H, S, E = q.shape
    C, HP = HG_CHUNK, HG_HEADS_PER_STEP
    masks = jnp.asarray(_hgrn_masks(C), dtype=BF16)
    tril = jnp.asarray(np.tril(np.ones((C, C), np.float32)), dtype=BF16)
    head_spec = pl.BlockSpec((1, HP, S, E), lambda b, h: (b, h, 0, 0))
    return pl.pallas_call(
        functools.partial(_hgrn_kernel, C=C),
        grid=(B, H // HP),
        in_specs=[
            head_spec, head_spec, head_spec, head_spec,
            pl.BlockSpec((1, HP * E), lambda b, h: (0, h)),
            pl.BlockSpec((C, C), lambda b, h: (0, 0)),
            pl.BlockSpec(masks.shape, lambda b, h: (0, 0, 0)),
        ],
        out_specs=pl.BlockSpec((1, S, HP * E), lambda b, h: (b, 0, h)),
        out_shape=jax.ShapeDtypeStruct((B, S, H * E), BF16),
        scratch_shapes=[pltpu.VMEM((HP, E, E), F32), pltpu.VMEM((HP, C, E), F32)],
        compiler_params=_cparams(("parallel", "parallel")),
        name="hgrn2",
    )(q, lf, v, sg, gain.reshape(1, H * E).astype(F32), tril, masks)


def _t5_bucket_np(dist):
    exact = REL_BUCKETS // 2
    d_f = np.maximum(dist, 1).astype(np.float32)
    log_b = exact + (np.log(d_f / np.float32(exact)) / np.float32(math.log(REL_MAX_DIST / exact))
                     * np.float32(REL_BUCKETS - exact)).astype(np.int32)
    return np.where(dist < exact, dist, np.minimum(log_b, REL_BUCKETS - 1))


def _attn_bias(rel_bias):
    blk = ATT_BLOCK
    period = 3 * blk
    out = []
    for gi, (window, dilation) in enumerate(ATT_GROUPS):
        n_back = window // dilation
        assert n_back <= blk
        hs = slice(gi * ATT_HEADS_PER_GROUP, (gi + 1) * ATT_HEADS_PER_GROUP)
        bucket = _t5_bucket_np(np.arange(n_back + 1) * dilation)
        by_delta = rel_bias[:, hs][bucket].astype(F32).T
        u = jnp.full((ATT_HEADS_PER_GROUP, period), NEG_BIG, F32)
        u = u.at[:, 2 * blk - 1 - n_back:2 * blk].set(by_delta[:, ::-1])
        flat = jnp.tile(u, (1, blk))[:, :blk * (period - 1)]
        out.append(flat.reshape(ATT_HEADS_PER_GROUP, blk, period - 1)[:, :, blk - 1:3 * blk - 1])
    return jnp.stack(out, axis=0)


def _attn_kernel(q_ref, k_ref, v_ref, bias_ref, o_ref, qf, kf, vf, og, lg):
    S = q_ref.shape[3]
    scale = HEAD_DIM ** -0.5
    blk = ATT_BLOCK

    for g, (window, d) in enumerate(ATT_GROUPS):
        L = S // d
        nb = L // blk
        if d > 1:
            qf[...] = q_ref[0, g, 0].astype(F32)
            kf[...] = k_ref[0, g, 0].astype(F32)
            vf[...] = v_ref[0, g, 0].astype(F32)

        def load(ref_bf, ref_f32, start, size, g=g, d=d):
            if d == 1:
                return ref_bf[0, g, 0, pl.ds(start, size), :]
            return ref_f32[pl.ds(start, size, stride=d), :].astype(BF16)

        def scores(r, n, g=g, d=d, load=load):
            start = r + n * blk * d
            q = load(q_ref, qf, start, blk)
            if n == 0:
                k = load(k_ref, kf, start, blk)
                v = load(v_ref, vf, start, blk)
                s = _dot_nt(q, k) * scale + bias_ref[g, 0, :, blk:]
            else:
                first = start - blk * d
                k = load(k_ref, kf, first, 2 * blk)
                v = load(v_ref, vf, first, 2 * blk)
                s = _dot_nt(q, k) * scale + bias_ref[g, 0]
            return start, s, v

        def softmax_pv(items, g=g, d=d):
            s = jnp.concatenate([it[1] for it in items], axis=0)
            m = jnp.max(s, axis=-1, keepdims=True)
            p = jnp.exp(s - m)
            den = jnp.sum(p, axis=-1, keepdims=True)
            pb = p.astype(BF16)
            lse = jnp.broadcast_to(m + jnp.log(den), (s.shape[0], HEAD_DIM))
            for j, (start, _, v) in enumerate(items):
                sl = slice(j * blk, (j + 1) * blk)
                rows = pl.ds(start, blk, stride=d) if d > 1 else pl.ds(start, blk)
                og[g, rows, :] = jnp.dot(pb[sl], v, preferred_element_type=F32) / den[sl]
                lg[g, rows, :] = lse[sl]

        blocks = [(r, n) for r in range(d) for n in range(nb)]
        first_blocks = [bn for bn in blocks if bn[1] == 0]
        later_blocks = [bn for bn in blocks if bn[1] > 0]
        for group in (first_blocks, later_blocks):
            for j0 in range(0, len(group), ATT_STACK):
                softmax_pv([scores(r, n) for r, n in group[j0:j0 + ATT_STACK]])

    rows_per_step = 256

    def mix(i, carry):
        rows = pl.ds(pl.multiple_of(i * rows_per_step, rows_per_step), rows_per_step)
        l0, l1, l2 = lg[0, rows, :], lg[1, rows, :], lg[2, rows, :]
        mx = jnp.maximum(jnp.maximum(l0, l1), l2)
        e0, e1, e2 = jnp.exp(l0 - mx), jnp.exp(l1 - mx), jnp.exp(l2 - mx)
        num = e0 * og[0, rows, :] + e1 * og[1, rows, :] + e2 * og[2, rows, :]
        o_ref[0, rows, :] = (num / (e0 + e1 + e2)).astype(o_ref.dtype)
        return carry

    lax.fori_loop(0, S // rows_per_step, mix, 0)


def _attention(qkv, bias):
    B, _, S, E = qkv.shape
    G, HG = len(ATT_GROUPS), ATT_HEADS_PER_GROUP
    x = qkv.reshape(B, 3, G, HG, S, E)

    def spec(which):
        return pl.BlockSpec((1, None, G, 1, S, E), lambda b, h, which=which: (b, which, 0, h, 0, 0))

    def kern(q_ref, k_ref, v_ref, bias_ref, o_ref, *scratch):
        _attn_kernel(q_ref, k_ref, v_ref, bias_ref, o_ref, *scratch)

    return pl.pallas_call(
        kern,
        grid=(B, HG),
        in_specs=[spec(0), spec(1), spec(2),
                  pl.BlockSpec((G, 1, ATT_BLOCK, 2 * ATT_BLOCK), lambda b, h: (0, h, 0, 0))],
        out_specs=pl.BlockSpec((1, S, E), lambda b, h: (b, 0, h)),
        out_shape=jax.ShapeDtypeStruct((B, S, HG * E), BF16),
        scratch_shapes=[pltpu.VMEM((S, E), F32)] * 3 + [pltpu.VMEM((G, S, E), F32)] * 2,
        compiler_params=_cparams(("parallel", "parallel")),
        name="dilated_attn",
    )(x, x, x, bias)


def _merge_kernel(ya_ref, yb_ref, wa_ref, wb_ref, ga_ref, gb_ref, o_ref, wa_bf, wb_bf):
    @pl.when(pl.program_id(1) == 0)
    def _():
        wa_bf[...] = wa_ref[...].astype(BF16)
        wb_bf[...] = wb_ref[...].astype(BF16)

    a = jnp.dot(ya_ref[...], wa_bf[...], preferred_element_type=F32)
    b = jnp.dot(yb_ref[...], wb_bf[...], preferred_element_type=F32)
    o_ref[...] = (ga_ref[...].astype(F32) * a + gb_ref[...].astype(F32) * b).astype(o_ref.dtype)


def _merge(ya, yb, wa, wb, gates, tm=1024, tn=1024):
    T, Ka = ya.shape
    Kb = yb.shape[1]
    N = wa.shape[1]
    nj = N // tn
    return pl.pallas_call(
        _merge_kernel,
        grid=(nj, T // tm),
        in_specs=[
            pl.BlockSpec((tm, Ka), lambda j, i: (i, 0)),
            pl.BlockSpec((tm, Kb), lambda j, i: (i, 0)),
            pl.BlockSpec((Ka, tn), lambda j, i: (0, j)),
            pl.BlockSpec((Kb, tn), lambda j, i: (0, j)),
            pl.BlockSpec((tm, tn), lambda j, i: (i, j)),
            pl.BlockSpec((tm, tn), lambda j, i, nj=nj: (i, j + nj)),
        ],
        out_specs=pl.BlockSpec((tm, tn), lambda j, i: (i, j)),
        out_shape=jax.ShapeDtypeStruct((T, N), BF16),
        scratch_shapes=[pltpu.VMEM((Ka, tn), BF16), pltpu.VMEM((Kb, tn), BF16)],
        compiler_params=_cparams(("parallel", "arbitrary")),
        name="branch_merge",
    )(ya, yb, wa, wb, gates, gates)


ROUTE_LANES = LANES


def _pack_halves(x_bf):
    bits = pltpu.bitcast(x_bf.astype(F32), jnp.uint32)
    n = bits.shape[1] // 2
    return (bits[:, :n] >> 16) | (bits[:, n:] & jnp.uint32(0xFFFF0000))


def _unpack_halves(p):
    lo = pltpu.bitcast(p << 16, F32)
    hi = pltpu.bitcast(p & jnp.uint32(0xFFFF0000), F32)
    return jnp.concatenate([lo, hi], axis=1)


ROW_TILE = 8


def _store_row_tiles(ref, packed):
    m = packed.shape[0]
    assert packed.shape[1] == ROW_TILE * LANES
    for c in range(ROW_TILE):
        ref[pl.ds(c, m, stride=ROW_TILE), :] = packed[:, c * LANES:(c + 1) * LANES]


def _load_row_tiles(ref, m):
    return jnp.concatenate([ref[pl.ds(c, m, stride=ROW_TILE), :] for c in range(ROW_TILE)], axis=1)


def _outproj_kernel(m_ref, w_ref, x_ref, g_ref, wrc_ref, br_ref, x1_ref, h2_ref, rt_ref):
    x1 = x_ref[...] + jnp.dot(m_ref[...], w_ref[...], preferred_element_type=F32)
    x1_ref[...] = x1
    ms = jnp.mean(x1 * x1, axis=-1, keepdims=True)
    h2 = x1 * lax.rsqrt(ms + NORM_EPS) * g_ref[...]
    h_hi = h2.astype(BF16)
    _store_row_tiles(h2_ref, _pack_halves(h_hi))
    h_lo = (h2 - h_hi.astype(F32)).astype(BF16)
    r = jnp.dot(h_hi, wrc_ref[...], preferred_element_type=F32)
    lg = (r[:, :ROUTE_LANES] + r[:, ROUTE_LANES:]
          + jnp.dot(h_lo, wrc_ref[:, :ROUTE_LANES], preferred_element_type=F32)) + br_ref[...]

    lane = lax.broadcasted_iota(jnp.int32, lg.shape, 1)
    lane_f = lane.astype(F32)
    big = float(ROUTE_LANES)
    is_group = (lane >= MOE_EXPERTS) & (lane < MOE_EXPERTS + MOE_GROUPS)
    lgg = jnp.where(is_group, lg, -jnp.inf)
    gmax = jnp.max(lgg, axis=-1, keepdims=True)
    gsel = jnp.min(jnp.where(lgg == gmax, lane_f - MOE_EXPERTS, big), axis=-1, keepdims=True)
    pg = 1.0 / jnp.sum(jnp.where(is_group, jnp.exp(lg - gmax), 0.0), axis=-1, keepdims=True)

    in_group = (lane < MOE_EXPERTS) & ((lane // MOE_EXPERTS_PER_GROUP).astype(F32) == gsel)
    le = jnp.where(in_group, lg, -jnp.inf)
    t1 = jnp.max(le, axis=-1, keepdims=True)
    i1 = jnp.min(jnp.where(le == t1, lane_f, big), axis=-1, keepdims=True)
    le2 = jnp.where(lane_f == i1, -jnp.inf, le)
    t2 = jnp.max(le2, axis=-1, keepdims=True)
    i2 = jnp.min(jnp.where(le2 == t2, lane_f, big), axis=-1, keepdims=True)
    e2 = jnp.exp(t2 - t1)
    w1 = pg / (1.0 + e2)
    w2 = pg * e2 / (1.0 + e2)
    rt_ref[...] = jnp.where(lane == 0, i1, jnp.where(lane == 1, i2, jnp.where(lane == 2, w1,
                            jnp.where(lane == 3, w2, 0.0))))


def _outproj(merged, w_out, x2d, gain, wr_cat, br, tm=512):
    T, D = x2d.shape
    row = lambda i: (i, 0)
    const = lambda i: (0, 0)
    return pl.pallas_call(
        _outproj_kernel,
        grid=(T // tm,),
        in_specs=[
            pl.BlockSpec((tm, D), row), pl.BlockSpec((D, D), const), pl.BlockSpec((tm, D), row),
            pl.BlockSpec((1, D), const), pl.BlockSpec((D, 2 * ROUTE_LANES), const),
            pl.BlockSpec((1, ROUTE_LANES), const),
        ],
        out_specs=[pl.BlockSpec((tm, D), row), pl.BlockSpec((tm * ROW_TILE, LANES), row),
                   pl.BlockSpec((tm, ROUTE_LANES), row)],
        out_shape=[jax.ShapeDtypeStruct((T, D), F32), jax.ShapeDtypeStruct((T * ROW_TILE, LANES), jnp.uint32),
                   jax.ShapeDtypeStruct((T, ROUTE_LANES), F32)],
        compiler_params=_cparams(("parallel",)),
        name="outproj_router",
    )(merged, w_out, x2d, gain.reshape(1, D).astype(F32), wr_cat, br)


def _dispatch_kernel(dest_ref, zb_ref, h_ref, xs_ref, zeros, sem, *, tq):
    base = pl.program_id(0) * tq
    block_rows = MOE_ROWS * ROW_TILE
    n_blocks = xs_ref.shape[0] // block_rows

    @pl.when(pl.program_id(0) == 0)
    def _():
        zeros[...] = jnp.zeros_like(zeros)

        def zero_copy(b):
            return pltpu.make_async_copy(zeros, xs_ref.at[pl.ds(pl.multiple_of(b * block_rows, block_rows), block_rows)],
                                         sem)

        def start(b, carry):
            @pl.when(zb_ref[b] != 0)
            def _():
                zero_copy(b).start()
            return carry

        def wait(b, carry):
            @pl.when(zb_ref[b] != 0)
            def _():
                zero_copy(b).wait()
            return carry

        lax.fori_loop(0, n_blocks, start, 0)
        lax.fori_loop(0, n_blocks, wait, 0)

    def token_rows(t):
        return pl.ds(pl.multiple_of(t * ROW_TILE, ROW_TILE), ROW_TILE)

    def copy(j, k):
        return pltpu.make_async_copy(h_ref.at[token_rows(j)], xs_ref.at[token_rows(dest_ref[2 * (base + j) + k])], sem)

    def issue(j, carry):
        for k in range(MOE_TOPK):
            copy(j, k).start(priority=k % 2)
        return carry

    lax.fori_loop(0, tq, issue, 0)
    for _ in range(MOE_TOPK):
        pltpu.make_async_copy(h_ref, xs_ref.at[pl.ds(0, tq * ROW_TILE)], sem).wait()


def _dispatch(h2, dest_flat, zero_block, n_rows, tq=512):
    T = h2.shape[0] // ROW_TILE
    return pl.pallas_call(
        functools.partial(_dispatch_kernel, tq=tq),
        grid_spec=pltpu.PrefetchScalarGridSpec(
            num_scalar_prefetch=2,
            grid=(T // tq,),
            in_specs=[pl.BlockSpec((tq * ROW_TILE, LANES), lambda i, d, z: (i, 0))],
            out_specs=pl.BlockSpec(memory_space=pl.ANY),
            scratch_shapes=[pltpu.VMEM((MOE_ROWS * ROW_TILE, LANES), h2.dtype), pltpu.SemaphoreType.DMA(())],
        ),
        out_shape=jax.ShapeDtypeStruct((n_rows * ROW_TILE, LANES), h2.dtype),
        compiler_params=_cparams(("arbitrary",)),
        name="moe_dispatch",
    )(dest_flat, zero_block, h2)


MOE_WEIGHT_SLOTS = 2


def _expert_kernel(be_ref, nu_ref, rank_ref, kth_ref, xs_ref, wg_hbm, wu_hbm, wd_hbm, o_ref,
                   wg_f, wu_f, wd_f, wg_s, wu_s, wd_s, sem):
    i = pl.program_id(0)
    e = be_ref[i]
    active = i < nu_ref[0]
    new_expert = active & ((i == 0) | (e != be_ref[jnp.maximum(i - 1, 0)]))
    k = rank_ref[e]
    slot = k % MOE_WEIGHT_SLOTS
    ahead = MOE_WEIGHT_SLOTS - 1

    def weight_copies(expert, s):
        return (pltpu.make_async_copy(wg_hbm.at[expert], wg_f.at[s], sem.at[s, 0]),
                pltpu.make_async_copy(wu_hbm.at[expert], wu_f.at[s], sem.at[s, 1]),
                pltpu.make_async_copy(wd_hbm.at[expert], wd_f.at[s], sem.at[s, 2]))

    def start_kth(j):
        nxt = kth_ref[j]

        @pl.when(nxt >= 0)
        def _():
            for c in weight_copies(nxt, j % MOE_WEIGHT_SLOTS):
                c.start()

    @pl.when(active & (i == 0))
    def _():
        for j in range(ahead):
            start_kth(j)

    @pl.when(new_expert)
    def _():
        start_kth(k + ahead)
        for c in weight_copies(e, slot):
            c.wait()
        wg_s[...] = wg_f[slot].astype(BF16)
        wu_s[...] = wu_f[slot].astype(BF16)
        wd_s[...] = wd_f[slot].astype(BF16)

    @pl.when(active)
    def _():
        x = _unpack_halves(_load_row_tiles(xs_ref, MOE_ROWS)).astype(BF16)
        g = jnp.dot(x, wg_s[...], preferred_element_type=F32)
        u = jnp.dot(x, wu_s[...], preferred_element_type=F32)
        a = (g * jax.nn.sigmoid(g) * u).astype(BF16)
        y = jnp.dot(a, wd_s[...], preferred_element_type=F32)
        _store_row_tiles(o_ref, _pack_halves(y.astype(BF16)))

    @pl.when(jnp.logical_not(active))
    def _():
        o_ref[...] = jnp.zeros_like(o_ref)


def _experts(xs, block_expert, n_used, expert_rank, kth_expert, w_gate, w_up, w_down):
    Dh = xs.shape[1]
    D, F = w_gate.shape[1], w_gate.shape[2]
    assert Dh == LANES and D == 2 * ROW_TILE * LANES
    nblk = block_expert.shape[0]
    P = nblk * MOE_ROWS * ROW_TILE
    rows = lambda i, be, nu, nx, sl: (jnp.minimum(i, nu[0] - 1), 0)
    hbm = pl.BlockSpec(memory_space=pl.ANY)
    return pl.pallas_call(
        _expert_kernel,
        grid_spec=pltpu.PrefetchScalarGridSpec(
            num_scalar_prefetch=4,
            grid=(nblk,),
            in_specs=[pl.BlockSpec((MOE_ROWS * ROW_TILE, Dh), rows), hbm, hbm, hbm],
            out_specs=pl.BlockSpec((MOE_ROWS * ROW_TILE, Dh), lambda i, be, nu, nx, sl: (i, 0)),
            scratch_shapes=[pltpu.VMEM((MOE_WEIGHT_SLOTS, D, F), F32), pltpu.VMEM((MOE_WEIGHT_SLOTS, D, F), F32),
                            pltpu.VMEM((MOE_WEIGHT_SLOTS, F, D), F32),
                            pltpu.VMEM((D, F), BF16), pltpu.VMEM((D, F), BF16), pltpu.VMEM((F, D), BF16),
                            pltpu.SemaphoreType.DMA((MOE_WEIGHT_SLOTS, 3))],
        ),
        out_shape=jax.ShapeDtypeStruct((P, Dh), jnp.uint32),
        compiler_params=_cparams(("arbitrary",)),
        name="moe_experts",
    )(block_expert, n_used, expert_rank, kth_expert, xs, w_gate, w_up, w_down)


def _combine_kernel(dest_ref, x1_ref, rt_ref, g_ref, ys_ref, o_ref, buf, sem, *, tq):
    i = pl.program_id(0)
    n = pl.num_programs(0)

    def token_rows(t):
        return pl.ds(pl.multiple_of(t * ROW_TILE, ROW_TILE), ROW_TILE)

    def issue(step, s):
        def body(j, carry):
            for k in range(MOE_TOPK):
                pltpu.make_async_copy(ys_ref.at[token_rows(dest_ref[2 * (step * tq + j) + k])],
                                      buf.at[s, k, token_rows(j)], sem.at[s]).start(priority=k % 2)
            return carry
        lax.fori_loop(0, tq, body, 0)

    @pl.when(i == 0)
    def _():
        issue(0, 0)

    for s in range(2):
        @pl.when((i + 1 < n) & ((i + 1) % 2 == s))
        def _():
            issue(i + 1, s)

    for s in range(2):
        @pl.when(i % 2 == s)
        def _():
            for k in range(MOE_TOPK):
                pltpu.make_async_copy(ys_ref.at[pl.ds(0, tq * ROW_TILE)], buf.at[s, k], sem.at[s]).wait()
            rt = rt_ref[...]
            y0 = _unpack_halves(_load_row_tiles(buf.at[s, 0], tq))
            y1 = _unpack_halves(_load_row_tiles(buf.at[s, 1], tq))
            x = x1_ref[...] + rt[:, 2:3] * y0 + rt[:, 3:4] * y1
            ms = jnp.mean(x * x, axis=-1, keepdims=True)
            o_ref[...] = x * lax.rsqrt(ms + NORM_EPS) * g_ref[...]


def _combine(x1, route, gain, ys, dest_flat, tq=256):
    T, D = x1.shape
    return pl.pallas_call(
        functools.partial(_combine_kernel, tq=tq),
        grid_spec=pltpu.PrefetchScalarGridSpec(
            num_scalar_prefetch=1,
            grid=(T // tq,),
            in_specs=[
                pl.BlockSpec((tq, D), lambda i, d: (i, 0)),
                pl.BlockSpec((tq, ROUTE_LANES), lambda i, d: (i, 0)),
                pl.BlockSpec((1, D), lambda i, d: (0, 0)),
                pl.BlockSpec(memory_space=pl.ANY),
            ],
            out_specs=pl.BlockSpec((tq, D), lambda i, d: (i, 0)),
            scratch_shapes=[pltpu.VMEM((2, MOE_TOPK, tq * ROW_TILE, LANES), jnp.uint32),
                            pltpu.SemaphoreType.DMA((2,))],
        ),
        out_shape=jax.ShapeDtypeStruct((T, D), F32),
        compiler_params=_cparams(("arbitrary",)),
        name="moe_combine",
    )(dest_flat, x1, route, gain.reshape(1, D).astype(F32), ys)


def _route_metadata(route, T):
    e = route[:, :MOE_TOPK].astype(jnp.int32)
    ids = jnp.arange(MOE_EXPERTS, dtype=jnp.int32)
    oh0 = e[:, 0, None] == ids
    oh1 = e[:, 1, None] == ids
    onehot = (oh0 | oh1).astype(jnp.int32)
    incl = jnp.cumsum(onehot, axis=0)
    counts = incl[-1]
    pcounts = (counts + MOE_ROWS - 1) // MOE_ROWS * MOE_ROWS
    pends = jnp.cumsum(pcounts)
    poffs = pends - pcounts
    slot = incl - onehot + poffs[None, :]
    dest = jnp.stack([jnp.sum(jnp.where(oh0, slot, 0), axis=1), jnp.sum(jnp.where(oh1, slot, 0), axis=1)], axis=1)
    nblk = (T * MOE_TOPK) // MOE_ROWS + MOE_EXPERTS
    first_row = jnp.arange(nblk, dtype=jnp.int32) * MOE_ROWS
    block_expert = jnp.minimum(jnp.sum((pends[None, :] <= first_row[:, None]).astype(jnp.int32), axis=1),
                               MOE_EXPERTS - 1).astype(jnp.int32)
    n_used = (pends[-1:] // MOE_ROWS).astype(jnp.int32)
    used = counts > 0
    expert_rank = (jnp.cumsum(used.astype(jnp.int32)) - 1).astype(jnp.int32)
    order = jnp.argsort(jnp.where(used, ids, ids + MOE_EXPERTS)).astype(jnp.int32)
    kth = jnp.where(ids < jnp.sum(used), order, -1)
    kth_expert = jnp.concatenate([kth, jnp.full((MOE_WEIGHT_SLOTS,), -1, jnp.int32)])
    blk = jnp.arange(nblk, dtype=jnp.int32)
    following = jnp.concatenate([block_expert[1:], block_expert[-1:]])
    zero_block = ((blk >= n_used[0] - 1) | (block_expert != following)).astype(jnp.int32)
    return dest.reshape(-1).astype(jnp.int32), zero_block, block_expert, n_used, expert_rank, kth_expert, nblk


def kernel(x, norm1_gain, w_in, hg_lb_logits, hg_norm_gain, rel_bias, w_branch_a, w_branch_b, w_out,
           norm2_gain, w_router_group, b_router_group, w_router_expert, b_router_expert,
           w_exp_gate, w_exp_up, w_exp_down, final_norm_gain):
    B, S, D = x.shape
    T = B * S
    depth = norm1_gain.shape[0]
    lower_bounds = jnp.cumsum(jax.nn.softmax(hg_lb_logits.astype(F32), axis=0), axis=0)
    att_w = ATT_HEADS * HEAD_DIM
    bias = _attn_bias(rel_bias)
    x2d = x.reshape(T, D)
    for layer in range(depth):
        zeros_lb = jnp.zeros((1, 3 * att_w), F32)
        hm = dict(head_major=True, B=B, S=S)
        h, q_a = _normproj(x2d, norm1_gain[layer], w_in, layer, D, B=B, S=S)
        lf_a = _inproj(h, w_in, layer, D, D, lower_bounds[layer].reshape(1, D), mode="forget", out_dtype=F32, **hm)
        i_a = _inproj(h, w_in, layer, 2 * D, D, zeros_lb, mode="none", out_dtype=BF16, **hm)
        sg_a = _inproj(h, w_in, layer, 3 * D, D, zeros_lb, mode="sigmoid", out_dtype=BF16, **hm)
        qkv_b = _inproj(h, w_in, layer, 4 * D, 3 * att_w, zeros_lb, mode="none", out_dtype=BF16, tn=att_w, **hm)
        gates = _inproj(h, w_in, layer, 4 * D + 3 * att_w, 2 * D, zeros_lb, mode="sigmoid",
                        head_major=False, out_dtype=BF16, B=B, S=S)

        y_a = _hgrn(q_a, lf_a, i_a, sg_a, hg_norm_gain[layer]).reshape(T, D)
        y_b = _attention(qkv_b, bias).reshape(T, ATT_HEADS_PER_GROUP * HEAD_DIM)
        merged = _merge(y_a, y_b, w_branch_a[layer], w_branch_b[layer], gates)

        n_pad = ROUTE_LANES - MOE_EXPERTS - MOE_GROUPS
        wr = jnp.concatenate([w_router_expert[layer].astype(F32), w_router_group[layer].astype(F32),
                              jnp.zeros((D, n_pad), F32)], axis=1)
        br = jnp.concatenate([b_router_expert[layer].astype(F32), b_router_group[layer].astype(F32),
                              jnp.zeros((n_pad,), F32)]).reshape(1, ROUTE_LANES)
        wr_hi = wr.astype(BF16)
        wr_lo = (wr - wr_hi.astype(F32)).astype(BF16)
        x1, h2, route = _outproj(merged, w_out[layer].astype(BF16), x2d, norm2_gain[layer],
                                 jnp.concatenate([wr_hi, wr_lo], axis=1), br)

        dest, zero_block, block_expert, n_used, expert_rank, kth_expert, nblk = _route_metadata(route, T)
        xs = _dispatch(h2, dest, zero_block, nblk * MOE_ROWS)
        ys = _experts(xs, block_expert, n_used, expert_rank, kth_expert,
                      w_exp_gate[layer], w_exp_up[layer], w_exp_down[layer])
        last = layer == depth - 1
        assert last, "the fused combine applies the final norm; deeper stacks need an un-normalised combine"
        x2d = _combine(x1, route, final_norm_gain, ys, dest)
    return x2d.reshape(B, S, D)
```

```python
import functools
import math

import numpy as np
import jax
import jax.numpy as jnp
from jax import lax
from jax.experimental import pallas as pl
from jax.experimental.pallas import tpu as pltpu

F32 = jnp.float32
BF16 = jnp.bfloat16

LANES = 128
NORM_EPS = 1e-6
HEAD_DIM = 128
ATT_GROUPS = ((128, 1), (512, 4), (2048, 16))
ATT_HEADS_PER_GROUP = 4
ATT_HEADS = len(ATT_GROUPS) * ATT_HEADS_PER_GROUP
ATT_BLOCK = 128
ATT_STACK = 16
REL_BUCKETS = 32
REL_MAX_DIST = 2048
MOE_GROUPS = 8
MOE_EXPERTS_PER_GROUP = 8
MOE_EXPERTS = MOE_GROUPS * MOE_EXPERTS_PER_GROUP
MOE_TOPK = 2
MOE_ROWS = 256
NEG_BIG = -1e30
VMEM_LIMIT = 56 * 1024 * 1024


def _cparams(sem):
    return pltpu.CompilerParams(dimension_semantics=sem, vmem_limit_bytes=VMEM_LIMIT)


def _normproj_kernel(x_ref, g_ref, w_ref, h_ref, o_ref, w_bf):
    @pl.when(pl.program_id(0) == 0)
    def _():
        w_bf[...] = w_ref[...].astype(BF16)

    x = x_ref[...]
    ms = jnp.mean(x * x, axis=-1, keepdims=True)
    h = (x * lax.rsqrt(ms + NORM_EPS) * g_ref[...]).astype(h_ref.dtype)
    h_ref[...] = h
    acc = jnp.dot(h, w_bf[...], preferred_element_type=F32)
    for hh in range(acc.shape[1] // HEAD_DIM):
        o_ref[0, hh] = acc[:, hh * HEAD_DIM:(hh + 1) * HEAD_DIM].astype(o_ref.dtype)


def _normproj(x2d, gain, w_all, layer, N, *, B, S, tm=512):
    T, D = x2d.shape
    assert S % tm == 0
    spb = S // tm
    return pl.pallas_call(
        _normproj_kernel,
        grid=(T // tm,),
        in_specs=[
            pl.BlockSpec((tm, D), lambda i: (i, 0)),
            pl.BlockSpec((1, D), lambda i: (0, 0)),
            pl.BlockSpec((None, D, N), lambda i: (layer, 0, 0), pipeline_mode=pl.Buffered(1)),
        ],
        out_specs=[
            pl.BlockSpec((tm, D), lambda i: (i, 0)),
            pl.BlockSpec((1, N // HEAD_DIM, tm, HEAD_DIM), lambda i: (i // spb, 0, i % spb, 0)),
        ],
        out_shape=[jax.ShapeDtypeStruct((T, D), BF16),
                   jax.ShapeDtypeStruct((B, N // HEAD_DIM, S, HEAD_DIM), BF16)],
        scratch_shapes=[pltpu.VMEM((D, N), BF16)],
        compiler_params=_cparams(("arbitrary",)),
        name="norm_inproj",
    )(x2d, gain.reshape(1, D).astype(F32), w_all)


def _inproj_kernel(h_ref, w_ref, lb_ref, o_ref, w_bf, *, mode, head_major):
    @pl.when(pl.program_id(1) == 0)
    def _():
        w_bf[...] = w_ref[0].astype(BF16)

    acc = jnp.dot(h_ref[...], w_bf[...], preferred_element_type=F32)
    if mode == "forget":
        lb = lb_ref[...]
        acc = lb + (1.0 - lb) * jax.nn.sigmoid(acc)
    elif mode == "sigmoid":
        acc = 0.5 * jnp.tanh(0.5 * acc) + 0.5
    if head_major:
        for hh in range(acc.shape[1] // HEAD_DIM):
            o_ref[0, hh] = acc[:, hh * HEAD_DIM:(hh + 1) * HEAD_DIM].astype(o_ref.dtype)
    else:
        o_ref[...] = acc.astype(o_ref.dtype)


def _inproj(h, w_all, layer, col0, N, lb, *, mode, head_major, out_dtype, B, S, tm=1024, tn=1024):
    T, D = h.shape
    assert N % tn == 0 and col0 % LANES == 0 and T % tm == 0 and S % tm == 0
    spb = S // tm
    if head_major:
        out_shape = jax.ShapeDtypeStruct((B, N // HEAD_DIM, S, HEAD_DIM), out_dtype)
        out_spec = pl.BlockSpec((1, tn // HEAD_DIM, tm, HEAD_DIM), lambda j, i: (i // spb, j, i % spb, 0))
    else:
        out_shape = jax.ShapeDtypeStruct((T, N), out_dtype)
        out_spec = pl.BlockSpec((tm, tn), lambda j, i: (i, j))
    return pl.pallas_call(
        functools.partial(_inproj_kernel, mode=mode, head_major=head_major),
        grid=(N // tn, T // tm),
        in_specs=[
            pl.BlockSpec((tm, D), lambda j, i: (i, 0)),
            pl.BlockSpec((pl.Element(1), pl.Element(D), pl.Element(tn)), lambda j, i: (layer, 0, pl.multiple_of(col0 + j * tn, LANES))),
            pl.BlockSpec((1, tn), lambda j, i: (0, j)),
        ],
        out_specs=out_spec,
        out_shape=out_shape,
        scratch_shapes=[pltpu.VMEM((D, tn), BF16)],
        compiler_params=_cparams(("parallel", "arbitrary")),
        name="inproj_" + mode,
    )(h, w_all, lb)


HG_CHUNK = 256
HG_DIAG = 16
HG_HEADS_PER_STEP = 4


def _hgrn_levels(C):
    out, m = [], C // 2
    while m >= HG_DIAG:
        out.append(m)
        m //= 2
    return out


def _hgrn_masks(C):
    t = np.arange(C)[:, None]
    s = np.arange(C)[None, :]
    masks = []
    for m in _hgrn_levels(C):
        masks.append((t // (2 * m) == s // (2 * m)) & ((t // m) % 2 == 1) & ((s // m) % 2 == 0))
    masks.append((t // HG_DIAG == s // HG_DIAG) & (t >= s))
    total = np.sum(np.stack(masks).astype(np.int32), axis=0)
    assert np.array_equal(total, (t >= s).astype(np.int32))
    return np.stack(masks).astype(np.float32)


def _dot_nt(a, b):
    return lax.dot_general(a, b, (((1,), (1,)), ((), ())), preferred_element_type=F32)


def _dot_tn(a, b):
    return lax.dot_general(a, b, (((0,), (0,)), ((), ())), preferred_element_type=F32)


def _hgrn_kernel(q_ref, f_ref, v_ref, sg_ref, gain_ref, tril_ref, mask_ref, o_ref, st_ref, b_ref, *, C):
    S = q_ref.shape[2]
    HP = q_ref.shape[1]
    levels = _hgrn_levels(C)
    nchunks = S // C
    tril = tril_ref[...]

    def ref_rows(h, block, row_of_block):
        parts = [jnp.broadcast_to(b_ref[h, pl.ds(row_of_block(p), 1), :], (block, HEAD_DIM))
                 for p in range(C // block)]
        return jnp.concatenate(parts, axis=0)

    def cumsum(c):
        r0 = pl.multiple_of(c * C, C)
        out = []
        for h in range(HP):
            lf2 = jnp.log2(f_ref[0, h, pl.ds(r0, C), :])
            hi = lf2.astype(BF16)
            lo = (lf2 - hi.astype(F32)).astype(BF16)
            r = jnp.dot(tril, jnp.concatenate([hi, lo], axis=1), preferred_element_type=F32)
            out.append(r[:, :HEAD_DIM] + r[:, HEAD_DIM:])
        return tuple(out)

    st_ref[...] = jnp.zeros_like(st_ref)

    def chunk(c, b_all):
        r0 = pl.multiple_of(c * C, C)
        rows = pl.ds(r0, C)
        b_next = cumsum(jnp.minimum(c + 1, nchunks - 1))
        for h in range(HP):
            b = b_all[h]
            b_ref[h] = b
            q = q_ref[0, h, rows, :].astype(F32)
            kk = 1.0 - f_ref[0, h, rows, :]
            b_last = b_ref[h, pl.ds(C - 1, 1), :]
            scores = jnp.zeros((C, C), BF16)
            half = C // 2
            ref_top = b_ref[h, pl.ds(half - 1, 1), :]
            a_top = (q[half:] * jnp.exp2(b[half:] - ref_top)).astype(BF16)
            k_top = (kk[:half] * jnp.exp2(ref_top - b[:half])).astype(BF16)
            s_top = _dot_nt(a_top, k_top).astype(BF16)
            for li, m in enumerate(levels):
                if m == half:
                    continue
                parts = []
                for p in range(C // (2 * m)):
                    ref = b_ref[h, pl.ds(2 * m * p + m - 1, 1), :]
                    parts += [ref - b[2 * m * p:2 * m * p + m], b[2 * m * p + m:2 * m * (p + 1)] - ref]
                e = jnp.exp2(jnp.concatenate(parts, axis=0))
                s_l = _dot_nt((q * e).astype(BF16), (kk * e).astype(BF16)).astype(BF16)
                scores = scores + mask_ref[li] * s_l
            d = b - ref_rows(h, HG_DIAG, lambda p: HG_DIAG * p + HG_DIAG // 2 - 1)
            s_l = _dot_nt((q * jnp.exp2(d)).astype(BF16), (kk * jnp.exp2(-d)).astype(BF16)).astype(BF16)
            scores = scores + mask_ref[len(levels)] * s_l
            v = v_ref[0, h, rows, :]
            st = st_ref[h]
            qe = (q * jnp.exp2(b)).astype(BF16)
            kd = (kk * jnp.exp2(b_last - b)).astype(BF16)
            o = jnp.dot(scores, v, preferred_element_type=F32) + _dot_nt(qe, st.astype(BF16))
            o_top = jnp.dot(s_top, v[:half], preferred_element_type=F32)
            o = jnp.concatenate([o[:half], o[half:] + o_top], axis=0)
            st_ref[h] = st * jnp.exp2(b_last) + _dot_tn(v, kd)
            ms = jnp.mean(o * o, axis=-1, keepdims=True)
            y = o * lax.rsqrt(ms + NORM_EPS) * gain_ref[:, h * HEAD_DIM:(h + 1) * HEAD_DIM]
            y = y * sg_ref[0, h, rows, :].astype(F32)
            o_ref[0, rows, h * HEAD_DIM:(h + 1) * HEAD_DIM] = y.astype(o_ref.dtype)
        return b_next

    lax.fori_loop(0, nchunks, chunk, cumsum(0))


def _hgrn(q, lf, v, sg, gain):
    B, H, S, E = q.shape
    C, HP = HG_CHUNK, HG_HEADS_PER_STEP
    masks = jnp.asarray(_hgrn_masks(C), dtype=BF16)
    tril = jnp.asarray(np.tril(np.ones((C, C), np.float32)), dtype=BF16)
    head_spec = pl.BlockSpec((1, HP, S, E), lambda b, h: (b, h, 0, 0))
    return pl.pallas_call(
        functools.partial(_hgrn_kernel, C=C),
        grid=(B, H // HP),
        in_specs=[
            head_spec, head_spec, head_spec, head_spec,
            pl.BlockSpec((1, HP * E), lambda b, h: (0, h)),
            pl.BlockSpec((C, C), lambda b, h: (0, 0)),
            pl.BlockSpec(masks.shape, lambda b, h: (0, 0, 0)),
        ],
        out_specs=pl.BlockSpec((1, S, HP * E), lambda b, h: (b, 0, h)),
        out_shape=jax.ShapeDtypeStruct((B, S, H * E), BF16),
        scratch_shapes=[pltpu.VMEM((HP, E, E), F32), pltpu.VMEM((HP, C, E), F32)],
        compiler_params=_cparams(("parallel", "parallel")),
        name="hgrn2",
    )(q, lf, v, sg, gain.reshape(1, H * E).astype(F32), tril, masks)


def _t5_bucket_np(dist):
    exact = REL_BUCKETS // 2
    d_f = np.maximum(dist, 1).astype(np.float32)
    log_b = exact + (np.log(d_f / np.float32(exact)) / np.float32(math.log(REL_MAX_DIST / exact))
                     * np.float32(REL_BUCKETS - exact)).astype(np.int32)
    return np.where(dist < exact, dist, np.minimum(log_b, REL_BUCKETS - 1))


def _attn_bias(rel_bias):
    blk = ATT_BLOCK
    period = 3 * blk
    out = []
    for gi, (window, dilation) in enumerate(ATT_GROUPS):
        n_back = window // dilation
        assert n_back <= blk
        hs = slice(gi * ATT_HEADS_PER_GROUP, (gi + 1) * ATT_HEADS_PER_GROUP)
        bucket = _t5_bucket_np(np.arange(n_back + 1) * dilation)
        by_delta = rel_bias[:, hs][bucket].astype(F32).T
        u = jnp.full((ATT_HEADS_PER_GROUP, period), NEG_BIG, F32)
        u = u.at[:, 2 * blk - 1 - n_back:2 * blk].set(by_delta[:, ::-1])
        flat = jnp.tile(u, (1, blk))[:, :blk * (period - 1)]
        out.append(flat.reshape(ATT_HEADS_PER_GROUP, blk, period - 1)[:, :, blk - 1:3 * blk - 1])
    return jnp.stack(out, axis=0)


def _attn_kernel(q_ref, k_ref, v_ref, bias_ref, o_ref, qf, kf, vf, og, lg):
    S = q_ref.shape[3]
    scale = HEAD_DIM ** -0.5
    blk = ATT_BLOCK

    for g, (window, d) in enumerate(ATT_GROUPS):
        L = S // d
        nb = L // blk
        if d > 1:
            qf[...] = q_ref[0, g, 0].astype(F32)
            kf[...] = k_ref[0, g, 0].astype(F32)
            vf[...] = v_ref[0, g, 0].astype(F32)

        def load(ref_bf, ref_f32, start, size, g=g, d=d):
            if d == 1:
                return ref_bf[0, g, 0, pl.ds(start, size), :]
            return ref_f32[pl.ds(start, size, stride=d), :].astype(BF16)

        def scores(r, n, g=g, d=d, load=load):
            start = r + n * blk * d
            q = load(q_ref, qf, start, blk)
            if n == 0:
                k = load(k_ref, kf, start, blk)
                v = load(v_ref, vf, start, blk)
                s = _dot_nt(q, k) * scale + bias_ref[g, 0, :, blk:]
            else:
                first = start - blk * d
                k = load(k_ref, kf, first, 2 * blk)
                v = load(v_ref, vf, first, 2 * blk)
                s = _dot_nt(q, k) * scale + bias_ref[g, 0]
            return start, s, v

        def softmax_pv(items, g=g, d=d):
            s = jnp.concatenate([it[1] for it in items], axis=0)
            m = jnp.max(s, axis=-1, keepdims=True)
            p = jnp.exp(s - m)
            den = jnp.sum(p, axis=-1, keepdims=True)
            pb = p.astype(BF16)
            lse = jnp.broadcast_to(m + jnp.log(den), (s.shape[0], HEAD_DIM))
            for j, (start, _, v) in enumerate(items):
                sl = slice(j * blk, (j + 1) * blk)
                rows = pl.ds(start, blk, stride=d) if d > 1 else pl.ds(start, blk)
                og[g, rows, :] = jnp.dot(pb[sl], v, preferred_element_type=F32) / den[sl]
                lg[g, rows, :] = lse[sl]

        blocks = [(r, n) for r in range(d) for n in range(nb)]
        first_blocks = [bn for bn in blocks if bn[1] == 0]
        later_blocks = [bn for bn in blocks if bn[1] > 0]
        for group in (first_blocks, later_blocks):
            for j0 in range(0, len(group), ATT_STACK):
                softmax_pv([scores(r, n) for r, n in group[j0:j0 + ATT_STACK]])

    rows_per_step = 256

    def mix(i, carry):
        rows = pl.ds(pl.multiple_of(i * rows_per_step, rows_per_step), rows_per_step)
        l0, l1, l2 = lg[0, rows, :], lg[1, rows, :], lg[2, rows, :]
        mx = jnp.maximum(jnp.maximum(l0, l1), l2)
        e0, e1, e2 = jnp.exp(l0 - mx), jnp.exp(l1 - mx), jnp.exp(l2 - mx)
        num = e0 * og[0, rows, :] + e1 * og[1, rows, :] + e2 * og[2, rows, :]
        o_ref[0, rows, :] = (num / (e0 + e1 + e2)).astype(o_ref.dtype)
        return carry

    lax.fori_loop(0, S // rows_per_step, mix, 0)


def _attention(qkv, bias):
    B, _, S, E = qkv.shape
    G, HG = len(ATT_GROUPS), ATT_HEADS_PER_GROUP
    x = qkv.reshape(B, 3, G, HG, S, E)

    def spec(which):
        return pl.BlockSpec((1, None, G, 1, S, E), lambda b, h, which=which: (b, which, 0, h, 0, 0))

    def kern(q_ref, k_ref, v_ref, bias_ref, o_ref, *scratch):
        _attn_kernel(q_ref, k_ref, v_ref, bias_ref, o_ref, *scratch)

    return pl.pallas_call(
        kern,
        grid=(B, HG),
        in_specs=[spec(0), spec(1), spec(2),
                  pl.BlockSpec((G, 1, ATT_BLOCK, 2 * ATT_BLOCK), lambda b, h: (0, h, 0, 0))],
        out_specs=pl.BlockSpec((1, S, E), lambda b, h: (b, 0, h)),
        out_shape=jax.ShapeDtypeStruct((B, S, HG * E), BF16),
        scratch_shapes=[pltpu.VMEM((S, E), F32)] * 3 + [pltpu.VMEM((G, S, E), F32)] * 2,
        compiler_params=_cparams(("parallel", "parallel")),
        name="dilated_attn",
    )(x, x, x, bias)


def _merge_kernel(ya_ref, yb_ref, wa_ref, wb_ref, ga_ref, gb_ref, o_ref, wa_bf, wb_bf):
    @pl.when(pl.program_id(1) == 0)
    def _():
        wa_bf[...] = wa_ref[...].astype(BF16)
        wb_bf[...] = wb_ref[...].astype(BF16)

    a = jnp.dot(ya_ref[...], wa_bf[...], preferred_element_type=F32)
    b = jnp.dot(yb_ref[...], wb_bf[...], preferred_element_type=F32)
    o_ref[...] = (ga_ref[...].astype(F32) * a + gb_ref[...].astype(F32) * b).astype(o_ref.dtype)


def _merge(ya, yb, wa, wb, gates, tm=1024, tn=1024):
    T, Ka = ya.shape
    Kb = yb.shape[1]
    N = wa.shape[1]
    nj = N // tn
    return pl.pallas_call(
        _merge_kernel,
        grid=(nj, T // tm),
        in_specs=[
            pl.BlockSpec((tm, Ka), lambda j, i: (i, 0)),
            pl.BlockSpec((tm, Kb), lambda j, i: (i, 0)),
            pl.BlockSpec((Ka, tn), lambda j, i: (0, j)),
            pl.BlockSpec((Kb, tn), lambda j, i: (0, j)),
            pl.BlockSpec((tm, tn), lambda j, i: (i, j)),
            pl.BlockSpec((tm, tn), lambda j, i, nj=nj: (i, j + nj)),
        ],
        out_specs=pl.BlockSpec((tm, tn), lambda j, i: (i, j)),
        out_shape=jax.ShapeDtypeStruct((T, N), BF16),
        scratch_shapes=[pltpu.VMEM((Ka, tn), BF16), pltpu.VMEM((Kb, tn), BF16)],
        compiler_params=_cparams(("parallel", "arbitrary")),
        name="branch_merge",
    )(ya, yb, wa, wb, gates, gates)


ROUTE_LANES = LANES


def _pack_halves(x_bf):
    bits = pltpu.bitcast(x_bf.astype(F32), jnp.uint32)
    n = bits.shape[1] // 2
    return (bits[:, :n] >> 16) | (bits[:, n:] & jnp.uint32(0xFFFF0000))


def _unpack_halves(p):
    lo = pltpu.bitcast(p << 16, F32)
    hi = pltpu.bitcast(p & jnp.uint32(0xFFFF0000), F32)
    return jnp.concatenate([lo, hi], axis=1)


ROW_TILE = 8


def _store_row_tiles(ref, packed):
    m = packed.shape[0]
    assert packed.shape[1] == ROW_TILE * LANES
    for c in range(ROW_TILE):
        ref[pl.ds(c, m, stride=ROW_TILE), :] = packed[:, c * LANES:(c + 1) * LANES]


def _load_row_tiles(ref, m):
    return jnp.concatenate([ref[pl.ds(c, m, stride=ROW_TILE), :] for c in range(ROW_TILE)], axis=1)


def _outproj_kernel(m_ref, w_ref, x_ref, g_ref, wrc_ref, br_ref, x1_ref, h2_ref, rt_ref):
    x1 = x_ref[...] + jnp.dot(m_ref[...], w_ref[...], preferred_element_type=F32)
    x1_ref[...] = x1
    ms = jnp.mean(x1 * x1, axis=-1, keepdims=True)
    h2 = x1 * lax.rsqrt(ms + NORM_EPS) * g_ref[...]
    h_hi = h2.astype(BF16)
    _store_row_tiles(h2_ref, _pack_halves(h_hi))
    h_lo = (h2 - h_hi.astype(F32)).astype(BF16)
    r = jnp.dot(h_hi, wrc_ref[...], preferred_element_type=F32)
    lg = (r[:, :ROUTE_LANES] + r[:, ROUTE_LANES:]
          + jnp.dot(h_lo, wrc_ref[:, :ROUTE_LANES], preferred_element_type=F32)) + br_ref[...]

    lane = lax.broadcasted_iota(jnp.int32, lg.shape, 1)
    lane_f = lane.astype(F32)
    big = float(ROUTE_LANES)
    is_group = (lane >= MOE_EXPERTS) & (lane < MOE_EXPERTS + MOE_GROUPS)
    lgg = jnp.where(is_group, lg, -jnp.inf)
    gmax = jnp.max(lgg, axis=-1, keepdims=True)
    gsel = jnp.min(jnp.where(lgg == gmax, lane_f - MOE_EXPERTS, big), axis=-1, keepdims=True)
    pg = 1.0 / jnp.sum(jnp.where(is_group, jnp.exp(lg - gmax), 0.0), axis=-1, keepdims=True)

    in_group = (lane < MOE_EXPERTS) & ((lane // MOE_EXPERTS_PER_GROUP).astype(F32) == gsel)
    le = jnp.where(in_group, lg, -jnp.inf)
    t1 = jnp.max(le, axis=-1, keepdims=True)
    i1 = jnp.min(jnp.where(le == t1, lane_f, big), axis=-1, keepdims=True)
    le2 = jnp.where(lane_f == i1, -jnp.inf, le)
    t2 = jnp.max(le2, axis=-1, keepdims=True)
    i2 = jnp.min(jnp.where(le2 == t2, lane_f, big), axis=-1, keepdims=True)
    e2 = jnp.exp(t2 - t1)
    w1 = pg / (1.0 + e2)
    w2 = pg * e2 / (1.0 + e2)
    rt_ref[...] = jnp.where(lane == 0, i1, jnp.where(lane == 1, i2, jnp.where(lane == 2, w1,
                            jnp.where(lane == 3, w2, 0.0))))


def _outproj(merged, w_out, x2d, gain, wr_cat, br, tm=512):
    T, D = x2d.shape
    row = lambda i: (i, 0)
    const = lambda i: (0, 0)
    return pl.pallas_call(
        _outproj_kernel,
        grid=(T // tm,),
        in_specs=[
            pl.BlockSpec((tm, D), row), pl.BlockSpec((D, D), const), pl.BlockSpec((tm, D), row),
            pl.BlockSpec((1, D), const), pl.BlockSpec((D, 2 * ROUTE_LANES), const),
            pl.BlockSpec((1, ROUTE_LANES), const),
        ],
        out_specs=[pl.BlockSpec((tm, D), row), pl.BlockSpec((tm * ROW_TILE, LANES), row),
                   pl.BlockSpec((tm, ROUTE_LANES), row)],
        out_shape=[jax.ShapeDtypeStruct((T, D), F32), jax.ShapeDtypeStruct((T * ROW_TILE, LANES), jnp.uint32),
                   jax.ShapeDtypeStruct((T, ROUTE_LANES), F32)],
        compiler_params=_cparams(("parallel",)),
        name="outproj_router",
    )(merged, w_out, x2d, gain.reshape(1, D).astype(F32), wr_cat, br)


def _dispatch_kernel(dest_ref, zb_ref, h_ref, xs_ref, zeros, sem, *, tq):
    base = pl.program_id(0) * tq
    block_rows = MOE_ROWS * ROW_TILE
    n_blocks = xs_ref.shape[0] // block_rows

    @pl.when(pl.program_id(0) == 0)
    def _():
        zeros[...] = jnp.zeros_like(zeros)

        def zero_copy(b):
            return pltpu.make_async_copy(zeros, xs_ref.at[pl.ds(pl.multiple_of(b * block_rows, block_rows), block_rows)],
                                         sem)

        def start(b, carry):
            @pl.when(zb_ref[b] != 0)
            def _():
                zero_copy(b).start()
            return carry

        def wait(b, carry):
            @pl.when(zb_ref[b] != 0)
            def _():
                zero_copy(b).wait()
            return carry

        lax.fori_loop(0, n_blocks, start, 0)
        lax.fori_loop(0, n_blocks, wait, 0)

    def token_rows(t):
        return pl.ds(pl.multiple_of(t * ROW_TILE, ROW_TILE), ROW_TILE)

    def copy(j, k):
        return pltpu.make_async_copy(h_ref.at[token_rows(j)], xs_ref.at[token_rows(dest_ref[2 * (base + j) + k])], sem)

    def issue(j, carry):
        for k in range(MOE_TOPK):
            copy(j, k).start(priority=k % 2)
        return carry

    lax.fori_loop(0, tq, issue, 0)
    for _ in range(MOE_TOPK):
        pltpu.make_async_copy(h_ref, xs_ref.at[pl.ds(0, tq * ROW_TILE)], sem).wait()


def _dispatch(h2, dest_flat, zero_block, n_rows, tq=512):
    T = h2.shape[0] // ROW_TILE
    return pl.pallas_call(
        functools.partial(_dispatch_kernel, tq=tq),
        grid_spec=pltpu.PrefetchScalarGridSpec(
            num_scalar_prefetch=2,
            grid=(T // tq,),
            in_specs=[pl.BlockSpec((tq * ROW_TILE, LANES), lambda i, d, z: (i, 0))],
            out_specs=pl.BlockSpec(memory_space=pl.ANY),
            scratch_shapes=[pltpu.VMEM((MOE_ROWS * ROW_TILE, LANES), h2.dtype), pltpu.SemaphoreType.DMA(())],
        ),
        out_shape=jax.ShapeDtypeStruct((n_rows * ROW_TILE, LANES), h2.dtype),
        compiler_params=_cparams(("arbitrary",)),
        name="moe_dispatch",
    )(dest_flat, zero_block, h2)


MOE_WEIGHT_SLOTS = 2


def _expert_kernel(be_ref, nu_ref, rank_ref, kth_ref, xs_ref, wg_hbm, wu_hbm, wd_hbm, o_ref,
                   wg_f, wu_f, wd_f, wg_s, wu_s, wd_s, sem):
    i = pl.program_id(0)
    e = be_ref[i]
    active = i < nu_ref[0]
    new_expert = active & ((i == 0) | (e != be_ref[jnp.maximum(i - 1, 0)]))
    k = rank_ref[e]
    slot = k % MOE_WEIGHT_SLOTS
    ahead = MOE_WEIGHT_SLOTS - 1

    def weight_copies(expert, s):
        return (pltpu.make_async_copy(wg_hbm.at[expert], wg_f.at[s], sem.at[s, 0]),
                pltpu.make_async_copy(wu_hbm.at[expert], wu_f.at[s], sem.at[s, 1]),
                pltpu.make_async_copy(wd_hbm.at[expert], wd_f.at[s], sem.at[s, 2]))

    def start_kth(j):
        nxt = kth_ref[j]

        @pl.when(nxt >= 0)
        def _():
            for c in weight_copies(nxt, j % MOE_WEIGHT_SLOTS):
                c.start()

    @pl.when(active & (i == 0))
    def _():
        for j in range(ahead):
            start_kth(j)

    @pl.when(new_expert)
    def _():
        start_kth(k + ahead)
        for c in weight_copies(e, slot):
            c.wait()
        wg_s[...] = wg_f[slot].astype(BF16)
        wu_s[...] = wu_f[slot].astype(BF16)
        wd_s[...] = wd_f[slot].astype(BF16)

    @pl.when(active)
    def _():
        x = _unpack_halves(_load_row_tiles(xs_ref, MOE_ROWS)).astype(BF16)
        g = jnp.dot(x, wg_s[...], preferred_element_type=F32)
        u = jnp.dot(x, wu_s[...], preferred_element_type=F32)
        a = (g * jax.nn.sigmoid(g) * u).astype(BF16)
        y = jnp.dot(a, wd_s[...], preferred_element_type=F32)
        _store_row_tiles(o_ref, _pack_halves(y.astype(BF16)))

    @pl.when(jnp.logical_not(active))
    def _():
        o_ref[...] = jnp.zeros_like(o_ref)


def _experts(xs, block_expert, n_used, expert_rank, kth_expert, w_gate, w_up, w_down):
    Dh = xs.shape[1]
    D, F = w_gate.shape[1], w_gate.shape[2]
    assert Dh == LANES and D == 2 * ROW_TILE * LANES
    nblk = block_expert.shape[0]
    P = nblk * MOE_ROWS * ROW_TILE
    rows = lambda i, be, nu, nx, sl: (jnp.minimum(i, nu[0] - 1), 0)
    hbm = pl.BlockSpec(memory_space=pl.ANY)
    return pl.pallas_call(
        _expert_kernel,
        grid_spec=pltpu.PrefetchScalarGridSpec(
            num_scalar_prefetch=4,
            grid=(nblk,),
            in_specs=[pl.BlockSpec((MOE_ROWS * ROW_TILE, Dh), rows), hbm, hbm, hbm],
            out_specs=pl.BlockSpec((MOE_ROWS * ROW_TILE, Dh), lambda i, be, nu, nx, sl: (i, 0)),
            scratch_shapes=[pltpu.VMEM((MOE_WEIGHT_SLOTS, D, F), F32), pltpu.VMEM((MOE_WEIGHT_SLOTS, D, F), F32),
                            pltpu.VMEM((MOE_WEIGHT_SLOTS, F, D), F32),
                            pltpu.VMEM((D, F), BF16), pltpu.VMEM((D, F), BF16), pltpu.VMEM((F, D), BF16),
                            pltpu.SemaphoreType.DMA((MOE_WEIGHT_SLOTS, 3))],
        ),
        out_shape=jax.ShapeDtypeStruct((P, Dh), jnp.uint32),
        compiler_params=_cparams(("arbitrary",)),
        name="moe_experts",
    )(block_expert, n_used, expert_rank, kth_expert, xs, w_gate, w_up, w_down)


def _combine_kernel(dest_ref, x1_ref, rt_ref, g_ref, ys_ref, o_ref, buf, sem, *, tq):
    i = pl.program_id(0)
    n = pl.num_programs(0)

    def token_rows(t):
        return pl.ds(pl.multiple_of(t * ROW_TILE, ROW_TILE), ROW_TILE)

    def issue(step, s):
        def body(j, carry):
            for k in range(MOE_TOPK):
                pltpu.make_async_copy(ys_ref.at[token_rows(dest_ref[2 * (step * tq + j) + k])],
                                      buf.at[s, k, token_rows(j)], sem.at[s]).start(priority=1)
            return carry
        lax.fori_loop(0, tq, body, 0)

    @pl.when(i == 0)
    def _():
        issue(0, 0)

    for s in range(2):
        @pl.when((i + 1 < n) & ((i + 1) % 2 == s))
        def _():
            issue(i + 1, s)

    for s in range(2):
        @pl.when(i % 2 == s)
        def _():
            for k in range(MOE_TOPK):
                pltpu.make_async_copy(ys_ref.at[pl.ds(0, tq * ROW_TILE)], buf.at[s, k], sem.at[s]).wait()
            rt = rt_ref[...]
            y0 = _unpack_halves(_load_row_tiles(buf.at[s, 0], tq))
            y1 = _unpack_halves(_load_row_tiles(buf.at[s, 1], tq))
            x = x1_ref[...] + rt[:, 2:3] * y0 + rt[:, 3:4] * y1
            ms = jnp.mean(x * x, axis=-1, keepdims=True)
            o_ref[...] = x * lax.rsqrt(ms + NORM_EPS) * g_ref[...]


def _combine(x1, route, gain, ys, dest_flat, tq=256):
    T, D = x1.shape
    return pl.pallas_call(
        functools.partial(_combine_kernel, tq=tq),
        grid_spec=pltpu.PrefetchScalarGridSpec(
            num_scalar_prefetch=1,
            grid=(T // tq,),
            in_specs=[
                pl.BlockSpec((tq, D), lambda i, d: (i, 0)),
                pl.BlockSpec((tq, ROUTE_LANES), lambda i, d: (i, 0)),
                pl.BlockSpec((1, D), lambda i, d: (0, 0)),
                pl.BlockSpec(memory_space=pl.ANY),
            ],
            out_specs=pl.BlockSpec((tq, D), lambda i, d: (i, 0)),
            scratch_shapes=[pltpu.VMEM((2, MOE_TOPK, tq * ROW_TILE, LANES), jnp.uint32),
                            pltpu.SemaphoreType.DMA((2,))],
        ),
        out_shape=jax.ShapeDtypeStruct((T, D), F32),
        compiler_params=_cparams(("arbitrary",)),
        name="moe_combine",
    )(dest_flat, x1, route, gain.reshape(1, D).astype(F32), ys)


def _route_metadata(route, T):
    e = route[:, :MOE_TOPK].astype(jnp.int32)
    ids = jnp.arange(MOE_EXPERTS, dtype=jnp.int32)
    oh0 = e[:, 0, None] == ids
    oh1 = e[:, 1, None] == ids
    onehot = (oh0 | oh1).astype(jnp.int32)
    incl = jnp.cumsum(onehot, axis=0)
    counts = incl[-1]
    pcounts = (counts + MOE_ROWS - 1) // MOE_ROWS * MOE_ROWS
    pends = jnp.cumsum(pcounts)
    poffs = pends - pcounts
    slot = incl - onehot + poffs[None, :]
    dest = jnp.stack([jnp.sum(jnp.where(oh0, slot, 0), axis=1), jnp.sum(jnp.where(oh1, slot, 0), axis=1)], axis=1)
    nblk = (T * MOE_TOPK) // MOE_ROWS + MOE_EXPERTS
    first_row = jnp.arange(nblk, dtype=jnp.int32) * MOE_ROWS
    block_expert = jnp.minimum(jnp.sum((pends[None, :] <= first_row[:, None]).astype(jnp.int32), axis=1),
                               MOE_EXPERTS - 1).astype(jnp.int32)
    n_used = (pends[-1:] // MOE_ROWS).astype(jnp.int32)
    used = counts > 0
    expert_rank = (jnp.cumsum(used.astype(jnp.int32)) - 1).astype(jnp.int32)
    order = jnp.argsort(jnp.where(used, ids, ids + MOE_EXPERTS)).astype(jnp.int32)
    kth = jnp.where(ids < jnp.sum(used), order, -1)
    kth_expert = jnp.concatenate([kth, jnp.full((MOE_WEIGHT_SLOTS,), -1, jnp.int32)])
    blk = jnp.arange(nblk, dtype=jnp.int32)
    following = jnp.concatenate([block_expert[1:], block_expert[-1:]])
    zero_block = ((blk >= n_used[0] - 1) | (block_expert != following)).astype(jnp.int32)
    return dest.reshape(-1).astype(jnp.int32), zero_block, block_expert, n_used, expert_rank, kth_expert, nblk


def kernel(x, norm1_gain, w_in, hg_lb_logits, hg_norm_gain, rel_bias, w_branch_a, w_branch_b, w_out,
           norm2_gain, w_router_group, b_router_group, w_router_expert, b_router_expert,
           w_exp_gate, w_exp_up, w_exp_down, final_norm_gain):
    B, S, D = x.shape
    T = B * S
    depth = norm1_gain.shape[0]
    lower_bounds = jnp.cumsum(jax.nn.softmax(hg_lb_logits.astype(F32), axis=0), axis=0)
    att_w = ATT_HEADS * HEAD_DIM
    bias = _attn_bias(rel_bias)
    x2d = x.reshape(T, D)
    for layer in range(depth):
        zeros_lb = jnp.zeros((1, 3 * att_w), F32)
        hm = dict(head_major=True, B=B, S=S)
        h, q_a = _normproj(x2d, norm1_gain[layer], w_in, layer, D, B=B, S=S)
        lf_a = _inproj(h, w_in, layer, D, D, lower_bounds[layer].reshape(1, D), mode="forget", out_dtype=F32, **hm)
        i_a = _inproj(h, w_in, layer, 2 * D, D, zeros_lb, mode="none", out_dtype=BF16, **hm)
        sg_a = _inproj(h, w_in, layer, 3 * D, D, zeros_lb, mode="sigmoid", out_dtype=BF16, **hm)
        qkv_b = _inproj(h, w_in, layer, 4 * D, 3 * att_w, zeros_lb, mode="none", out_dtype=BF16, tn=att_w, **hm)
        gates = _inproj(h, w_in, layer, 4 * D + 3 * att_w, 2 * D, zeros_lb, mode="sigmoid",
                        head_major=False, out_dtype=BF16, B=B, S=S)

        y_a = _hgrn(q_a, lf_a, i_a, sg_a, hg_norm_gain[layer]).reshape(T, D)
        y_b = _attention(qkv_b, bias).reshape(T, ATT_HEADS_PER_GROUP * HEAD_DIM)
        merged = _merge(y_a, y_b, w_branch_a[layer], w_branch_b[layer], gates)

        n_pad = ROUTE_LANES - MOE_EXPERTS - MOE_GROUPS
        wr = jnp.concatenate([w_router_expert[layer].astype(F32), w_router_group[layer].astype(F32),
                              jnp.zeros((D, n_pad), F32)], axis=1)
        br = jnp.concatenate([b_router_expert[layer].astype(F32), b_router_group[layer].astype(F32),
                              jnp.zeros((n_pad,), F32)]).reshape(1, ROUTE_LANES)
        wr_hi = wr.astype(BF16)
        wr_lo = (wr - wr_hi.astype(F32)).astype(BF16)
        x1, h2, route = _outproj(merged, w_out[layer].astype(BF16), x2d, norm2_gain[layer],
                                 jnp.concatenate([wr_hi, wr_lo], axis=1), br)

        dest, zero_block, block_expert, n_used, expert_rank, kth_expert, nblk = _route_metadata(route, T)
        xs = _dispatch(h2, dest, zero_block, nblk * MOE_ROWS)
        ys = _experts(xs, block_expert, n_used, expert_rank, kth_expert,
                      w_exp_gate[layer], w_exp_up[layer], w_exp_down[layer])
        last = layer == depth - 1
        assert last, "the fused combine applies the final norm; deeper stacks need an un-normalised combine"
        x2d = _combine(x1, route, final_norm_gain, ys, dest)
    return x2d.reshape(B, S, D)
```

```python
import functools
import math

import numpy as np
import jax
import jax.numpy as jnp
from jax import lax
from jax.experimental import pallas as pl
from jax.experimental.pallas import tpu as pltpu

F32 = jnp.float32
BF16 = jnp.bfloat16

LANES = 128
NORM_EPS = 1e-6
HEAD_DIM = 128
ATT_GROUPS = ((128, 1), (512, 4), (2048, 16))
ATT_HEADS_PER_GROUP = 4
ATT_HEADS = len(ATT_GROUPS) * ATT_HEADS_PER_GROUP
ATT_BLOCK = 128
ATT_STACK = 16
REL_BUCKETS = 32
REL_MAX_DIST = 2048
MOE_GROUPS = 8
MOE_EXPERTS_PER_GROUP = 8
MOE_EXPERTS = MOE_GROUPS * MOE_EXPERTS_PER_GROUP
MOE_TOPK = 2
MOE_ROWS = 256
NEG_BIG = -1e30
VMEM_LIMIT = 56 * 1024 * 1024


def _cparams(sem):
    return pltpu.CompilerParams(dimension_semantics=sem, vmem_limit_bytes=VMEM_LIMIT)


def _normproj_kernel(x_ref, g_ref, w_ref, h_ref, o_ref, w_bf):
    @pl.when(pl.program_id(0) == 0)
    def _():
        w_bf[...] = w_ref[...].astype(BF16)

    x = x_ref[...]
    ms = jnp.mean(x * x, axis=-1, keepdims=True)
    h = (x * lax.rsqrt(ms + NORM_EPS) * g_ref[...]).astype(h_ref.dtype)
    h_ref[...] = h
    acc = jnp.dot(h, w_bf[...], preferred_element_type=F32)
    for hh in range(acc.shape[1] // HEAD_DIM):
        o_ref[0, hh] = acc[:, hh * HEAD_DIM:(hh + 1) * HEAD_DIM].astype(o_ref.dtype)


def _normproj(x2d, gain, w_all, layer, N, *, B, S, tm=512):
    T, D = x2d.shape
    assert S % tm == 0
    spb = S // tm
    return pl.pallas_call(
        _normproj_kernel,
        grid=(T // tm,),
        in_specs=[
            pl.BlockSpec((tm, D), lambda i: (i, 0)),
            pl.BlockSpec((1, D), lambda i: (0, 0)),
            pl.BlockSpec((None, D, N), lambda i: (layer, 0, 0), pipeline_mode=pl.Buffered(1)),
        ],
        out_specs=[
            pl.BlockSpec((tm, D), lambda i: (i, 0)),
            pl.BlockSpec((1, N // HEAD_DIM, tm, HEAD_DIM), lambda i: (i // spb, 0, i % spb, 0)),
        ],
        out_shape=[jax.ShapeDtypeStruct((T, D), BF16),
                   jax.ShapeDtypeStruct((B, N // HEAD_DIM, S, HEAD_DIM), BF16)],
        scratch_shapes=[pltpu.VMEM((D, N), BF16)],
        compiler_params=_cparams(("arbitrary",)),
        name="norm_inproj",
    )(x2d, gain.reshape(1, D).astype(F32), w_all)


def _inproj_kernel(h_ref, w_ref, lb_ref, o_ref, w_bf, *, mode, head_major):
    @pl.when(pl.program_id(1) == 0)
    def _():
        w_bf[...] = w_ref[0].astype(BF16)

    acc = jnp.dot(h_ref[...], w_bf[...], preferred_element_type=F32)
    if mode == "forget":
        lb = lb_ref[...]
        acc = lb + (1.0 - lb) * jax.nn.sigmoid(acc)
    elif mode == "sigmoid":
        acc = 0.5 * jnp.tanh(0.5 * acc) + 0.5
    if head_major:
        for hh in range(acc.shape[1] // HEAD_DIM):
            o_ref[0, hh] = acc[:, hh * HEAD_DIM:(hh + 1) * HEAD_DIM].astype(o_ref.dtype)
    else:
        o_ref[...] = acc.astype(o_ref.dtype)


def _inproj(h, w_all, layer, col0, N, lb, *, mode, head_major, out_dtype, B, S, tm=1024, tn=1024):
    T, D = h.shape
    assert N % tn == 0 and col0 % LANES == 0 and T % tm == 0 and S % tm == 0
    spb = S // tm
    if head_major:
        out_shape = jax.ShapeDtypeStruct((B, N // HEAD_DIM, S, HEAD_DIM), out_dtype)
        out_spec = pl.BlockSpec((1, tn // HEAD_DIM, tm, HEAD_DIM), lambda j, i: (i // spb, j, i % spb, 0))
    else:
        out_shape = jax.ShapeDtypeStruct((T, N), out_dtype)
        out_spec = pl.BlockSpec((tm, tn), lambda j, i: (i, j))
    return pl.pallas_call(
        functools.partial(_inproj_kernel, mode=mode, head_major=head_major),
        grid=(N // tn, T // tm),
        in_specs=[
            pl.BlockSpec((tm, D), lambda j, i: (i, 0)),
            pl.BlockSpec((pl.Element(1), pl.Element(D), pl.Element(tn)), lambda j, i: (layer, 0, pl.multiple_of(col0 + j * tn, LANES))),
            pl.BlockSpec((1, tn), lambda j, i: (0, j)),
        ],
        out_specs=out_spec,
        out_shape=out_shape,
        scratch_shapes=[pltpu.VMEM((D, tn), BF16)],
        compiler_params=_cparams(("parallel", "arbitrary")),
        name="inproj_" + mode,
    )(h, w_all, lb)


HG_CHUNK = 256
HG_DIAG = 16
HG_HEADS_PER_STEP = 4


def _hgrn_levels(C):
    out, m = [], C // 2
    while m >= HG_DIAG:
        out.append(m)
        m //= 2
    return out


def _hgrn_masks(C):
    t = np.arange(C)[:, None]
    s = np.arange(C)[None, :]
    masks = []
    for m in _hgrn_levels(C):
        masks.append((t // (2 * m) == s // (2 * m)) & ((t // m) % 2 == 1) & ((s // m) % 2 == 0))
    masks.append((t // HG_DIAG == s // HG_DIAG) & (t >= s))
    total = np.sum(np.stack(masks).astype(np.int32), axis=0)
    assert np.array_equal(total, (t >= s).astype(np.int32))
    return np.stack(masks).astype(np.float32)


def _dot_nt(a, b):
    return lax.dot_general(a, b, (((1,), (1,)), ((), ())), preferred_element_type=F32)


def _dot_tn(a, b):
    return lax.dot_general(a, b, (((0,), (0,)), ((), ())), preferred_element_type=F32)


def _hgrn_kernel(q_ref, f_ref, v_ref, sg_ref, gain_ref, tril_ref, mask_ref, o_ref, st_ref, b_ref, *, C):
    S = q_ref.shape[2]
    HP = q_ref.shape[1]
    levels = _hgrn_levels(C)
    nchunks = S // C
    tril = tril_ref[...]

    def ref_rows(h, block, row_of_block):
        parts = [jnp.broadcast_to(b_ref[h, pl.ds(row_of_block(p), 1), :], (block, HEAD_DIM))
                 for p in range(C // block)]
        return jnp.concatenate(parts, axis=0)

    def cumsum(c):
        r0 = pl.multiple_of(c * C, C)
        out = []
        for h in range(HP):
            lf2 = jnp.log2(f_ref[0, h, pl.ds(r0, C), :])
            hi = lf2.astype(BF16)
            lo = (lf2 - hi.astype(F32)).astype(BF16)
            r = jnp.dot(tril, jnp.concatenate([hi, lo], axis=1), preferred_element_type=F32)
            out.append(r[:, :HEAD_DIM] + r[:, HEAD_DIM:])
        return tuple(out)

    st_ref[...] = jnp.zeros_like(st_ref)

    def chunk(c, b_all):
        r0 = pl.multiple_of(c * C, C)
        rows = pl.ds(r0, C)
        b_next = cumsum(jnp.minimum(c + 1, nchunks - 1))
        for h in range(HP):
            b = b_all[h]
            b_ref[h] = b
            q = q_ref[0, h, rows, :].astype(F32)
            kk = 1.0 - f_ref[0, h, rows, :]
            b_last = b_ref[h, pl.ds(C - 1, 1), :]
            scores = jnp.zeros((C, C), BF16)
            half = C // 2
            ref_top = b_ref[h, pl.ds(half - 1, 1), :]
            a_top = (q[half:] * jnp.exp2(b[half:] - ref_top)).astype(BF16)
            k_top = (kk[:half] * jnp.exp2(ref_top - b[:half])).astype(BF16)
            s_top = _dot_nt(a_top, k_top).astype(BF16)
            for li, m in enumerate(levels):
                if m == half:
                    continue
                parts = []
                for p in range(C // (2 * m)):
                    ref = b_ref[h, pl.ds(2 * m * p + m - 1, 1), :]
                    parts += [ref - b[2 * m * p:2 * m * p + m], b[2 * m * p + m:2 * m * (p + 1)] - ref]
                e = jnp.exp2(jnp.concatenate(parts, axis=0))
                s_l = _dot_nt((q * e).astype(BF16), (kk * e).astype(BF16)).astype(BF16)
                scores = scores + mask_ref[li] * s_l
            d = b - ref_rows(h, HG_DIAG, lambda p: HG_DIAG * p + HG_DIAG // 2 - 1)
            s_l = _dot_nt((q * jnp.exp2(d)).astype(BF16), (kk * jnp.exp2(-d)).astype(BF16)).astype(BF16)
            scores = scores + mask_ref[len(levels)] * s_l
            v = v_ref[0, h, rows, :]
            st = st_ref[h]
            qe = (q * jnp.exp2(b)).astype(BF16)
            kd = (kk * jnp.exp2(b_last - b)).astype(BF16)
            o = jnp.dot(scores, v, preferred_element_type=F32) + _dot_nt(qe, st.astype(BF16))
            o_top = jnp.dot(s_top, v[:half], preferred_element_type=F32)
            o = jnp.concatenate([o[:half], o[half:] + o_top], axis=0)
            st_ref[h] = st * jnp.exp2(b_last) + _dot_tn(v, kd)
            ms = jnp.mean(o * o, axis=-1, keepdims=True)
            y = o * lax.rsqrt(ms + NORM_EPS) * gain_ref[:, h * HEAD_DIM:(h + 1) * HEAD_DIM]
            y = y * sg_ref[0, h, rows, :].astype(F32)
            o_ref[0, rows, h * HEAD_DIM:(h + 1) * HEAD_DIM] = y.astype(o_ref.dtype)
        return b_next

    lax.fori_loop(0, nchunks, chunk, cumsum(0))


def _hgrn(q, lf, v, sg, gain):
    B, H, S, E = q.shape
    C, HP = HG_CHUNK, HG_HEADS_PER_STEP
    masks = jnp.asarray(_hgrn_masks(C), dtype=BF16)
    tril = jnp.asarray(np.tril(np.ones((C, C), np.float32)), dtype=BF16)
    head_spec = pl.BlockSpec((1, HP, S, E), lambda b, h: (b, h, 0, 0))
    return pl.pallas_call(
        functools.partial(_hgrn_kernel, C=C),
        grid=(B, H // HP),
        in_specs=[
            head_spec, head_spec, head_spec, head_spec,
            pl.BlockSpec((1, HP * E), lambda b, h: (0, h)),
            pl.BlockSpec((C, C), lambda b, h: (0, 0)),
            pl.BlockSpec(masks.shape, lambda b, h: (0, 0, 0)),
        ],
        out_specs=pl.BlockSpec((1, S, HP * E), lambda b, h: (b, 0, h)),
        out_shape=jax.ShapeDtypeStruct((B, S, H * E), BF16),
        scratch_shapes=[pltpu.VMEM((HP, E, E), F32), pltpu.VMEM((HP, C, E), F32)],
        compiler_params=_cparams(("parallel", "parallel")),
        name="hgrn2",
    )(q, lf, v, sg, gain.reshape(1, H * E).astype(F32), tril, masks)


def _t5_bucket_np(dist):
    exact = REL_BUCKETS // 2
    d_f = np.maximum(dist, 1).astype(np.float32)
    log_b = exact + (np.log(d_f / np.float32(exact)) / np.float32(math.log(REL_MAX_DIST / exact))
                     * np.float32(REL_BUCKETS - exact)).astype(np.int32)
    return np.where(dist < exact, dist, np.minimum(log_b, REL_BUCKETS - 1))


def _attn_bias(rel_bias):
    blk = ATT_BLOCK
    period = 3 * blk
    out = []
    for gi, (window, dilation) in enumerate(ATT_GROUPS):
        n_back = window // dilation
        assert n_back <= blk
        hs = slice(gi * ATT_HEADS_PER_GROUP, (gi + 1) * ATT_HEADS_PER_GROUP)
        bucket = _t5_bucket_np(np.arange(n_back + 1) * dilation)
        by_delta = rel_bias[:, hs][bucket].astype(F32).T
        u = jnp.full((ATT_HEADS_PER_GROUP, period), NEG_BIG, F32)
        u = u.at[:, 2 * blk - 1 - n_back:2 * blk].set(by_delta[:, ::-1])
        flat = jnp.tile(u, (1, blk))[:, :blk * (period - 1)]
        out.append(flat.reshape(ATT_HEADS_PER_GROUP, blk, period - 1)[:, :, blk - 1:3 * blk - 1])
    return jnp.stack(out, axis=0)


def _attn_kernel(q_ref, k_ref, v_ref, bias_ref, o_ref, qf, kf, vf, og, lg):
    S = q_ref.shape[3]
    scale = HEAD_DIM ** -0.5
    blk = ATT_BLOCK

    for g, (window, d) in enumerate(ATT_GROUPS):
        L = S // d
        nb = L // blk
        if d > 1:
            qf[...] = q_ref[0, g, 0].astype(F32)
            kf[...] = k_ref[0, g, 0].astype(F32)
            vf[...] = v_ref[0, g, 0].astype(F32)

        def load(ref_bf, ref_f32, start, size, g=g, d=d):
            if d == 1:
                return ref_bf[0, g, 0, pl.ds(start, size), :]
            return ref_f32[pl.ds(start, size, stride=d), :].astype(BF16)

        def scores(r, n, g=g, d=d, load=load):
            start = r + n * blk * d
            q = load(q_ref, qf, start, blk)
            if n == 0:
                k = load(k_ref, kf, start, blk)
                v = load(v_ref, vf, start, blk)
                s = _dot_nt(q, k) * scale + bias_ref[g, 0, :, blk:]
            else:
                first = start - blk * d
                k = load(k_ref, kf, first, 2 * blk)
                v = load(v_ref, vf, first, 2 * blk)
                s = _dot_nt(q, k) * scale + bias_ref[g, 0]
            return start, s, v

        def softmax_pv(items, g=g, d=d):
            s = jnp.concatenate([it[1] for it in items], axis=0)
            m = jnp.max(s, axis=-1, keepdims=True)
            p = jnp.exp(s - m)
            den = jnp.sum(p, axis=-1, keepdims=True)
            pb = p.astype(BF16)
            lse = jnp.broadcast_to(m + jnp.log(den), (s.shape[0], HEAD_DIM))
            for j, (start, _, v) in enumerate(items):
                sl = slice(j * blk, (j + 1) * blk)
                rows = pl.ds(start, blk, stride=d) if d > 1 else pl.ds(start, blk)
                og[g, rows, :] = jnp.dot(pb[sl], v, preferred_element_type=F32) / den[sl]
                lg[g, rows, :] = lse[sl]

        blocks = [(r, n) for r in range(d) for n in range(nb)]
        first_blocks = [bn for bn in blocks if bn[1] == 0]
        later_blocks = [bn for bn in blocks if bn[1] > 0]
        for group in (first_blocks, later_blocks):
            for j0 in range(0, len(group), ATT_STACK):
                softmax_pv([scores(r, n) for r, n in group[j0:j0 + ATT_STACK]])

    rows_per_step = 256

    def mix(i, carry):
        rows = pl.ds(pl.multiple_of(i * rows_per_step, rows_per_step), rows_per_step)
        l0, l1, l2 = lg[0, rows, :], lg[1, rows, :], lg[2, rows, :]
        mx = jnp.maximum(jnp.maximum(l0, l1), l2)
        e0, e1, e2 = jnp.exp(l0 - mx), jnp.exp(l1 - mx), jnp.exp(l2 - mx)
        num = e0 * og[0, rows, :] + e1 * og[1, rows, :] + e2 * og[2, rows, :]
        o_ref[0, rows, :] = (num / (e0 + e1 + e2)).astype(o_ref.dtype)
        return carry

    lax.fori_loop(0, S // rows_per_step, mix, 0)


def _attention(qkv, bias):
    B, _, S, E = qkv.shape
    G, HG = len(ATT_GROUPS), ATT_HEADS_PER_GROUP
    x = qkv.reshape(B, 3, G, HG, S, E)

    def spec(which):
        return pl.BlockSpec((1, None, G, 1, S, E), lambda b, h, which=which: (b, which, 0, h, 0, 0))

    def kern(q_ref, k_ref, v_ref, bias_ref, o_ref, *scratch):
        _attn_kernel(q_ref, k_ref, v_ref, bias_ref, o_ref, *scratch)

    return pl.pallas_call(
        kern,
        grid=(B, HG),
        in_specs=[spec(0), spec(1), spec(2),
                  pl.BlockSpec((G, 1, ATT_BLOCK, 2 * ATT_BLOCK), lambda b, h: (0, h, 0, 0))],
        out_specs=pl.BlockSpec((1, S, E), lambda b, h: (b, 0, h)),
        out_shape=jax.ShapeDtypeStruct((B, S, HG * E), BF16),
        scratch_shapes=[pltpu.VMEM((S, E), F32)] * 3 + [pltpu.VMEM((G, S, E), F32)] * 2,
        compiler_params=_cparams(("parallel", "parallel")),
        name="dilated_attn",
    )(x, x, x, bias)


def _merge_kernel(ya_ref, yb_ref, wa_ref, wb_ref, ga_ref, gb_ref, o_ref, wa_bf, wb_bf):
    @pl.when(pl.program_id(1) == 0)
    def _():
        wa_bf[...] = wa_ref[...].astype(BF16)
        wb_bf[...] = wb_ref[...].astype(BF16)

    a = jnp.dot(ya_ref[...], wa_bf[...], preferred_element_type=F32)
    b = jnp.dot(yb_ref[...], wb_bf[...], preferred_element_type=F32)
    o_ref[...] = (ga_ref[...].astype(F32) * a + gb_ref[...].astype(F32) * b).astype(o_ref.dtype)


def _merge(ya, yb, wa, wb, gates, tm=1024, tn=1024):
    T, Ka = ya.shape
    Kb = yb.shape[1]
    N = wa.shape[1]
    nj = N // tn
    return pl.pallas_call(
        _merge_kernel,
        grid=(nj, T // tm),
        in_specs=[
            pl.BlockSpec((tm, Ka), lambda j, i: (i, 0)),
            pl.BlockSpec((tm, Kb), lambda j, i: (i, 0)),
            pl.BlockSpec((Ka, tn), lambda j, i: (0, j)),
            pl.BlockSpec((Kb, tn), lambda j, i: (0, j)),
            pl.BlockSpec((tm, tn), lambda j, i: (i, j)),
            pl.BlockSpec((tm, tn), lambda j, i, nj=nj: (i, j + nj)),
        ],
        out_specs=pl.BlockSpec((tm, tn), lambda j, i: (i, j)),
        out_shape=jax.ShapeDtypeStruct((T, N), BF16),
        scratch_shapes=[pltpu.VMEM((Ka, tn), BF16), pltpu.VMEM((Kb, tn), BF16)],
        compiler_params=_cparams(("parallel", "arbitrary")),
        name="branch_merge",
    )(ya, yb, wa, wb, gates, gates)


ROUTE_LANES = LANES


def _pack_halves(x_bf):
    bits = pltpu.bitcast(x_bf.astype(F32), jnp.uint32)
    n = bits.shape[1] // 2
    return (bits[:, :n] >> 16) | (bits[:, n:] & jnp.uint32(0xFFFF0000))


def _unpack_halves(p):
    lo = pltpu.bitcast(p << 16, F32)
    hi = pltpu.bitcast(p & jnp.uint32(0xFFFF0000), F32)
    return jnp.concatenate([lo, hi], axis=1)


ROW_TILE = 8


def _store_row_tiles(ref, packed):
    m = packed.shape[0]
    assert packed.shape[1] == ROW_TILE * LANES
    for c in range(ROW_TILE):
        ref[pl.ds(c, m, stride=ROW_TILE), :] = packed[:, c * LANES:(c + 1) * LANES]


def _load_row_tiles(ref, m):
    return jnp.concatenate([ref[pl.ds(c, m, stride=ROW_TILE), :] for c in range(ROW_TILE)], axis=1)


def _outproj_kernel(m_ref, w_ref, x_ref, g_ref, wrc_ref, br_ref, x1_ref, h2_ref, rt_ref):
    x1 = x_ref[...] + jnp.dot(m_ref[...], w_ref[...], preferred_element_type=F32)
    x1_ref[...] = x1
    ms = jnp.mean(x1 * x1, axis=-1, keepdims=True)
    h2 = x1 * lax.rsqrt(ms + NORM_EPS) * g_ref[...]
    h_hi = h2.astype(BF16)
    _store_row_tiles(h2_ref, _pack_halves(h_hi))
    h_lo = (h2 - h_hi.astype(F32)).astype(BF16)
    r = jnp.dot(h_hi, wrc_ref[...], preferred_element_type=F32)
    lg = (r[:, :ROUTE_LANES] + r[:, ROUTE_LANES:]
          + jnp.dot(h_lo, wrc_ref[:, :ROUTE_LANES], preferred_element_type=F32)) + br_ref[...]

    lane = lax.broadcasted_iota(jnp.int32, lg.shape, 1)
    lane_f = lane.astype(F32)
    big = float(ROUTE_LANES)
    is_group = (lane >= MOE_EXPERTS) & (lane < MOE_EXPERTS + MOE_GROUPS)
    lgg = jnp.where(is_group, lg, -jnp.inf)
    gmax = jnp.max(lgg, axis=-1, keepdims=True)
    gsel = jnp.min(jnp.where(lgg == gmax, lane_f - MOE_EXPERTS, big), axis=-1, keepdims=True)
    pg = 1.0 / jnp.sum(jnp.where(is_group, jnp.exp(lg - gmax), 0.0), axis=-1, keepdims=True)

    in_group = (lane < MOE_EXPERTS) & ((lane // MOE_EXPERTS_PER_GROUP).astype(F32) == gsel)
    le = jnp.where(in_group, lg, -jnp.inf)
    t1 = jnp.max(le, axis=-1, keepdims=True)
    i1 = jnp.min(jnp.where(le == t1, lane_f, big), axis=-1, keepdims=True)
    le2 = jnp.where(lane_f == i1, -jnp.inf, le)
    t2 = jnp.max(le2, axis=-1, keepdims=True)
    i2 = jnp.min(jnp.where(le2 == t2, lane_f, big), axis=-1, keepdims=True)
    e2 = jnp.exp(t2 - t1)
    w1 = pg / (1.0 + e2)
    w2 = pg * e2 / (1.0 + e2)
    rt_ref[...] = jnp.where(lane == 0, i1, jnp.where(lane == 1, i2, jnp.where(lane == 2, w1,
                            jnp.where(lane == 3, w2, 0.0))))


def _outproj(merged, w_out, x2d, gain, wr_cat, br, tm=512):
    T, D = x2d.shape
    row = lambda i: (i, 0)
    const = lambda i: (0, 0)
    return pl.pallas_call(
        _outproj_kernel,
        grid=(T // tm,),
        in_specs=[
            pl.BlockSpec((tm, D), row), pl.BlockSpec((D, D), const), pl.BlockSpec((tm, D), row),
            pl.BlockSpec((1, D), const), pl.BlockSpec((D, 2 * ROUTE_LANES), const),
            pl.BlockSpec((1, ROUTE_LANES), const),
        ],
        out_specs=[pl.BlockSpec((tm, D), row), pl.BlockSpec((tm * ROW_TILE, LANES), row),
                   pl.BlockSpec((tm, ROUTE_LANES), row)],
        out_shape=[jax.ShapeDtypeStruct((T, D), F32), jax.ShapeDtypeStruct((T * ROW_TILE, LANES), jnp.uint32),
                   jax.ShapeDtypeStruct((T, ROUTE_LANES), F32)],
        compiler_params=_cparams(("parallel",)),
        name="outproj_router",
    )(merged, w_out, x2d, gain.reshape(1, D).astype(F32), wr_cat, br)


def _dispatch_kernel(dest_ref, zb_ref, h_ref, xs_ref, zeros, sem, *, tq):
    base = pl.program_id(0) * tq
    block_rows = MOE_ROWS * ROW_TILE
    n_blocks = xs_ref.shape[0] // block_rows

    @pl.when(pl.program_id(0) == 0)
    def _():
        zeros[...] = jnp.zeros_like(zeros)

        def zero_copy(b):
            return pltpu.make_async_copy(zeros, xs_ref.at[pl.ds(pl.multiple_of(b * block_rows, block_rows), block_rows)],
                                         sem)

        def start(b, carry):
            @pl.when(zb_ref[b] != 0)
            def _():
                zero_copy(b).start()
            return carry

        def wait(b, carry):
            @pl.when(zb_ref[b] != 0)
            def _():
                zero_copy(b).wait()
            return carry

        lax.fori_loop(0, n_blocks, start, 0)
        lax.fori_loop(0, n_blocks, wait, 0)

    def token_rows(t):
        return pl.ds(pl.multiple_of(t * ROW_TILE, ROW_TILE), ROW_TILE)

    def copy(j, k):
        return pltpu.make_async_copy(h_ref.at[token_rows(j)], xs_ref.at[token_rows(dest_ref[2 * (base + j) + k])], sem)

    def issue(j, carry):
        for k in range(MOE_TOPK):
            copy(j, k).start(priority=k % 2)
        return carry

    lax.fori_loop(0, tq, issue, 0)
    for _ in range(MOE_TOPK):
        pltpu.make_async_copy(h_ref, xs_ref.at[pl.ds(0, tq * ROW_TILE)], sem).wait()


def _dispatch(h2, dest_flat, zero_block, n_rows, tq=2048):
    T = h2.shape[0] // ROW_TILE
    return pl.pallas_call(
        functools.partial(_dispatch_kernel, tq=tq),
        grid_spec=pltpu.PrefetchScalarGridSpec(
            num_scalar_prefetch=2,
            grid=(T // tq,),
            in_specs=[pl.BlockSpec((tq * ROW_TILE, LANES), lambda i, d, z: (i, 0))],
            out_specs=pl.BlockSpec(memory_space=pl.ANY),
            scratch_shapes=[pltpu.VMEM((MOE_ROWS * ROW_TILE, LANES), h2.dtype), pltpu.SemaphoreType.DMA(())],
        ),
        out_shape=jax.ShapeDtypeStruct((n_rows * ROW_TILE, LANES), h2.dtype),
        compiler_params=_cparams(("arbitrary",)),
        name="moe_dispatch",
    )(dest_flat, zero_block, h2)


MOE_WEIGHT_SLOTS = 2


def _expert_kernel(be_ref, nu_ref, rank_ref, kth_ref, xs_ref, wg_hbm, wu_hbm, wd_hbm, o_ref,
                   wg_f, wu_f, wd_f, wg_s, wu_s, wd_s, sem):
    i = pl.program_id(0)
    e = be_ref[i]
    active = i < nu_ref[0]
    new_expert = active & ((i == 0) | (e != be_ref[jnp.maximum(i - 1, 0)]))
    k = rank_ref[e]
    slot = k % MOE_WEIGHT_SLOTS
    ahead = MOE_WEIGHT_SLOTS - 1

    def weight_copies(expert, s):
        return (pltpu.make_async_copy(wg_hbm.at[expert], wg_f.at[s], sem.at[s, 0]),
                pltpu.make_async_copy(wu_hbm.at[expert], wu_f.at[s], sem.at[s, 1]),
                pltpu.make_async_copy(wd_hbm.at[expert], wd_f.at[s], sem.at[s, 2]))

    def start_kth(j):
        nxt = kth_ref[j]

        @pl.when(nxt >= 0)
        def _():
            for c in weight_copies(nxt, j % MOE_WEIGHT_SLOTS):
                c.start()

    @pl.when(active & (i == 0))
    def _():
        for j in range(ahead):
            start_kth(j)

    @pl.when(new_expert)
    def _():
        start_kth(k + ahead)
        for c in weight_copies(e, slot):
            c.wait()
        wg_s[...] = wg_f[slot].astype(BF16)
        wu_s[...] = wu_f[slot].astype(BF16)
        wd_s[...] = wd_f[slot].astype(BF16)

    @pl.when(active)
    def _():
        x = _unpack_halves(_load_row_tiles(xs_ref, MOE_ROWS)).astype(BF16)
        g = jnp.dot(x, wg_s[...], preferred_element_type=F32)
        u = jnp.dot(x, wu_s[...], preferred_element_type=F32)
        a = (g * jax.nn.sigmoid(g) * u).astype(BF16)
        y = jnp.dot(a, wd_s[...], preferred_element_type=F32)
        _store_row_tiles(o_ref, _pack_halves(y.astype(BF16)))

    @pl.when(jnp.logical_not(active))
    def _():
        o_ref[...] = jnp.zeros_like(o_ref)


def _experts(xs, block_expert, n_used, expert_rank, kth_expert, w_gate, w_up, w_down):
    Dh = xs.shape[1]
    D, F = w_gate.shape[1], w_gate.shape[2]
    assert Dh == LANES and D == 2 * ROW_TILE * LANES
    nblk = block_expert.shape[0]
    P = nblk * MOE_ROWS * ROW_TILE
    rows = lambda i, be, nu, nx, sl: (jnp.minimum(i, nu[0] - 1), 0)
    hbm = pl.BlockSpec(memory_space=pl.ANY)
    return pl.pallas_call(
        _expert_kernel,
        grid_spec=pltpu.PrefetchScalarGridSpec(
            num_scalar_prefetch=4,
            grid=(nblk,),
            in_specs=[pl.BlockSpec((MOE_ROWS * ROW_TILE, Dh), rows), hbm, hbm, hbm],
            out_specs=pl.BlockSpec((MOE_ROWS * ROW_TILE, Dh), lambda i, be, nu, nx, sl: (i, 0)),
            scratch_shapes=[pltpu.VMEM((MOE_WEIGHT_SLOTS, D, F), F32), pltpu.VMEM((MOE_WEIGHT_SLOTS, D, F), F32),
                            pltpu.VMEM((MOE_WEIGHT_SLOTS, F, D), F32),
                            pltpu.VMEM((D, F), BF16), pltpu.VMEM((D, F), BF16), pltpu.VMEM((F, D), BF16),
                            pltpu.SemaphoreType.DMA((MOE_WEIGHT_SLOTS, 3))],
        ),
        out_shape=jax.ShapeDtypeStruct((P, Dh), jnp.uint32),
        compiler_params=_cparams(("arbitrary",)),
        name="moe_experts",
    )(block_expert, n_used, expert_rank, kth_expert, xs, w_gate, w_up, w_down)


def _combine_kernel(dest_ref, x1_ref, rt_ref, g_ref, ys_ref, o_ref, buf, sem, *, tq):
    i = pl.program_id(0)
    n = pl.num_programs(0)

    def token_rows(t):
        return pl.ds(pl.multiple_of(t * ROW_TILE, ROW_TILE), ROW_TILE)

    def issue(step, s):
        def body(j, carry):
            for k in range(MOE_TOPK):
                pltpu.make_async_copy(ys_ref.at[token_rows(dest_ref[2 * (step * tq + j) + k])],
                                      buf.at[s, k, token_rows(j)], sem.at[s]).start(priority=k % 2)
            return carry
        lax.fori_loop(0, tq, body, 0)

    @pl.when(i == 0)
    def _():
        issue(0, 0)

    for s in range(2):
        @pl.when((i + 1 < n) & ((i + 1) % 2 == s))
        def _():
            issue(i + 1, s)

    for s in range(2):
        @pl.when(i % 2 == s)
        def _():
            for k in range(MOE_TOPK):
                pltpu.make_async_copy(ys_ref.at[pl.ds(0, tq * ROW_TILE)], buf.at[s, k], sem.at[s]).wait()
            rt = rt_ref[...]
            y0 = _unpack_halves(_load_row_tiles(buf.at[s, 0], tq))
            y1 = _unpack_halves(_load_row_tiles(buf.at[s, 1], tq))
            x = x1_ref[...] + rt[:, 2:3] * y0 + rt[:, 3:4] * y1
            ms = jnp.mean(x * x, axis=-1, keepdims=True)
            o_ref[...] = x * lax.rsqrt(ms + NORM_EPS) * g_ref[...]


def _combine(x1, route, gain, ys, dest_flat, tq=256):
    T, D = x1.shape
    return pl.pallas_call(
        functools.partial(_combine_kernel, tq=tq),
        grid_spec=pltpu.PrefetchScalarGridSpec(
            num_scalar_prefetch=1,
            grid=(T // tq,),
            in_specs=[
                pl.BlockSpec((tq, D), lambda i, d: (i, 0)),
                pl.BlockSpec((tq, ROUTE_LANES), lambda i, d: (i, 0)),
                pl.BlockSpec((1, D), lambda i, d: (0, 0)),
                pl.BlockSpec(memory_space=pl.ANY),
            ],
            out_specs=pl.BlockSpec((tq, D), lambda i, d: (i, 0)),
            scratch_shapes=[pltpu.VMEM((2, MOE_TOPK, tq * ROW_TILE, LANES), jnp.uint32),
                            pltpu.SemaphoreType.DMA((2,))],
        ),
        out_shape=jax.ShapeDtypeStruct((T, D), F32),
        compiler_params=_cparams(("arbitrary",)),
        name="moe_combine",
    )(dest_flat, x1, route, gain.reshape(1, D).astype(F32), ys)


def _route_metadata(route, T):
    e = route[:, :MOE_TOPK].astype(jnp.int32)
    ids = jnp.arange(MOE_EXPERTS, dtype=jnp.int32)
    oh0 = e[:, 0, None] == ids
    oh1 = e[:, 1, None] == ids
    onehot = (oh0 | oh1).astype(jnp.int32)
    incl = jnp.cumsum(onehot, axis=0)
    counts = incl[-1]
    pcounts = (counts + MOE_ROWS - 1) // MOE_ROWS * MOE_ROWS
    pends = jnp.cumsum(pcounts)
    poffs = pends - pcounts
    slot = incl - onehot + poffs[None, :]
    dest = jnp.stack([jnp.sum(jnp.where(oh0, slot, 0), axis=1), jnp.sum(jnp.where(oh1, slot, 0), axis=1)], axis=1)
    nblk = (T * MOE_TOPK) // MOE_ROWS + MOE_EXPERTS
    first_row = jnp.arange(nblk, dtype=jnp.int32) * MOE_ROWS
    block_expert = jnp.minimum(jnp.sum((pends[None, :] <= first_row[:, None]).astype(jnp.int32), axis=1),
                               MOE_EXPERTS - 1).astype(jnp.int32)
    n_used = (pends[-1:] // MOE_ROWS).astype(jnp.int32)
    used = counts > 0
    expert_rank = (jnp.cumsum(used.astype(jnp.int32)) - 1).astype(jnp.int32)
    order = jnp.argsort(jnp.where(used, ids, ids + MOE_EXPERTS)).astype(jnp.int32)
    kth = jnp.where(ids < jnp.sum(used), order, -1)
    kth_expert = jnp.concatenate([kth, jnp.full((MOE_WEIGHT_SLOTS,), -1, jnp.int32)])
    blk = jnp.arange(nblk, dtype=jnp.int32)
    following = jnp.concatenate([block_expert[1:], block_expert[-1:]])
    zero_block = ((blk >= n_used[0] - 1) | (block_expert != following)).astype(jnp.int32)
    return dest.reshape(-1).astype(jnp.int32), zero_block, block_expert, n_used, expert_rank, kth_expert, nblk


def kernel(x, norm1_gain, w_in, hg_lb_logits, hg_norm_gain, rel_bias, w_branch_a, w_branch_b, w_out,
           norm2_gain, w_router_group, b_router_group, w_router_expert, b_router_expert,
           w_exp_gate, w_exp_up, w_exp_down, final_norm_gain):
    B, S, D = x.shape
    T = B * S
    depth = norm1_gain.shape[0]
    lower_bounds = jnp.cumsum(jax.nn.softmax(hg_lb_logits.astype(F32), axis=0), axis=0)
    att_w = ATT_HEADS * HEAD_DIM
    bias = _attn_bias(rel_bias)
    x2d = x.reshape(T, D)
    for layer in range(depth):
        zeros_lb = jnp.zeros((1, 3 * att_w), F32)
        hm = dict(head_major=True, B=B, S=S)
        h, q_a = _normproj(x2d, norm1_gain[layer], w_in, layer, D, B=B, S=S)
        lf_a = _inproj(h, w_in, layer, D, D, lower_bounds[layer].reshape(1, D), mode="forget", out_dtype=F32, **hm)
        i_a = _inproj(h, w_in, layer, 2 * D, D, zeros_lb, mode="none", out_dtype=BF16, **hm)
        sg_a = _inproj(h, w_in, layer, 3 * D, D, zeros_lb, mode="sigmoid", out_dtype=BF16, **hm)
        qkv_b = _inproj(h, w_in, layer, 4 * D, 3 * att_w, zeros_lb, mode="none", out_dtype=BF16, tn=att_w, **hm)
        gates = _inproj(h, w_in, layer, 4 * D + 3 * att_w, 2 * D, zeros_lb, mode="sigmoid",
                        head_major=False, out_dtype=BF16, B=B, S=S)

        y_a = _hgrn(q_a, lf_a, i_a, sg_a, hg_norm_gain[layer]).reshape(T, D)
        y_b = _attention(qkv_b, bias).reshape(T, ATT_HEADS_PER_GROUP * HEAD_DIM)
        merged = _merge(y_a, y_b, w_branch_a[layer], w_branch_b[layer], gates)

        n_pad = ROUTE_LANES - MOE_EXPERTS - MOE_GROUPS
        wr = jnp.concatenate([w_router_expert[layer].astype(F32), w_router_group[layer].astype(F32),
                              jnp.zeros((D, n_pad), F32)], axis=1)
        br = jnp.concatenate([b_router_expert[layer].astype(F32), b_router_group[layer].astype(F32),
                              jnp.zeros((n_pad,), F32)]).reshape(1, ROUTE_LANES)
        wr_hi = wr.astype(BF16)
        wr_lo = (wr - wr_hi.astype(F32)).astype(BF16)
        x1, h2, route = _outproj(merged, w_out[layer].astype(BF16), x2d, norm2_gain[layer],
                                 jnp.concatenate([wr_hi, wr_lo], axis=1), br)

        dest, zero_block, block_expert, n_used, expert_rank, kth_expert, nblk = _route_metadata(route, T)
        xs = _dispatch(h2, dest, zero_block, nblk * MOE_ROWS)
        ys = _experts(xs, block_expert, n_used, expert_rank, kth_expert,
                      w_exp_gate[layer], w_exp_up[layer], w_exp_down[layer])
        last = layer == depth - 1
        assert last, "the fused combine applies the final norm; deeper stacks need an un-normalised combine"
        x2d = _combine(x1, route, final_norm_gain, ys, dest)
    return x2d.reshape(B, S, D)
```

```python
import functools
import math

import numpy as np
import jax
import jax.numpy as jnp
from jax import lax
from jax.experimental import pallas as pl
from jax.experimental.pallas import tpu as pltpu

F32 = jnp.float32
BF16 = jnp.bfloat16

LANES = 128
NORM_EPS = 1e-6
HEAD_DIM = 128
ATT_GROUPS = ((128, 1), (512, 4), (2048, 16))
ATT_HEADS_PER_GROUP = 4
ATT_HEADS = len(ATT_GROUPS) * ATT_HEADS_PER_GROUP
ATT_BLOCK = 128
ATT_STACK = 16
REL_BUCKETS = 32
REL_MAX_DIST = 2048
MOE_GROUPS = 8
MOE_EXPERTS_PER_GROUP = 8
MOE_EXPERTS = MOE_GROUPS * MOE_EXPERTS_PER_GROUP
MOE_TOPK = 2
MOE_ROWS = 256
NEG_BIG = -1e30
VMEM_LIMIT = 56 * 1024 * 1024


def _cparams(sem):
    return pltpu.CompilerParams(dimension_semantics=sem, vmem_limit_bytes=VMEM_LIMIT)


def _normproj_kernel(x_ref, g_ref, w_ref, h_ref, o_ref, w_bf):
    @pl.when(pl.program_id(0) == 0)
    def _():
        w_bf[...] = w_ref[...].astype(BF16)

    x = x_ref[...]
    ms = jnp.mean(x * x, axis=-1, keepdims=True)
    h = (x * lax.rsqrt(ms + NORM_EPS) * g_ref[...]).astype(h_ref.dtype)
    h_ref[...] = h
    acc = jnp.dot(h, w_bf[...], preferred_element_type=F32)
    for hh in range(acc.shape[1] // HEAD_DIM):
        o_ref[0, hh] = acc[:, hh * HEAD_DIM:(hh + 1) * HEAD_DIM].astype(o_ref.dtype)


def _normproj(x2d, gain, w_all, layer, N, *, B, S, tm=512):
    T, D = x2d.shape
    assert S % tm == 0
    spb = S // tm
    return pl.pallas_call(
        _normproj_kernel,
        grid=(T // tm,),
        in_specs=[
            pl.BlockSpec((tm, D), lambda i: (i, 0)),
            pl.BlockSpec((1, D), lambda i: (0, 0)),
            pl.BlockSpec((None, D, N), lambda i: (layer, 0, 0), pipeline_mode=pl.Buffered(1)),
        ],
        out_specs=[
            pl.BlockSpec((tm, D), lambda i: (i, 0)),
            pl.BlockSpec((1, N // HEAD_DIM, tm, HEAD_DIM), lambda i: (i // spb, 0, i % spb, 0)),
        ],
        out_shape=[jax.ShapeDtypeStruct((T, D), BF16),
                   jax.ShapeDtypeStruct((B, N // HEAD_DIM, S, HEAD_DIM), BF16)],
        scratch_shapes=[pltpu.VMEM((D, N), BF16)],
        compiler_params=_cparams(("arbitrary",)),
        name="norm_inproj",
    )(x2d, gain.reshape(1, D).astype(F32), w_all)


def _inproj_kernel(h_ref, w_ref, lb_ref, o_ref, w_bf, *, mode, head_major):
    @pl.when(pl.program_id(1) == 0)
    def _():
        w_bf[...] = w_ref[0].astype(BF16)

    acc = jnp.dot(h_ref[...], w_bf[...], preferred_element_type=F32)
    if mode == "forget":
        lb = lb_ref[...]
        acc = lb + (1.0 - lb) * jax.nn.sigmoid(acc)
    elif mode == "sigmoid":
        acc = 0.5 * jnp.tanh(0.5 * acc) + 0.5
    if head_major:
        for hh in range(acc.shape[1] // HEAD_DIM):
            o_ref[0, hh] = acc[:, hh * HEAD_DIM:(hh + 1) * HEAD_DIM].astype(o_ref.dtype)
    else:
        o_ref[...] = acc.astype(o_ref.dtype)


def _inproj(h, w_all, layer, col0, N, lb, *, mode, head_major, out_dtype, B, S, tm=1024, tn=1024):
    T, D = h.shape
    assert N % tn == 0 and col0 % LANES == 0 and T % tm == 0 and S % tm == 0
    spb = S // tm
    if head_major:
        out_shape = jax.ShapeDtypeStruct((B, N // HEAD_DIM, S, HEAD_DIM), out_dtype)
        out_spec = pl.BlockSpec((1, tn // HEAD_DIM, tm, HEAD_DIM), lambda j, i: (i // spb, j, i % spb, 0))
    else:
        out_shape = jax.ShapeDtypeStruct((T, N), out_dtype)
        out_spec = pl.BlockSpec((tm, tn), lambda j, i: (i, j))
    return pl.pallas_call(
        functools.partial(_inproj_kernel, mode=mode, head_major=head_major),
        grid=(N // tn, T // tm),
        in_specs=[
            pl.BlockSpec((tm, D), lambda j, i: (i, 0)),
            pl.BlockSpec((pl.Element(1), pl.Element(D), pl.Element(tn)), lambda j, i: (layer, 0, pl.multiple_of(col0 + j * tn, LANES))),
            pl.BlockSpec((1, tn), lambda j, i: (0, j)),
        ],
        out_specs=out_spec,
        out_shape=out_shape,
        scratch_shapes=[pltpu.VMEM((D, tn), BF16)],
        compiler_params=_cparams(("parallel", "arbitrary")),
        name="inproj_" + mode,
    )(h, w_all, lb)


HG_CHUNK = 256
HG_DIAG = 16
HG_HEADS_PER_STEP = 4


def _hgrn_levels(C):
    out, m = [], C // 2
    while m >= HG_DIAG:
        out.append(m)
        m //= 2
    return out


def _hgrn_masks(C):
    t = np.arange(C)[:, None]
    s = np.arange(C)[None, :]
    masks = []
    for m in _hgrn_levels(C):
        masks.append((t // (2 * m) == s // (2 * m)) & ((t // m) % 2 == 1) & ((s // m) % 2 == 0))
    masks.append((t // HG_DIAG == s // HG_DIAG) & (t >= s))
    total = np.sum(np.stack(masks).astype(np.int32), axis=0)
    assert np.array_equal(total, (t >= s).astype(np.int32))
    return np.stack(masks).astype(np.float32)


def _dot_nt(a, b):
    return lax.dot_general(a, b, (((1,), (1,)), ((), ())), preferred_element_type=F32)


def _dot_tn(a, b):
    return lax.dot_general(a, b, (((0,), (0,)), ((), ())), preferred_element_type=F32)


def _hgrn_kernel(q_ref, f_ref, v_ref, sg_ref, gain_ref, tril_ref, mask_ref, o_ref, st_ref, b_ref, *, C):
    S = q_ref.shape[2]
    HP = q_ref.shape[1]
    levels = _hgrn_levels(C)
    nchunks = S // C
    tril = tril_ref[...]

    def ref_rows(h, block, row_of_block):
        parts = [jnp.broadcast_to(b_ref[h, pl.ds(row_of_block(p), 1), :], (block, HEAD_DIM))
                 for p in range(C // block)]
        return jnp.concatenate(parts, axis=0)

    def cumsum(c):
        r0 = pl.multiple_of(c * C, C)
        out = []
        for h in range(HP):
            lf2 = jnp.log2(f_ref[0, h, pl.ds(r0, C), :])
            hi = lf2.astype(BF16)
            lo = (lf2 - hi.astype(F32)).astype(BF16)
            cat = jnp.concatenate([hi, lo], axis=1)
            half = C // 2
            tri = tril[:half, :half]
            r_a = jnp.dot(tri, cat[:half], preferred_element_type=F32)
            r_b = jnp.dot(tri, cat[half:], preferred_element_type=F32)
            b_a = r_a[:, :HEAD_DIM] + r_a[:, HEAD_DIM:]
            b_b = r_b[:, :HEAD_DIM] + r_b[:, HEAD_DIM:] + b_a[half - 1:half, :]
            out.append(jnp.concatenate([b_a, b_b], axis=0))
        return tuple(out)

    st_ref[...] = jnp.zeros_like(st_ref)

    def chunk(c, b_all):
        r0 = pl.multiple_of(c * C, C)
        rows = pl.ds(r0, C)
        b_next = cumsum(jnp.minimum(c + 1, nchunks - 1))
        for h in range(HP):
            b = b_all[h]
            b_ref[h] = b
            q = q_ref[0, h, rows, :].astype(F32)
            kk = 1.0 - f_ref[0, h, rows, :]
            b_last = b_ref[h, pl.ds(C - 1, 1), :]
            scores = jnp.zeros((C, C), BF16)
            half = C // 2
            ref_top = b_ref[h, pl.ds(half - 1, 1), :]
            a_top = (q[half:] * jnp.exp2(b[half:] - ref_top)).astype(BF16)
            k_top = (kk[:half] * jnp.exp2(ref_top - b[:half])).astype(BF16)
            s_top = _dot_nt(a_top, k_top).astype(BF16)
            for li, m in enumerate(levels):
                if m == half:
                    continue
                parts = []
                for p in range(C // (2 * m)):
                    ref = b_ref[h, pl.ds(2 * m * p + m - 1, 1), :]
                    parts += [ref - b[2 * m * p:2 * m * p + m], b[2 * m * p + m:2 * m * (p + 1)] - ref]
                e = jnp.exp2(jnp.concatenate(parts, axis=0))
                s_l = _dot_nt((q * e).astype(BF16), (kk * e).astype(BF16)).astype(BF16)
                scores = scores + mask_ref[li] * s_l
            d = b - ref_rows(h, HG_DIAG, lambda p: HG_DIAG * p + HG_DIAG // 2 - 1)
            s_l = _dot_nt((q * jnp.exp2(d)).astype(BF16), (kk * jnp.exp2(-d)).astype(BF16)).astype(BF16)
            scores = scores + mask_ref[len(levels)] * s_l
            v = v_ref[0, h, rows, :]
            st = st_ref[h]
            qe = (q * jnp.exp2(b)).astype(BF16)
            kd = (kk * jnp.exp2(b_last - b)).astype(BF16)
            o = jnp.dot(scores, v, preferred_element_type=F32) + _dot_nt(qe, st.astype(BF16))
            o_top = jnp.dot(s_top, v[:half], preferred_element_type=F32)
            o = jnp.concatenate([o[:half], o[half:] + o_top], axis=0)
            st_ref[h] = st * jnp.exp2(b_last) + _dot_tn(v, kd)
            ms = jnp.mean(o * o, axis=-1, keepdims=True)
            y = o * lax.rsqrt(ms + NORM_EPS) * gain_ref[:, h * HEAD_DIM:(h + 1) * HEAD_DIM]
            y = y * sg_ref[0, h, rows, :].astype(F32)
            o_ref[0, rows, h * HEAD_DIM:(h + 1) * HEAD_DIM] = y.astype(o_ref.dtype)
        return b_next

    lax.fori_loop(0, nchunks, chunk, cumsum(0))


def _hgrn(q, lf, v, sg, gain):
    B, H, S, E = q.shape
    C, HP = HG_CHUNK, HG_HEADS_PER_STEP
    masks = jnp.asarray(_hgrn_masks(C), dtype=BF16)
    tril = jnp.asarray(np.tril(np.ones((C, C), np.float32)), dtype=BF16)
    head_spec = pl.BlockSpec((1, HP, S, E), lambda b, h: (b, h, 0, 0))
    return pl.pallas_call(
        functools.partial(_hgrn_kernel, C=C),
        grid=(B, H // HP),
        in_specs=[
            head_spec, head_spec, head_spec, head_spec,
            pl.BlockSpec((1, HP * E), lambda b, h: (0, h)),
            pl.BlockSpec((C, C), lambda b, h: (0, 0)),
            pl.BlockSpec(masks.shape, lambda b, h: (0, 0, 0)),
        ],
        out_specs=pl.BlockSpec((1, S, HP * E), lambda b, h: (b, 0, h)),
        out_shape=jax.ShapeDtypeStruct((B, S, H * E), BF16),
        scratch_shapes=[pltpu.VMEM((HP, E, E), F32), pltpu.VMEM((HP, C, E), F32)],
        compiler_params=_cparams(("parallel", "parallel")),
        name="hgrn2",
    )(q, lf, v, sg, gain.reshape(1, H * E).astype(F32), tril, masks)


def _t5_bucket_np(dist):
    exact = REL_BUCKETS // 2
    d_f = np.maximum(dist, 1).astype(np.float32)
    log_b = exact + (np.log(d_f / np.float32(exact)) / np.float32(math.log(REL_MAX_DIST / exact))
                     * np.float32(REL_BUCKETS - exact)).astype(np.int32)
    return np.where(dist < exact, dist, np.minimum(log_b, REL_BUCKETS - 1))


def _attn_bias(rel_bias):
    blk = ATT_BLOCK
    period = 3 * blk
    out = []
    for gi, (window, dilation) in enumerate(ATT_GROUPS):
        n_back = window // dilation
        assert n_back <= blk
        hs = slice(gi * ATT_HEADS_PER_GROUP, (gi + 1) * ATT_HEADS_PER_GROUP)
        bucket = _t5_bucket_np(np.arange(n_back + 1) * dilation)
        by_delta = rel_bias[:, hs][bucket].astype(F32).T
        u = jnp.full((ATT_HEADS_PER_GROUP, period), NEG_BIG, F32)
        u = u.at[:, 2 * blk - 1 - n_back:2 * blk].set(by_delta[:, ::-1])
        flat = jnp.tile(u, (1, blk))[:, :blk * (period - 1)]
        out.append(flat.reshape(ATT_HEADS_PER_GROUP, blk, period - 1)[:, :, blk - 1:3 * blk - 1])
    return jnp.stack(out, axis=0)


def _attn_kernel(q_ref, k_ref, v_ref, bias_ref, o_ref, qf, kf, vf, og, lg):
    S = q_ref.shape[3]
    scale = HEAD_DIM ** -0.5
    blk = ATT_BLOCK

    for g, (window, d) in enumerate(ATT_GROUPS):
        L = S // d
        nb = L // blk
        if d > 1:
            qf[...] = q_ref[0, g, 0].astype(F32)
            kf[...] = k_ref[0, g, 0].astype(F32)
            vf[...] = v_ref[0, g, 0].astype(F32)

        def load(ref_bf, ref_f32, start, size, g=g, d=d):
            if d == 1:
                return ref_bf[0, g, 0, pl.ds(start, size), :]
            return ref_f32[pl.ds(start, size, stride=d), :].astype(BF16)

        def scores(r, n, g=g, d=d, load=load):
            start = r + n * blk * d
            q = load(q_ref, qf, start, blk)
            if n == 0:
                k = load(k_ref, kf, start, blk)
                v = load(v_ref, vf, start, blk)
                s = _dot_nt(q, k) * scale + bias_ref[g, 0, :, blk:]
            else:
                first = start - blk * d
                k = load(k_ref, kf, first, 2 * blk)
                v = load(v_ref, vf, first, 2 * blk)
                s = _dot_nt(q, k) * scale + bias_ref[g, 0]
            return start, s, v

        def softmax_pv(items, g=g, d=d):
            s = jnp.concatenate([it[1] for it in items], axis=0)
            m = jnp.max(s, axis=-1, keepdims=True)
            p = jnp.exp(s - m)
            den = jnp.sum(p, axis=-1, keepdims=True)
            pb = p.astype(BF16)
            lse = jnp.broadcast_to(m + jnp.log(den), (s.shape[0], HEAD_DIM))
            for j, (start, _, v) in enumerate(items):
                sl = slice(j * blk, (j + 1) * blk)
                rows = pl.ds(start, blk, stride=d) if d > 1 else pl.ds(start, blk)
                og[g, rows, :] = jnp.dot(pb[sl], v, preferred_element_type=F32) / den[sl]
                lg[g, rows, :] = lse[sl]

        blocks = [(r, n) for r in range(d) for n in range(nb)]
        first_blocks = [bn for bn in blocks if bn[1] == 0]
        later_blocks = [bn for bn in blocks if bn[1] > 0]
        for group in (first_blocks, later_blocks):
            for j0 in range(0, len(group), ATT_STACK):
                softmax_pv([scores(r, n) for r, n in group[j0:j0 + ATT_STACK]])

    rows_per_step = 256

    def mix(i, carry):
        rows = pl.ds(pl.multiple_of(i * rows_per_step, rows_per_step), rows_per_step)
        l0, l1, l2 = lg[0, rows, :], lg[1, rows, :], lg[2, rows, :]
        mx = jnp.maximum(jnp.maximum(l0, l1), l2)
        e0, e1, e2 = jnp.exp(l0 - mx), jnp.exp(l1 - mx), jnp.exp(l2 - mx)
        num = e0 * og[0, rows, :] + e1 * og[1, rows, :] + e2 * og[2, rows, :]
        o_ref[0, rows, :] = (num / (e0 + e1 + e2)).astype(o_ref.dtype)
        return carry

    lax.fori_loop(0, S // rows_per_step, mix, 0)


def _attention(qkv, bias):
    B, _, S, E = qkv.shape
    G, HG = len(ATT_GROUPS), ATT_HEADS_PER_GROUP
    x = qkv.reshape(B, 3, G, HG, S, E)

    def spec(which):
        return pl.BlockSpec((1, None, G, 1, S, E), lambda b, h, which=which: (b, which, 0, h, 0, 0))

    def kern(q_ref, k_ref, v_ref, bias_ref, o_ref, *scratch):
        _attn_kernel(q_ref, k_ref, v_ref, bias_ref, o_ref, *scratch)

    return pl.pallas_call(
        kern,
        grid=(B, HG),
        in_specs=[spec(0), spec(1), spec(2),
                  pl.BlockSpec((G, 1, ATT_BLOCK, 2 * ATT_BLOCK), lambda b, h: (0, h, 0, 0))],
        out_specs=pl.BlockSpec((1, S, E), lambda b, h: (b, 0, h)),
        out_shape=jax.ShapeDtypeStruct((B, S, HG * E), BF16),
        scratch_shapes=[pltpu.VMEM((S, E), F32)] * 3 + [pltpu.VMEM((G, S, E), F32)] * 2,
        compiler_params=_cparams(("parallel", "parallel")),
        name="dilated_attn",
    )(x, x, x, bias)


def _merge_kernel(ya_ref, yb_ref, wa_ref, wb_ref, ga_ref, gb_ref, o_ref, wa_bf, wb_bf):
    @pl.when(pl.program_id(1) == 0)
    def _():
        wa_bf[...] = wa_ref[...].astype(BF16)
        wb_bf[...] = wb_ref[...].astype(BF16)

    a = jnp.dot(ya_ref[...], wa_bf[...], preferred_element_type=F32)
    b = jnp.dot(yb_ref[...], wb_bf[...], preferred_element_type=F32)
    o_ref[...] = (ga_ref[...].astype(F32) * a + gb_ref[...].astype(F32) * b).astype(o_ref.dtype)


def _merge(ya, yb, wa, wb, gates, tm=1024, tn=1024):
    T, Ka = ya.shape
    Kb = yb.shape[1]
    N = wa.shape[1]
    nj = N // tn
    return pl.pallas_call(
        _merge_kernel,
        grid=(nj, T // tm),
        in_specs=[
            pl.BlockSpec((tm, Ka), lambda j, i: (i, 0)),
            pl.BlockSpec((tm, Kb), lambda j, i: (i, 0)),
            pl.BlockSpec((Ka, tn), lambda j, i: (0, j)),
            pl.BlockSpec((Kb, tn), lambda j, i: (0, j)),
            pl.BlockSpec((tm, tn), lambda j, i: (i, j)),
            pl.BlockSpec((tm, tn), lambda j, i, nj=nj: (i, j + nj)),
        ],
        out_specs=pl.BlockSpec((tm, tn), lambda j, i: (i, j)),
        out_shape=jax.ShapeDtypeStruct((T, N), BF16),
        scratch_shapes=[pltpu.VMEM((Ka, tn), BF16), pltpu.VMEM((Kb, tn), BF16)],
        compiler_params=_cparams(("parallel", "arbitrary")),
        name="branch_merge",
    )(ya, yb, wa, wb, gates, gates)


ROUTE_LANES = LANES


def _pack_halves(x_bf):
    bits = pltpu.bitcast(x_bf.astype(F32), jnp.uint32)
    n = bits.shape[1] // 2
    return (bits[:, :n] >> 16) | (bits[:, n:] & jnp.uint32(0xFFFF0000))


def _unpack_halves(p):
    lo = pltpu.bitcast(p << 16, F32)
    hi = pltpu.bitcast(p & jnp.uint32(0xFFFF0000), F32)
    return jnp.concatenate([lo, hi], axis=1)


ROW_TILE = 8


def _store_row_tiles(ref, packed):
    m = packed.shape[0]
    assert packed.shape[1] == ROW_TILE * LANES
    for c in range(ROW_TILE):
        ref[pl.ds(c, m, stride=ROW_TILE), :] = packed[:, c * LANES:(c + 1) * LANES]


def _load_row_tiles(ref, m):
    return jnp.concatenate([ref[pl.ds(c, m, stride=ROW_TILE), :] for c in range(ROW_TILE)], axis=1)


def _outproj_kernel(m_ref, w_ref, x_ref, g_ref, wrc_ref, br_ref, x1_ref, h2_ref, rt_ref):
    x1 = x_ref[...] + jnp.dot(m_ref[...], w_ref[...], preferred_element_type=F32)
    x1_ref[...] = x1
    ms = jnp.mean(x1 * x1, axis=-1, keepdims=True)
    h2 = x1 * lax.rsqrt(ms + NORM_EPS) * g_ref[...]
    h_hi = h2.astype(BF16)
    _store_row_tiles(h2_ref, _pack_halves(h_hi))
    h_lo = (h2 - h_hi.astype(F32)).astype(BF16)
    r = jnp.dot(h_hi, wrc_ref[...], preferred_element_type=F32)
    lg = (r[:, :ROUTE_LANES] + r[:, ROUTE_LANES:]
          + jnp.dot(h_lo, wrc_ref[:, :ROUTE_LANES], preferred_element_type=F32)) + br_ref[...]

    lane = lax.broadcasted_iota(jnp.int32, lg.shape, 1)
    lane_f = lane.astype(F32)
    big = float(ROUTE_LANES)
    is_group = (lane >= MOE_EXPERTS) & (lane < MOE_EXPERTS + MOE_GROUPS)
    lgg = jnp.where(is_group, lg, -jnp.inf)
    gmax = jnp.max(lgg, axis=-1, keepdims=True)
    gsel = jnp.min(jnp.where(lgg == gmax, lane_f - MOE_EXPERTS, big), axis=-1, keepdims=True)
    pg = 1.0 / jnp.sum(jnp.where(is_group, jnp.exp(lg - gmax), 0.0), axis=-1, keepdims=True)

    in_group = (lane < MOE_EXPERTS) & ((lane // MOE_EXPERTS_PER_GROUP).astype(F32) == gsel)
    le = jnp.where(in_group, lg, -jnp.inf)
    t1 = jnp.max(le, axis=-1, keepdims=True)
    i1 = jnp.min(jnp.where(le == t1, lane_f, big), axis=-1, keepdims=True)
    le2 = jnp.where(lane_f == i1, -jnp.inf, le)
    t2 = jnp.max(le2, axis=-1, keepdims=True)
    i2 = jnp.min(jnp.where(le2 == t2, lane_f, big), axis=-1, keepdims=True)
    e2 = jnp.exp(t2 - t1)
    w1 = pg / (1.0 + e2)
    w2 = pg * e2 / (1.0 + e2)
    rt_ref[...] = jnp.where(lane == 0, i1, jnp.where(lane == 1, i2, jnp.where(lane == 2, w1,
                            jnp.where(lane == 3, w2, 0.0))))


def _outproj(merged, w_out, x2d, gain, wr_cat, br, tm=512):
    T, D = x2d.shape
    row = lambda i: (i, 0)
    const = lambda i: (0, 0)
    return pl.pallas_call(
        _outproj_kernel,
        grid=(T // tm,),
        in_specs=[
            pl.BlockSpec((tm, D), row), pl.BlockSpec((D, D), const), pl.BlockSpec((tm, D), row),
            pl.BlockSpec((1, D), const), pl.BlockSpec((D, 2 * ROUTE_LANES), const),
            pl.BlockSpec((1, ROUTE_LANES), const),
        ],
        out_specs=[pl.BlockSpec((tm, D), row), pl.BlockSpec((tm * ROW_TILE, LANES), row),
                   pl.BlockSpec((tm, ROUTE_LANES), row)],
        out_shape=[jax.ShapeDtypeStruct((T, D), F32), jax.ShapeDtypeStruct((T * ROW_TILE, LANES), jnp.uint32),
                   jax.ShapeDtypeStruct((T, ROUTE_LANES), F32)],
        compiler_params=_cparams(("parallel",)),
        name="outproj_router",
    )(merged, w_out, x2d, gain.reshape(1, D).astype(F32), wr_cat, br)


def _dispatch_kernel(dest_ref, zb_ref, h_ref, xs_ref, zeros, sem, *, tq):
    base = pl.program_id(0) * tq
    block_rows = MOE_ROWS * ROW_TILE
    n_blocks = xs_ref.shape[0] // block_rows

    @pl.when(pl.program_id(0) == 0)
    def _():
        zeros[...] = jnp.zeros_like(zeros)

        def zero_copy(b):
            return pltpu.make_async_copy(zeros, xs_ref.at[pl.ds(pl.multiple_of(b * block_rows, block_rows), block_rows)],
                                         sem)

        def start(b, carry):
            @pl.when(zb_ref[b] != 0)
            def _():
                zero_copy(b).start()
            return carry

        def wait(b, carry):
            @pl.when(zb_ref[b] != 0)
            def _():
                zero_copy(b).wait()
            return carry

        lax.fori_loop(0, n_blocks, start, 0)
        lax.fori_loop(0, n_blocks, wait, 0)

    def token_rows(t):
        return pl.ds(pl.multiple_of(t * ROW_TILE, ROW_TILE), ROW_TILE)

    def copy(j, k):
        return pltpu.make_async_copy(h_ref.at[token_rows(j)], xs_ref.at[token_rows(dest_ref[2 * (base + j) + k])], sem)

    def issue(j, carry):
        for k in range(MOE_TOPK):
            copy(j, k).start(priority=k % 2)
        return carry

    lax.fori_loop(0, tq, issue, 0)
    for _ in range(MOE_TOPK):
        pltpu.make_async_copy(h_ref, xs_ref.at[pl.ds(0, tq * ROW_TILE)], sem).wait()


def _dispatch(h2, dest_flat, zero_block, n_rows, tq=2048):
    T = h2.shape[0] // ROW_TILE
    return pl.pallas_call(
        functools.partial(_dispatch_kernel, tq=tq),
        grid_spec=pltpu.PrefetchScalarGridSpec(
            num_scalar_prefetch=2,
            grid=(T // tq,),
            in_specs=[pl.BlockSpec((tq * ROW_TILE, LANES), lambda i, d, z: (i, 0))],
            out_specs=pl.BlockSpec(memory_space=pl.ANY),
            scratch_shapes=[pltpu.VMEM((MOE_ROWS * ROW_TILE, LANES), h2.dtype), pltpu.SemaphoreType.DMA(())],
        ),
        out_shape=jax.ShapeDtypeStruct((n_rows * ROW_TILE, LANES), h2.dtype),
        compiler_params=_cparams(("arbitrary",)),
        name="moe_dispatch",
    )(dest_flat, zero_block, h2)


MOE_WEIGHT_SLOTS = 2


def _expert_kernel(be_ref, nu_ref, rank_ref, kth_ref, xs_ref, wg_hbm, wu_hbm, wd_hbm, o_ref,
                   wg_f, wu_f, wd_f, wg_s, wu_s, wd_s, sem):
    i = pl.program_id(0)
    e = be_ref[i]
    active = i < nu_ref[0]
    new_expert = active & ((i == 0) | (e != be_ref[jnp.maximum(i - 1, 0)]))
    k = rank_ref[e]
    slot = k % MOE_WEIGHT_SLOTS
    ahead = MOE_WEIGHT_SLOTS - 1

    def weight_copies(expert, s):
        return (pltpu.make_async_copy(wg_hbm.at[expert], wg_f.at[s], sem.at[s, 0]),
                pltpu.make_async_copy(wu_hbm.at[expert], wu_f.at[s], sem.at[s, 1]),
                pltpu.make_async_copy(wd_hbm.at[expert], wd_f.at[s], sem.at[s, 2]))

    def start_kth(j):
        nxt = kth_ref[j]

        @pl.when(nxt >= 0)
        def _():
            for c in weight_copies(nxt, j % MOE_WEIGHT_SLOTS):
                c.start()

    @pl.when(active & (i == 0))
    def _():
        for j in range(ahead):
            start_kth(j)

    @pl.when(new_expert)
    def _():
        start_kth(k + ahead)
        for c in weight_copies(e, slot):
            c.wait()
        wg_s[...] = wg_f[slot].astype(BF16)
        wu_s[...] = wu_f[slot].astype(BF16)
        wd_s[...] = wd_f[slot].astype(BF16)

    @pl.when(active)
    def _():
        x = _unpack_halves(_load_row_tiles(xs_ref, MOE_ROWS)).astype(BF16)
        g = jnp.dot(x, wg_s[...], preferred_element_type=F32)
        u = jnp.dot(x, wu_s[...], preferred_element_type=F32)
        a = (g * jax.nn.sigmoid(g) * u).astype(BF16)
        y = jnp.dot(a, wd_s[...], preferred_element_type=F32)
        _store_row_tiles(o_ref, _pack_halves(y.astype(BF16)))

    @pl.when(jnp.logical_not(active))
    def _():
        o_ref[...] = jnp.zeros_like(o_ref)


def _experts(xs, block_expert, n_used, expert_rank, kth_expert, w_gate, w_up, w_down):
    Dh = xs.shape[1]
    D, F = w_gate.shape[1], w_gate.shape[2]
    assert Dh == LANES and D == 2 * ROW_TILE * LANES
    nblk = block_expert.shape[0]
    P = nblk * MOE_ROWS * ROW_TILE
    rows = lambda i, be, nu, nx, sl: (jnp.minimum(i, nu[0] - 1), 0)
    hbm = pl.BlockSpec(memory_space=pl.ANY)
    return pl.pallas_call(
        _expert_kernel,
        grid_spec=pltpu.PrefetchScalarGridSpec(
            num_scalar_prefetch=4,
            grid=(nblk,),
            in_specs=[pl.BlockSpec((MOE_ROWS * ROW_TILE, Dh), rows), hbm, hbm, hbm],
            out_specs=pl.BlockSpec((MOE_ROWS * ROW_TILE, Dh), lambda i, be, nu, nx, sl: (i, 0)),
            scratch_shapes=[pltpu.VMEM((MOE_WEIGHT_SLOTS, D, F), F32), pltpu.VMEM((MOE_WEIGHT_SLOTS, D, F), F32),
                            pltpu.VMEM((MOE_WEIGHT_SLOTS, F, D), F32),
                            pltpu.VMEM((D, F), BF16), pltpu.VMEM((D, F), BF16), pltpu.VMEM((F, D), BF16),
                            pltpu.SemaphoreType.DMA((MOE_WEIGHT_SLOTS, 3))],
        ),
        out_shape=jax.ShapeDtypeStruct((P, Dh), jnp.uint32),
        compiler_params=_cparams(("arbitrary",)),
        name="moe_experts",
    )(block_expert, n_used, expert_rank, kth_expert, xs, w_gate, w_up, w_down)


def _combine_kernel(dest_ref, x1_ref, rt_ref, g_ref, ys_ref, o_ref, buf, sem, *, tq):
    i = pl.program_id(0)
    n = pl.num_programs(0)

    def token_rows(t):
        return pl.ds(pl.multiple_of(t * ROW_TILE, ROW_TILE), ROW_TILE)

    def issue(step, s):
        def body(j, carry):
            for k in range(MOE_TOPK):
                pltpu.make_async_copy(ys_ref.at[token_rows(dest_ref[2 * (step * tq + j) + k])],
                                      buf.at[s, k, token_rows(j)], sem.at[s]).start(priority=k % 2)
            return carry
        lax.fori_loop(0, tq, body, 0)

    @pl.when(i == 0)
    def _():
        issue(0, 0)

    for s in range(2):
        @pl.when((i + 1 < n) & ((i + 1) % 2 == s))
        def _():
            issue(i + 1, s)

    for s in range(2):
        @pl.when(i % 2 == s)
        def _():
            for k in range(MOE_TOPK):
                pltpu.make_async_copy(ys_ref.at[pl.ds(0, tq * ROW_TILE)], buf.at[s, k], sem.at[s]).wait()
            rt = rt_ref[...]
            y0 = _unpack_halves(_load_row_tiles(buf.at[s, 0], tq))
            y1 = _unpack_halves(_load_row_tiles(buf.at[s, 1], tq))
            x = x1_ref[...] + rt[:, 2:3] * y0 + rt[:, 3:4] * y1
            ms = jnp.mean(x * x, axis=-1, keepdims=True)
            o_ref[...] = x * lax.rsqrt(ms + NORM_EPS) * g_ref[...]


def _combine(x1, route, gain, ys, dest_flat, tq=256):
    T, D = x1.shape
    return pl.pallas_call(
        functools.partial(_combine_kernel, tq=tq),
        grid_spec=pltpu.PrefetchScalarGridSpec(
            num_scalar_prefetch=1,
            grid=(T // tq,),
            in_specs=[
                pl.BlockSpec((tq, D), lambda i, d: (i, 0)),
                pl.BlockSpec((tq, ROUTE_LANES), lambda i, d: (i, 0)),
                pl.BlockSpec((1, D), lambda i, d: (0, 0)),
                pl.BlockSpec(memory_space=pl.ANY),
            ],
            out_specs=pl.BlockSpec((tq, D), lambda i, d: (i, 0)),
            scratch_shapes=[pltpu.VMEM((2, MOE_TOPK, tq * ROW_TILE, LANES), jnp.uint32),
                            pltpu.SemaphoreType.DMA((2,))],
        ),
        out_shape=jax.ShapeDtypeStruct((T, D), F32),
        compiler_params=_cparams(("arbitrary",)),
        name="moe_combine",
    )(dest_flat, x1, route, gain.reshape(1, D).astype(F32), ys)


def _route_metadata(route, T):
    e = route[:, :MOE_TOPK].astype(jnp.int32)
    ids = jnp.arange(MOE_EXPERTS, dtype=jnp.int32)
    oh0 = e[:, 0, None] == ids
    oh1 = e[:, 1, None] == ids
    onehot = (oh0 | oh1).astype(jnp.int32)
    incl = jnp.cumsum(onehot, axis=0)
    counts = incl[-1]
    pcounts = (counts + MOE_ROWS - 1) // MOE_ROWS * MOE_ROWS
    pends = jnp.cumsum(pcounts)
    poffs = pends - pcounts
    slot = incl - onehot + poffs[None, :]
    dest = jnp.stack([jnp.sum(jnp.where(oh0, slot, 0), axis=1), jnp.sum(jnp.where(oh1, slot, 0), axis=1)], axis=1)
    nblk = (T * MOE_TOPK) // MOE_ROWS + MOE_EXPERTS
    first_row = jnp.arange(nblk, dtype=jnp.int32) * MOE_ROWS
    block_expert = jnp.minimum(jnp.sum((pends[None, :] <= first_row[:, None]).astype(jnp.int32), axis=1),
                               MOE_EXPERTS - 1).astype(jnp.int32)
    n_used = (pends[-1:] // MOE_ROWS).astype(jnp.int32)
    used = counts > 0
    expert_rank = (jnp.cumsum(used.astype(jnp.int32)) - 1).astype(jnp.int32)
    order = jnp.argsort(jnp.where(used, ids, ids + MOE_EXPERTS)).astype(jnp.int32)
    kth = jnp.where(ids < jnp.sum(used), order, -1)
    kth_expert = jnp.concatenate([kth, jnp.full((MOE_WEIGHT_SLOTS,), -1, jnp.int32)])
    blk = jnp.arange(nblk, dtype=jnp.int32)
    following = jnp.concatenate([block_expert[1:], block_expert[-1:]])
    zero_block = ((blk >= n_used[0] - 1) | (block_expert != following)).astype(jnp.int32)
    return dest.reshape(-1).astype(jnp.int32), zero_block, block_expert, n_used, expert_rank, kth_expert, nblk


def kernel(x, norm1_gain, w_in, hg_lb_logits, hg_norm_gain, rel_bias, w_branch_a, w_branch_b, w_out,
           norm2_gain, w_router_group, b_router_group, w_router_expert, b_router_expert,
           w_exp_gate, w_exp_up, w_exp_down, final_norm_gain):
    B, S, D = x.shape
    T = B * S
    depth = norm1_gain.shape[0]
    lower_bounds = jnp.cumsum(jax.nn.softmax(hg_lb_logits.astype(F32), axis=0), axis=0)
    att_w = ATT_HEADS * HEAD_DIM
    bias = _attn_bias(rel_bias)
    x2d = x.reshape(T, D)
    for layer in range(depth):
        zeros_lb = jnp.zeros((1, 3 * att_w), F32)
        hm = dict(head_major=True, B=B, S=S)
        h, q_a = _normproj(x2d, norm1_gain[layer], w_in, layer, D, B=B, S=S)
        lf_a = _inproj(h, w_in, layer, D, D, lower_bounds[layer].reshape(1, D), mode="forget", out_dtype=F32, **hm)
        i_a = _inproj(h, w_in, layer, 2 * D, D, zeros_lb, mode="none", out_dtype=BF16, **hm)
        sg_a = _inproj(h, w_in, layer, 3 * D, D, zeros_lb, mode="sigmoid", out_dtype=BF16, **hm)
        qkv_b = _inproj(h, w_in, layer, 4 * D, 3 * att_w, zeros_lb, mode="none", out_dtype=BF16, tn=att_w, **hm)
        gates = _inproj(h, w_in, layer, 4 * D + 3 * att_w, 2 * D, zeros_lb, mode="sigmoid",
                        head_major=False, out_dtype=BF16, B=B, S=S)

        y_a = _hgrn(q_a, lf_a, i_a, sg_a, hg_norm_gain[layer]).reshape(T, D)
        y_b = _attention(qkv_b, bias).reshape(T, ATT_HEADS_PER_GROUP * HEAD_DIM)
        merged = _merge(y_a, y_b, w_branch_a[layer], w_branch_b[layer], gates)

        n_pad = ROUTE_LANES - MOE_EXPERTS - MOE_GROUPS
        wr = jnp.concatenate([w_router_expert[layer].astype(F32), w_router_group[layer].astype(F32),
                              jnp.zeros((D, n_pad), F32)], axis=1)
        br = jnp.concatenate([b_router_expert[layer].astype(F32), b_router_group[layer].astype(F32),
                              jnp.zeros((n_pad,), F32)]).reshape(1, ROUTE_LANES)
        wr_hi = wr.astype(BF16)
        wr_lo = (wr - wr_hi.astype(F32)).astype(BF16)
        x1, h2, route = _outproj(merged, w_out[layer].astype(BF16), x2d, norm2_gain[layer],
                                 jnp.concatenate([wr_hi, wr_lo], axis=1), br)

        dest, zero_block, block_expert, n_used, expert_rank, kth_expert, nblk = _route_metadata(route, T)
        xs = _dispatch(h2, dest, zero_block, nblk * MOE_ROWS)
        ys = _experts(xs, block_expert, n_used, expert_rank, kth_expert,
                      w_exp_gate[layer], w_exp_up[layer], w_exp_down[layer])
        last = layer == depth - 1
        assert last, "the fused combine applies the final norm; deeper stacks need an un-normalised combine"
        x2d = _combine(x1, route, final_norm_gain, ys, dest)
    return x2d.reshape(B, S, D)
```

```python
import functools
import math

import numpy as np
import jax
import jax.numpy as jnp
from jax import lax
from jax.experimental import pallas as pl
from jax.experimental.pallas import tpu as pltpu

F32 = jnp.float32
BF16 = jnp.bfloat16

LANES = 128
NORM_EPS = 1e-6
HEAD_DIM = 128
ATT_GROUPS = ((128, 1), (512, 4), (2048, 16))
ATT_HEADS_PER_GROUP = 4
ATT_HEADS = len(ATT_GROUPS) * ATT_HEADS_PER_GROUP
ATT_BLOCK = 128
ATT_STACK = 16
REL_BUCKETS = 32
REL_MAX_DIST = 2048
MOE_GROUPS = 8
MOE_EXPERTS_PER_GROUP = 8
MOE_EXPERTS = MOE_GROUPS * MOE_EXPERTS_PER_GROUP
MOE_TOPK = 2
MOE_ROWS = 256
NEG_BIG = -1e30
VMEM_LIMIT = 56 * 1024 * 1024


def _cparams(sem):
    return pltpu.CompilerParams(dimension_semantics=sem, vmem_limit_bytes=VMEM_LIMIT)


def _normproj_kernel(x_ref, g_ref, w_ref, h_ref, o_ref, w_bf):
    @pl.when(pl.program_id(0) == 0)
    def _():
        w_bf[...] = w_ref[...].astype(BF16)

    x = x_ref[...]
    ms = jnp.mean(x * x, axis=-1, keepdims=True)
    h = (x * lax.rsqrt(ms + NORM_EPS) * g_ref[...]).astype(h_ref.dtype)
    h_ref[...] = h
    acc = jnp.dot(h, w_bf[...], preferred_element_type=F32)
    for hh in range(acc.shape[1] // HEAD_DIM):
        o_ref[0, hh] = acc[:, hh * HEAD_DIM:(hh + 1) * HEAD_DIM].astype(o_ref.dtype)


def _normproj(x2d, gain, w_all, layer, N, *, B, S, tm=512):
    T, D = x2d.shape
    assert S % tm == 0
    spb = S // tm
    return pl.pallas_call(
        _normproj_kernel,
        grid=(T // tm,),
        in_specs=[
            pl.BlockSpec((tm, D), lambda i: (i, 0)),
            pl.BlockSpec((1, D), lambda i: (0, 0)),
            pl.BlockSpec((None, D, N), lambda i: (layer, 0, 0), pipeline_mode=pl.Buffered(1)),
        ],
        out_specs=[
            pl.BlockSpec((tm, D), lambda i: (i, 0)),
            pl.BlockSpec((1, N // HEAD_DIM, tm, HEAD_DIM), lambda i: (i // spb, 0, i % spb, 0)),
        ],
        out_shape=[jax.ShapeDtypeStruct((T, D), BF16),
                   jax.ShapeDtypeStruct((B, N // HEAD_DIM, S, HEAD_DIM), BF16)],
        scratch_shapes=[pltpu.VMEM((D, N), BF16)],
        compiler_params=_cparams(("arbitrary",)),
        name="norm_inproj",
    )(x2d, gain.reshape(1, D).astype(F32), w_all)


def _inproj_kernel(h_ref, w_ref, lb_ref, o_ref, w_bf, *, mode, head_major):
    @pl.when(pl.program_id(1) == 0)
    def _():
        w_bf[...] = w_ref[0].astype(BF16)

    acc = jnp.dot(h_ref[...], w_bf[...], preferred_element_type=F32)
    if mode == "forget":
        lb = lb_ref[...]
        acc = lb + (1.0 - lb) * jax.nn.sigmoid(acc)
    elif mode == "sigmoid":
        acc = 0.5 * jnp.tanh(0.5 * acc) + 0.5
    if head_major:
        for hh in range(acc.shape[1] // HEAD_DIM):
            o_ref[0, hh] = acc[:, hh * HEAD_DIM:(hh + 1) * HEAD_DIM].astype(o_ref.dtype)
    else:
        o_ref[...] = acc.astype(o_ref.dtype)


def _inproj(h, w_all, layer, col0, N, lb, *, mode, head_major, out_dtype, B, S, tm=1024, tn=1024):
    T, D = h.shape
    assert N % tn == 0 and col0 % LANES == 0 and T % tm == 0 and S % tm == 0
    spb = S // tm
    if head_major:
        out_shape = jax.ShapeDtypeStruct((B, N // HEAD_DIM, S, HEAD_DIM), out_dtype)
        out_spec = pl.BlockSpec((1, tn // HEAD_DIM, tm, HEAD_DIM), lambda j, i: (i // spb, j, i % spb, 0))
    else:
        out_shape = jax.ShapeDtypeStruct((T, N), out_dtype)
        out_spec = pl.BlockSpec((tm, tn), lambda j, i: (i, j))
    return pl.pallas_call(
        functools.partial(_inproj_kernel, mode=mode, head_major=head_major),
        grid=(N // tn, T // tm),
        in_specs=[
            pl.BlockSpec((tm, D), lambda j, i: (i, 0)),
            pl.BlockSpec((pl.Element(1), pl.Element(D), pl.Element(tn)), lambda j, i: (layer, 0, pl.multiple_of(col0 + j * tn, LANES))),
            pl.BlockSpec((1, tn), lambda j, i: (0, j)),
        ],
        out_specs=out_spec,
        out_shape=out_shape,
        scratch_shapes=[pltpu.VMEM((D, tn), BF16)],
        compiler_params=_cparams(("parallel", "arbitrary")),
        name="inproj_" + mode,
    )(h, w_all, lb)


HG_CHUNK = 256
HG_DIAG = 16
HG_HEADS_PER_STEP = 4


def _hgrn_levels(C):
    out, m = [], C // 2
    while m >= HG_DIAG:
        out.append(m)
        m //= 2
    return out


def _hgrn_masks(C):
    t = np.arange(C)[:, None]
    s = np.arange(C)[None, :]
    masks = []
    for m in _hgrn_levels(C):
        masks.append((t // (2 * m) == s // (2 * m)) & ((t // m) % 2 == 1) & ((s // m) % 2 == 0))
    masks.append((t // HG_DIAG == s // HG_DIAG) & (t >= s))
    total = np.sum(np.stack(masks).astype(np.int32), axis=0)
    assert np.array_equal(total, (t >= s).astype(np.int32))
    return np.stack(masks).astype(np.float32)


def _dot_nt(a, b):
    return lax.dot_general(a, b, (((1,), (1,)), ((), ())), preferred_element_type=F32)


def _dot_tn(a, b):
    return lax.dot_general(a, b, (((0,), (0,)), ((), ())), preferred_element_type=F32)


def _hgrn_kernel(q_ref, f_ref, v_ref, sg_ref, gain_ref, tril_ref, mask_ref, o_ref, st_ref, b_ref, *, C):
    S = q_ref.shape[2]
    HP = q_ref.shape[1]
    levels = _hgrn_levels(C)
    nchunks = S // C
    tril = tril_ref[...]

    def ref_rows(h, block, row_of_block):
        parts = [jnp.broadcast_to(b_ref[h, pl.ds(row_of_block(p), 1), :], (block, HEAD_DIM))
                 for p in range(C // block)]
        return jnp.concatenate(parts, axis=0)

    def cumsum(c):
        r0 = pl.multiple_of(c * C, C)
        out = []
        for h in range(HP):
            lf2 = jnp.log2(f_ref[0, h, pl.ds(r0, C), :])
            hi = lf2.astype(BF16)
            lo = (lf2 - hi.astype(F32)).astype(BF16)
            cat = jnp.concatenate([hi, lo], axis=1)
            half = C // 2
            tri = tril[:half, :half]
            r_a = jnp.dot(tri, cat[:half], preferred_element_type=F32)
            r_b = jnp.dot(tri, cat[half:], preferred_element_type=F32)
            b_a = r_a[:, :HEAD_DIM] + r_a[:, HEAD_DIM:]
            b_b = r_b[:, :HEAD_DIM] + r_b[:, HEAD_DIM:] + b_a[half - 1:half, :]
            out.append(jnp.concatenate([b_a, b_b], axis=0))
        return tuple(out)

    st_ref[...] = jnp.zeros_like(st_ref)

    def chunk(c, b_all):
        r0 = pl.multiple_of(c * C, C)
        rows = pl.ds(r0, C)
        b_next = cumsum(jnp.minimum(c + 1, nchunks - 1))
        for h in range(HP):
            b = b_all[h]
            b_ref[h] = b
            q = q_ref[0, h, rows, :].astype(F32)
            kk = 1.0 - f_ref[0, h, rows, :]
            b_last = b_ref[h, pl.ds(C - 1, 1), :]
            scores = jnp.zeros((C, C), BF16)
            half = C // 2
            ref_top = b_ref[h, pl.ds(half - 1, 1), :]
            a_top = (q[half:] * jnp.exp2(b[half:] - ref_top)).astype(BF16)
            k_top = (kk[:half] * jnp.exp2(ref_top - b[:half])).astype(BF16)
            s_top = _dot_nt(a_top, k_top).astype(BF16)
            for li, m in enumerate(levels):
                if m == half:
                    continue
                parts = []
                for p in range(C // (2 * m)):
                    ref = b_ref[h, pl.ds(2 * m * p + m - 1, 1), :]
                    parts += [ref - b[2 * m * p:2 * m * p + m], b[2 * m * p + m:2 * m * (p + 1)] - ref]
                e = jnp.exp2(jnp.concatenate(parts, axis=0))
                s_l = _dot_nt((q * e).astype(BF16), (kk * e).astype(BF16)).astype(BF16)
                scores = scores + mask_ref[li] * s_l
            d = b - ref_rows(h, HG_DIAG, lambda p: HG_DIAG * p + HG_DIAG // 2 - 1)
            s_l = _dot_nt((q * jnp.exp2(d)).astype(BF16), (kk * jnp.exp2(-d)).astype(BF16)).astype(BF16)
            scores = scores + mask_ref[len(levels)] * s_l
            v = v_ref[0, h, rows, :]
            st = st_ref[h]
            qe = (q * jnp.exp2(b)).astype(BF16)
            kd = (kk * jnp.exp2(b_last - b)).astype(BF16)
            o = jnp.dot(scores, v, preferred_element_type=F32) + _dot_nt(qe, st.astype(BF16))
            o_top = jnp.dot(s_top, v[:half], preferred_element_type=F32)
            o = jnp.concatenate([o[:half], o[half:] + o_top], axis=0)
            st_ref[h] = st * jnp.exp2(b_last) + _dot_tn(v, kd)
            ms = jnp.mean(o * o, axis=-1, keepdims=True)
            y = o * lax.rsqrt(ms + NORM_EPS) * gain_ref[:, h * HEAD_DIM:(h + 1) * HEAD_DIM]
            y = y * sg_ref[0, h, rows, :].astype(F32)
            o_ref[0, rows, h * HEAD_DIM:(h + 1) * HEAD_DIM] = y.astype(o_ref.dtype)
        return b_next

    lax.fori_loop(0, nchunks, chunk, cumsum(0))


def _hgrn(q, lf, v, sg, gain):
    B, H, S, E = q.shape
    C, HP = HG_CHUNK, HG_HEADS_PER_STEP
    masks = jnp.asarray(_hgrn_masks(C), dtype=BF16)
    tril = jnp.asarray(np.tril(np.ones((C, C), np.float32)), dtype=BF16)
    head_spec = pl.BlockSpec((1, HP, S, E), lambda b, h: (b, h, 0, 0))
    return pl.pallas_call(
        functools.partial(_hgrn_kernel, C=C),
        grid=(B, H // HP),
        in_specs=[
            head_spec, head_spec, head_spec, head_spec,
            pl.BlockSpec((1, HP * E), lambda b, h: (0, h)),
            pl.BlockSpec((C, C), lambda b, h: (0, 0)),
            pl.BlockSpec(masks.shape, lambda b, h: (0, 0, 0)),
        ],
        out_specs=pl.BlockSpec((1, S, HP * E), lambda b, h: (b, 0, h)),
        out_shape=jax.ShapeDtypeStruct((B, S, H * E), BF16),
        scratch_shapes=[pltpu.VMEM((HP, E, E), F32), pltpu.VMEM((HP, C, E), F32)],
        compiler_params=_cparams(("parallel", "parallel")),
        name="hgrn2",
    )(q, lf, v, sg, gain.reshape(1, H * E).astype(F32), tril, masks)


def _t5_bucket_np(dist):
    exact = REL_BUCKETS // 2
    d_f = np.maximum(dist, 1).astype(np.float32)
    log_b = exact + (np.log(d_f / np.float32(exact)) / np.float32(math.log(REL_MAX_DIST / exact))
                     * np.float32(REL_BUCKETS - exact)).astype(np.int32)
    return np.where(dist < exact, dist, np.minimum(log_b, REL_BUCKETS - 1))


def _attn_bias(rel_bias):
    blk = ATT_BLOCK
    period = 3 * blk
    out = []
    for gi, (window, dilation) in enumerate(ATT_GROUPS):
        n_back = window // dilation
        assert n_back <= blk
        hs = slice(gi * ATT_HEADS_PER_GROUP, (gi + 1) * ATT_HEADS_PER_GROUP)
        bucket = _t5_bucket_np(np.arange(n_back + 1) * dilation)
        by_delta = rel_bias[:, hs][bucket].astype(F32).T
        u = jnp.full((ATT_HEADS_PER_GROUP, period), NEG_BIG, F32)
        u = u.at[:, 2 * blk - 1 - n_back:2 * blk].set(by_delta[:, ::-1])
        flat = jnp.tile(u, (1, blk))[:, :blk * (period - 1)]
        out.append(flat.reshape(ATT_HEADS_PER_GROUP, blk, period - 1)[:, :, blk - 1:3 * blk - 1])
    return jnp.stack(out, axis=0)


def _attn_kernel(q_ref, k_ref, v_ref, bias_ref, o_ref, qf, kf, vf, og, lg):
    S = q_ref.shape[3]
    scale = HEAD_DIM ** -0.5
    blk = ATT_BLOCK

    for g, (window, d) in enumerate(ATT_GROUPS):
        L = S // d
        nb = L // blk
        if d > 1:
            qf[...] = q_ref[0, g, 0].astype(F32)
            kf[...] = k_ref[0, g, 0].astype(F32)
            vf[...] = v_ref[0, g, 0].astype(F32)

        def load(ref_bf, ref_f32, start, size, g=g, d=d):
            if d == 1:
                return ref_bf[0, g, 0, pl.ds(start, size), :]
            return ref_f32[pl.ds(start, size, stride=d), :].astype(BF16)

        def scores(r, n, g=g, d=d, load=load):
            start = r + n * blk * d
            q = load(q_ref, qf, start, blk)
            if n == 0:
                k = load(k_ref, kf, start, blk)
                v = load(v_ref, vf, start, blk)
                s = _dot_nt(q, k) * scale + bias_ref[g, 0, :, blk:]
            else:
                first = start - blk * d
                k = load(k_ref, kf, first, 2 * blk)
                v = load(v_ref, vf, first, 2 * blk)
                s = _dot_nt(q, k) * scale + bias_ref[g, 0]
            return start, s, v

        def softmax_pv(items, g=g, d=d):
            s = jnp.concatenate([it[1] for it in items], axis=0)
            m = jnp.max(s, axis=-1, keepdims=True)
            p = jnp.exp(s - m)
            den = jnp.sum(p, axis=-1, keepdims=True)
            pb = p.astype(BF16)
            lse = jnp.broadcast_to(m + jnp.log(den), (s.shape[0], HEAD_DIM))
            for j, (start, _, v) in enumerate(items):
                sl = slice(j * blk, (j + 1) * blk)
                rows = pl.ds(start, blk, stride=d) if d > 1 else pl.ds(start, blk)
                og[g, rows, :] = jnp.dot(pb[sl], v, preferred_element_type=F32) / den[sl]
                lg[g, rows, :] = lse[sl]

        blocks = [(r, n) for r in range(d) for n in range(nb)]
        first_blocks = [bn for bn in blocks if bn[1] == 0]
        later_blocks = [bn for bn in blocks if bn[1] > 0]
        for group in (first_blocks, later_blocks):
            for j0 in range(0, len(group), ATT_STACK):
                softmax_pv([scores(r, n) for r, n in group[j0:j0 + ATT_STACK]])

    rows_per_step = 256

    def mix(i, carry):
        rows = pl.ds(pl.multiple_of(i * rows_per_step, rows_per_step), rows_per_step)
        l0, l1, l2 = lg[0, rows, :], lg[1, rows, :], lg[2, rows, :]
        mx = jnp.maximum(jnp.maximum(l0, l1), l2)
        e0, e1, e2 = jnp.exp(l0 - mx), jnp.exp(l1 - mx), jnp.exp(l2 - mx)
        num = e0 * og[0, rows, :] + e1 * og[1, rows, :] + e2 * og[2, rows, :]
        o_ref[0, rows, :] = (num / (e0 + e1 + e2)).astype(o_ref.dtype)
        return carry

    lax.fori_loop(0, S // rows_per_step, mix, 0)


def _attention(qkv, bias):
    B, _, S, E = qkv.shape
    G, HG = len(ATT_GROUPS), ATT_HEADS_PER_GROUP
    x = qkv.reshape(B, 3, G, HG, S, E)

    def spec(which):
        return pl.BlockSpec((1, None, G, 1, S, E), lambda b, h, which=which: (b, which, 0, h, 0, 0))

    def kern(q_ref, k_ref, v_ref, bias_ref, o_ref, *scratch):
        _attn_kernel(q_ref, k_ref, v_ref, bias_ref, o_ref, *scratch)

    return pl.pallas_call(
        kern,
        grid=(B, HG),
        in_specs=[spec(0), spec(1), spec(2),
                  pl.BlockSpec((G, 1, ATT_BLOCK, 2 * ATT_BLOCK), lambda b, h: (0, h, 0, 0))],
        out_specs=pl.BlockSpec((1, S, E), lambda b, h: (b, 0, h)),
        out_shape=jax.ShapeDtypeStruct((B, S, HG * E), BF16),
        scratch_shapes=[pltpu.VMEM((S, E), F32)] * 3 + [pltpu.VMEM((G, S, E), F32)] * 2,
        compiler_params=_cparams(("parallel", "parallel")),
        name="dilated_attn",
    )(x, x, x, bias)


def _merge_kernel(ya_ref, yb_ref, wa_ref, wb_ref, ga_ref, gb_ref, o_ref, wa_bf, wb_bf):
    @pl.when(pl.program_id(1) == 0)
    def _():
        wa_bf[...] = wa_ref[...].astype(BF16)
        wb_bf[...] = wb_ref[...].astype(BF16)

    a = jnp.dot(ya_ref[...], wa_bf[...], preferred_element_type=F32)
    b = jnp.dot(yb_ref[...], wb_bf[...], preferred_element_type=F32)
    o_ref[...] = (ga_ref[...].astype(F32) * a + gb_ref[...].astype(F32) * b).astype(o_ref.dtype)


def _merge(ya, yb, wa, wb, gates, tm=1024, tn=1024):
    T, Ka = ya.shape
    Kb = yb.shape[1]
    N = wa.shape[1]
    nj = N // tn
    return pl.pallas_call(
        _merge_kernel,
        grid=(nj, T // tm),
        in_specs=[
            pl.BlockSpec((tm, Ka), lambda j, i: (i, 0)),
            pl.BlockSpec((tm, Kb), lambda j, i: (i, 0)),
            pl.BlockSpec((Ka, tn), lambda j, i: (0, j)),
            pl.BlockSpec((Kb, tn), lambda j, i: (0, j)),
            pl.BlockSpec((tm, tn), lambda j, i: (i, j)),
            pl.BlockSpec((tm, tn), lambda j, i, nj=nj: (i, j + nj)),
        ],
        out_specs=pl.BlockSpec((tm, tn), lambda j, i: (i, j)),
        out_shape=jax.ShapeDtypeStruct((T, N), BF16),
        scratch_shapes=[pltpu.VMEM((Ka, tn), BF16), pltpu.VMEM((Kb, tn), BF16)],
        compiler_params=_cparams(("parallel", "arbitrary")),
        name="branch_merge",
    )(ya, yb, wa, wb, gates, gates)


ROUTE_LANES = LANES


def _pack_halves(x_bf):
    bits = pltpu.bitcast(x_bf.astype(F32), jnp.uint32)
    n = bits.shape[1] // 2
    return (bits[:, :n] >> 16) | (bits[:, n:] & jnp.uint32(0xFFFF0000))


def _unpack_halves(p):
    lo = pltpu.bitcast(p << 16, F32)
    hi = pltpu.bitcast(p & jnp.uint32(0xFFFF0000), F32)
    return jnp.concatenate([lo, hi], axis=1)


ROW_TILE = 8


def _store_row_tiles(ref, packed):
    m = packed.shape[0]
    assert packed.shape[1] == ROW_TILE * LANES
    for c in range(ROW_TILE):
        ref[pl.ds(c, m, stride=ROW_TILE), :] = packed[:, c * LANES:(c + 1) * LANES]


def _load_row_tiles(ref, m):
    return jnp.concatenate([ref[pl.ds(c, m, stride=ROW_TILE), :] for c in range(ROW_TILE)], axis=1)


def _outproj_kernel(m_ref, w_ref, x_ref, g_ref, wrc_ref, br_ref, x1_ref, h2_ref, rt_ref, lg_scr):
    @pl.when(pl.program_id(0) == 0)
    def _():
        lg_scr[...] = jnp.zeros_like(lg_scr)

    lg_prev = lg_scr[...]
    x1 = x_ref[...] + jnp.dot(m_ref[...], w_ref[...], preferred_element_type=F32)
    x1_ref[...] = x1
    ms = jnp.mean(x1 * x1, axis=-1, keepdims=True)
    h2 = x1 * lax.rsqrt(ms + NORM_EPS) * g_ref[...]
    h_hi = h2.astype(BF16)
    _store_row_tiles(h2_ref, _pack_halves(h_hi))
    h_lo = (h2 - h_hi.astype(F32)).astype(BF16)
    r = jnp.dot(h_hi, wrc_ref[...], preferred_element_type=F32)
    lg = (r[:, :ROUTE_LANES] + r[:, ROUTE_LANES:]
          + jnp.dot(h_lo, wrc_ref[:, :ROUTE_LANES], preferred_element_type=F32)) + br_ref[...]
    lg_scr[...] = lg
    lg = lg_prev

    lane = lax.broadcasted_iota(jnp.int32, lg.shape, 1)
    lane_f = lane.astype(F32)
    big = float(ROUTE_LANES)
    is_group = (lane >= MOE_EXPERTS) & (lane < MOE_EXPERTS + MOE_GROUPS)
    lgg = jnp.where(is_group, lg, -jnp.inf)
    gmax = jnp.max(lgg, axis=-1, keepdims=True)
    gsel = jnp.min(jnp.where(lgg == gmax, lane_f - MOE_EXPERTS, big), axis=-1, keepdims=True)
    pg = 1.0 / jnp.sum(jnp.where(is_group, jnp.exp(lg - gmax), 0.0), axis=-1, keepdims=True)

    in_group = (lane < MOE_EXPERTS) & ((lane // MOE_EXPERTS_PER_GROUP).astype(F32) == gsel)
    le = jnp.where(in_group, lg, -jnp.inf)
    t1 = jnp.max(le, axis=-1, keepdims=True)
    i1 = jnp.min(jnp.where(le == t1, lane_f, big), axis=-1, keepdims=True)
    le2 = jnp.where(lane_f == i1, -jnp.inf, le)
    t2 = jnp.max(le2, axis=-1, keepdims=True)
    i2 = jnp.min(jnp.where(le2 == t2, lane_f, big), axis=-1, keepdims=True)
    e2 = jnp.exp(t2 - t1)
    w1 = pg / (1.0 + e2)
    w2 = pg * e2 / (1.0 + e2)
    rt_ref[...] = jnp.where(lane == 0, i1, jnp.where(lane == 1, i2, jnp.where(lane == 2, w1,
                            jnp.where(lane == 3, w2, 0.0))))


def _outproj(merged, w_out, x2d, gain, wr_cat, br, tm=512):
    T, D = x2d.shape
    n = T // tm
    row = lambda i: (jnp.minimum(i, n - 1), 0)
    prev_row = lambda i: (jnp.maximum(i - 1, 0), 0)
    const = lambda i: (0, 0)
    return pl.pallas_call(
        _outproj_kernel,
        grid=(n + 1,),
        in_specs=[
            pl.BlockSpec((tm, D), row), pl.BlockSpec((D, D), const), pl.BlockSpec((tm, D), row),
            pl.BlockSpec((1, D), const), pl.BlockSpec((D, 2 * ROUTE_LANES), const),
            pl.BlockSpec((1, ROUTE_LANES), const),
        ],
        out_specs=[pl.BlockSpec((tm, D), row), pl.BlockSpec((tm * ROW_TILE, LANES), row),
                   pl.BlockSpec((tm, ROUTE_LANES), prev_row)],
        out_shape=[jax.ShapeDtypeStruct((T, D), F32), jax.ShapeDtypeStruct((T * ROW_TILE, LANES), jnp.uint32),
                   jax.ShapeDtypeStruct((T, ROUTE_LANES), F32)],
        scratch_shapes=[pltpu.VMEM((tm, ROUTE_LANES), F32)],
        compiler_params=_cparams(("arbitrary",)),
        name="outproj_router",
    )(merged, w_out, x2d, gain.reshape(1, D).astype(F32), wr_cat, br)


def _dispatch_kernel(dest_ref, zb_ref, h_ref, xs_ref, zeros, sem, *, tq):
    base = pl.program_id(0) * tq
    block_rows = MOE_ROWS * ROW_TILE
    n_blocks = xs_ref.shape[0] // block_rows

    @pl.when(pl.program_id(0) == 0)
    def _():
        zeros[...] = jnp.zeros_like(zeros)

        def zero_copy(b):
            return pltpu.make_async_copy(zeros, xs_ref.at[pl.ds(pl.multiple_of(b * block_rows, block_rows), block_rows)],
                                         sem)

        def start(b, carry):
            @pl.when(zb_ref[b] != 0)
            def _():
                zero_copy(b).start()
            return carry

        def wait(b, carry):
            @pl.when(zb_ref[b] != 0)
            def _():
                zero_copy(b).wait()
            return carry

        lax.fori_loop(0, n_blocks, start, 0)
        lax.fori_loop(0, n_blocks, wait, 0)

    def token_rows(t):
        return pl.ds(pl.multiple_of(t * ROW_TILE, ROW_TILE), ROW_TILE)

    def copy(j, k):
        return pltpu.make_async_copy(h_ref.at[token_rows(j)], xs_ref.at[token_rows(dest_ref[2 * (base + j) + k])], sem)

    def issue(j, carry):
        for k in range(MOE_TOPK):
            copy(j, k).start(priority=k % 2)
        return carry

    lax.fori_loop(0, tq, issue, 0)
    for _ in range(MOE_TOPK):
        pltpu.make_async_copy(h_ref, xs_ref.at[pl.ds(0, tq * ROW_TILE)], sem).wait()


def _dispatch(h2, dest_flat, zero_block, n_rows, tq=2048):
    T = h2.shape[0] // ROW_TILE
    return pl.pallas_call(
        functools.partial(_dispatch_kernel, tq=tq),
        grid_spec=pltpu.PrefetchScalarGridSpec(
            num_scalar_prefetch=2,
            grid=(T // tq,),
            in_specs=[pl.BlockSpec((tq * ROW_TILE, LANES), lambda i, d, z: (i, 0))],
            out_specs=pl.BlockSpec(memory_space=pl.ANY),
            scratch_shapes=[pltpu.VMEM((MOE_ROWS * ROW_TILE, LANES), h2.dtype), pltpu.SemaphoreType.DMA(())],
        ),
        out_shape=jax.ShapeDtypeStruct((n_rows * ROW_TILE, LANES), h2.dtype),
        compiler_params=_cparams(("arbitrary",)),
        name="moe_dispatch",
    )(dest_flat, zero_block, h2)


MOE_WEIGHT_SLOTS = 2


def _expert_kernel(be_ref, nu_ref, rank_ref, kth_ref, xs_ref, wg_hbm, wu_hbm, wd_hbm, o_ref,
                   wg_f, wu_f, wd_f, wg_s, wu_s, wd_s, sem):
    i = pl.program_id(0)
    e = be_ref[i]
    active = i < nu_ref[0]
    new_expert = active & ((i == 0) | (e != be_ref[jnp.maximum(i - 1, 0)]))
    k = rank_ref[e]
    slot = k % MOE_WEIGHT_SLOTS
    ahead = MOE_WEIGHT_SLOTS - 1

    def weight_copies(expert, s):
        return (pltpu.make_async_copy(wg_hbm.at[expert], wg_f.at[s], sem.at[s, 0]),
                pltpu.make_async_copy(wu_hbm.at[expert], wu_f.at[s], sem.at[s, 1]),
                pltpu.make_async_copy(wd_hbm.at[expert], wd_f.at[s], sem.at[s, 2]))

    def start_kth(j):
        nxt = kth_ref[j]

        @pl.when(nxt >= 0)
        def _():
            for c in weight_copies(nxt, j % MOE_WEIGHT_SLOTS):
                c.start()

    @pl.when(active & (i == 0))
    def _():
        for j in range(ahead):
            start_kth(j)

    @pl.when(new_expert)
    def _():
        start_kth(k + ahead)
        for c in weight_copies(e, slot):
            c.wait()
        wg_s[...] = wg_f[slot].astype(BF16)
        wu_s[...] = wu_f[slot].astype(BF16)
        wd_s[...] = wd_f[slot].astype(BF16)

    @pl.when(active)
    def _():
        x = _unpack_halves(_load_row_tiles(xs_ref, MOE_ROWS)).astype(BF16)
        g = jnp.dot(x, wg_s[...], preferred_element_type=F32)
        u = jnp.dot(x, wu_s[...], preferred_element_type=F32)
        a = (g * jax.nn.sigmoid(g) * u).astype(BF16)
        y = jnp.dot(a, wd_s[...], preferred_element_type=F32)
        _store_row_tiles(o_ref, _pack_halves(y.astype(BF16)))

    @pl.when(jnp.logical_not(active))
    def _():
        o_ref[...] = jnp.zeros_like(o_ref)


def _experts(xs, block_expert, n_used, expert_rank, kth_expert, w_gate, w_up, w_down):
    Dh = xs.shape[1]
    D, F = w_gate.shape[1], w_gate.shape[2]
    assert Dh == LANES and D == 2 * ROW_TILE * LANES
    nblk = block_expert.shape[0]
    P = nblk * MOE_ROWS * ROW_TILE
    rows = lambda i, be, nu, nx, sl: (jnp.minimum(i, nu[0] - 1), 0)
    hbm = pl.BlockSpec(memory_space=pl.ANY)
    return pl.pallas_call(
        _expert_kernel,
        grid_spec=pltpu.PrefetchScalarGridSpec(
            num_scalar_prefetch=4,
            grid=(nblk,),
            in_specs=[pl.BlockSpec((MOE_ROWS * ROW_TILE, Dh), rows), hbm, hbm, hbm],
            out_specs=pl.BlockSpec((MOE_ROWS * ROW_TILE, Dh), lambda i, be, nu, nx, sl: (i, 0)),
            scratch_shapes=[pltpu.VMEM((MOE_WEIGHT_SLOTS, D, F), F32), pltpu.VMEM((MOE_WEIGHT_SLOTS, D, F), F32),
                            pltpu.VMEM((MOE_WEIGHT_SLOTS, F, D), F32),
                            pltpu.VMEM((D, F), BF16), pltpu.VMEM((D, F), BF16), pltpu.VMEM((F, D), BF16),
                            pltpu.SemaphoreType.DMA((MOE_WEIGHT_SLOTS, 3))],
        ),
        out_shape=jax.ShapeDtypeStruct((P, Dh), jnp.uint32),
        compiler_params=_cparams(("arbitrary",)),
        name="moe_experts",
    )(block_expert, n_used, expert_rank, kth_expert, xs, w_gate, w_up, w_down)


def _combine_kernel(dest_ref, x1_ref, rt_ref, g_ref, ys_ref, o_ref, buf, sem, *, tq):
    i = pl.program_id(0)
    n = pl.num_programs(0)

    def token_rows(t):
        return pl.ds(pl.multiple_of(t * ROW_TILE, ROW_TILE), ROW_TILE)

    def issue(step, s):
        def body(j, carry):
            for k in range(MOE_TOPK):
                pltpu.make_async_copy(ys_ref.at[token_rows(dest_ref[2 * (step * tq + j) + k])],
                                      buf.at[s, k, token_rows(j)], sem.at[s]).start(priority=k % 2)
            return carry
        lax.fori_loop(0, tq, body, 0)

    @pl.when(i == 0)
    def _():
        issue(0, 0)

    for s in range(2):
        @pl.when((i + 1 < n) & ((i + 1) % 2 == s))
        def _():
            issue(i + 1, s)

    for s in range(2):
        @pl.when(i % 2 == s)
        def _():
            for k in range(MOE_TOPK):
                pltpu.make_async_copy(ys_ref.at[pl.ds(0, tq * ROW_TILE)], buf.at[s, k], sem.at[s]).wait()
            rt = rt_ref[...]
            y0 = _unpack_halves(_load_row_tiles(buf.at[s, 0], tq))
            y1 = _unpack_halves(_load_row_tiles(buf.at[s, 1], tq))
            x = x1_ref[...] + rt[:, 2:3] * y0 + rt[:, 3:4] * y1
            ms = jnp.mean(x * x, axis=-1, keepdims=True)
            o_ref[...] = x * lax.rsqrt(ms + NORM_EPS) * g_ref[...]


def _combine(x1, route, gain, ys, dest_flat, tq=256):
    T, D = x1.shape
    return pl.pallas_call(
        functools.partial(_combine_kernel, tq=tq),
        grid_spec=pltpu.PrefetchScalarGridSpec(
            num_scalar_prefetch=1,
            grid=(T // tq,),
            in_specs=[
                pl.BlockSpec((tq, D), lambda i, d: (i, 0)),
                pl.BlockSpec((tq, ROUTE_LANES), lambda i, d: (i, 0)),
                pl.BlockSpec((1, D), lambda i, d: (0, 0)),
                pl.BlockSpec(memory_space=pl.ANY),
            ],
            out_specs=pl.BlockSpec((tq, D), lambda i, d: (i, 0)),
            scratch_shapes=[pltpu.VMEM((2, MOE_TOPK, tq * ROW_TILE, LANES), jnp.uint32),
                            pltpu.SemaphoreType.DMA((2,))],
        ),
        out_shape=jax.ShapeDtypeStruct((T, D), F32),
        compiler_params=_cparams(("arbitrary",)),
        name="moe_combine",
    )(dest_flat, x1, route, gain.reshape(1, D).astype(F32), ys)


def _route_metadata(route, T):
    e = route[:, :MOE_TOPK].astype(jnp.int32)
    ids = jnp.arange(MOE_EXPERTS, dtype=jnp.int32)
    oh0 = e[:, 0, None] == ids
    oh1 = e[:, 1, None] == ids
    onehot = (oh0 | oh1).astype(jnp.int32)
    incl = jnp.cumsum(onehot, axis=0)
    counts = incl[-1]
    pcounts = (counts + MOE_ROWS - 1) // MOE_ROWS * MOE_ROWS
    pends = jnp.cumsum(pcounts)
    poffs = pends - pcounts
    slot = incl - onehot + poffs[None, :]
    dest = jnp.stack([jnp.sum(jnp.where(oh0, slot, 0), axis=1), jnp.sum(jnp.where(oh1, slot, 0), axis=1)], axis=1)
    nblk = (T * MOE_TOPK) // MOE_ROWS + MOE_EXPERTS
    first_row = jnp.arange(nblk, dtype=jnp.int32) * MOE_ROWS
    block_expert = jnp.minimum(jnp.sum((pends[None, :] <= first_row[:, None]).astype(jnp.int32), axis=1),
                               MOE_EXPERTS - 1).astype(jnp.int32)
    n_used = (pends[-1:] // MOE_ROWS).astype(jnp.int32)
    used = counts > 0
    expert_rank = (jnp.cumsum(used.astype(jnp.int32)) - 1).astype(jnp.int32)
    order = jnp.argsort(jnp.where(used, ids, ids + MOE_EXPERTS)).astype(jnp.int32)
    kth = jnp.where(ids < jnp.sum(used), order, -1)
    kth_expert = jnp.concatenate([kth, jnp.full((MOE_WEIGHT_SLOTS,), -1, jnp.int32)])
    blk = jnp.arange(nblk, dtype=jnp.int32)
    following = jnp.concatenate([block_expert[1:], block_expert[-1:]])
    zero_block = ((blk >= n_used[0] - 1) | (block_expert != following)).astype(jnp.int32)
    return dest.reshape(-1).astype(jnp.int32), zero_block, block_expert, n_used, expert_rank, kth_expert, nblk


def kernel(x, norm1_gain, w_in, hg_lb_logits, hg_norm_gain, rel_bias, w_branch_a, w_branch_b, w_out,
           norm2_gain, w_router_group, b_router_group, w_router_expert, b_router_expert,
           w_exp_gate, w_exp_up, w_exp_down, final_norm_gain):
    B, S, D = x.shape
    T = B * S
    depth = norm1_gain.shape[0]
    lower_bounds = jnp.cumsum(jax.nn.softmax(hg_lb_logits.astype(F32), axis=0), axis=0)
    att_w = ATT_HEADS * HEAD_DIM
    bias = _attn_bias(rel_bias)
    x2d = x.reshape(T, D)
    for layer in range(depth):
        zeros_lb = jnp.zeros((1, 3 * att_w), F32)
        hm = dict(head_major=True, B=B, S=S)
        h, q_a = _normproj(x2d, norm1_gain[layer], w_in, layer, D, B=B, S=S)
        lf_a = _inproj(h, w_in, layer, D, D, lower_bounds[layer].reshape(1, D), mode="forget", out_dtype=F32, **hm)
        i_a = _inproj(h, w_in, layer, 2 * D, D, zeros_lb, mode="none", out_dtype=BF16, **hm)
        sg_a = _inproj(h, w_in, layer, 3 * D, D, zeros_lb, mode="sigmoid", out_dtype=BF16, **hm)
        qkv_b = _inproj(h, w_in, layer, 4 * D, 3 * att_w, zeros_lb, mode="none", out_dtype=BF16, tn=att_w, **hm)
        gates = _inproj(h, w_in, layer, 4 * D + 3 * att_w, 2 * D, zeros_lb, mode="sigmoid",
                        head_major=False, out_dtype=BF16, B=B, S=S)

        y_a = _hgrn(q_a, lf_a, i_a, sg_a, hg_norm_gain[layer]).reshape(T, D)
        y_b = _attention(qkv_b, bias).reshape(T, ATT_HEADS_PER_GROUP * HEAD_DIM)
        merged = _merge(y_a, y_b, w_branch_a[layer], w_branch_b[layer], gates)

        n_pad = ROUTE_LANES - MOE_EXPERTS - MOE_GROUPS
        wr = jnp.concatenate([w_router_expert[layer].astype(F32), w_router_group[layer].astype(F32),
                              jnp.zeros((D, n_pad), F32)], axis=1)
        br = jnp.concatenate([b_router_expert[layer].astype(F32), b_router_group[layer].astype(F32),
                              jnp.zeros((n_pad,), F32)]).reshape(1, ROUTE_LANES)
        wr_hi = wr.astype(BF16)
        wr_lo = (wr - wr_hi.astype(F32)).astype(BF16)
        x1, h2, route = _outproj(merged, w_out[layer].astype(BF16), x2d, norm2_gain[layer],
                                 jnp.concatenate([wr_hi, wr_lo], axis=1), br)

        dest, zero_block, block_expert, n_used, expert_rank, kth_expert, nblk = _route_metadata(route, T)
        xs = _dispatch(h2, dest, zero_block, nblk * MOE_ROWS)
        ys = _experts(xs, block_expert, n_used, expert_rank, kth_expert,
                      w_exp_gate[layer], w_exp_up[layer], w_exp_down[layer])
        last = layer == depth - 1
        assert last, "the fused combine applies the final norm; deeper stacks need an un-normalised combine"
        x2d = _combine(x1, route, final_norm_gain, ys, dest)
    return x2d.reshape(B, S, D)
```
